```python
import math
import jax, jax.numpy as jnp
from jax import lax
import numpy as np


D_MODEL = 1024
BATCH = 2
SEQ = 8192
DEPTH = 1

MIX_WIDTH = D_MODEL
HEAD_DIM = 64
ATTN_HEADS = (MIX_WIDTH // 2) // HEAD_DIM
ATTN_WIDTH = ATTN_HEADS * HEAD_DIM
SSM_GROUP_CH = 16
SSM_GROUPS = (MIX_WIDTH - ATTN_WIDTH) // SSM_GROUP_CH
SSM_WIDTH = SSM_GROUPS * SSM_GROUP_CH
SSM_STATE = 64
DT_MIN = 1e-3
DT_MAX = 1e-1
MOBA_BLOCK = 256
MOBA_TOPK = 3
QUERY_CHUNK = 128
ROPE_THETA = 10000.0
D_FF = 4 * D_MODEL
EPS = 1e-6
IN_PROJ_WIDTH = 3 * ATTN_WIDTH + SSM_WIDTH

kernel_name = 'hymba_moba_s5_hybrid_layer'


def rmsnorm(x, g):
    xf = x.astype(jnp.float32)
    y = xf * lax.rsqrt(jnp.mean(jnp.square(xf), axis=-1, keepdims=True) + EPS)
    return (y * g.astype(jnp.float32)).astype(x.dtype)


def rope(x):
    L = x.shape[2]
    half = HEAD_DIM // 2
    inv_freq = ROPE_THETA ** (-jnp.arange(half, dtype=jnp.float32) / half)
    ang = jnp.arange(L, dtype=jnp.float32)[:, None] * inv_freq[None, :]
    cos, sin = jnp.cos(ang), jnp.sin(ang)
    xf = x.astype(jnp.float32)
    x1, x2 = xf[..., :half], xf[..., half:]
    out = jnp.concatenate([x1 * cos - x2 * sin, x1 * sin + x2 * cos], axis=-1)
    return out.astype(x.dtype)


def moba_attention(q, k, v):
    B, H, L, D = q.shape
    nb = -(-L // MOBA_BLOCK)
    pad = nb * MOBA_BLOCK - L
    kp = jnp.pad(k, ((0, 0), (0, 0), (0, pad), (0, 0)))
    vp = jnp.pad(v, ((0, 0), (0, 0), (0, pad), (0, 0)))
    k_blk = kp.reshape(B, H, nb, MOBA_BLOCK, D)
    v_blk = vp.reshape(B, H, nb, MOBA_BLOCK, D)
    counts = jnp.clip(L - jnp.arange(nb) * MOBA_BLOCK, 1, MOBA_BLOCK).astype(jnp.float32)
    k_mean = k_blk.astype(jnp.float32).sum(axis=3) / counts[:, None]
    gate = jnp.einsum('bhld,bhnd->bhln', q.astype(jnp.float32), k_mean)
    q_blk_id = jnp.arange(L) // MOBA_BLOCK
    fully_past = jnp.arange(nb)[None, :] < q_blk_id[:, None]
    gate = jnp.where(fully_past, gate, -jnp.inf)
    topk = min(MOBA_TOPK, nb)
    _, sel = lax.top_k(gate, topk)
    sel_valid = sel < q_blk_id[None, None, :, None]

    scale = HEAD_DIM ** -0.5
    n_chunks = L // QUERY_CHUNK
    b_idx = jnp.arange(B)[:, None, None, None]
    h_idx = jnp.arange(H)[None, :, None, None]

    def one_chunk(ci):
        start = ci * QUERY_CHUNK
        qc = lax.dynamic_slice_in_dim(q, start, QUERY_CHUNK, axis=2)
        sc = lax.dynamic_slice_in_dim(sel, start, QUERY_CHUNK, axis=2)
        vc = lax.dynamic_slice_in_dim(sel_valid, start, QUERY_CHUNK, axis=2)
        own = start // MOBA_BLOCK
        k_sel = k_blk[b_idx, h_idx, sc]
        v_sel = v_blk[b_idx, h_idx, sc]
        k_own = lax.dynamic_index_in_dim(k_blk, own, axis=2, keepdims=False)
        v_own = lax.dynamic_index_in_dim(v_blk, own, axis=2, keepdims=False)
        s_sel = jnp.einsum('bhqd,bhqtkd->bhqtk', qc, k_sel,
                           preferred_element_type=jnp.float32) * scale
        s_sel = jnp.where(vc[..., None], s_sel, -jnp.inf).reshape(B, H, QUERY_CHUNK, topk * MOBA_BLOCK)
        s_own = jnp.einsum('bhqd,bhkd->bhqk', qc, k_own,
                           preferred_element_type=jnp.float32) * scale
        q_pos = start + jnp.arange(QUERY_CHUNK)
        k_pos = own * MOBA_BLOCK + jnp.arange(MOBA_BLOCK)
        s_own = jnp.where(k_pos[None, :] <= q_pos[:, None], s_own, -jnp.inf)
        p = jax.nn.softmax(jnp.concatenate([s_sel, s_own], axis=-1), axis=-1)
        p_sel = p[..., :topk * MOBA_BLOCK].reshape(B, H, QUERY_CHUNK, topk, MOBA_BLOCK).astype(v.dtype)
        p_own = p[..., topk * MOBA_BLOCK:].astype(v.dtype)
        return (jnp.einsum('bhqtk,bhqtkd->bhqd', p_sel, v_sel)
                + jnp.einsum('bhqk,bhkd->bhqd', p_own, v_own))

    outs = lax.map(one_chunk, jnp.arange(n_chunks))
    return outs.transpose(1, 2, 0, 3, 4).reshape(B, H, L, D)


def s5_mixer(u, lam_re, lam_im, log_dt, b_re, b_im, c_re, c_im, d_skip, w_glu, b_glu):
    Bsz, L, _ = u.shape
    uf = u.astype(jnp.float32).reshape(Bsz, L, SSM_GROUPS, SSM_GROUP_CH)
    lam = lax.complex(lam_re.astype(jnp.float32), lam_im.astype(jnp.float32))
    dt = jnp.exp(log_dt.astype(jnp.float32))[:, None]
    lam_bar = jnp.exp(lam * dt)
    b_mat = lax.complex(b_re.astype(jnp.float32), b_im.astype(jnp.float32))
    b_bar = ((lam_bar - 1.0) / lam)[..., None] * b_mat
    bu = jnp.einsum('gpn,blgn->blgp', b_bar, uf.astype(jnp.complex64))
    a = jnp.broadcast_to(lam_bar, bu.shape)

    def combine(e1, e2):
        a1, x1 = e1
        a2, x2 = e2
        return a1 * a2, a2 * x1 + x2

    _, states = lax.associative_scan(combine, (a, bu), axis=1)
    c_mat = lax.complex(c_re.astype(jnp.float32), c_im.astype(jnp.float32))
    y = jnp.einsum('gnp,blgp->blgn', c_mat, states).real + d_skip.astype(jnp.float32) * uf
    y = jax.nn.gelu(y.reshape(Bsz, L, SSM_WIDTH))
    y = y * jax.nn.sigmoid(y @ w_glu.astype(jnp.float32) + b_glu.astype(jnp.float32))
    return y.astype(u.dtype)


def setup_inputs(seed: int = 0) -> dict:
    key = jax.random.key(seed)
    ks = jax.random.split(key, 24)
    f32 = jnp.float32

    def nrm(k, shape, s):
        return s * jax.random.normal(k, shape, f32)

    n_idx = jnp.arange(SSM_STATE, dtype=f32)
    x = jax.random.normal(ks[0], (BATCH, SEQ, D_MODEL), f32)
    c = jax.random.normal(ks[1], (BATCH, D_MODEL), f32)
    w_ada = nrm(ks[2], (DEPTH, D_MODEL, 6 * D_MODEL), 0.5 * D_MODEL ** -0.5)
    b_ada = nrm(ks[3], (DEPTH, 6 * D_MODEL), 0.01)
    g_mix = 1.0 + nrm(ks[4], (DEPTH, D_MODEL), 0.02)
    w_in = nrm(ks[5], (DEPTH, D_MODEL, IN_PROJ_WIDTH), D_MODEL ** -0.5)
    g_attn_out = 1.0 + nrm(ks[6], (DEPTH, ATTN_WIDTH), 0.02)
    lam_re = -0.5 + nrm(ks[7], (DEPTH, SSM_GROUPS, SSM_STATE), 0.01)
    lam_im = math.pi * n_idx + nrm(ks[8], (DEPTH, SSM_GROUPS, SSM_STATE), 0.01)
    log_dt = jax.random.uniform(ks[9], (DEPTH, SSM_GROUPS), f32, math.log(DT_MIN), math.log(DT_MAX))
    b_re = nrm(ks[10], (DEPTH, SSM_GROUPS, SSM_STATE, SSM_GROUP_CH), (2 * SSM_GROUP_CH) ** -0.5)
    b_im = nrm(ks[11], (DEPTH, SSM_GROUPS, SSM_STATE, SSM_GROUP_CH), (2 * SSM_GROUP_CH) ** -0.5)
    c_re = nrm(ks[12], (DEPTH, SSM_GROUPS, SSM_GROUP_CH, SSM_STATE), SSM_STATE ** -0.5)
    c_im = nrm(ks[13], (DEPTH, SSM_GROUPS, SSM_GROUP_CH, SSM_STATE), SSM_STATE ** -0.5)
    d_skip = nrm(ks[14], (DEPTH, SSM_GROUPS, SSM_GROUP_CH), 1.0)
    w_glu = nrm(ks[15], (DEPTH, SSM_WIDTH, SSM_WIDTH), SSM_WIDTH ** -0.5)
    b_glu = nrm(ks[16], (DEPTH, SSM_WIDTH), 0.01)
    g_ssm_out = 1.0 + nrm(ks[17], (DEPTH, SSM_WIDTH), 0.02)
    w_out = nrm(ks[18], (DEPTH, MIX_WIDTH, D_MODEL), MIX_WIDTH ** -0.5)
    g_mlp = 1.0 + nrm(ks[19], (DEPTH, D_MODEL), 0.02)
    w_fc1 = nrm(ks[20], (DEPTH, D_MODEL, D_FF), D_MODEL ** -0.5)
    w_fc2 = nrm(ks[21], (DEPTH, D_FF, D_MODEL), D_FF ** -0.5)
    g_final = 1.0 + nrm(ks[22], (D_MODEL,), 0.02)
    return {'x': x, 'c': c, 'w_ada': w_ada, 'b_ada': b_ada, 'g_mix': g_mix, 'w_in': w_in,
            'g_attn_out': g_attn_out, 'lam_re': lam_re, 'lam_im': lam_im, 'log_dt': log_dt,
            'b_re': b_re, 'b_im': b_im, 'c_re': c_re, 'c_im': c_im, 'd_skip': d_skip,
            'w_glu': w_glu, 'b_glu': b_glu, 'g_ssm_out': g_ssm_out, 'w_out': w_out,
            'g_mlp': g_mlp, 'w_fc1': w_fc1, 'w_fc2': w_fc2, 'g_final': g_final}


def reference(x, c, w_ada, b_ada, g_mix, w_in, g_attn_out, lam_re, lam_im, log_dt,
              b_re, b_im, c_re, c_im, d_skip, w_glu, b_glu, g_ssm_out, w_out,
              g_mlp, w_fc1, w_fc2, g_final):
    Bsz, L, _ = x.shape

    def to_heads(t):
        return t.reshape(Bsz, L, ATTN_HEADS, HEAD_DIM).transpose(0, 2, 1, 3)

    for l in range(DEPTH):
        mod = jax.nn.silu(c) @ w_ada[l] + b_ada[l]
        sh1, sc1, gt1, sh2, sc2, gt2 = [m[:, None, :] for m in jnp.split(mod, 6, axis=-1)]

        h = rmsnorm(x, g_mix[l]) * (1.0 + sc1) + sh1
        proj = h @ w_in[l]
        q, k, v, u = jnp.split(proj, [ATTN_WIDTH, 2 * ATTN_WIDTH, 3 * ATTN_WIDTH], axis=-1)
        q = rope(to_heads(q))
        k = rope(to_heads(k))
        attn = moba_attention(q, k, to_heads(v)).transpose(0, 2, 1, 3).reshape(Bsz, L, ATTN_WIDTH)
        ssm = s5_mixer(u, lam_re[l], lam_im[l], log_dt[l], b_re[l], b_im[l], c_re[l], c_im[l],
                       d_skip[l], w_glu[l], b_glu[l])
        mix = jnp.concatenate([rmsnorm(attn, g_attn_out[l]), rmsnorm(ssm, g_ssm_out[l])], axis=-1)
        x = x + gt1 * (mix @ w_out[l])

        h = rmsnorm(x, g_mlp[l]) * (1.0 + sc2) + sh2
        x = x + gt2 * (jnp.square(jax.nn.relu(h @ w_fc1[l])) @ w_fc2[l])

    return rmsnorm(x, g_final)
```

```python
import functools
import math

import jax
import jax.numpy as jnp
from jax import lax
from jax.experimental import pallas as pl
from jax.experimental.pallas import tpu as pltpu

F32 = jnp.float32
BF16 = jnp.bfloat16

HEAD_DIM = 64
ATTN_HEADS = 8
ATTN_WIDTH = ATTN_HEADS * HEAD_DIM
SSM_GROUPS = 32
SSM_GROUP_CH = 16
SSM_WIDTH = SSM_GROUPS * SSM_GROUP_CH
SSM_STATE = 64
SSM_NSTATE = SSM_GROUPS * SSM_STATE
MOBA_BLOCK = 256
MOBA_TOPK = 3
ROPE_THETA = 10000.0
EPS = 1e-6

LANES = 128
SUBLANES = 8
N_SEG = SUBLANES
HEADS_PER_TILE = LANES // HEAD_DIM
MASK_VALUE = -(2.0 ** 100)
VMEM_LIMIT_BYTES = 56 * 1024 * 1024


def _split_bf16(a):
    hi = a.astype(BF16)
    lo = (a - hi.astype(F32)).astype(BF16)
    return hi, lo


def _dot(a, b):
    return jnp.dot(a, b, preferred_element_type=F32)


def _dot_nt(a, b):
    return lax.dot_general(a, b, (((1,), (1,)), ((), ())), preferred_element_type=F32)


def _rmsnorm(x, g):
    return x * lax.rsqrt(jnp.mean(x * x, axis=-1, keepdims=True) + EPS) * g


def _params(*semantics):
    return pltpu.CompilerParams(dimension_semantics=semantics, vmem_limit_bytes=VMEM_LIMIT_BYTES)


def _adaln_kernel(c_ref, w_ref, b_ref, o_ref):
    c = c_ref[...]
    s_hi, s_lo = _split_bf16(c * jax.nn.sigmoid(c))
    w_hi, w_lo = _split_bf16(w_ref[...])
    o_ref[...] = _dot(s_hi, w_hi) + _dot(s_lo, w_hi) + _dot(s_hi, w_lo) + b_ref[...]


def _adaln(c_pad, w, b):
    rows, d = c_pad.shape
    n = w.shape[1]
    tn = n // 4
    return pl.pallas_call(
        _adaln_kernel,
        grid=(n // tn,),
        in_specs=[pl.BlockSpec((rows, d), lambda j: (0, 0)),
                  pl.BlockSpec((d, tn), lambda j: (0, j)),
                  pl.BlockSpec((1, tn), lambda j: (0, j))],
        out_specs=pl.BlockSpec((rows, tn), lambda j: (0, j)),
        out_shape=jax.ShapeDtypeStruct((rows, n), F32),
        compiler_params=_params("arbitrary"),
        name="adaln",
    )(c_pad, w, b)


def _inproj_kernel(x_ref, mod_ref, g_ref, wqk_hi_ref, wqk_lo_ref, wvu_ref, cos_ref, sin_ref,
                   q_ref, k_ref, v_ref, u_ref, km_ref):
    x = x_ref[0]
    tm, d = x.shape
    sh1 = mod_ref[0, :, 0:d]
    sc1 = mod_ref[0, :, d:2 * d]
    h = _rmsnorm(x, g_ref[...]) * (1.0 + sc1) + sh1
    h_hi, h_lo = _split_bf16(h)
    qk = (_dot(h_hi, wqk_hi_ref[...]) + _dot(h_lo, wqk_hi_ref[...])
          + _dot(h_hi, wqk_lo_ref[...]))
    vu = _dot(h_hi, wvu_ref[...])

    reps = ATTN_WIDTH // LANES
    cos = jnp.concatenate([cos_ref[...]] * reps, axis=1)
    sin = jnp.concatenate([sin_ref[...]] * reps, axis=1)
    lane = lax.broadcasted_iota(jnp.int32, (tm, ATTN_WIDTH), 1)
    first_half = (lane & (HEAD_DIM // 2)) == 0

    def rope(t):
        partner = jnp.where(first_half,
                            pltpu.roll(t, ATTN_WIDTH - HEAD_DIM // 2, 1),
                            pltpu.roll(t, HEAD_DIM // 2, 1))
        return t * cos + partner * sin

    q = rope(qk[:, :ATTN_WIDTH])
    k = rope(qk[:, ATTN_WIDTH:])
    q_ref[0] = q
    k_ref[0] = k
    v_ref[0] = vu[:, :ATTN_WIDTH].astype(BF16)
    u_ref[0] = vu[:, ATTN_WIDTH:].astype(BF16)
    km_ref[0, 0] = jnp.sum(k, axis=0, keepdims=True) * (1.0 / tm)


def _inproj(x, mod3, g_mix, wqk_hi, wqk_lo, wvu, cos_t, sin_t):
    bsz, seq, d = x.shape
    tm = MOBA_BLOCK
    nt = seq // tm
    nj = (seq // N_SEG) // tm
    const = lambda b, t: (0, 0)
    return pl.pallas_call(
        _inproj_kernel,
        grid=(bsz, nt),
        in_specs=[pl.BlockSpec((1, tm, d), lambda b, t: (b, t, 0)),
                  pl.BlockSpec((1, 1, mod3.shape[-1]), lambda b, t: (b, 0, 0)),
                  pl.BlockSpec((1, d), const),
                  pl.BlockSpec(wqk_hi.shape, const),
                  pl.BlockSpec(wqk_lo.shape, const),
                  pl.BlockSpec(wvu.shape, const),
                  pl.BlockSpec((tm, LANES), lambda b, t: (t, 0)),
                  pl.BlockSpec((tm, LANES), lambda b, t: (t, 0))],
        out_specs=[pl.BlockSpec((1, tm, ATTN_WIDTH), lambda b, t: (b, t, 0)),
                   pl.BlockSpec((1, tm, ATTN_WIDTH), lambda b, t: (b, t, 0)),
                   pl.BlockSpec((1, tm, ATTN_WIDTH), lambda b, t: (b, t, 0)),
                   pl.BlockSpec((1, tm, SSM_WIDTH), lambda b, t: (b, t % nj, t // nj)),
                   pl.BlockSpec((1, 1, 1, ATTN_WIDTH), lambda b, t: (b, t, 0, 0))],
        out_shape=[jax.ShapeDtypeStruct((bsz, seq, ATTN_WIDTH), F32),
                   jax.ShapeDtypeStruct((bsz, seq, ATTN_WIDTH), F32),
                   jax.ShapeDtypeStruct((bsz, seq, ATTN_WIDTH), BF16),
                   jax.ShapeDtypeStruct((bsz, seq // N_SEG, N_SEG * SSM_WIDTH), BF16),
                   jax.ShapeDtypeStruct((bsz, nt, 1, ATTN_WIDTH), F32)],
        compiler_params=_params("arbitrary", "arbitrary"),
        name="inproj",
    )(x, mod3, g_mix, wqk_hi, wqk_lo, wvu, cos_t, sin_t)


def _router_kernel(q_ref, k_ref, km_ref, qa_ref, ka_ref, *, nb):
    i = pl.program_id(2)
    q = q_ref[0]
    k = k_ref[0]
    km = km_ref[0]
    tq = q.shape[0]
    lane = lax.broadcasted_iota(jnp.int32, (tq, LANES), 1)
    blk = lax.broadcasted_iota(jnp.int32, (nb, tq), 0)
    past = blk < i
    km_hi, km_lo = _split_bf16(km)
    scale = HEAD_DIM ** -0.5
    for h in range(HEADS_PER_TILE):
        in_head = (lane // HEAD_DIM) == h
        qh = jnp.where(in_head, q, 0.0)
        qh_hi, qh_lo = _split_bf16(qh)
        gate = (_dot_nt(km_hi, qh_hi) + _dot_nt(km_lo, qh_hi) + _dot_nt(km_hi, qh_lo))
        gate = jnp.where(past, gate, -jnp.inf)
        rank = jnp.zeros((nb, tq), F32)
        for jj in range(nb):
            row = gate[jj:jj + 1, :]
            beats = (row > gate) | ((row == gate) & (blk > jj))
            rank = rank + jnp.where(beats, 1.0, 0.0)
        attend = ((rank < MOBA_TOPK) & past) | (blk == i)
        bias_t = jnp.where(attend, 0.0, MASK_VALUE)
        padded = jnp.concatenate([jnp.zeros((HEAD_DIM, tq), F32), bias_t,
                                  jnp.zeros((LANES - HEAD_DIM - nb, tq), F32)], axis=0)
        bias = padded.T
        kh = jnp.where(in_head, k, 0.0)
        if h:
            qh = pltpu.roll(qh, LANES - h * HEAD_DIM, 1)
            kh = pltpu.roll(kh, LANES - h * HEAD_DIM, 1)
        onehot = jnp.where(lane == HEAD_DIM + i, 1.0, 0.0)
        qa_ref[0, h] = jnp.where(lane < HEAD_DIM, qh * scale, bias).astype(BF16)
        ka_ref[0, h] = jnp.where(lane < HEAD_DIM, kh, onehot).astype(BF16)


def _router(q, k, kmean):
    bsz, seq, _ = q.shape
    tq = MOBA_BLOCK
    nb = seq // tq
    assert nb % SUBLANES == 0 and nb <= LANES - HEAD_DIM
    tiles = ATTN_HEADS // HEADS_PER_TILE
    return pl.pallas_call(
        functools.partial(_router_kernel, nb=nb),
        grid=(bsz, tiles, nb),
        in_specs=[pl.BlockSpec((1, tq, LANES), lambda b, p, i: (b, i, p)),
                  pl.BlockSpec((1, tq, LANES), lambda b, p, i: (b, i, p)),
                  pl.BlockSpec((1, nb, LANES), lambda b, p, i: (b, 0, p))],
        out_specs=[pl.BlockSpec((1, HEADS_PER_TILE, tq, LANES), lambda b, p, i: (b, p, i, 0)),
                   pl.BlockSpec((1, HEADS_PER_TILE, tq, LANES), lambda b, p, i: (b, p, i, 0))],
        out_shape=[jax.ShapeDtypeStruct((bsz, ATTN_HEADS, seq, LANES), BF16),
                   jax.ShapeDtypeStruct((bsz, ATTN_HEADS, seq, LANES), BF16)],
        compiler_params=_params("arbitrary", "arbitrary", "arbitrary"),
        name="router",
    )(q, k, kmean)


def _attn_kernel(qa_ref, ka_ref, v_ref, o_ref):
    i = pl.program_id(2)
    tq = qa_ref.shape[2]
    row = lax.broadcasted_iota(jnp.int32, (tq, tq), 0)
    col = lax.broadcasted_iota(jnp.int32, (tq, tq), 1)
    causal = col <= row
    own = pl.multiple_of(i * tq, tq)
    outs = []
    for h in range(HEADS_PER_TILE):
        q = qa_ref[0, h]
        s = _dot_nt(q, ka_ref[0, h, pl.ds(own, tq), :])
        s = jnp.where(causal, s, MASK_VALUE)
        m = jnp.max(s, axis=1, keepdims=True)
        p = jnp.exp(s - m)
        l = jnp.sum(p, axis=1, keepdims=True)
        acc = _dot(p.astype(BF16), v_ref[0, pl.ds(own, tq), :])

        def body(j, carry, q=q, h=h):
            m, l, acc = carry
            start = pl.multiple_of(j * tq, tq)
            s = _dot_nt(q, ka_ref[0, h, pl.ds(start, tq), :])
            m_new = jnp.maximum(m, jnp.max(s, axis=1, keepdims=True))
            alpha = jnp.exp(m - m_new)
            p = jnp.exp(s - m_new)
            l = alpha * l + jnp.sum(p, axis=1, keepdims=True)
            acc = alpha * acc + _dot(p.astype(BF16), v_ref[0, pl.ds(start, tq), :])
            return m_new, l, acc

        m, l, acc = lax.fori_loop(0, i, body, (m, l, acc))
        outs.append(acc / l)
    lane = lax.broadcasted_iota(jnp.int32, (tq, LANES), 1)
    out = outs[0]
    for h in range(1, HEADS_PER_TILE):
        out = jnp.where(lane // HEAD_DIM == h, outs[h], out)
    o_ref[0] = out


def _attention(q_aug, k_aug, v):
    bsz, _, seq, _ = q_aug.shape
    tq = MOBA_BLOCK
    tiles = ATTN_HEADS // HEADS_PER_TILE
    return pl.pallas_call(
        _attn_kernel,
        grid=(bsz, tiles, seq // tq),
        in_specs=[pl.BlockSpec((1, HEADS_PER_TILE, tq, LANES), lambda b, p, i: (b, p, i, 0)),
                  pl.BlockSpec((1, HEADS_PER_TILE, seq, LANES), lambda b, p, i: (b, p, 0, 0)),
                  pl.BlockSpec((1, seq, LANES), lambda b, p, i: (b, 0, p))],
        out_specs=pl.BlockSpec((1, tq, LANES), lambda b, p, i: (b, i, p)),
        out_shape=jax.ShapeDtypeStruct((bsz, seq, ATTN_WIDTH), F32),
        compiler_params=_params("arbitrary", "arbitrary", "arbitrary"),
        name="attn",
    )(q_aug, k_aug, v)


S5_ROWS = 512
S5_CHUNK = 512


def _s5_scan(bu_scr, st_scr, lam_ref, rows, keep_states):
    for c in range(SSM_NSTATE // S5_CHUNK):
        re = slice(c * S5_CHUNK, (c + 1) * S5_CHUNK)
        im = slice(SSM_NSTATE + c * S5_CHUNK, SSM_NSTATE + (c + 1) * S5_CHUNK)
        lr = jnp.broadcast_to(lam_ref[0:1, re], (N_SEG, S5_CHUNK))
        li = jnp.broadcast_to(lam_ref[1:2, re], (N_SEG, S5_CHUNK))

        def step(j, carry, re=re, im=im, lr=lr, li=li):
            xr, xi = carry
            r0 = pl.multiple_of(j * N_SEG, N_SEG)
            nxr = lr * xr - li * xi + bu_scr[pl.ds(r0, N_SEG), re]
            nxi = lr * xi + li * xr + bu_scr[pl.ds(r0, N_SEG), im]
            if keep_states:
                bu_scr[pl.ds(r0, N_SEG), re] = nxr
                bu_scr[pl.ds(r0, N_SEG), im] = nxi
            return nxr, nxi

        xr, xi = lax.fori_loop(0, rows // N_SEG, step, (st_scr[:, re], st_scr[:, im]), unroll=4)
        st_scr[:, re] = xr
        st_scr[:, im] = xi


def _s5_final_kernel(u_ref, bbig_ref, lam_ref, fin_ref, bu_scr, st_scr):
    jb = pl.program_id(1)

    @pl.when(jb == 0)
    def _():
        st_scr[...] = jnp.zeros_like(st_scr)

    bu_scr[...] = _dot(u_ref[0], bbig_ref[...])
    _s5_scan(bu_scr, st_scr, lam_ref, u_ref.shape[1], keep_states=False)

    @pl.when(jb == pl.num_programs(1) - 1)
    def _():
        fin_ref[0] = st_scr[...]


def _s5_main_kernel(u_ref, bbig_ref, cbig_ref, lam_ref, lamseg_ref, fin_ref, d_ref, wglu_ref, bglu_ref,
                    o_ref, bu_scr, st_scr):
    jb = pl.program_id(1)
    ns = SSM_NSTATE

    @pl.when(jb == 0)
    def _():
        pr = lamseg_ref[0:1, :]
        pi = lamseg_ref[1:2, :]
        sr = jnp.zeros((1, ns), F32)
        si = jnp.zeros((1, ns), F32)
        st_scr[0:1, :] = jnp.zeros((1, 2 * ns), F32)
        for s in range(1, N_SEG):
            fr = fin_ref[0, s - 1:s, 0:ns]
            fi = fin_ref[0, s - 1:s, ns:2 * ns]
            sr, si = pr * sr - pi * si + fr, pr * si + pi * sr + fi
            st_scr[s:s + 1, 0:ns] = sr
            st_scr[s:s + 1, ns:2 * ns] = si

    u = u_ref[0]
    bu_scr[...] = _dot(u, bbig_ref[...])
    _s5_scan(bu_scr, st_scr, lam_ref, u.shape[0], keep_states=True)
    y = _dot(bu_scr[...].astype(BF16), cbig_ref[...]) + d_ref[...] * u.astype(F32)
    y = y * (0.5 * (1.0 + jnp.tanh(math.sqrt(2.0 / math.pi) * (y + 0.044715 * (y * y * y)))))
    z = _dot(y.astype(BF16), wglu_ref[...]) + bglu_ref[...]
    o_ref[0] = y * jax.nn.sigmoid(z)


def _s5(u_rows, bbig, cbig, lamvec, lamseg, d_skip, w_glu, b_glu):
    bsz, seq, _ = u_rows.shape
    rows = S5_ROWS
    nblk = seq // rows
    const = lambda b, j: (0, 0)
    scratch = [pltpu.VMEM((rows, 2 * SSM_NSTATE), F32), pltpu.VMEM((N_SEG, 2 * SSM_NSTATE), F32)]
    fin = pl.pallas_call(
        _s5_final_kernel,
        grid=(bsz, nblk),
        in_specs=[pl.BlockSpec((1, rows, SSM_WIDTH), lambda b, j: (b, j, 0)),
                  pl.BlockSpec(bbig.shape, const),
                  pl.BlockSpec(lamvec.shape, const)],
        out_specs=pl.BlockSpec((1, N_SEG, 2 * SSM_NSTATE), lambda b, j: (b, 0, 0)),
        out_shape=jax.ShapeDtypeStruct((bsz, N_SEG, 2 * SSM_NSTATE), F32),
        scratch_shapes=scratch,
        compiler_params=_params("arbitrary", "arbitrary"),
        name="s5_fin",
    )(u_rows, bbig, lamvec)
    return pl.pallas_call(
        _s5_main_kernel,
        grid=(bsz, nblk),
        in_specs=[pl.BlockSpec((1, rows, SSM_WIDTH), lambda b, j: (b, j, 0)),
                  pl.BlockSpec(bbig.shape, const),
                  pl.BlockSpec(cbig.shape, const),
                  pl.BlockSpec(lamvec.shape, const),
                  pl.BlockSpec(lamseg.shape, const),
                  pl.BlockSpec((1, N_SEG, 2 * SSM_NSTATE), lambda b, j: (b, 0, 0)),
                  pl.BlockSpec(d_skip.shape, const),
                  pl.BlockSpec(w_glu.shape, const),
                  pl.BlockSpec(b_glu.shape, const)],
        out_specs=pl.BlockSpec((1, rows, SSM_WIDTH), lambda b, j: (b, j, 0)),
        out_shape=jax.ShapeDtypeStruct((bsz, seq, SSM_WIDTH), F32),
        scratch_shapes=scratch,
        compiler_params=_params("arbitrary", "arbitrary"),
        name="s5_main",
    )(u_rows, bbig, cbig, lamvec, lamseg, fin, d_skip, w_glu, b_glu)


FF_CHUNK = 1024


def _outmlp_kernel(x_ref, attn_ref, ssm_ref, mod_ref, ga_ref, gs_ref, gm_ref, gf_ref,
                   wout_ref, w1_ref, w2_ref, o_ref, *, final_norm):
    x = x_ref[0]
    tm, d = x.shape
    gt1 = mod_ref[0, :, 2 * d:3 * d]
    sh2 = mod_ref[0, :, 3 * d:4 * d]
    sc2 = mod_ref[0, :, 4 * d:5 * d]
    gt2 = mod_ref[0, :, 5 * d:6 * d]
    a = _rmsnorm(attn_ref[0], ga_ref[...]).astype(BF16)
    s = _rmsnorm(ssm_ref[0], gs_ref[...]).astype(BF16)
    mixed = _dot(a, wout_ref[0:ATTN_WIDTH, :]) + _dot(s, wout_ref[ATTN_WIDTH:, :])
    x1 = x + gt1 * mixed
    h = (_rmsnorm(x1, gm_ref[...]) * (1.0 + sc2) + sh2).astype(BF16)
    ff = w1_ref.shape[1]
    acc = jnp.zeros((tm, d), F32)
    for c in range(ff // FF_CHUNK):
        t = jnp.maximum(_dot(h, w1_ref[:, c * FF_CHUNK:(c + 1) * FF_CHUNK]), 0.0)
        acc = acc + _dot((t * t).astype(BF16), w2_ref[c * FF_CHUNK:(c + 1) * FF_CHUNK, :])
    x2 = x1 + gt2 * acc
    o_ref[0] = _rmsnorm(x2, gf_ref[...]) if final_norm else x2


def _outmlp(x, attn, ssm_seg, mod3, g_attn, g_ssm, g_mlp, g_final, w_out, w_fc1, w_fc2, final_norm):
    bsz, seq, d = x.shape
    tm = 256
    nj = (seq // N_SEG) // tm
    const = lambda b, t: (0, 0)
    return pl.pallas_call(
        functools.partial(_outmlp_kernel, final_norm=final_norm),
        grid=(bsz, seq // tm),
        in_specs=[pl.BlockSpec((1, tm, d), lambda b, t: (b, t, 0)),
                  pl.BlockSpec((1, tm, ATTN_WIDTH), lambda b, t: (b, t, 0)),
                  pl.BlockSpec((1, tm, SSM_WIDTH), lambda b, t: (b, t % nj, t // nj)),
                  pl.BlockSpec((1, 1, mod3.shape[-1]), lambda b, t: (b, 0, 0)),
                  pl.BlockSpec(g_attn.shape, const),
                  pl.BlockSpec(g_ssm.shape, const),
                  pl.BlockSpec(g_mlp.shape, const),
                  pl.BlockSpec(g_final.shape, const),
                  pl.BlockSpec(w_out.shape, const),
                  pl.BlockSpec(w_fc1.shape, const),
                  pl.BlockSpec(w_fc2.shape, const)],
        out_specs=pl.BlockSpec((1, tm, d), lambda b, t: (b, t, 0)),
        out_shape=jax.ShapeDtypeStruct((bsz, seq, d), F32),
        compiler_params=_params("arbitrary", "arbitrary"),
        name="outmlp",
    )(x, attn, ssm_seg, mod3, g_attn, g_ssm, g_mlp, g_final, w_out, w_fc1, w_fc2)


def _rope_tables(seq):
    half = HEAD_DIM // 2
    inv_freq = ROPE_THETA ** (-jnp.arange(half, dtype=F32) / half)
    ang = jnp.arange(seq, dtype=F32)[:, None] * inv_freq[None, :]
    cos, sin = jnp.cos(ang), jnp.sin(ang)
    reps = LANES // HEAD_DIM
    cos_t = jnp.tile(cos, (1, 2 * reps))
    sin_t = jnp.tile(jnp.concatenate([-sin, sin], axis=1), (1, reps))
    return cos_t, sin_t


def _s5_params(lam_re, lam_im, log_dt, b_re, b_im, c_re, c_im, seg_len):
    lam = lax.complex(lam_re.astype(F32), lam_im.astype(F32))
    dt = jnp.exp(log_dt.astype(F32))[:, None]
    lam_bar = jnp.exp(lam * dt)
    b_bar = ((lam_bar - 1.0) / lam)[..., None] * lax.complex(b_re.astype(F32), b_im.astype(F32))
    lam_seg = jnp.exp(lam * dt * seg_len)
    eye = jnp.eye(SSM_GROUPS, dtype=F32)

    def diag_in(m):
        return jnp.einsum('gpn,gh->gnhp', m, eye).reshape(SSM_WIDTH, SSM_NSTATE)

    def diag_out(m):
        return jnp.einsum('gnp,gh->gphn', m, eye).reshape(SSM_NSTATE, SSM_WIDTH)

    bbig = jnp.concatenate([diag_in(jnp.real(b_bar)), diag_in(jnp.imag(b_bar))], axis=1).astype(BF16)
    cbig = jnp.concatenate([diag_out(c_re.astype(F32)), diag_out(-c_im.astype(F32))], axis=0).astype(BF16)
    lamvec = jnp.stack([jnp.real(lam_bar).reshape(-1), jnp.imag(lam_bar).reshape(-1)])
    lamseg = jnp.stack([jnp.real(lam_seg).reshape(-1), jnp.imag(lam_seg).reshape(-1)])
    return bbig, cbig, lamvec, lamseg


def kernel(x, c, w_ada, b_ada, g_mix, w_in, g_attn_out, lam_re, lam_im, log_dt, b_re, b_im, c_re, c_im,
           d_skip, w_glu, b_glu, g_ssm_out, w_out, g_mlp, w_fc1, w_fc2, g_final):
    bsz, seq, d = x.shape
    depth = w_ada.shape[0]
    assert seq % (N_SEG * MOBA_BLOCK) == 0 and seq % S5_ROWS == 0 and bsz <= SUBLANES
    cos_t, sin_t = _rope_tables(seq)
    c_pad = jnp.zeros((SUBLANES, d), F32).at[:bsz].set(c.astype(F32))

    for l in range(depth):
        mod = _adaln(c_pad, w_ada[l], b_ada[l][None, :])
        mod3 = mod[:bsz, None, :]

        wqk_hi, wqk_lo = _split_bf16(w_in[l][:, :2 * ATTN_WIDTH])
        wvu = w_in[l][:, 2 * ATTN_WIDTH:].astype(BF16)
        q, k, v, u_seg, kmean = _inproj(x, mod3, g_mix[l][None, :], wqk_hi, wqk_lo, wvu, cos_t, sin_t)

        q_aug, k_aug = _router(q, k, kmean.reshape(bsz, seq // MOBA_BLOCK, ATTN_WIDTH))
        attn = _attention(q_aug, k_aug, v)

        bbig, cbig, lamvec, lamseg = _s5_params(lam_re[l], lam_im[l], log_dt[l], b_re[l], b_im[l],
                                                c_re[l], c_im[l], seq // N_SEG)
        u_rows = u_seg.reshape(bsz, seq, SSM_WIDTH)
        ssm_rows = _s5(u_rows, bbig, cbig, lamvec, lamseg, d_skip[l].reshape(1, SSM_WIDTH),
                       w_glu[l].astype(BF16), b_glu[l][None, :])
        ssm_seg = ssm_rows.reshape(bsz, seq // N_SEG, N_SEG * SSM_WIDTH)

        x = _outmlp(x, attn, ssm_seg, mod3, g_attn_out[l][None, :], g_ssm_out[l][None, :], g_mlp[l][None, :],
                    g_final[None, :], w_out[l].astype(BF16), w_fc1[l].astype(BF16), w_fc2[l].astype(BF16),
                    final_norm=(l == depth - 1))
    return x
```

```python
import functools
import math

import jax
import jax.numpy as jnp
from jax import lax
from jax.experimental import pallas as pl
from jax.experimental.pallas import tpu as pltpu

F32 = jnp.float32
BF16 = jnp.bfloat16

HEAD_DIM = 64
ATTN_HEADS = 8
ATTN_WIDTH = ATTN_HEADS * HEAD_DIM
SSM_GROUPS = 32
SSM_GROUP_CH = 16
SSM_WIDTH = SSM_GROUPS * SSM_GROUP_CH
SSM_STATE = 64
SSM_NSTATE = SSM_GROUPS * SSM_STATE
MOBA_BLOCK = 256
MOBA_TOPK = 3
ROPE_THETA = 10000.0
EPS = 1e-6

LANES = 128
SUBLANES = 8
N_SEG = SUBLANES
HEADS_PER_TILE = LANES // HEAD_DIM
MASK_VALUE = -(2.0 ** 100)
VMEM_LIMIT_BYTES = 56 * 1024 * 1024


def _split_bf16(a):
    hi = a.astype(BF16)
    lo = (a - hi.astype(F32)).astype(BF16)
    return hi, lo


def _dot(a, b):
    return jnp.dot(a, b, preferred_element_type=F32)


def _dot_nt(a, b):
    return lax.dot_general(a, b, (((1,), (1,)), ((), ())), preferred_element_type=F32)


def _rmsnorm(x, g):
    return x * lax.rsqrt(jnp.mean(x * x, axis=-1, keepdims=True) + EPS) * g


def _params(*semantics):
    return pltpu.CompilerParams(dimension_semantics=semantics, vmem_limit_bytes=VMEM_LIMIT_BYTES)


def _adaln_kernel(c_ref, w_ref, b_ref, o_ref):
    c = c_ref[...]
    s_hi, s_lo = _split_bf16(c * jax.nn.sigmoid(c))
    w_hi, w_lo = _split_bf16(w_ref[...])
    o_ref[...] = _dot(s_hi, w_hi) + _dot(s_lo, w_hi) + _dot(s_hi, w_lo) + b_ref[...]


def _adaln(c_pad, w, b):
    rows, d = c_pad.shape
    n = w.shape[1]
    tn = n // 4
    return pl.pallas_call(
        _adaln_kernel,
        grid=(n // tn,),
        in_specs=[pl.BlockSpec((rows, d), lambda j: (0, 0)),
                  pl.BlockSpec((d, tn), lambda j: (0, j)),
                  pl.BlockSpec((1, tn), lambda j: (0, j))],
        out_specs=pl.BlockSpec((rows, tn), lambda j: (0, j)),
        out_shape=jax.ShapeDtypeStruct((rows, n), F32),
        compiler_params=_params("arbitrary"),
        name="adaln",
    )(c_pad, w, b)


def _inproj_kernel(x_ref, mod_ref, g_ref, wqk_hi_ref, wqk_lo_ref, wvu_ref, cos_ref, sin_ref,
                   q_ref, k_ref, v_ref, u_ref, km_ref):
    x = x_ref[0]
    tm, d = x.shape
    sh1 = mod_ref[0, :, 0:d]
    sc1 = mod_ref[0, :, d:2 * d]
    h = _rmsnorm(x, g_ref[...]) * (1.0 + sc1) + sh1
    h_hi, h_lo = _split_bf16(h)
    qk = (_dot(h_hi, wqk_hi_ref[...]) + _dot(h_lo, wqk_hi_ref[...])
          + _dot(h_hi, wqk_lo_ref[...]))
    vu = _dot(h_hi, wvu_ref[...])

    reps = ATTN_WIDTH // LANES
    cos = jnp.concatenate([cos_ref[...]] * reps, axis=1)
    sin = jnp.concatenate([sin_ref[...]] * reps, axis=1)
    lane = lax.broadcasted_iota(jnp.int32, (tm, ATTN_WIDTH), 1)
    first_half = (lane & (HEAD_DIM // 2)) == 0

    def rope(t):
        partner = jnp.where(first_half,
                            pltpu.roll(t, ATTN_WIDTH - HEAD_DIM // 2, 1),
                            pltpu.roll(t, HEAD_DIM // 2, 1))
        return t * cos + partner * sin

    q = rope(qk[:, :ATTN_WIDTH])
    k = rope(qk[:, ATTN_WIDTH:])
    q_ref[0] = q
    k_ref[0] = k
    v_ref[0] = vu[:, :ATTN_WIDTH].astype(BF16)
    u_ref[0] = vu[:, ATTN_WIDTH:].astype(BF16)
    km_ref[0, 0] = jnp.sum(k, axis=0, keepdims=True) * (1.0 / tm)


def _inproj(x, mod3, g_mix, wqk_hi, wqk_lo, wvu, cos_t, sin_t):
    bsz, seq, d = x.shape
    tm = MOBA_BLOCK
    nt = seq // tm
    nj = (seq // N_SEG) // tm
    const = lambda b, t: (0, 0)
    return pl.pallas_call(
        _inproj_kernel,
        grid=(bsz, nt),
        in_specs=[pl.BlockSpec((1, tm, d), lambda b, t: (b, t, 0)),
                  pl.BlockSpec((1, 1, mod3.shape[-1]), lambda b, t: (b, 0, 0)),
                  pl.BlockSpec((1, d), const),
                  pl.BlockSpec(wqk_hi.shape, const),
                  pl.BlockSpec(wqk_lo.shape, const),
                  pl.BlockSpec(wvu.shape, const),
                  pl.BlockSpec((tm, LANES), lambda b, t: (t, 0)),
                  pl.BlockSpec((tm, LANES), lambda b, t: (t, 0))],
        out_specs=[pl.BlockSpec((1, tm, ATTN_WIDTH), lambda b, t: (b, t, 0)),
                   pl.BlockSpec((1, tm, ATTN_WIDTH), lambda b, t: (b, t, 0)),
                   pl.BlockSpec((1, tm, ATTN_WIDTH), lambda b, t: (b, t, 0)),
                   pl.BlockSpec((1, tm, SSM_WIDTH), lambda b, t: (b, t % nj, t // nj)),
                   pl.BlockSpec((1, 1, 1, ATTN_WIDTH), lambda b, t: (b, t, 0, 0))],
        out_shape=[jax.ShapeDtypeStruct((bsz, seq, ATTN_WIDTH), F32),
                   jax.ShapeDtypeStruct((bsz, seq, ATTN_WIDTH), F32),
                   jax.ShapeDtypeStruct((bsz, seq, ATTN_WIDTH), BF16),
                   jax.ShapeDtypeStruct((bsz, seq // N_SEG, N_SEG * SSM_WIDTH), BF16),
                   jax.ShapeDtypeStruct((bsz, nt, 1, ATTN_WIDTH), F32)],
        compiler_params=_params("arbitrary", "arbitrary"),
        name="inproj",
    )(x, mod3, g_mix, wqk_hi, wqk_lo, wvu, cos_t, sin_t)


def _router_kernel(q_ref, k_ref, km_ref, qa_ref, ka_ref, *, nb):
    i = pl.program_id(2)
    q = q_ref[0]
    k = k_ref[0]
    km = km_ref[0]
    tq = q.shape[0]
    lane = lax.broadcasted_iota(jnp.int32, (tq, LANES), 1)
    blk = lax.broadcasted_iota(jnp.int32, (nb, tq), 0)
    past = blk < i
    km_hi, km_lo = _split_bf16(km)
    scale = HEAD_DIM ** -0.5 * math.log2(math.e)
    for h in range(HEADS_PER_TILE):
        in_head = (lane // HEAD_DIM) == h
        qh = jnp.where(in_head, q, 0.0)
        qh_hi, qh_lo = _split_bf16(qh)
        gate = (_dot_nt(km_hi, qh_hi) + _dot_nt(km_lo, qh_hi) + _dot_nt(km_hi, qh_lo))
        gate = jnp.where(past, gate, -jnp.inf)
        rank = jnp.zeros((nb, tq), F32)
        for jj in range(nb):
            row = gate[jj:jj + 1, :]
            beats = (row > gate) | ((row == gate) & (blk > jj))
            rank = rank + jnp.where(beats, 1.0, 0.0)
        attend = ((rank < MOBA_TOPK) & past) | (blk == i)
        bias_t = jnp.where(attend, 0.0, MASK_VALUE)
        padded = jnp.concatenate([jnp.zeros((HEAD_DIM, tq), F32), bias_t,
                                  jnp.zeros((LANES - HEAD_DIM - nb, tq), F32)], axis=0)
        bias = padded.T
        kh = jnp.where(in_head, k, 0.0)
        if h:
            qh = pltpu.roll(qh, LANES - h * HEAD_DIM, 1)
            kh = pltpu.roll(kh, LANES - h * HEAD_DIM, 1)
        onehot = jnp.where(lane == HEAD_DIM + i, 1.0, 0.0)
        qa_ref[0, h] = jnp.where(lane < HEAD_DIM, qh * scale, bias).astype(BF16)
        ka_ref[0, h] = jnp.where(lane < HEAD_DIM, kh, onehot).astype(BF16)


def _router(q, k, kmean):
    bsz, seq, _ = q.shape
    tq = MOBA_BLOCK
    nb = seq // tq
    assert nb % SUBLANES == 0 and nb <= LANES - HEAD_DIM
    tiles = ATTN_HEADS // HEADS_PER_TILE
    return pl.pallas_call(
        functools.partial(_router_kernel, nb=nb),
        grid=(bsz, tiles, nb),
        in_specs=[pl.BlockSpec((1, tq, LANES), lambda b, p, i: (b, i, p)),
                  pl.BlockSpec((1, tq, LANES), lambda b, p, i: (b, i, p)),
                  pl.BlockSpec((1, nb, LANES), lambda b, p, i: (b, 0, p))],
        out_specs=[pl.BlockSpec((1, HEADS_PER_TILE, tq, LANES), lambda b, p, i: (b, p, i, 0)),
                   pl.BlockSpec((1, HEADS_PER_TILE, tq, LANES), lambda b, p, i: (b, p, i, 0))],
        out_shape=[jax.ShapeDtypeStruct((bsz, ATTN_HEADS, seq, LANES), BF16),
                   jax.ShapeDtypeStruct((bsz, ATTN_HEADS, seq, LANES), BF16)],
        compiler_params=_params("arbitrary", "arbitrary", "arbitrary"),
        name="router",
    )(q, k, kmean)


ATTN_HEADS_PER_STEP = 8


def _attn_kernel(qa_ref, ka_ref, v_ref, o_ref, m_scr, l_scr, acc_scr):
    i = pl.program_id(1)
    nh = qa_ref.shape[1]
    tq = qa_ref.shape[2]
    nparts = tq // LANES
    row = lax.broadcasted_iota(jnp.int32, (tq, tq), 0)
    col = lax.broadcasted_iota(jnp.int32, (tq, tq), 1)
    causal = col <= row

    def update(h, start, mask, first):
        vt = h // HEADS_PER_TILE
        s = _dot_nt(qa_ref[0, h], ka_ref[0, h, pl.ds(start, tq), :])
        if mask is not None:
            s = jnp.where(mask, s, MASK_VALUE)
        parts = [s[:, c * LANES:(c + 1) * LANES] for c in range(nparts)]
        rowmax = jnp.max(functools.reduce(jnp.maximum, parts), axis=1, keepdims=True)
        if first:
            m_new = jnp.broadcast_to(rowmax, (tq, LANES))
        else:
            m_old = m_scr[h]
            m_new = jnp.maximum(m_old, rowmax)
            alpha = jnp.exp2(m_old - m_new)
        ps = [jnp.exp2(p - m_new) for p in parts]
        rowsum = jnp.sum(functools.reduce(jnp.add, ps), axis=1, keepdims=True)
        pv = _dot(jnp.concatenate(ps, axis=1).astype(BF16),
                  v_ref[0, pl.ds(start, tq), vt * LANES:(vt + 1) * LANES])
        if first:
            l_scr[h] = jnp.broadcast_to(rowsum, (tq, LANES))
            acc_scr[h] = pv
        else:
            l_scr[h] = alpha * l_scr[h] + rowsum
            acc_scr[h] = alpha * acc_scr[h] + pv
        m_scr[h] = m_new

    own = pl.multiple_of(i * tq, tq)
    for h in range(nh):
        update(h, own, causal, first=True)

    def body(j, carry):
        start = pl.multiple_of(j * tq, tq)
        for h in range(nh):
            update(h, start, None, first=False)
        return carry

    lax.fori_loop(0, i, body, 0)

    lane = lax.broadcasted_iota(jnp.int32, (tq, LANES), 1)
    for vt in range(nh // HEADS_PER_TILE):
        out = None
        for hh in range(HEADS_PER_TILE):
            h = vt * HEADS_PER_TILE + hh
            o = acc_scr[h] / l_scr[h]
            out = o if out is None else jnp.where(lane // HEAD_DIM == hh, o, out)
        o_ref[0, :, vt * LANES:(vt + 1) * LANES] = out


def _attention(q_aug, k_aug, v):
    bsz, _, seq, _ = q_aug.shape
    tq = MOBA_BLOCK
    nh = ATTN_HEADS_PER_STEP
    groups = ATTN_HEADS // nh
    vw = (nh // HEADS_PER_TILE) * LANES
    resident = pl.Buffered(1)
    return pl.pallas_call(
        _attn_kernel,
        grid=(bsz * groups, seq // tq),
        in_specs=[pl.BlockSpec((1, nh, tq, LANES), lambda g, i: (g // groups, g % groups, i, 0)),
                  pl.BlockSpec((1, nh, seq, LANES), lambda g, i: (g // groups, g % groups, 0, 0),
                               pipeline_mode=resident),
                  pl.BlockSpec((1, seq, vw), lambda g, i: (g // groups, 0, g % groups),
                               pipeline_mode=resident)],
        out_specs=pl.BlockSpec((1, tq, vw), lambda g, i: (g // groups, i, g % groups)),
        out_shape=jax.ShapeDtypeStruct((bsz, seq, ATTN_WIDTH), F32),
        scratch_shapes=[pltpu.VMEM((nh, tq, LANES), F32)] * 3,
        compiler_params=_params("arbitrary", "arbitrary"),
        name="attn",
    )(q_aug, k_aug, v)


S5_ROWS = 512
S5_CHUNK = 512


def _s5_scan(bu_scr, st_scr, lam_ref, rows, keep_states):
    for c in range(SSM_NSTATE // S5_CHUNK):
        re = slice(c * S5_CHUNK, (c + 1) * S5_CHUNK)
        im = slice(SSM_NSTATE + c * S5_CHUNK, SSM_NSTATE + (c + 1) * S5_CHUNK)
        lr = jnp.broadcast_to(lam_ref[0:1, re], (N_SEG, S5_CHUNK))
        li = jnp.broadcast_to(lam_ref[1:2, re], (N_SEG, S5_CHUNK))

        def step(j, carry, re=re, im=im, lr=lr, li=li):
            xr, xi = carry
            r0 = pl.multiple_of(j * N_SEG, N_SEG)
            nxr = lr * xr - li * xi + bu_scr[pl.ds(r0, N_SEG), re]
            nxi = lr * xi + li * xr + bu_scr[pl.ds(r0, N_SEG), im]
            if keep_states:
                bu_scr[pl.ds(r0, N_SEG), re] = nxr
                bu_scr[pl.ds(r0, N_SEG), im] = nxi
            return nxr, nxi

        xr, xi = lax.fori_loop(0, rows // N_SEG, step, (st_scr[:, re], st_scr[:, im]), unroll=4)
        st_scr[:, re] = xr
        st_scr[:, im] = xi


def _s5_final_kernel(u_ref, bbig_ref, lam_ref, fin_ref, bu_scr, st_scr):
    jb = pl.program_id(1)

    @pl.when(jb == 0)
    def _():
        st_scr[...] = jnp.zeros_like(st_scr)

    bu_scr[...] = _dot(u_ref[0], bbig_ref[...])
    _s5_scan(bu_scr, st_scr, lam_ref, u_ref.shape[1], keep_states=False)

    @pl.when(jb == pl.num_programs(1) - 1)
    def _():
        fin_ref[0] = st_scr[...]


def _s5_main_kernel(u_ref, bbig_ref, cbig_ref, lam_ref, lamseg_ref, fin_ref, d_ref, wglu_ref, bglu_ref,
                    o_ref, bu_scr, st_scr):
    jb = pl.program_id(1)
    ns = SSM_NSTATE

    @pl.when(jb == 0)
    def _():
        pr = lamseg_ref[0:1, :]
        pi = lamseg_ref[1:2, :]
        sr = jnp.zeros((1, ns), F32)
        si = jnp.zeros((1, ns), F32)
        st_scr[0:1, :] = jnp.zeros((1, 2 * ns), F32)
        for s in range(1, N_SEG):
            fr = fin_ref[0, s - 1:s, 0:ns]
            fi = fin_ref[0, s - 1:s, ns:2 * ns]
            sr, si = pr * sr - pi * si + fr, pr * si + pi * sr + fi
            st_scr[s:s + 1, 0:ns] = sr
            st_scr[s:s + 1, ns:2 * ns] = si

    u = u_ref[0]
    bu_scr[...] = _dot(u, bbig_ref[...])
    _s5_scan(bu_scr, st_scr, lam_ref, u.shape[0], keep_states=True)
    y = _dot(bu_scr[...].astype(BF16), cbig_ref[...]) + d_ref[...] * u.astype(F32)
    y = y * (0.5 * (1.0 + jnp.tanh(math.sqrt(2.0 / math.pi) * (y + 0.044715 * (y * y * y)))))
    z = _dot(y.astype(BF16), wglu_ref[...]) + bglu_ref[...]
    o_ref[0] = y * jax.nn.sigmoid(z)


def _s5(u_rows, bbig, cbig, lamvec, lamseg, d_skip, w_glu, b_glu):
    bsz, seq, _ = u_rows.shape
    rows = S5_ROWS
    nblk = seq // rows
    const = lambda b, j: (0, 0)
    scratch = [pltpu.VMEM((rows, 2 * SSM_NSTATE), F32), pltpu.VMEM((N_SEG, 2 * SSM_NSTATE), F32)]
    fin = pl.pallas_call(
        _s5_final_kernel,
        grid=(bsz, nblk),
        in_specs=[pl.BlockSpec((1, rows, SSM_WIDTH), lambda b, j: (b, j, 0)),
                  pl.BlockSpec(bbig.shape, const),
                  pl.BlockSpec(lamvec.shape, const)],
        out_specs=pl.BlockSpec((1, N_SEG, 2 * SSM_NSTATE), lambda b, j: (b, 0, 0)),
        out_shape=jax.ShapeDtypeStruct((bsz, N_SEG, 2 * SSM_NSTATE), F32),
        scratch_shapes=scratch,
        compiler_params=_params("arbitrary", "arbitrary"),
        name="s5_fin",
    )(u_rows, bbig, lamvec)
    return pl.pallas_call(
        _s5_main_kernel,
        grid=(bsz, nblk),
        in_specs=[pl.BlockSpec((1, rows, SSM_WIDTH), lambda b, j: (b, j, 0)),
                  pl.BlockSpec(bbig.shape, const),
                  pl.BlockSpec(cbig.shape, const),
                  pl.BlockSpec(lamvec.shape, const),
                  pl.BlockSpec(lamseg.shape, const),
                  pl.BlockSpec((1, N_SEG, 2 * SSM_NSTATE), lambda b, j: (b, 0, 0)),
                  pl.BlockSpec(d_skip.shape, const),
                  pl.BlockSpec(w_glu.shape, const),
                  pl.BlockSpec(b_glu.shape, const)],
        out_specs=pl.BlockSpec((1, rows, SSM_WIDTH), lambda b, j: (b, j, 0)),
        out_shape=jax.ShapeDtypeStruct((bsz, seq, SSM_WIDTH), F32),
        scratch_shapes=scratch,
        compiler_params=_params("arbitrary", "arbitrary"),
        name="s5_main",
    )(u_rows, bbig, cbig, lamvec, lamseg, fin, d_skip, w_glu, b_glu)


FF_CHUNK = 1024


def _outmlp_kernel(x_ref, attn_ref, ssm_ref, mod_ref, ga_ref, gs_ref, gm_ref, gf_ref,
                   wout_ref, w1_ref, w2_ref, o_ref, *, final_norm):
    x = x_ref[0]
    tm, d = x.shape
    gt1 = mod_ref[0, :, 2 * d:3 * d]
    sh2 = mod_ref[0, :, 3 * d:4 * d]
    sc2 = mod_ref[0, :, 4 * d:5 * d]
    gt2 = mod_ref[0, :, 5 * d:6 * d]
    a = _rmsnorm(attn_ref[0], ga_ref[...]).astype(BF16)
    s = _rmsnorm(ssm_ref[0], gs_ref[...]).astype(BF16)
    mixed = _dot(a, wout_ref[0:ATTN_WIDTH, :]) + _dot(s, wout_ref[ATTN_WIDTH:, :])
    x1 = x + gt1 * mixed
    h = (_rmsnorm(x1, gm_ref[...]) * (1.0 + sc2) + sh2).astype(BF16)
    ff = w1_ref.shape[1]
    acc = jnp.zeros((tm, d), F32)
    for c in range(ff // FF_CHUNK):
        t = jnp.maximum(_dot(h, w1_ref[:, c * FF_CHUNK:(c + 1) * FF_CHUNK]), 0.0)
        acc = acc + _dot((t * t).astype(BF16), w2_ref[c * FF_CHUNK:(c + 1) * FF_CHUNK, :])
    x2 = x1 + gt2 * acc
    o_ref[0] = _rmsnorm(x2, gf_ref[...]) if final_norm else x2


def _outmlp(x, attn, ssm_seg, mod3, g_attn, g_ssm, g_mlp, g_final, w_out, w_fc1, w_fc2, final_norm):
    bsz, seq, d = x.shape
    tm = 256
    nj = (seq // N_SEG) // tm
    const = lambda b, t: (0, 0)
    return pl.pallas_call(
        functools.partial(_outmlp_kernel, final_norm=final_norm),
        grid=(bsz, seq // tm),
        in_specs=[pl.BlockSpec((1, tm, d), lambda b, t: (b, t, 0)),
                  pl.BlockSpec((1, tm, ATTN_WIDTH), lambda b, t: (b, t, 0)),
                  pl.BlockSpec((1, tm, SSM_WIDTH), lambda b, t: (b, t % nj, t // nj)),
                  pl.BlockSpec((1, 1, mod3.shape[-1]), lambda b, t: (b, 0, 0)),
                  pl.BlockSpec(g_attn.shape, const),
                  pl.BlockSpec(g_ssm.shape, const),
                  pl.BlockSpec(g_mlp.shape, const),
                  pl.BlockSpec(g_final.shape, const),
                  pl.BlockSpec(w_out.shape, const),
                  pl.BlockSpec(w_fc1.shape, const),
                  pl.BlockSpec(w_fc2.shape, const)],
        out_specs=pl.BlockSpec((1, tm, d), lambda b, t: (b, t, 0)),
        out_shape=jax.ShapeDtypeStruct((bsz, seq, d), F32),
        compiler_params=_params("arbitrary", "arbitrary"),
        name="outmlp",
    )(x, attn, ssm_seg, mod3, g_attn, g_ssm, g_mlp, g_final, w_out, w_fc1, w_fc2)


def _rope_tables(seq):
    half = HEAD_DIM // 2
    inv_freq = ROPE_THETA ** (-jnp.arange(half, dtype=F32) / half)
    ang = jnp.arange(seq, dtype=F32)[:, None] * inv_freq[None, :]
    cos, sin = jnp.cos(ang), jnp.sin(ang)
    reps = LANES // HEAD_DIM
    cos_t = jnp.tile(cos, (1, 2 * reps))
    sin_t = jnp.tile(jnp.concatenate([-sin, sin], axis=1), (1, reps))
    return cos_t, sin_t


def _s5_params(lam_re, lam_im, log_dt, b_re, b_im, c_re, c_im, seg_len):
    lr, li = lam_re.astype(F32), lam_im.astype(F32)
    dt = jnp.exp(log_dt.astype(F32))[:, None]

    def cexp(scale):
        mag = jnp.exp(lr * scale)
        return mag * jnp.cos(li * scale), mag * jnp.sin(li * scale)

    ar, ai = cexp(dt)
    den = lr * lr + li * li
    cr = ((ar - 1.0) * lr + ai * li) / den
    ci = (ai * lr - (ar - 1.0) * li) / den
    bbar_re = cr[..., None] * b_re.astype(F32) - ci[..., None] * b_im.astype(F32)
    bbar_im = cr[..., None] * b_im.astype(F32) + ci[..., None] * b_re.astype(F32)
    sr, si = cexp(dt * seg_len)
    eye = jnp.eye(SSM_GROUPS, dtype=F32)

    def diag_in(m):
        return jnp.einsum('gpn,gh->gnhp', m, eye).reshape(SSM_WIDTH, SSM_NSTATE)

    def diag_out(m):
        return jnp.einsum('gnp,gh->gphn', m, eye).reshape(SSM_NSTATE, SSM_WIDTH)

    bbig = jnp.concatenate([diag_in(bbar_re), diag_in(bbar_im)], axis=1).astype(BF16)
    cbig = jnp.concatenate([diag_out(c_re.astype(F32)), diag_out(-c_im.astype(F32))], axis=0).astype(BF16)
    lamvec = jnp.stack([ar.reshape(-1), ai.reshape(-1)])
    lamseg = jnp.stack([sr.reshape(-1), si.reshape(-1)])
    return bbig, cbig, lamvec, lamseg


def kernel(x, c, w_ada, b_ada, g_mix, w_in, g_attn_out, lam_re, lam_im, log_dt, b_re, b_im, c_re, c_im,
           d_skip, w_glu, b_glu, g_ssm_out, w_out, g_mlp, w_fc1, w_fc2, g_final):
    bsz, seq, d = x.shape
    depth = w_ada.shape[0]
    assert seq % (N_SEG * MOBA_BLOCK) == 0 and seq % S5_ROWS == 0 and bsz <= SUBLANES
    cos_t, sin_t = _rope_tables(seq)
    c_pad = jnp.zeros((SUBLANES, d), F32).at[:bsz].set(c.astype(F32))

    for l in range(depth):
        mod = _adaln(c_pad, w_ada[l], b_ada[l][None, :])
        mod3 = mod[:bsz, None, :]

        wqk_hi, wqk_lo = _split_bf16(w_in[l][:, :2 * ATTN_WIDTH])
        wvu = w_in[l][:, 2 * ATTN_WIDTH:].astype(BF16)
        q, k, v, u_seg, kmean = _inproj(x, mod3, g_mix[l][None, :], wqk_hi, wqk_lo, wvu, cos_t, sin_t)

        q_aug, k_aug = _router(q, k, kmean.reshape(bsz, seq // MOBA_BLOCK, ATTN_WIDTH))
        attn = _attention(q_aug, k_aug, v)

        bbig, cbig, lamvec, lamseg = _s5_params(lam_re[l], lam_im[l], log_dt[l], b_re[l], b_im[l],
                                                c_re[l], c_im[l], seq // N_SEG)
        u_rows = u_seg.reshape(bsz, seq, SSM_WIDTH)
        ssm_rows = _s5(u_rows, bbig, cbig, lamvec, lamseg, d_skip[l].reshape(1, SSM_WIDTH),
                       w_glu[l].astype(BF16), b_glu[l][None, :])
        ssm_seg = ssm_rows.reshape(bsz, seq // N_SEG, N_SEG * SSM_WIDTH)

        x = _outmlp(x, attn, ssm_seg, mod3, g_attn_out[l][None, :], g_ssm_out[l][None, :], g_mlp[l][None, :],
                    g_final[None, :], w_out[l].astype(BF16), w_fc1[l].astype(BF16), w_fc2[l].astype(BF16),
                    final_norm=(l == depth - 1))
    return x
```

```python
import functools
import math

import jax
import jax.numpy as jnp
from jax import lax
from jax.experimental import pallas as pl
from jax.experimental.pallas import tpu as pltpu

F32 = jnp.float32
BF16 = jnp.bfloat16

HEAD_DIM = 64
ATTN_HEADS = 8
ATTN_WIDTH = ATTN_HEADS * HEAD_DIM
SSM_GROUPS = 32
SSM_GROUP_CH = 16
SSM_WIDTH = SSM_GROUPS * SSM_GROUP_CH
SSM_STATE = 64
SSM_NSTATE = SSM_GROUPS * SSM_STATE
MOBA_BLOCK = 256
MOBA_TOPK = 3
ROPE_THETA = 10000.0
EPS = 1e-6

LANES = 128
SUBLANES = 8
N_SEG = SUBLANES
HEADS_PER_TILE = LANES // HEAD_DIM
MASK_VALUE = -(2.0 ** 100)
VMEM_LIMIT_BYTES = 56 * 1024 * 1024


def _split_bf16(a):
    hi = a.astype(BF16)
    lo = (a - hi.astype(F32)).astype(BF16)
    return hi, lo


def _dot(a, b):
    return jnp.dot(a, b, preferred_element_type=F32)


def _dot_nt(a, b):
    return lax.dot_general(a, b, (((1,), (1,)), ((), ())), preferred_element_type=F32)


def _rmsnorm(x, g):
    return x * lax.rsqrt(jnp.mean(x * x, axis=-1, keepdims=True) + EPS) * g


def _params(*semantics):
    return pltpu.CompilerParams(dimension_semantics=semantics, vmem_limit_bytes=VMEM_LIMIT_BYTES)


def _adaln_kernel(c_ref, w_ref, b_ref, o_ref):
    c = c_ref[...]
    s_hi, s_lo = _split_bf16(c * jax.nn.sigmoid(c))
    w_hi, w_lo = _split_bf16(w_ref[...])
    o_ref[...] = _dot(s_hi, w_hi) + _dot(s_lo, w_hi) + _dot(s_hi, w_lo) + b_ref[...]


def _adaln(c_pad, w, b):
    rows, d = c_pad.shape
    n = w.shape[1]
    tn = n // 4
    return pl.pallas_call(
        _adaln_kernel,
        grid=(n // tn,),
        in_specs=[pl.BlockSpec((rows, d), lambda j: (0, 0)),
                  pl.BlockSpec((d, tn), lambda j: (0, j)),
                  pl.BlockSpec((1, tn), lambda j: (0, j))],
        out_specs=pl.BlockSpec((rows, tn), lambda j: (0, j)),
        out_shape=jax.ShapeDtypeStruct((rows, n), F32),
        compiler_params=_params("arbitrary"),
        name="adaln",
    )(c_pad, w, b)


def _inproj_kernel(x_ref, mod_ref, g_ref, wqk_hi_ref, wqk_lo_ref, wvu_ref, cos_ref, sin_ref,
                   q_ref, k_ref, v_ref, u_ref, km_ref):
    x = x_ref[0]
    tm, d = x.shape
    sh1 = mod_ref[0, :, 0:d]
    sc1 = mod_ref[0, :, d:2 * d]
    h = _rmsnorm(x, g_ref[...]) * (1.0 + sc1) + sh1
    h_hi, h_lo = _split_bf16(h)
    qk = (_dot(h_hi, wqk_hi_ref[...]) + _dot(h_lo, wqk_hi_ref[...])
          + _dot(h_hi, wqk_lo_ref[...]))
    vu = _dot(h_hi, wvu_ref[...])

    reps = ATTN_WIDTH // LANES
    cos = jnp.concatenate([cos_ref[...]] * reps, axis=1)
    sin = jnp.concatenate([sin_ref[...]] * reps, axis=1)
    lane = lax.broadcasted_iota(jnp.int32, (tm, ATTN_WIDTH), 1)
    first_half = (lane & (HEAD_DIM // 2)) == 0

    def rope(t):
        partner = jnp.where(first_half,
                            pltpu.roll(t, ATTN_WIDTH - HEAD_DIM // 2, 1),
                            pltpu.roll(t, HEAD_DIM // 2, 1))
        return t * cos + partner * sin

    q = rope(qk[:, :ATTN_WIDTH])
    k = rope(qk[:, ATTN_WIDTH:])
    q_ref[0] = q
    k_ref[0] = k
    v_ref[0] = vu[:, :ATTN_WIDTH].astype(BF16)
    u_ref[0] = vu[:, ATTN_WIDTH:].astype(BF16)
    km_ref[0, 0] = jnp.sum(k, axis=0, keepdims=True) * (1.0 / tm)


def _inproj(x, mod3, g_mix, wqk_hi, wqk_lo, wvu, cos_t, sin_t):
    bsz, seq, d = x.shape
    tm = MOBA_BLOCK
    nt = seq // tm
    nj = (seq // N_SEG) // tm
    const = lambda b, t: (0, 0)
    return pl.pallas_call(
        _inproj_kernel,
        grid=(bsz, nt),
        in_specs=[pl.BlockSpec((1, tm, d), lambda b, t: (b, t, 0)),
                  pl.BlockSpec((1, 1, mod3.shape[-1]), lambda b, t: (b, 0, 0)),
                  pl.BlockSpec((1, d), const),
                  pl.BlockSpec(wqk_hi.shape, const),
                  pl.BlockSpec(wqk_lo.shape, const),
                  pl.BlockSpec(wvu.shape, const),
                  pl.BlockSpec((tm, LANES), lambda b, t: (t, 0)),
                  pl.BlockSpec((tm, LANES), lambda b, t: (t, 0))],
        out_specs=[pl.BlockSpec((1, tm, ATTN_WIDTH), lambda b, t: (b, t, 0)),
                   pl.BlockSpec((1, tm, ATTN_WIDTH), lambda b, t: (b, t, 0)),
                   pl.BlockSpec((1, tm, ATTN_WIDTH), lambda b, t: (b, t, 0)),
                   pl.BlockSpec((1, tm, SSM_WIDTH), lambda b, t: (b, t % nj, t // nj)),
                   pl.BlockSpec((1, 1, 1, ATTN_WIDTH), lambda b, t: (b, t, 0, 0))],
        out_shape=[jax.ShapeDtypeStruct((bsz, seq, ATTN_WIDTH), F32),
                   jax.ShapeDtypeStruct((bsz, seq, ATTN_WIDTH), F32),
                   jax.ShapeDtypeStruct((bsz, seq, ATTN_WIDTH), BF16),
                   jax.ShapeDtypeStruct((bsz, seq // N_SEG, N_SEG * SSM_WIDTH), BF16),
                   jax.ShapeDtypeStruct((bsz, nt, 1, ATTN_WIDTH), F32)],
        compiler_params=_params("arbitrary", "arbitrary"),
        name="inproj",
    )(x, mod3, g_mix, wqk_hi, wqk_lo, wvu, cos_t, sin_t)


VT_ROWS = HEAD_DIM + 16


def _router_kernel(q_ref, k_ref, v_ref, km_ref, qt_ref, ka_ref, vt_ref, *, nb):
    i = pl.program_id(2)
    q = q_ref[0]
    k = k_ref[0]
    km = km_ref[0]
    tq = q.shape[0]
    lane = lax.broadcasted_iota(jnp.int32, (tq, LANES), 1)
    km_lane = lax.broadcasted_iota(jnp.int32, (nb, LANES), 1)
    blk = lax.broadcasted_iota(jnp.int32, (nb, tq), 0)
    past = blk < i
    q_t = q.T
    v_t = v_ref[0].astype(F32).T
    qt_hi, qt_lo = _split_bf16(q_t)
    ones_rows = jnp.where(lax.broadcasted_iota(jnp.int32, (VT_ROWS - HEAD_DIM, tq), 0) == 0, 1.0, 0.0)
    scale = HEAD_DIM ** -0.5 * math.log2(math.e)
    for h in range(HEADS_PER_TILE):
        kmh_hi, kmh_lo = _split_bf16(jnp.where(km_lane // HEAD_DIM == h, km, 0.0))
        gate = _dot(kmh_hi, qt_hi) + _dot(kmh_lo, qt_hi) + _dot(kmh_hi, qt_lo)
        gate = jnp.where(past, gate, -jnp.inf)
        rank = jnp.zeros((nb, tq), F32)
        for jj in range(nb):
            row = gate[jj:jj + 1, :]
            beats = (row > gate) | ((row == gate) & (blk > jj))
            rank = rank + jnp.where(beats, 1.0, 0.0)
        attend = ((rank < MOBA_TOPK) & past) | (blk == i)
        bias_t = jnp.where(attend, 0.0, MASK_VALUE)
        head_rows = slice(h * HEAD_DIM, (h + 1) * HEAD_DIM)
        qt_ref[0, h, 0] = jnp.concatenate(
            [q_t[head_rows] * scale, bias_t, jnp.zeros((LANES - HEAD_DIM - nb, tq), F32)], axis=0).astype(BF16)
        vt_ref[0, h, 0] = jnp.concatenate([v_t[head_rows], ones_rows], axis=0).astype(BF16)
        kh = jnp.where(lane // HEAD_DIM == h, k, 0.0)
        if h:
            kh = pltpu.roll(kh, LANES - h * HEAD_DIM, 1)
        onehot = jnp.where(lane == HEAD_DIM + i, 1.0, 0.0)
        ka_ref[0, h] = jnp.where(lane < HEAD_DIM, kh, onehot).astype(BF16)


def _router(q, k, v, kmean):
    bsz, seq, _ = q.shape
    tq = MOBA_BLOCK
    nb = seq // tq
    assert nb % SUBLANES == 0 and nb <= LANES - HEAD_DIM
    tiles = ATTN_HEADS // HEADS_PER_TILE
    tile_spec = pl.BlockSpec((1, tq, LANES), lambda b, p, i: (b, i, p))
    return pl.pallas_call(
        functools.partial(_router_kernel, nb=nb),
        grid=(bsz, tiles, nb),
        in_specs=[tile_spec, tile_spec, tile_spec,
                  pl.BlockSpec((1, nb, LANES), lambda b, p, i: (b, 0, p))],
        out_specs=[pl.BlockSpec((1, HEADS_PER_TILE, 1, LANES, tq), lambda b, p, i: (b, p, i, 0, 0)),
                   pl.BlockSpec((1, HEADS_PER_TILE, tq, LANES), lambda b, p, i: (b, p, i, 0)),
                   pl.BlockSpec((1, HEADS_PER_TILE, 1, VT_ROWS, tq), lambda b, p, i: (b, p, i, 0, 0))],
        out_shape=[jax.ShapeDtypeStruct((bsz, ATTN_HEADS, nb, LANES, tq), BF16),
                   jax.ShapeDtypeStruct((bsz, ATTN_HEADS, seq, LANES), BF16),
                   jax.ShapeDtypeStruct((bsz, ATTN_HEADS, nb, VT_ROWS, tq), BF16)],
        compiler_params=_params("arbitrary", "arbitrary", "arbitrary"),
        name="router",
    )(q, k, v, kmean)


ATTN_HEADS_PER_STEP = 8


RUNNING_MAX_INIT = -3.0e38


def _attn_kernel(qt_ref, ka_ref, vt_ref, o_ref, sa_scr, sb_scr, m_scr, acc_scr):
    i = pl.program_id(1)
    nh = qt_ref.shape[1]
    tq = qt_ref.shape[4]
    key = lax.broadcasted_iota(jnp.int32, (tq, tq), 0)
    qry = lax.broadcasted_iota(jnp.int32, (tq, tq), 1)
    causal = key <= qry

    def scores(h, j, s_scr, mask):
        start = pl.multiple_of(j * tq, tq)
        s = _dot(ka_ref[0, h, pl.ds(start, tq), :], qt_ref[0, h, 0])
        if mask is not None:
            s = jnp.where(mask, s, MASK_VALUE)
        s_scr[h] = s

    def consume(h, j, s_scr):
        s = s_scr[h]
        m_old = m_scr[h]
        m_new = jnp.maximum(m_old, jnp.max(s, axis=0, keepdims=True))
        alpha = jnp.exp2(m_old - m_new)
        p = jnp.exp2(s - m_new).astype(BF16)
        acc_scr[h] = alpha * acc_scr[h] + _dot(vt_ref[0, h, j], p)
        m_scr[h] = m_new

    def stage(j_next, next_scr, mask, j_done, done_scr):
        for h in range(nh):
            scores(h, j_next, next_scr, mask)
            consume(h, j_done, done_scr)

    m_scr[...] = jnp.full(m_scr.shape, RUNNING_MAX_INIT, F32)
    acc_scr[...] = jnp.zeros(acc_scr.shape, F32)
    pairs = (i + 2) // 2
    for h in range(nh):
        scores(h, i, sa_scr, causal)

    def body(t, carry):
        stage(2 * t, sb_scr, None, jnp.where(t == 0, i, 2 * t - 1), sa_scr)
        stage(2 * t + 1, sa_scr, None, 2 * t, sb_scr)
        return carry

    lax.fori_loop(0, pairs - 1, body, 0)
    tail = 2 * pairs - 2
    real = tail < i
    tail_blk = jnp.where(real, tail, i)
    stage(tail_blk, sb_scr, jnp.broadcast_to(real, (tq, tq)),
          jnp.where(pairs == 1, i, tail - 1), sa_scr)
    for h in range(nh):
        consume(h, tail_blk, sb_scr)

    for vt in range(nh // HEADS_PER_TILE):
        rows = []
        for hh in range(HEADS_PER_TILE):
            acc = acc_scr[vt * HEADS_PER_TILE + hh]
            rows.append(acc[0:HEAD_DIM] / acc[HEAD_DIM:HEAD_DIM + 1])
        o_ref[0, :, vt * LANES:(vt + 1) * LANES] = jnp.concatenate(rows, axis=0).T


def _attention(q_t, k_aug, v_t):
    bsz, _, nb, _, tq = q_t.shape
    seq = nb * tq
    nh = ATTN_HEADS_PER_STEP
    groups = ATTN_HEADS // nh
    ow = (nh // HEADS_PER_TILE) * LANES
    resident = pl.Buffered(1)
    return pl.pallas_call(
        _attn_kernel,
        grid=(bsz * groups, nb),
        in_specs=[pl.BlockSpec((1, nh, 1, LANES, tq), lambda g, i: (g // groups, g % groups, i, 0, 0)),
                  pl.BlockSpec((1, nh, seq, LANES), lambda g, i: (g // groups, g % groups, 0, 0),
                               pipeline_mode=resident),
                  pl.BlockSpec((1, nh, nb, VT_ROWS, tq), lambda g, i: (g // groups, g % groups, 0, 0, 0),
                               pipeline_mode=resident)],
        out_specs=pl.BlockSpec((1, tq, ow), lambda g, i: (g // groups, i, g % groups)),
        out_shape=jax.ShapeDtypeStruct((bsz, seq, ATTN_WIDTH), F32),
        scratch_shapes=[pltpu.VMEM((nh, tq, tq), F32), pltpu.VMEM((nh, tq, tq), F32), pltpu.VMEM((nh, 1, tq), F32),
                        pltpu.VMEM((nh, VT_ROWS, tq), F32)],
        compiler_params=_params("arbitrary", "arbitrary"),
        name="attn",
    )(q_t, k_aug, v_t)


S5_ROWS = 512
S5_CHUNK = 512


def _s5_scan(bu_scr, st_scr, lam_ref, rows, keep_states):
    for c in range(SSM_NSTATE // S5_CHUNK):
        re = slice(c * S5_CHUNK, (c + 1) * S5_CHUNK)
        im = slice(SSM_NSTATE + c * S5_CHUNK, SSM_NSTATE + (c + 1) * S5_CHUNK)
        lr = jnp.broadcast_to(lam_ref[0:1, re], (N_SEG, S5_CHUNK))
        li = jnp.broadcast_to(lam_ref[1:2, re], (N_SEG, S5_CHUNK))

        def step(j, carry, re=re, im=im, lr=lr, li=li):
            xr, xi = carry
            r0 = pl.multiple_of(j * N_SEG, N_SEG)
            nxr = lr * xr - li * xi + bu_scr[pl.ds(r0, N_SEG), re]
            nxi = lr * xi + li * xr + bu_scr[pl.ds(r0, N_SEG), im]
            if keep_states:
                bu_scr[pl.ds(r0, N_SEG), re] = nxr
                bu_scr[pl.ds(r0, N_SEG), im] = nxi
            return nxr, nxi

        xr, xi = lax.fori_loop(0, rows // N_SEG, step, (st_scr[:, re], st_scr[:, im]), unroll=4)
        st_scr[:, re] = xr
        st_scr[:, im] = xi


def _s5_final_kernel(u_ref, bbig_ref, lam_ref, fin_ref, bu_scr, st_scr):
    jb = pl.program_id(1)

    @pl.when(jb == 0)
    def _():
        st_scr[...] = jnp.zeros_like(st_scr)

    bu_scr[...] = _dot(u_ref[0], bbig_ref[...])
    _s5_scan(bu_scr, st_scr, lam_ref, u_ref.shape[1], keep_states=False)

    @pl.when(jb == pl.num_programs(1) - 1)
    def _():
        fin_ref[0] = st_scr[...]


def _s5_main_kernel(u_ref, bbig_ref, cbig_ref, lam_ref, lamseg_ref, fin_ref, d_ref, wglu_ref, bglu_ref,
                    o_ref, bu_scr, st_scr):
    jb = pl.program_id(1)
    ns = SSM_NSTATE

    @pl.when(jb == 0)
    def _():
        pr = lamseg_ref[0:1, :]
        pi = lamseg_ref[1:2, :]
        sr = jnp.zeros((1, ns), F32)
        si = jnp.zeros((1, ns), F32)
        st_scr[0:1, :] = jnp.zeros((1, 2 * ns), F32)
        for s in range(1, N_SEG):
            fr = fin_ref[0, s - 1:s, 0:ns]
            fi = fin_ref[0, s - 1:s, ns:2 * ns]
            sr, si = pr * sr - pi * si + fr, pr * si + pi * sr + fi
            st_scr[s:s + 1, 0:ns] = sr
            st_scr[s:s + 1, ns:2 * ns] = si

    u = u_ref[0]
    bu_scr[...] = _dot(u, bbig_ref[...])
    _s5_scan(bu_scr, st_scr, lam_ref, u.shape[0], keep_states=True)
    y = _dot(bu_scr[...].astype(BF16), cbig_ref[...]) + d_ref[...] * u.astype(F32)
    y = y * (0.5 * (1.0 + jnp.tanh(math.sqrt(2.0 / math.pi) * (y + 0.044715 * (y * y * y)))))
    z = _dot(y.astype(BF16), wglu_ref[...]) + bglu_ref[...]
    o_ref[0] = y * jax.nn.sigmoid(z)


def _s5(u_rows, bbig, cbig, lamvec, lamseg, d_skip, w_glu, b_glu):
    bsz, seq, _ = u_rows.shape
    rows = S5_ROWS
    nblk = seq // rows
    const = lambda b, j: (0, 0)
    scratch = [pltpu.VMEM((rows, 2 * SSM_NSTATE), F32), pltpu.VMEM((N_SEG, 2 * SSM_NSTATE), F32)]
    fin = pl.pallas_call(
        _s5_final_kernel,
        grid=(bsz, nblk),
        in_specs=[pl.BlockSpec((1, rows, SSM_WIDTH), lambda b, j: (b, j, 0)),
                  pl.BlockSpec(bbig.shape, const),
                  pl.BlockSpec(lamvec.shape, const)],
        out_specs=pl.BlockSpec((1, N_SEG, 2 * SSM_NSTATE), lambda b, j: (b, 0, 0)),
        out_shape=jax.ShapeDtypeStruct((bsz, N_SEG, 2 * SSM_NSTATE), F32),
        scratch_shapes=scratch,
        compiler_params=_params("arbitrary", "arbitrary"),
        name="s5_fin",
    )(u_rows, bbig, lamvec)
    return pl.pallas_call(
        _s5_main_kernel,
        grid=(bsz, nblk),
        in_specs=[pl.BlockSpec((1, rows, SSM_WIDTH), lambda b, j: (b, j, 0)),
                  pl.BlockSpec(bbig.shape, const),
                  pl.BlockSpec(cbig.shape, const),
                  pl.BlockSpec(lamvec.shape, const),
                  pl.BlockSpec(lamseg.shape, const),
                  pl.BlockSpec((1, N_SEG, 2 * SSM_NSTATE), lambda b, j: (b, 0, 0)),
                  pl.BlockSpec(d_skip.shape, const),
                  pl.BlockSpec(w_glu.shape, const),
                  pl.BlockSpec(b_glu.shape, const)],
        out_specs=pl.BlockSpec((1, rows, SSM_WIDTH), lambda b, j: (b, j, 0)),
        out_shape=jax.ShapeDtypeStruct((bsz, seq, SSM_WIDTH), F32),
        scratch_shapes=scratch,
        compiler_params=_params("arbitrary", "arbitrary"),
        name="s5_main",
    )(u_rows, bbig, cbig, lamvec, lamseg, fin, d_skip, w_glu, b_glu)


FF_CHUNK = 1024


def _outmlp_kernel(x_ref, attn_ref, ssm_ref, mod_ref, ga_ref, gs_ref, gm_ref, gf_ref,
                   wout_ref, w1_ref, w2_ref, o_ref, *, final_norm):
    x = x_ref[0]
    tm, d = x.shape
    gt1 = mod_ref[0, :, 2 * d:3 * d]
    sh2 = mod_ref[0, :, 3 * d:4 * d]
    sc2 = mod_ref[0, :, 4 * d:5 * d]
    gt2 = mod_ref[0, :, 5 * d:6 * d]
    a = _rmsnorm(attn_ref[0], ga_ref[...]).astype(BF16)
    s = _rmsnorm(ssm_ref[0], gs_ref[...]).astype(BF16)
    mixed = _dot(a, wout_ref[0:ATTN_WIDTH, :]) + _dot(s, wout_ref[ATTN_WIDTH:, :])
    x1 = x + gt1 * mixed
    h = (_rmsnorm(x1, gm_ref[...]) * (1.0 + sc2) + sh2).astype(BF16)
    ff = w1_ref.shape[1]
    acc = jnp.zeros((tm, d), F32)
    for c in range(ff // FF_CHUNK):
        t = jnp.maximum(_dot(h, w1_ref[:, c * FF_CHUNK:(c + 1) * FF_CHUNK]), 0.0)
        acc = acc + _dot((t * t).astype(BF16), w2_ref[c * FF_CHUNK:(c + 1) * FF_CHUNK, :])
    x2 = x1 + gt2 * acc
    o_ref[0] = _rmsnorm(x2, gf_ref[...]) if final_norm else x2


def _outmlp(x, attn, ssm_seg, mod3, g_attn, g_ssm, g_mlp, g_final, w_out, w_fc1, w_fc2, final_norm):
    bsz, seq, d = x.shape
    tm = 256
    nj = (seq // N_SEG) // tm
    const = lambda b, t: (0, 0)
    return pl.pallas_call(
        functools.partial(_outmlp_kernel, final_norm=final_norm),
        grid=(bsz, seq // tm),
        in_specs=[pl.BlockSpec((1, tm, d), lambda b, t: (b, t, 0)),
                  pl.BlockSpec((1, tm, ATTN_WIDTH), lambda b, t: (b, t, 0)),
                  pl.BlockSpec((1, tm, SSM_WIDTH), lambda b, t: (b, t % nj, t // nj)),
                  pl.BlockSpec((1, 1, mod3.shape[-1]), lambda b, t: (b, 0, 0)),
                  pl.BlockSpec(g_attn.shape, const),
                  pl.BlockSpec(g_ssm.shape, const),
                  pl.BlockSpec(g_mlp.shape, const),
                  pl.BlockSpec(g_final.shape, const),
                  pl.BlockSpec(w_out.shape, const),
                  pl.BlockSpec(w_fc1.shape, const),
                  pl.BlockSpec(w_fc2.shape, const)],
        out_specs=pl.BlockSpec((1, tm, d), lambda b, t: (b, t, 0)),
        out_shape=jax.ShapeDtypeStruct((bsz, seq, d), F32),
        compiler_params=_params("arbitrary", "arbitrary"),
        name="outmlp",
    )(x, attn, ssm_seg, mod3, g_attn, g_ssm, g_mlp, g_final, w_out, w_fc1, w_fc2)


def _rope_tables(seq):
    half = HEAD_DIM // 2
    inv_freq = ROPE_THETA ** (-jnp.arange(half, dtype=F32) / half)
    ang = jnp.arange(seq, dtype=F32)[:, None] * inv_freq[None, :]
    cos, sin = jnp.cos(ang), jnp.sin(ang)
    reps = LANES // HEAD_DIM
    cos_t = jnp.tile(cos, (1, 2 * reps))
    sin_t = jnp.tile(jnp.concatenate([-sin, sin], axis=1), (1, reps))
    return cos_t, sin_t


def _s5_params(lam_re, lam_im, log_dt, b_re, b_im, c_re, c_im, seg_len):
    lr, li = lam_re.astype(F32), lam_im.astype(F32)
    dt = jnp.exp(log_dt.astype(F32))[:, None]

    def cexp(scale):
        mag = jnp.exp(lr * scale)
        return mag * jnp.cos(li * scale), mag * jnp.sin(li * scale)

    ar, ai = cexp(dt)
    den = lr * lr + li * li
    cr = ((ar - 1.0) * lr + ai * li) / den
    ci = (ai * lr - (ar - 1.0) * li) / den
    bbar_re = cr[..., None] * b_re.astype(F32) - ci[..., None] * b_im.astype(F32)
    bbar_im = cr[..., None] * b_im.astype(F32) + ci[..., None] * b_re.astype(F32)
    sr, si = cexp(dt * seg_len)
    eye = jnp.eye(SSM_GROUPS, dtype=F32)

    def diag_in(m):
        return jnp.einsum('gpn,gh->gnhp', m, eye).reshape(SSM_WIDTH, SSM_NSTATE)

    def diag_out(m):
        return jnp.einsum('gnp,gh->gphn', m, eye).reshape(SSM_NSTATE, SSM_WIDTH)

    bbig = jnp.concatenate([diag_in(bbar_re), diag_in(bbar_im)], axis=1).astype(BF16)
    cbig = jnp.concatenate([diag_out(c_re.astype(F32)), diag_out(-c_im.astype(F32))], axis=0).astype(BF16)
    lamvec = jnp.stack([ar.reshape(-1), ai.reshape(-1)])
    lamseg = jnp.stack([sr.reshape(-1), si.reshape(-1)])
    return bbig, cbig, lamvec, lamseg


def kernel(x, c, w_ada, b_ada, g_mix, w_in, g_attn_out, lam_re, lam_im, log_dt, b_re, b_im, c_re, c_im,
           d_skip, w_glu, b_glu, g_ssm_out, w_out, g_mlp, w_fc1, w_fc2, g_final):
    bsz, seq, d = x.shape
    depth = w_ada.shape[0]
    assert seq % (N_SEG * MOBA_BLOCK) == 0 and seq % S5_ROWS == 0 and bsz <= SUBLANES
    cos_t, sin_t = _rope_tables(seq)
    c_pad = jnp.zeros((SUBLANES, d), F32).at[:bsz].set(c.astype(F32))

    for l in range(depth):
        mod = _adaln(c_pad, w_ada[l], b_ada[l][None, :])
        mod3 = mod[:bsz, None, :]

        wqk_hi, wqk_lo = _split_bf16(w_in[l][:, :2 * ATTN_WIDTH])
        wvu = w_in[l][:, 2 * ATTN_WIDTH:].astype(BF16)
        q, k, v, u_seg, kmean = _inproj(x, mod3, g_mix[l][None, :], wqk_hi, wqk_lo, wvu, cos_t, sin_t)

        q_t, k_aug, v_t = _router(q, k, v, kmean.reshape(bsz, seq // MOBA_BLOCK, ATTN_WIDTH))
        attn = _attention(q_t, k_aug, v_t)

        bbig, cbig, lamvec, lamseg = _s5_params(lam_re[l], lam_im[l], log_dt[l], b_re[l], b_im[l],
                                                c_re[l], c_im[l], seq // N_SEG)
        u_rows = u_seg.reshape(bsz, seq, SSM_WIDTH)
        ssm_rows = _s5(u_rows, bbig, cbig, lamvec, lamseg, d_skip[l].reshape(1, SSM_WIDTH),
                       w_glu[l].astype(BF16), b_glu[l][None, :])
        ssm_seg = ssm_rows.reshape(bsz, seq // N_SEG, N_SEG * SSM_WIDTH)

        x = _outmlp(x, attn, ssm_seg, mod3, g_attn_out[l][None, :], g_ssm_out[l][None, :], g_mlp[l][None, :],
                    g_final[None, :], w_out[l].astype(BF16), w_fc1[l].astype(BF16), w_fc2[l].astype(BF16),
                    final_norm=(l == depth - 1))
    return x
```

```python
import functools
import math

import jax
import jax.numpy as jnp
from jax import lax
from jax.experimental import pallas as pl
from jax.experimental.pallas import tpu as pltpu

F32 = jnp.float32
BF16 = jnp.bfloat16

HEAD_DIM = 64
ATTN_HEADS = 8
ATTN_WIDTH = ATTN_HEADS * HEAD_DIM
SSM_GROUPS = 32
SSM_GROUP_CH = 16
SSM_WIDTH = SSM_GROUPS * SSM_GROUP_CH
SSM_STATE = 64
SSM_NSTATE = SSM_GROUPS * SSM_STATE
MOBA_BLOCK = 256
MOBA_TOPK = 3
ROPE_THETA = 10000.0
EPS = 1e-6

LANES = 128
SUBLANES = 8
N_SEG = SUBLANES
HEADS_PER_TILE = LANES // HEAD_DIM
MASK_VALUE = -(2.0 ** 100)
VMEM_LIMIT_BYTES = 56 * 1024 * 1024


def _split_bf16(a):
    hi = a.astype(BF16)
    lo = (a - hi.astype(F32)).astype(BF16)
    return hi, lo


def _dot(a, b):
    return jnp.dot(a, b, preferred_element_type=F32)


def _rmsnorm(x, g):
    return x * lax.rsqrt(jnp.mean(x * x, axis=-1, keepdims=True) + EPS) * g


def _params(*semantics):
    return pltpu.CompilerParams(dimension_semantics=semantics, vmem_limit_bytes=VMEM_LIMIT_BYTES)


def _adaln_kernel(c_ref, w_ref, b_ref, o_ref):
    c = c_ref[...]
    s_hi, s_lo = _split_bf16(c * jax.nn.sigmoid(c))
    w_hi, w_lo = _split_bf16(w_ref[...])
    o_ref[...] = _dot(s_hi, w_hi) + _dot(s_lo, w_hi) + _dot(s_hi, w_lo) + b_ref[...]


def _adaln(c_pad, w, b):
    rows, d = c_pad.shape
    n = w.shape[1]
    tn = n // 4
    return pl.pallas_call(
        _adaln_kernel,
        grid=(n // tn,),
        in_specs=[pl.BlockSpec((rows, d), lambda j: (0, 0)),
                  pl.BlockSpec((d, tn), lambda j: (0, j)),
                  pl.BlockSpec((1, tn), lambda j: (0, j))],
        out_specs=pl.BlockSpec((rows, tn), lambda j: (0, j)),
        out_shape=jax.ShapeDtypeStruct((rows, n), F32),
        compiler_params=_params("arbitrary"),
        name="adaln",
    )(c_pad, w, b)


VT_ROWS = HEAD_DIM + 16


def _top_blocks(gate, blk, past, own):
    nb = gate.shape[0]
    g = jnp.where(past, gate, -jnp.inf)
    chosen = None
    for _ in range(MOBA_TOPK):
        best = jnp.max(g, axis=0, keepdims=True)
        first = jnp.min(jnp.where(g == best, blk, float(nb)), axis=0, keepdims=True)
        pick = blk == first
        chosen = pick if chosen is None else (chosen | pick)
        g = jnp.where(pick, -jnp.inf, g)
    return (chosen & past) | own


def _inproj_kernel(x_ref, mod_ref, g_ref, wqk_hi_ref, wqk_lo_ref, wvu_ref, cos_ref, sin_ref,
                   qt_ref, ka_ref, vt_ref, u_ref, km_scr):
    i = pl.program_id(1)
    x = x_ref[0]
    tm, d = x.shape
    nb = km_scr.shape[0]
    sh1 = mod_ref[0, :, 0:d]
    sc1 = mod_ref[0, :, d:2 * d]
    h = _rmsnorm(x, g_ref[...]) * (1.0 + sc1) + sh1
    h_hi, h_lo = _split_bf16(h)
    qk = (_dot(h_hi, wqk_hi_ref[...]) + _dot(h_lo, wqk_hi_ref[...])
          + _dot(h_hi, wqk_lo_ref[...]))
    vu = _dot(h_hi, wvu_ref[...])
    u_ref[0] = vu[:, ATTN_WIDTH:].astype(BF16)

    reps = ATTN_WIDTH // LANES
    cos = jnp.concatenate([cos_ref[...]] * reps, axis=1)
    sin = jnp.concatenate([sin_ref[...]] * reps, axis=1)
    wide_lane = lax.broadcasted_iota(jnp.int32, (tm, ATTN_WIDTH), 1)
    first_half = (wide_lane & (HEAD_DIM // 2)) == 0

    def rope(t):
        partner = jnp.where(first_half,
                            pltpu.roll(t, ATTN_WIDTH - HEAD_DIM // 2, 1),
                            pltpu.roll(t, HEAD_DIM // 2, 1))
        return t * cos + partner * sin

    q = rope(qk[:, :ATTN_WIDTH])
    k = rope(qk[:, ATTN_WIDTH:])
    v = vu[:, :ATTN_WIDTH]

    @pl.when(i == 0)
    def _():
        km_scr[...] = jnp.zeros_like(km_scr)

    km_scr[pl.ds(i, 1), :] = jnp.sum(k, axis=0, keepdims=True) * (1.0 / tm)
    km_all = km_scr[...]

    lane = lax.broadcasted_iota(jnp.int32, (tm, LANES), 1)
    km_lane = lax.broadcasted_iota(jnp.int32, (nb, LANES), 1)
    blk_i = lax.broadcasted_iota(jnp.int32, (nb, tm), 0)
    blk = blk_i.astype(F32)
    past = blk_i < i
    own = blk_i == i
    ones_rows = jnp.where(lax.broadcasted_iota(jnp.int32, (VT_ROWS - HEAD_DIM, tm), 0) == 0, 1.0, 0.0)
    onehot = jnp.where(lane == HEAD_DIM + i, 1.0, 0.0)
    scale = HEAD_DIM ** -0.5 * math.log2(math.e)
    for tile in range(ATTN_WIDTH // LANES):
        cols = slice(tile * LANES, (tile + 1) * LANES)
        q_t = q[:, cols].T
        v_t = v[:, cols].T
        k_tile = k[:, cols]
        km = km_all[:, cols]
        qt_hi, qt_lo = _split_bf16(q_t)
        for hh in range(HEADS_PER_TILE):
            h_idx = tile * HEADS_PER_TILE + hh
            kmh_hi, kmh_lo = _split_bf16(jnp.where(km_lane // HEAD_DIM == hh, km, 0.0))
            gate = _dot(kmh_hi, qt_hi) + _dot(kmh_lo, qt_hi) + _dot(kmh_hi, qt_lo)
            bias_t = jnp.where(_top_blocks(gate, blk, past, own), 0.0, MASK_VALUE)
            head_rows = slice(hh * HEAD_DIM, (hh + 1) * HEAD_DIM)
            qt_ref[0, h_idx, 0] = jnp.concatenate(
                [q_t[head_rows] * scale, bias_t, jnp.zeros((LANES - HEAD_DIM - nb, tm), F32)],
                axis=0).astype(BF16)
            vt_ref[0, h_idx, 0] = jnp.concatenate([v_t[head_rows], ones_rows], axis=0).astype(BF16)
            kh = jnp.where(lane // HEAD_DIM == hh, k_tile, 0.0)
            if hh:
                kh = pltpu.roll(kh, LANES - hh * HEAD_DIM, 1)
            ka_ref[0, h_idx] = jnp.where(lane < HEAD_DIM, kh, onehot).astype(BF16)


def _inproj(x, mod3, g_mix, wqk_hi, wqk_lo, wvu, cos_t, sin_t):
    bsz, seq, d = x.shape
    tm = MOBA_BLOCK
    nb = seq // tm
    assert nb % SUBLANES == 0 and nb <= LANES - HEAD_DIM
    nj = (seq // N_SEG) // tm
    const = lambda b, t: (0, 0)
    return pl.pallas_call(
        _inproj_kernel,
        grid=(bsz, nb),
        in_specs=[pl.BlockSpec((1, tm, d), lambda b, t: (b, t, 0)),
                  pl.BlockSpec((1, 1, mod3.shape[-1]), lambda b, t: (b, 0, 0)),
                  pl.BlockSpec((1, d), const),
                  pl.BlockSpec(wqk_hi.shape, const),
                  pl.BlockSpec(wqk_lo.shape, const),
                  pl.BlockSpec(wvu.shape, const),
                  pl.BlockSpec((tm, LANES), lambda b, t: (t, 0)),
                  pl.BlockSpec((tm, LANES), lambda b, t: (t, 0))],
        out_specs=[pl.BlockSpec((1, ATTN_HEADS, 1, LANES, tm), lambda b, t: (b, 0, t, 0, 0)),
                   pl.BlockSpec((1, ATTN_HEADS, tm, LANES), lambda b, t: (b, 0, t, 0)),
                   pl.BlockSpec((1, ATTN_HEADS, 1, VT_ROWS, tm), lambda b, t: (b, 0, t, 0, 0)),
                   pl.BlockSpec((1, tm, SSM_WIDTH), lambda b, t: (b, t % nj, t // nj))],
        out_shape=[jax.ShapeDtypeStruct((bsz, ATTN_HEADS, nb, LANES, tm), BF16),
                   jax.ShapeDtypeStruct((bsz, ATTN_HEADS, seq, LANES), BF16),
                   jax.ShapeDtypeStruct((bsz, ATTN_HEADS, nb, VT_ROWS, tm), BF16),
                   jax.ShapeDtypeStruct((bsz, seq // N_SEG, N_SEG * SSM_WIDTH), BF16)],
        scratch_shapes=[pltpu.VMEM((nb, ATTN_WIDTH), F32)],
        compiler_params=_params("arbitrary", "arbitrary"),
        name="inproj",
    )(x, mod3, g_mix, wqk_hi, wqk_lo, wvu, cos_t, sin_t)


ATTN_HEADS_PER_STEP = 8


RUNNING_MAX_INIT = -3.0e38


def _attn_kernel(qt_ref, ka_ref, vt_ref, o_ref, sa_scr, sb_scr, m_scr, acc_scr):
    i = pl.program_id(1)
    nh = qt_ref.shape[1]
    tq = qt_ref.shape[4]
    key = lax.broadcasted_iota(jnp.int32, (tq, tq), 0)
    qry = lax.broadcasted_iota(jnp.int32, (tq, tq), 1)
    causal = key <= qry

    def scores(h, j, s_scr, mask):
        start = pl.multiple_of(j * tq, tq)
        s = _dot(ka_ref[0, h, pl.ds(start, tq), :], qt_ref[0, h, 0])
        if mask is not None:
            s = jnp.where(mask, s, MASK_VALUE)
        s_scr[h] = s

    def consume(h, j, s_scr):
        s = s_scr[h]
        m_old = m_scr[h]
        m_new = jnp.maximum(m_old, jnp.max(s, axis=0, keepdims=True))
        alpha = jnp.exp2(m_old - m_new)
        p = jnp.exp2(s - m_new).astype(BF16)
        acc_scr[h] = alpha * acc_scr[h] + _dot(vt_ref[0, h, j], p)
        m_scr[h] = m_new

    def stage(j_next, next_scr, mask, j_done, done_scr):
        for h in range(nh):
            scores(h, j_next, next_scr, mask)
            consume(h, j_done, done_scr)

    m_scr[...] = jnp.full(m_scr.shape, RUNNING_MAX_INIT, F32)
    acc_scr[...] = jnp.zeros(acc_scr.shape, F32)
    pairs = (i + 2) // 2
    for h in range(nh):
        scores(h, i, sa_scr, causal)

    def body(t, carry):
        stage(2 * t, sb_scr, None, jnp.where(t == 0, i, 2 * t - 1), sa_scr)
        stage(2 * t + 1, sa_scr, None, 2 * t, sb_scr)
        return carry

    lax.fori_loop(0, pairs - 1, body, 0)
    tail = 2 * pairs - 2
    real = tail < i
    tail_blk = jnp.where(real, tail, i)
    stage(tail_blk, sb_scr, jnp.broadcast_to(real, (tq, tq)),
          jnp.where(pairs == 1, i, tail - 1), sa_scr)
    for h in range(nh):
        consume(h, tail_blk, sb_scr)

    for vt in range(nh // HEADS_PER_TILE):
        rows = []
        for hh in range(HEADS_PER_TILE):
            acc = acc_scr[vt * HEADS_PER_TILE + hh]
            rows.append(acc[0:HEAD_DIM] / acc[HEAD_DIM:HEAD_DIM + 1])
        o_ref[0, :, vt * LANES:(vt + 1) * LANES] = jnp.concatenate(rows, axis=0).T


def _attention(q_t, k_aug, v_t):
    bsz, _, nb, _, tq = q_t.shape
    seq = nb * tq
    nh = ATTN_HEADS_PER_STEP
    groups = ATTN_HEADS // nh
    ow = (nh // HEADS_PER_TILE) * LANES
    resident = pl.Buffered(1)
    return pl.pallas_call(
        _attn_kernel,
        grid=(bsz * groups, nb),
        in_specs=[pl.BlockSpec((1, nh, 1, LANES, tq), lambda g, i: (g // groups, g % groups, i, 0, 0)),
                  pl.BlockSpec((1, nh, seq, LANES), lambda g, i: (g // groups, g % groups, 0, 0),
                               pipeline_mode=resident),
                  pl.BlockSpec((1, nh, nb, VT_ROWS, tq), lambda g, i: (g // groups, g % groups, 0, 0, 0),
                               pipeline_mode=resident)],
        out_specs=pl.BlockSpec((1, tq, ow), lambda g, i: (g // groups, i, g % groups)),
        out_shape=jax.ShapeDtypeStruct((bsz, seq, ATTN_WIDTH), F32),
        scratch_shapes=[pltpu.VMEM((nh, tq, tq), F32), pltpu.VMEM((nh, tq, tq), F32), pltpu.VMEM((nh, 1, tq), F32),
                        pltpu.VMEM((nh, VT_ROWS, tq), F32)],
        compiler_params=_params("arbitrary", "arbitrary"),
        name="attn",
    )(q_t, k_aug, v_t)


S5_ROWS = 512
S5_CHUNK = 512


def _s5_scan(bu_scr, st_scr, lam_ref, rows, keep_states):
    for c in range(SSM_NSTATE // S5_CHUNK):
        re = slice(c * S5_CHUNK, (c + 1) * S5_CHUNK)
        im = slice(SSM_NSTATE + c * S5_CHUNK, SSM_NSTATE + (c + 1) * S5_CHUNK)
        lr = jnp.broadcast_to(lam_ref[0:1, re], (N_SEG, S5_CHUNK))
        li = jnp.broadcast_to(lam_ref[1:2, re], (N_SEG, S5_CHUNK))

        def step(j, carry, re=re, im=im, lr=lr, li=li):
            xr, xi = carry
            r0 = pl.multiple_of(j * N_SEG, N_SEG)
            nxr = lr * xr - li * xi + bu_scr[pl.ds(r0, N_SEG), re]
            nxi = lr * xi + li * xr + bu_scr[pl.ds(r0, N_SEG), im]
            if keep_states:
                bu_scr[pl.ds(r0, N_SEG), re] = nxr
                bu_scr[pl.ds(r0, N_SEG), im] = nxi
            return nxr, nxi

        xr, xi = lax.fori_loop(0, rows // N_SEG, step, (st_scr[:, re], st_scr[:, im]), unroll=4)
        st_scr[:, re] = xr
        st_scr[:, im] = xi


def _s5_final_kernel(u_ref, bbig_ref, lam_ref, fin_ref, bu_scr, st_scr):
    jb = pl.program_id(1)

    @pl.when(jb == 0)
    def _():
        st_scr[...] = jnp.zeros_like(st_scr)

    bu_scr[...] = _dot(u_ref[0], bbig_ref[...])
    _s5_scan(bu_scr, st_scr, lam_ref, u_ref.shape[1], keep_states=False)

    @pl.when(jb == pl.num_programs(1) - 1)
    def _():
        fin_ref[0] = st_scr[...]


def _s5_main_kernel(u_ref, bbig_ref, cbig_ref, lam_ref, lamseg_ref, fin_ref, d_ref, wglu_ref, bglu_ref,
                    o_ref, bu_scr, st_scr):
    jb = pl.program_id(1)
    ns = SSM_NSTATE

    @pl.when(jb == 0)
    def _():
        pr = lamseg_ref[0:1, :]
        pi = lamseg_ref[1:2, :]
        sr = jnp.zeros((1, ns), F32)
        si = jnp.zeros((1, ns), F32)
        st_scr[0:1, :] = jnp.zeros((1, 2 * ns), F32)
        for s in range(1, N_SEG):
            fr = fin_ref[0, s - 1:s, 0:ns]
            fi = fin_ref[0, s - 1:s, ns:2 * ns]
            sr, si = pr * sr - pi * si + fr, pr * si + pi * sr + fi
            st_scr[s:s + 1, 0:ns] = sr
            st_scr[s:s + 1, ns:2 * ns] = si

    u = u_ref[0]
    bu_scr[...] = _dot(u, bbig_ref[...])
    _s5_scan(bu_scr, st_scr, lam_ref, u.shape[0], keep_states=True)
    y = _dot(bu_scr[...].astype(BF16), cbig_ref[...]) + d_ref[...] * u.astype(F32)
    y = y * (0.5 * (1.0 + jnp.tanh(math.sqrt(2.0 / math.pi) * (y + 0.044715 * (y * y * y)))))
    z = _dot(y.astype(BF16), wglu_ref[...]) + bglu_ref[...]
    o_ref[0] = y * jax.nn.sigmoid(z)


def _s5(u_rows, bbig, cbig, lamvec, lamseg, d_skip, w_glu, b_glu):
    bsz, seq, _ = u_rows.shape
    rows = S5_ROWS
    nblk = seq // rows
    const = lambda b, j: (0, 0)
    scratch = [pltpu.VMEM((rows, 2 * SSM_NSTATE), F32), pltpu.VMEM((N_SEG, 2 * SSM_NSTATE), F32)]
    fin = pl.pallas_call(
        _s5_final_kernel,
        grid=(bsz, nblk),
        in_specs=[pl.BlockSpec((1, rows, SSM_WIDTH), lambda b, j: (b, j, 0)),
                  pl.BlockSpec(bbig.shape, const),
                  pl.BlockSpec(lamvec.shape, const)],
        out_specs=pl.BlockSpec((1, N_SEG, 2 * SSM_NSTATE), lambda b, j: (b, 0, 0)),
        out_shape=jax.ShapeDtypeStruct((bsz, N_SEG, 2 * SSM_NSTATE), F32),
        scratch_shapes=scratch,
        compiler_params=_params("arbitrary", "arbitrary"),
        name="s5_fin",
    )(u_rows, bbig, lamvec)
    return pl.pallas_call(
        _s5_main_kernel,
        grid=(bsz, nblk),
        in_specs=[pl.BlockSpec((1, rows, SSM_WIDTH), lambda b, j: (b, j, 0)),
                  pl.BlockSpec(bbig.shape, const),
                  pl.BlockSpec(cbig.shape, const),
                  pl.BlockSpec(lamvec.shape, const),
                  pl.BlockSpec(lamseg.shape, const),
                  pl.BlockSpec((1, N_SEG, 2 * SSM_NSTATE), lambda b, j: (b, 0, 0)),
                  pl.BlockSpec(d_skip.shape, const),
                  pl.BlockSpec(w_glu.shape, const),
                  pl.BlockSpec(b_glu.shape, const)],
        out_specs=pl.BlockSpec((1, rows, SSM_WIDTH), lambda b, j: (b, j, 0)),
        out_shape=jax.ShapeDtypeStruct((bsz, seq, SSM_WIDTH), F32),
        scratch_shapes=scratch,
        compiler_params=_params("arbitrary", "arbitrary"),
        name="s5_main",
    )(u_rows, bbig, cbig, lamvec, lamseg, fin, d_skip, w_glu, b_glu)


FF_CHUNK = 1024


def _outmlp_kernel(x_ref, attn_ref, ssm_ref, mod_ref, ga_ref, gs_ref, gm_ref, gf_ref,
                   wout_ref, w1_ref, w2_ref, o_ref, *, final_norm):
    x = x_ref[0]
    tm, d = x.shape
    gt1 = mod_ref[0, :, 2 * d:3 * d]
    sh2 = mod_ref[0, :, 3 * d:4 * d]
    sc2 = mod_ref[0, :, 4 * d:5 * d]
    gt2 = mod_ref[0, :, 5 * d:6 * d]
    a = _rmsnorm(attn_ref[0], ga_ref[...]).astype(BF16)
    s = _rmsnorm(ssm_ref[0], gs_ref[...]).astype(BF16)
    mixed = _dot(a, wout_ref[0:ATTN_WIDTH, :]) + _dot(s, wout_ref[ATTN_WIDTH:, :])
    x1 = x + gt1 * mixed
    h = (_rmsnorm(x1, gm_ref[...]) * (1.0 + sc2) + sh2).astype(BF16)
    ff = w1_ref.shape[1]
    acc = jnp.zeros((tm, d), F32)
    for c in range(ff // FF_CHUNK):
        t = jnp.maximum(_dot(h, w1_ref[:, c * FF_CHUNK:(c + 1) * FF_CHUNK]), 0.0)
        acc = acc + _dot((t * t).astype(BF16), w2_ref[c * FF_CHUNK:(c + 1) * FF_CHUNK, :])
    x2 = x1 + gt2 * acc
    o_ref[0] = _rmsnorm(x2, gf_ref[...]) if final_norm else x2


def _outmlp(x, attn, ssm_seg, mod3, g_attn, g_ssm, g_mlp, g_final, w_out, w_fc1, w_fc2, final_norm):
    bsz, seq, d = x.shape
    tm = 256
    nj = (seq // N_SEG) // tm
    const = lambda b, t: (0, 0)
    return pl.pallas_call(
        functools.partial(_outmlp_kernel, final_norm=final_norm),
        grid=(bsz, seq // tm),
        in_specs=[pl.BlockSpec((1, tm, d), lambda b, t: (b, t, 0)),
                  pl.BlockSpec((1, tm, ATTN_WIDTH), lambda b, t: (b, t, 0)),
                  pl.BlockSpec((1, tm, SSM_WIDTH), lambda b, t: (b, t % nj, t // nj)),
                  pl.BlockSpec((1, 1, mod3.shape[-1]), lambda b, t: (b, 0, 0)),
                  pl.BlockSpec(g_attn.shape, const),
                  pl.BlockSpec(g_ssm.shape, const),
                  pl.BlockSpec(g_mlp.shape, const),
                  pl.BlockSpec(g_final.shape, const),
                  pl.BlockSpec(w_out.shape, const),
                  pl.BlockSpec(w_fc1.shape, const),
                  pl.BlockSpec(w_fc2.shape, const)],
        out_specs=pl.BlockSpec((1, tm, d), lambda b, t: (b, t, 0)),
        out_shape=jax.ShapeDtypeStruct((bsz, seq, d), F32),
        compiler_params=_params("arbitrary", "arbitrary"),
        name="outmlp",
    )(x, attn, ssm_seg, mod3, g_attn, g_ssm, g_mlp, g_final, w_out, w_fc1, w_fc2)


def _rope_tables(seq):
    half = HEAD_DIM // 2
    inv_freq = ROPE_THETA ** (-jnp.arange(half, dtype=F32) / half)
    ang = jnp.arange(seq, dtype=F32)[:, None] * inv_freq[None, :]
    cos, sin = jnp.cos(ang), jnp.sin(ang)
    reps = LANES // HEAD_DIM
    cos_t = jnp.tile(cos, (1, 2 * reps))
    sin_t = jnp.tile(jnp.concatenate([-sin, sin], axis=1), (1, reps))
    return cos_t, sin_t


def _s5_params(lam_re, lam_im, log_dt, b_re, b_im, c_re, c_im, seg_len):
    lr, li = lam_re.astype(F32), lam_im.astype(F32)
    dt = jnp.exp(log_dt.astype(F32))[:, None]

    def cexp(scale):
        mag = jnp.exp(lr * scale)
        return mag * jnp.cos(li * scale), mag * jnp.sin(li * scale)

    ar, ai = cexp(dt)
    den = lr * lr + li * li
    cr = ((ar - 1.0) * lr + ai * li) / den
    ci = (ai * lr - (ar - 1.0) * li) / den
    bbar_re = cr[..., None] * b_re.astype(F32) - ci[..., None] * b_im.astype(F32)
    bbar_im = cr[..., None] * b_im.astype(F32) + ci[..., None] * b_re.astype(F32)
    sr, si = cexp(dt * seg_len)
    eye = jnp.eye(SSM_GROUPS, dtype=F32)

    def diag_in(m):
        return jnp.einsum('gpn,gh->gnhp', m, eye).reshape(SSM_WIDTH, SSM_NSTATE)

    def diag_out(m):
        return jnp.einsum('gnp,gh->gphn', m, eye).reshape(SSM_NSTATE, SSM_WIDTH)

    bbig = jnp.concatenate([diag_in(bbar_re), diag_in(bbar_im)], axis=1).astype(BF16)
    cbig = jnp.concatenate([diag_out(c_re.astype(F32)), diag_out(-c_im.astype(F32))], axis=0).astype(BF16)
    lamvec = jnp.stack([ar.reshape(-1), ai.reshape(-1)])
    lamseg = jnp.stack([sr.reshape(-1), si.reshape(-1)])
    return bbig, cbig, lamvec, lamseg


def kernel(x, c, w_ada, b_ada, g_mix, w_in, g_attn_out, lam_re, lam_im, log_dt, b_re, b_im, c_re, c_im,
           d_skip, w_glu, b_glu, g_ssm_out, w_out, g_mlp, w_fc1, w_fc2, g_final):
    bsz, seq, d = x.shape
    depth = w_ada.shape[0]
    assert seq % (N_SEG * MOBA_BLOCK) == 0 and seq % S5_ROWS == 0 and bsz <= SUBLANES
    cos_t, sin_t = _rope_tables(seq)
    c_pad = jnp.zeros((SUBLANES, d), F32).at[:bsz].set(c.astype(F32))

    for l in range(depth):
        mod = _adaln(c_pad, w_ada[l], b_ada[l][None, :])
        mod3 = mod[:bsz, None, :]

        wqk_hi, wqk_lo = _split_bf16(w_in[l][:, :2 * ATTN_WIDTH])
        wvu = w_in[l][:, 2 * ATTN_WIDTH:].astype(BF16)
        q_t, k_aug, v_t, u_seg = _inproj(x, mod3, g_mix[l][None, :], wqk_hi, wqk_lo, wvu, cos_t, sin_t)
        attn = _attention(q_t, k_aug, v_t)

        bbig, cbig, lamvec, lamseg = _s5_params(lam_re[l], lam_im[l], log_dt[l], b_re[l], b_im[l],
                                                c_re[l], c_im[l], seq // N_SEG)
        u_rows = u_seg.reshape(bsz, seq, SSM_WIDTH)
        ssm_rows = _s5(u_rows, bbig, cbig, lamvec, lamseg, d_skip[l].reshape(1, SSM_WIDTH),
                       w_glu[l].astype(BF16), b_glu[l][None, :])
        ssm_seg = ssm_rows.reshape(bsz, seq // N_SEG, N_SEG * SSM_WIDTH)

        x = _outmlp(x, attn, ssm_seg, mod3, g_attn_out[l][None, :], g_ssm_out[l][None, :], g_mlp[l][None, :],
                    g_final[None, :], w_out[l].astype(BF16), w_fc1[l].astype(BF16), w_fc2[l].astype(BF16),
                    final_norm=(l == depth - 1))
    return x
```

```python
import functools
import math

import jax
import jax.numpy as jnp
from jax import lax
from jax.experimental import pallas as pl
from jax.experimental.pallas import tpu as pltpu

F32 = jnp.float32
BF16 = jnp.bfloat16

HEAD_DIM = 64
ATTN_HEADS = 8
ATTN_WIDTH = ATTN_HEADS * HEAD_DIM
SSM_GROUPS = 32
SSM_GROUP_CH = 16
SSM_WIDTH = SSM_GROUPS * SSM_GROUP_CH
SSM_STATE = 64
SSM_NSTATE = SSM_GROUPS * SSM_STATE
MOBA_BLOCK = 256
MOBA_TOPK = 3
ROPE_THETA = 10000.0
EPS = 1e-6

LANES = 128
SUBLANES = 8
N_SEG = SUBLANES
HEADS_PER_TILE = LANES // HEAD_DIM
MASK_VALUE = -(2.0 ** 100)
VMEM_LIMIT_BYTES = 56 * 1024 * 1024


def _split_bf16(a):
    hi = a.astype(BF16)
    lo = (a - hi.astype(F32)).astype(BF16)
    return hi, lo


def _dot(a, b):
    return jnp.dot(a, b, preferred_element_type=F32)


def _rmsnorm(x, g):
    return x * lax.rsqrt(jnp.mean(x * x, axis=-1, keepdims=True) + EPS) * g


def _params(*semantics):
    return pltpu.CompilerParams(dimension_semantics=semantics, vmem_limit_bytes=VMEM_LIMIT_BYTES)


def _adaln_kernel(c_ref, w_ref, b_ref, o_ref):
    c = c_ref[...]
    s_hi, s_lo = _split_bf16(c * jax.nn.sigmoid(c))
    w_hi, w_lo = _split_bf16(w_ref[...])
    o_ref[...] = _dot(s_hi, w_hi) + _dot(s_lo, w_hi) + _dot(s_hi, w_lo) + b_ref[...]


def _adaln(c_pad, w, b):
    rows, d = c_pad.shape
    n = w.shape[1]
    tn = n // 4
    return pl.pallas_call(
        _adaln_kernel,
        grid=(n // tn,),
        in_specs=[pl.BlockSpec((rows, d), lambda j: (0, 0)),
                  pl.BlockSpec((d, tn), lambda j: (0, j)),
                  pl.BlockSpec((1, tn), lambda j: (0, j))],
        out_specs=pl.BlockSpec((rows, tn), lambda j: (0, j)),
        out_shape=jax.ShapeDtypeStruct((rows, n), F32),
        compiler_params=_params("arbitrary"),
        name="adaln",
    )(c_pad, w, b)


VT_ROWS = HEAD_DIM + 16


def _top_blocks(gate, blk, past, own):
    nb = gate.shape[0]
    g = jnp.where(past, gate, -jnp.inf)
    chosen = None
    for _ in range(MOBA_TOPK):
        best = jnp.max(g, axis=0, keepdims=True)
        first = jnp.min(jnp.where(g == best, blk, float(nb)), axis=0, keepdims=True)
        pick = blk == first
        chosen = pick if chosen is None else (chosen | pick)
        g = jnp.where(pick, -jnp.inf, g)
    return (chosen & past) | own


def _inproj_kernel(x_ref, mod_ref, g_ref, wqk_hi_ref, wqk_lo_ref, wvu_ref, cos_ref, sin_ref,
                   qt_ref, ka_ref, vt_ref, u_ref, ut_ref, km_scr):
    i = pl.program_id(1)
    x = x_ref[0]
    tm, d = x.shape
    nb = km_scr.shape[0]
    sh1 = mod_ref[0, :, 0:d]
    sc1 = mod_ref[0, :, d:2 * d]
    h = _rmsnorm(x, g_ref[...]) * (1.0 + sc1) + sh1
    h_hi, h_lo = _split_bf16(h)
    qk = (_dot(h_hi, wqk_hi_ref[...]) + _dot(h_lo, wqk_hi_ref[...])
          + _dot(h_hi, wqk_lo_ref[...]))
    vu = _dot(h_hi, wvu_ref[...])
    u = vu[:, ATTN_WIDTH:]
    u_ref[0] = u
    ut_ref[0] = u.T.astype(BF16)

    reps = ATTN_WIDTH // LANES
    cos = jnp.concatenate([cos_ref[...]] * reps, axis=1)
    sin = jnp.concatenate([sin_ref[...]] * reps, axis=1)
    wide_lane = lax.broadcasted_iota(jnp.int32, (tm, ATTN_WIDTH), 1)
    first_half = (wide_lane & (HEAD_DIM // 2)) == 0

    def rope(t):
        partner = jnp.where(first_half,
                            pltpu.roll(t, ATTN_WIDTH - HEAD_DIM // 2, 1),
                            pltpu.roll(t, HEAD_DIM // 2, 1))
        return t * cos + partner * sin

    q = rope(qk[:, :ATTN_WIDTH])
    k = rope(qk[:, ATTN_WIDTH:])
    v = vu[:, :ATTN_WIDTH]

    @pl.when(i == 0)
    def _():
        km_scr[...] = jnp.zeros_like(km_scr)

    km_scr[pl.ds(i, 1), :] = jnp.sum(k, axis=0, keepdims=True) * (1.0 / tm)
    km_all = km_scr[...]

    lane = lax.broadcasted_iota(jnp.int32, (tm, LANES), 1)
    km_lane = lax.broadcasted_iota(jnp.int32, (nb, LANES), 1)
    blk_i = lax.broadcasted_iota(jnp.int32, (nb, tm), 0)
    blk = blk_i.astype(F32)
    past = blk_i < i
    own = blk_i == i
    ones_rows = jnp.where(lax.broadcasted_iota(jnp.int32, (VT_ROWS - HEAD_DIM, tm), 0) == 0, 1.0, 0.0)
    onehot = jnp.where(lane == HEAD_DIM + i, 1.0, 0.0)
    scale = HEAD_DIM ** -0.5 * math.log2(math.e)
    for tile in range(ATTN_WIDTH // LANES):
        cols = slice(tile * LANES, (tile + 1) * LANES)
        q_t = q[:, cols].T
        v_t = v[:, cols].T
        k_tile = k[:, cols]
        km = km_all[:, cols]
        qt_hi, qt_lo = _split_bf16(q_t)
        for hh in range(HEADS_PER_TILE):
            h_idx = tile * HEADS_PER_TILE + hh
            kmh_hi, kmh_lo = _split_bf16(jnp.where(km_lane // HEAD_DIM == hh, km, 0.0))
            gate = _dot(kmh_hi, qt_hi) + _dot(kmh_lo, qt_hi) + _dot(kmh_hi, qt_lo)
            bias_t = jnp.where(_top_blocks(gate, blk, past, own), 0.0, MASK_VALUE)
            head_rows = slice(hh * HEAD_DIM, (hh + 1) * HEAD_DIM)
            qt_ref[0, h_idx, 0] = jnp.concatenate(
                [q_t[head_rows] * scale, bias_t, jnp.zeros((LANES - HEAD_DIM - nb, tm), F32)],
                axis=0).astype(BF16)
            vt_ref[0, h_idx, 0] = jnp.concatenate([v_t[head_rows], ones_rows], axis=0).astype(BF16)
            kh = jnp.where(lane // HEAD_DIM == hh, k_tile, 0.0)
            if hh:
                kh = pltpu.roll(kh, LANES - hh * HEAD_DIM, 1)
            ka_ref[0, h_idx] = jnp.where(lane < HEAD_DIM, kh, onehot).astype(BF16)


def _inproj(x, mod3, g_mix, wqk_hi, wqk_lo, wvu, cos_t, sin_t):
    bsz, seq, d = x.shape
    tm = MOBA_BLOCK
    nb = seq // tm
    assert nb % SUBLANES == 0 and nb <= LANES - HEAD_DIM
    const = lambda b, t: (0, 0)
    return pl.pallas_call(
        _inproj_kernel,
        grid=(bsz, nb),
        in_specs=[pl.BlockSpec((1, tm, d), lambda b, t: (b, t, 0)),
                  pl.BlockSpec((1, 1, mod3.shape[-1]), lambda b, t: (b, 0, 0)),
                  pl.BlockSpec((1, d), const),
                  pl.BlockSpec(wqk_hi.shape, const),
                  pl.BlockSpec(wqk_lo.shape, const),
                  pl.BlockSpec(wvu.shape, const),
                  pl.BlockSpec((tm, LANES), lambda b, t: (t, 0)),
                  pl.BlockSpec((tm, LANES), lambda b, t: (t, 0))],
        out_specs=[pl.BlockSpec((1, ATTN_HEADS, 1, LANES, tm), lambda b, t: (b, 0, t, 0, 0)),
                   pl.BlockSpec((1, ATTN_HEADS, tm, LANES), lambda b, t: (b, 0, t, 0)),
                   pl.BlockSpec((1, ATTN_HEADS, 1, VT_ROWS, tm), lambda b, t: (b, 0, t, 0, 0)),
                   pl.BlockSpec((1, tm, SSM_WIDTH), lambda b, t: (b, t, 0)),
                   pl.BlockSpec((1, SSM_WIDTH, tm), lambda b, t: (b, 0, t))],
        out_shape=[jax.ShapeDtypeStruct((bsz, ATTN_HEADS, nb, LANES, tm), BF16),
                   jax.ShapeDtypeStruct((bsz, ATTN_HEADS, seq, LANES), BF16),
                   jax.ShapeDtypeStruct((bsz, ATTN_HEADS, nb, VT_ROWS, tm), BF16),
                   jax.ShapeDtypeStruct((bsz, seq, SSM_WIDTH), F32),
                   jax.ShapeDtypeStruct((bsz, SSM_WIDTH, seq), BF16)],
        scratch_shapes=[pltpu.VMEM((nb, ATTN_WIDTH), F32)],
        compiler_params=_params("arbitrary", "arbitrary"),
        name="inproj",
    )(x, mod3, g_mix, wqk_hi, wqk_lo, wvu, cos_t, sin_t)


ATTN_HEADS_PER_STEP = 8


RUNNING_MAX_INIT = -3.0e38


def _attn_kernel(qt_ref, ka_ref, vt_ref, o_ref, sa_scr, sb_scr, m_scr, acc_scr):
    i = pl.program_id(1)
    nh = qt_ref.shape[1]
    tq = qt_ref.shape[4]
    key = lax.broadcasted_iota(jnp.int32, (tq, tq), 0)
    qry = lax.broadcasted_iota(jnp.int32, (tq, tq), 1)
    causal = key <= qry

    def scores(h, j, s_scr, mask):
        start = pl.multiple_of(j * tq, tq)
        s = _dot(ka_ref[0, h, pl.ds(start, tq), :], qt_ref[0, h, 0])
        if mask is not None:
            s = jnp.where(mask, s, MASK_VALUE)
        s_scr[h] = s

    def consume(h, j, s_scr):
        s = s_scr[h]
        m_old = m_scr[h]
        m_new = jnp.maximum(m_old, jnp.max(s, axis=0, keepdims=True))
        alpha = jnp.exp2(m_old - m_new)
        p = jnp.exp2(s - m_new).astype(BF16)
        acc_scr[h] = alpha * acc_scr[h] + _dot(vt_ref[0, h, j], p)
        m_scr[h] = m_new

    def stage(j_next, next_scr, mask, j_done, done_scr):
        for h in range(nh):
            scores(h, j_next, next_scr, mask)
            consume(h, j_done, done_scr)

    m_scr[...] = jnp.full(m_scr.shape, RUNNING_MAX_INIT, F32)
    acc_scr[...] = jnp.zeros(acc_scr.shape, F32)
    pairs = (i + 2) // 2
    for h in range(nh):
        scores(h, i, sa_scr, causal)

    def body(t, carry):
        stage(2 * t, sb_scr, None, jnp.where(t == 0, i, 2 * t - 1), sa_scr)
        stage(2 * t + 1, sa_scr, None, 2 * t, sb_scr)
        return carry

    lax.fori_loop(0, pairs - 1, body, 0)
    tail = 2 * pairs - 2
    real = tail < i
    tail_blk = jnp.where(real, tail, i)
    stage(tail_blk, sb_scr, jnp.broadcast_to(real, (tq, tq)),
          jnp.where(pairs == 1, i, tail - 1), sa_scr)
    for h in range(nh):
        consume(h, tail_blk, sb_scr)

    for vt in range(nh // HEADS_PER_TILE):
        rows = []
        for hh in range(HEADS_PER_TILE):
            acc = acc_scr[vt * HEADS_PER_TILE + hh]
            rows.append(acc[0:HEAD_DIM] / acc[HEAD_DIM:HEAD_DIM + 1])
        o_ref[0, :, vt * LANES:(vt + 1) * LANES] = jnp.concatenate(rows, axis=0).T


def _attention(q_t, k_aug, v_t):
    bsz, _, nb, _, tq = q_t.shape
    seq = nb * tq
    nh = ATTN_HEADS_PER_STEP
    groups = ATTN_HEADS // nh
    ow = (nh // HEADS_PER_TILE) * LANES
    resident = pl.Buffered(1)
    return pl.pallas_call(
        _attn_kernel,
        grid=(bsz * groups, nb),
        in_specs=[pl.BlockSpec((1, nh, 1, LANES, tq), lambda g, i: (g // groups, g % groups, i, 0, 0)),
                  pl.BlockSpec((1, nh, seq, LANES), lambda g, i: (g // groups, g % groups, 0, 0),
                               pipeline_mode=resident),
                  pl.BlockSpec((1, nh, nb, VT_ROWS, tq), lambda g, i: (g // groups, g % groups, 0, 0, 0),
                               pipeline_mode=resident)],
        out_specs=pl.BlockSpec((1, tq, ow), lambda g, i: (g // groups, i, g % groups)),
        out_shape=jax.ShapeDtypeStruct((bsz, seq, ATTN_WIDTH), F32),
        scratch_shapes=[pltpu.VMEM((nh, tq, tq), F32), pltpu.VMEM((nh, tq, tq), F32), pltpu.VMEM((nh, 1, tq), F32),
                        pltpu.VMEM((nh, VT_ROWS, tq), F32)],
        compiler_params=_params("arbitrary", "arbitrary"),
        name="attn",
    )(q_t, k_aug, v_t)


S5_STEPS = 64
S5_CHUNKS = 4
S5_CHUNK_GROUPS = SSM_GROUPS // S5_CHUNKS
S5_CHUNK_STATES = S5_CHUNK_GROUPS * SSM_STATE
S5_CHUNK_LANES = S5_CHUNK_GROUPS * SSM_GROUP_CH


S5_CARRY_GROUPS = LANES // SSM_STATE


def _cmul(ar, ai, br, bi):
    return ar * br - ai * bi, ar * bi + ai * br


def _s5_carry_kernel(ut_ref, lam_ref, bt_ref, o_ref):
    bsz, nrows, seq = ut_ref.shape
    nch = nrows // S5_CARRY_GROUPS
    ls = seq // N_SEG
    reps = bsz * N_SEG
    sub = lax.broadcasted_iota(jnp.int32, (SUBLANES, LANES), 0)
    lr = jnp.broadcast_to(lam_ref[0, 0:1], (SUBLANES, LANES))
    li = jnp.broadcast_to(lam_ref[0, 1:2], (SUBLANES, LANES))
    pr = jnp.ones((SUBLANES, LANES), F32)
    pi = jnp.zeros((SUBLANES, LANES), F32)
    tr = jnp.zeros((SUBLANES, LANES), F32)
    ti = jnp.zeros((SUBLANES, LANES), F32)
    for r in range(SUBLANES):
        tr = jnp.where(sub == SUBLANES - 1 - r, pr, tr)
        ti = jnp.where(sub == SUBLANES - 1 - r, pi, ti)
        pr, pi = _cmul(pr, pi, lr, li)
    span = SUBLANES
    while span < ls:
        nr, ni = _cmul(tr, ti, pr[0:1], pi[0:1])
        tr = jnp.concatenate([nr, tr], axis=0)
        ti = jnp.concatenate([ni, ti], axis=0)
        pr, pi = _cmul(pr, pi, pr, pi)
        span *= 2
    lhs = jnp.concatenate([ut_ref[b, g * nch:(g + 1) * nch, s * ls:(s + 1) * ls]
                           for g in range(S5_CARRY_GROUPS) for b in range(bsz) for s in range(N_SEG)],
                          axis=0)
    gr = _dot(lhs, tr.astype(BF16))
    gi = _dot(lhs, ti.astype(BF16))
    btr = jnp.concatenate([bt_ref[0, 0]] * (reps * S5_CARRY_GROUPS), axis=0)
    bti = jnp.concatenate([bt_ref[0, 1]] * (reps * S5_CARRY_GROUPS), axis=0)
    wr, wi = _cmul(gr, gi, btr, bti)
    fr = wr.reshape(reps * S5_CARRY_GROUPS, nch, LANES).sum(axis=1)
    fi = wi.reshape(reps * S5_CARRY_GROUPS, nch, LANES).sum(axis=1)
    grp = lax.broadcasted_iota(jnp.int32, (reps, LANES), 1) // SSM_STATE
    fin_r, fin_i = fr[0:reps], fi[0:reps]
    for g in range(1, S5_CARRY_GROUPS):
        fin_r = jnp.where(grp == g, fr[g * reps:(g + 1) * reps], fin_r)
        fin_i = jnp.where(grp == g, fi[g * reps:(g + 1) * reps], fin_i)
    rows_r, rows_i = [], []
    for b in range(bsz):
        sr = jnp.zeros((1, LANES), F32)
        si = jnp.zeros((1, LANES), F32)
        rows_r.append(sr)
        rows_i.append(si)
        for seg in range(1, N_SEG):
            r = b * N_SEG + seg - 1
            mr, mi = _cmul(sr, si, pr[0:1], pi[0:1])
            sr, si = mr + fin_r[r:r + 1], mi + fin_i[r:r + 1]
            rows_r.append(sr)
            rows_i.append(si)
    o_ref[0, 0] = jnp.concatenate(rows_r, axis=0)
    o_ref[0, 1] = jnp.concatenate(rows_i, axis=0)


def _s5_main_kernel(u_ref, bc_ref, cre_ref, cim_ref, lam_ref, sinit_ref, d_ref, wglu_ref, bglu_ref, o_ref,
                    il_scr, bu_scr, st_scr, y_scr):
    jb = pl.program_id(1)
    nseg, steps, width = u_ref.shape
    cs = S5_CHUNK_STATES

    @pl.when(jb == 0)
    def _():
        st_scr[...] = sinit_ref[...]

    for j in range(steps):
        il_scr[j * nseg:(j + 1) * nseg, :] = u_ref[:, j, :]
    u = il_scr[...]
    u_bf = u.astype(BF16)
    for c in range(S5_CHUNKS):
        bu_scr[c] = _dot(u_bf[:, c * S5_CHUNK_LANES:(c + 1) * S5_CHUNK_LANES], bc_ref[c])
    ys = []
    for c in range(S5_CHUNKS):
        lr = jnp.broadcast_to(lam_ref[c, 0:1, :], (nseg, cs))
        li = jnp.broadcast_to(lam_ref[c, 1:2, :], (nseg, cs))
        xr = st_scr[c, :, 0:cs]
        xi = st_scr[c, :, cs:2 * cs]
        for j in range(steps):
            rows = slice(j * nseg, (j + 1) * nseg)
            xr, xi = (lr * xr - li * xi + bu_scr[c, rows, 0:cs],
                      lr * xi + li * xr + bu_scr[c, rows, cs:2 * cs])
            bu_scr[c, rows, 0:cs] = xr
            bu_scr[c, rows, cs:2 * cs] = xi
        st_scr[c, :, 0:cs] = xr
        st_scr[c, :, cs:2 * cs] = xi
        ys.append(_dot(bu_scr[c, :, 0:cs].astype(BF16), cre_ref[c])
                  + _dot(bu_scr[c, :, cs:2 * cs].astype(BF16), cim_ref[c]))
    y = jnp.concatenate(ys, axis=1) + d_ref[...] * u
    y = y * (0.5 * (1.0 + jnp.tanh(math.sqrt(2.0 / math.pi) * (y + 0.044715 * (y * y * y)))))
    z = _dot(y.astype(BF16), wglu_ref[...]) + bglu_ref[...]
    y = y * jax.nn.sigmoid(z)
    for t in range(width // LANES):
        y_scr[t] = y[:, t * LANES:(t + 1) * LANES]
    for s in range(nseg):
        for t in range(width // LANES):
            o_ref[s, :, t * LANES:(t + 1) * LANES] = y_scr[t, pl.ds(s, steps, stride=nseg), :]


def _s5(u, u_t, prm, d_skip, w_glu, b_glu):
    bsz, seq, _ = u.shape
    ls = seq // N_SEG
    steps = S5_STEPS
    rows = steps * N_SEG
    ngrp = SSM_GROUPS // S5_CARRY_GROUPS
    carry = pl.pallas_call(
        _s5_carry_kernel,
        grid=(ngrp,),
        in_specs=[pl.BlockSpec((bsz, S5_CARRY_GROUPS * SSM_GROUP_CH, seq), lambda g: (0, g, 0)),
                  pl.BlockSpec((1, 2, LANES), lambda g: (g, 0, 0)),
                  pl.BlockSpec((1, 2, SSM_GROUP_CH, LANES), lambda g: (g, 0, 0, 0))],
        out_specs=pl.BlockSpec((1, 2, bsz * N_SEG, LANES), lambda g: (g, 0, 0, 0)),
        out_shape=jax.ShapeDtypeStruct((ngrp, 2, bsz * N_SEG, LANES), F32),
        compiler_params=_params("arbitrary"),
        name="s5_carry",
    )(u_t, prm["lam_carry"], prm["bt_carry"])
    sinit = carry.reshape(S5_CHUNKS, ngrp // S5_CHUNKS, 2, bsz, N_SEG, LANES)
    sinit = sinit.transpose(3, 0, 4, 2, 1, 5).reshape(bsz, S5_CHUNKS, N_SEG, 2 * S5_CHUNK_STATES)

    def const(shape):
        return pl.BlockSpec(shape, lambda b, j: (0,) * len(shape))

    blk = pl.BlockSpec((None, N_SEG, steps, SSM_WIDTH), lambda b, j: (b, 0, j, 0))
    out = pl.pallas_call(
        _s5_main_kernel,
        grid=(bsz, ls // steps),
        in_specs=[blk, const(prm["bc"].shape), const(prm["cre"].shape), const(prm["cim"].shape),
                  const(prm["lam_chunk"].shape),
                  pl.BlockSpec((None, S5_CHUNKS, N_SEG, 2 * S5_CHUNK_STATES), lambda b, j: (b, 0, 0, 0)),
                  const(d_skip.shape), const(w_glu.shape), const(b_glu.shape)],
        out_specs=blk,
        out_shape=jax.ShapeDtypeStruct((bsz, N_SEG, ls, SSM_WIDTH), F32),
        scratch_shapes=[pltpu.VMEM((rows, SSM_WIDTH), F32),
                        pltpu.VMEM((S5_CHUNKS, rows, 2 * S5_CHUNK_STATES), F32),
                        pltpu.VMEM((S5_CHUNKS, N_SEG, 2 * S5_CHUNK_STATES), F32),
                        pltpu.VMEM((SSM_WIDTH // LANES, rows, LANES), F32)],
        compiler_params=_params("arbitrary", "arbitrary"),
        name="s5_main",
    )(u.reshape(bsz, N_SEG, ls, SSM_WIDTH), prm["bc"], prm["cre"], prm["cim"], prm["lam_chunk"], sinit,
      d_skip, w_glu, b_glu)
    return out.reshape(bsz, seq, SSM_WIDTH)


FF_CHUNK = 1024


def _outmlp_kernel(x_ref, attn_ref, ssm_ref, mod_ref, ga_ref, gs_ref, gm_ref, gf_ref,
                   wout_ref, w1_ref, w2_ref, o_ref, *, final_norm):
    x = x_ref[0]
    tm, d = x.shape
    gt1 = mod_ref[0, :, 2 * d:3 * d]
    sh2 = mod_ref[0, :, 3 * d:4 * d]
    sc2 = mod_ref[0, :, 4 * d:5 * d]
    gt2 = mod_ref[0, :, 5 * d:6 * d]
    a = _rmsnorm(attn_ref[0], ga_ref[...]).astype(BF16)
    s = _rmsnorm(ssm_ref[0], gs_ref[...]).astype(BF16)
    mixed = _dot(a, wout_ref[0:ATTN_WIDTH, :]) + _dot(s, wout_ref[ATTN_WIDTH:, :])
    x1 = x + gt1 * mixed
    h = (_rmsnorm(x1, gm_ref[...]) * (1.0 + sc2) + sh2).astype(BF16)
    ff = w1_ref.shape[1]
    acc = jnp.zeros((tm, d), F32)
    for c in range(ff // FF_CHUNK):
        t = jnp.maximum(_dot(h, w1_ref[:, c * FF_CHUNK:(c + 1) * FF_CHUNK]), 0.0)
        acc = acc + _dot((t * t).astype(BF16), w2_ref[c * FF_CHUNK:(c + 1) * FF_CHUNK, :])
    x2 = x1 + gt2 * acc
    o_ref[0] = _rmsnorm(x2, gf_ref[...]) if final_norm else x2


def _outmlp(x, attn, ssm, mod3, g_attn, g_ssm, g_mlp, g_final, w_out, w_fc1, w_fc2, final_norm):
    bsz, seq, d = x.shape
    tm = 256
    const = lambda b, t: (0, 0)
    return pl.pallas_call(
        functools.partial(_outmlp_kernel, final_norm=final_norm),
        grid=(bsz, seq // tm),
        in_specs=[pl.BlockSpec((1, tm, d), lambda b, t: (b, t, 0)),
                  pl.BlockSpec((1, tm, ATTN_WIDTH), lambda b, t: (b, t, 0)),
                  pl.BlockSpec((1, tm, SSM_WIDTH), lambda b, t: (b, t, 0)),
                  pl.BlockSpec((1, 1, mod3.shape[-1]), lambda b, t: (b, 0, 0)),
                  pl.BlockSpec(g_attn.shape, const),
                  pl.BlockSpec(g_ssm.shape, const),
                  pl.BlockSpec(g_mlp.shape, const),
                  pl.BlockSpec(g_final.shape, const),
                  pl.BlockSpec(w_out.shape, const),
                  pl.BlockSpec(w_fc1.shape, const),
                  pl.BlockSpec(w_fc2.shape, const)],
        out_specs=pl.BlockSpec((1, tm, d), lambda b, t: (b, t, 0)),
        out_shape=jax.ShapeDtypeStruct((bsz, seq, d), F32),
        compiler_params=_params("arbitrary", "arbitrary"),
        name="outmlp",
    )(x, attn, ssm, mod3, g_attn, g_ssm, g_mlp, g_final, w_out, w_fc1, w_fc2)


def _rope_tables(seq):
    half = HEAD_DIM // 2
    inv_freq = ROPE_THETA ** (-jnp.arange(half, dtype=F32) / half)
    ang = jnp.arange(seq, dtype=F32)[:, None] * inv_freq[None, :]
    cos, sin = jnp.cos(ang), jnp.sin(ang)
    reps = LANES // HEAD_DIM
    cos_t = jnp.tile(cos, (1, 2 * reps))
    sin_t = jnp.tile(jnp.concatenate([-sin, sin], axis=1), (1, reps))
    return cos_t, sin_t


def _s5_params(lam_re, lam_im, log_dt, b_re, b_im, c_re, c_im):
    lr, li = lam_re.astype(F32), lam_im.astype(F32)
    dt = jnp.exp(log_dt.astype(F32))[:, None]
    mag = jnp.exp(lr * dt)
    ar, ai = mag * jnp.cos(li * dt), mag * jnp.sin(li * dt)
    den = lr * lr + li * li
    cr = ((ar - 1.0) * lr + ai * li) / den
    ci = (ai * lr - (ar - 1.0) * li) / den
    bbar_re = cr[..., None] * b_re.astype(F32) - ci[..., None] * b_im.astype(F32)
    bbar_im = cr[..., None] * b_im.astype(F32) + ci[..., None] * b_re.astype(F32)
    eye = jnp.eye(S5_CHUNK_GROUPS, dtype=F32)

    def chunked(m):
        return m.reshape((S5_CHUNKS, S5_CHUNK_GROUPS) + m.shape[1:])

    def diag_in(m):
        return jnp.einsum('cgpn,gh->cgnhp', chunked(m), eye).reshape(S5_CHUNKS, S5_CHUNK_LANES, S5_CHUNK_STATES)

    def diag_out(m):
        return jnp.einsum('cgnp,gh->cgphn', chunked(m), eye).reshape(S5_CHUNKS, S5_CHUNK_STATES, S5_CHUNK_LANES)

    ngrp = SSM_GROUPS // S5_CARRY_GROUPS

    def lane_tiles(m):
        m = m.reshape(ngrp, S5_CARRY_GROUPS, SSM_GROUP_CH, SSM_STATE)
        return m.transpose(0, 2, 1, 3).reshape(ngrp, SSM_GROUP_CH, LANES)

    return {
        "bc": jnp.concatenate([diag_in(bbar_re), diag_in(bbar_im)], axis=2).astype(BF16),
        "cre": diag_out(c_re.astype(F32)).astype(BF16),
        "cim": diag_out(-c_im.astype(F32)).astype(BF16),
        "lam_chunk": jnp.stack([ar.reshape(S5_CHUNKS, S5_CHUNK_STATES),
                                ai.reshape(S5_CHUNKS, S5_CHUNK_STATES)], axis=1),
        "lam_carry": jnp.stack([ar.reshape(ngrp, LANES), ai.reshape(ngrp, LANES)], axis=1),
        "bt_carry": jnp.stack([lane_tiles(bbar_re.transpose(0, 2, 1)),
                               lane_tiles(bbar_im.transpose(0, 2, 1))], axis=1),
    }


def kernel(x, c, w_ada, b_ada, g_mix, w_in, g_attn_out, lam_re, lam_im, log_dt, b_re, b_im, c_re, c_im,
           d_skip, w_glu, b_glu, g_ssm_out, w_out, g_mlp, w_fc1, w_fc2, g_final):
    bsz, seq, d = x.shape
    depth = w_ada.shape[0]
    assert seq % (N_SEG * MOBA_BLOCK) == 0 and (seq // N_SEG) % S5_STEPS == 0 and bsz <= SUBLANES
    cos_t, sin_t = _rope_tables(seq)
    c_pad = jnp.zeros((SUBLANES, d), F32).at[:bsz].set(c.astype(F32))

    for l in range(depth):
        mod = _adaln(c_pad, w_ada[l], b_ada[l][None, :])
        mod3 = mod[:bsz, None, :]

        wqk_hi, wqk_lo = _split_bf16(w_in[l][:, :2 * ATTN_WIDTH])
        wvu = w_in[l][:, 2 * ATTN_WIDTH:].astype(BF16)
        q_t, k_aug, v_t, u, u_t = _inproj(x, mod3, g_mix[l][None, :], wqk_hi, wqk_lo, wvu, cos_t, sin_t)
        attn = _attention(q_t, k_aug, v_t)

        prm = _s5_params(lam_re[l], lam_im[l], log_dt[l], b_re[l], b_im[l], c_re[l], c_im[l])
        ssm = _s5(u, u_t, prm, d_skip[l].reshape(1, SSM_WIDTH), w_glu[l].astype(BF16), b_glu[l][None, :])

        x = _outmlp(x, attn, ssm, mod3, g_attn_out[l][None, :], g_ssm_out[l][None, :], g_mlp[l][None, :],
                    g_final[None, :], w_out[l].astype(BF16), w_fc1[l].astype(BF16), w_fc2[l].astype(BF16),
                    final_norm=(l == depth - 1))
    return x
```

```python
import functools
import math

import jax
import jax.numpy as jnp
from jax import lax
from jax.experimental import pallas as pl
from jax.experimental.pallas import tpu as pltpu

F32 = jnp.float32
BF16 = jnp.bfloat16

HEAD_DIM = 64
ATTN_HEADS = 8
ATTN_WIDTH = ATTN_HEADS * HEAD_DIM
SSM_GROUPS = 32
SSM_GROUP_CH = 16
SSM_WIDTH = SSM_GROUPS * SSM_GROUP_CH
SSM_STATE = 64
SSM_NSTATE = SSM_GROUPS * SSM_STATE
MOBA_BLOCK = 256
MOBA_TOPK = 3
ROPE_THETA = 10000.0
EPS = 1e-6

LANES = 128
SUBLANES = 8
N_SEG = SUBLANES
HEADS_PER_TILE = LANES // HEAD_DIM
MASK_VALUE = -(2.0 ** 100)
VMEM_LIMIT_BYTES = 56 * 1024 * 1024


def _split_bf16(a):
    hi = a.astype(BF16)
    lo = (a - hi.astype(F32)).astype(BF16)
    return hi, lo


def _dot(a, b):
    return jnp.dot(a, b, preferred_element_type=F32)


def _rmsnorm(x, g):
    return x * lax.rsqrt(jnp.mean(x * x, axis=-1, keepdims=True) + EPS) * g


def _params(*semantics):
    return pltpu.CompilerParams(dimension_semantics=semantics, vmem_limit_bytes=VMEM_LIMIT_BYTES)


def _adaln_kernel(c_ref, w_ref, b_ref, o_ref):
    c = c_ref[...]
    s_hi, s_lo = _split_bf16(c * jax.nn.sigmoid(c))
    w_hi, w_lo = _split_bf16(w_ref[...])
    o_ref[...] = _dot(s_hi, w_hi) + _dot(s_lo, w_hi) + _dot(s_hi, w_lo) + b_ref[...]


def _adaln(c_pad, w, b):
    rows, d = c_pad.shape
    n = w.shape[1]
    tn = n // 4
    return pl.pallas_call(
        _adaln_kernel,
        grid=(n // tn,),
        in_specs=[pl.BlockSpec((rows, d), lambda j: (0, 0)),
                  pl.BlockSpec((d, tn), lambda j: (0, j)),
                  pl.BlockSpec((1, tn), lambda j: (0, j))],
        out_specs=pl.BlockSpec((rows, tn), lambda j: (0, j)),
        out_shape=jax.ShapeDtypeStruct((rows, n), F32),
        compiler_params=_params("arbitrary"),
        name="adaln",
    )(c_pad, w, b)


VT_ROWS = HEAD_DIM + 16


def _top_blocks(gate, blk, past, own):
    nb = gate.shape[0]
    g = jnp.where(past, gate, -jnp.inf)
    chosen = None
    for _ in range(MOBA_TOPK):
        best = jnp.max(g, axis=0, keepdims=True)
        first = jnp.min(jnp.where(g == best, blk, float(nb)), axis=0, keepdims=True)
        pick = blk == first
        chosen = pick if chosen is None else (chosen | pick)
        g = jnp.where(pick, -jnp.inf, g)
    return (chosen & past) | own


def _inproj_kernel(x_ref, mod_ref, g_ref, wqk_hi_ref, wqk_lo_ref, wvu_ref, cos_ref, sin_ref,
                   qt_ref, ka_ref, vt_ref, u_ref, ut_ref, km_scr):
    i = pl.program_id(1)
    x = x_ref[0]
    tm, d = x.shape
    nb = km_scr.shape[0]
    sh1 = mod_ref[0, :, 0:d]
    sc1 = mod_ref[0, :, d:2 * d]
    h = _rmsnorm(x, g_ref[...]) * (1.0 + sc1) + sh1
    h_hi, h_lo = _split_bf16(h)
    qk = (_dot(h_hi, wqk_hi_ref[...]) + _dot(h_lo, wqk_hi_ref[...])
          + _dot(h_hi, wqk_lo_ref[...]))
    vu = _dot(h_hi, wvu_ref[...])
    u = vu[:, ATTN_WIDTH:]
    u_ref[0] = u
    ut_ref[0] = u.T.astype(BF16)

    reps = ATTN_WIDTH // LANES
    cos = jnp.concatenate([cos_ref[...]] * reps, axis=1)
    sin = jnp.concatenate([sin_ref[...]] * reps, axis=1)
    wide_lane = lax.broadcasted_iota(jnp.int32, (tm, ATTN_WIDTH), 1)
    first_half = (wide_lane & (HEAD_DIM // 2)) == 0

    def rope(t):
        partner = jnp.where(first_half,
                            pltpu.roll(t, ATTN_WIDTH - HEAD_DIM // 2, 1),
                            pltpu.roll(t, HEAD_DIM // 2, 1))
        return t * cos + partner * sin

    q = rope(qk[:, :ATTN_WIDTH])
    k = rope(qk[:, ATTN_WIDTH:])
    v = vu[:, :ATTN_WIDTH]

    @pl.when(i == 0)
    def _():
        km_scr[...] = jnp.zeros_like(km_scr)

    km_scr[pl.ds(i, 1), :] = jnp.sum(k, axis=0, keepdims=True) * (1.0 / tm)
    km_all = km_scr[...]

    lane = lax.broadcasted_iota(jnp.int32, (tm, LANES), 1)
    km_lane = lax.broadcasted_iota(jnp.int32, (nb, LANES), 1)
    blk_i = lax.broadcasted_iota(jnp.int32, (nb, tm), 0)
    blk = blk_i.astype(F32)
    past = blk_i < i
    own = blk_i == i
    ones_rows = jnp.where(lax.broadcasted_iota(jnp.int32, (VT_ROWS - HEAD_DIM, tm), 0) == 0, 1.0, 0.0)
    onehot = jnp.where(lane == HEAD_DIM + i, 1.0, 0.0)
    scale = HEAD_DIM ** -0.5 * math.log2(math.e)
    for tile in range(ATTN_WIDTH // LANES):
        cols = slice(tile * LANES, (tile + 1) * LANES)
        q_t = q[:, cols].T
        v_t = v[:, cols].T
        k_tile = k[:, cols]
        km = km_all[:, cols]
        qt_hi, qt_lo = _split_bf16(q_t)
        for hh in range(HEADS_PER_TILE):
            h_idx = tile * HEADS_PER_TILE + hh
            kmh_hi, kmh_lo = _split_bf16(jnp.where(km_lane // HEAD_DIM == hh, km, 0.0))
            gate = _dot(kmh_hi, qt_hi) + _dot(kmh_lo, qt_hi) + _dot(kmh_hi, qt_lo)
            bias_t = jnp.where(_top_blocks(gate, blk, past, own), 0.0, MASK_VALUE)
            head_rows = slice(hh * HEAD_DIM, (hh + 1) * HEAD_DIM)
            qt_ref[0, h_idx, 0] = jnp.concatenate(
                [q_t[head_rows] * scale, bias_t, jnp.zeros((LANES - HEAD_DIM - nb, tm), F32)],
                axis=0).astype(BF16)
            vt_ref[0, h_idx, 0] = jnp.concatenate([v_t[head_rows], ones_rows], axis=0).astype(BF16)
            kh = jnp.where(lane // HEAD_DIM == hh, k_tile, 0.0)
            if hh:
                kh = pltpu.roll(kh, LANES - hh * HEAD_DIM, 1)
            ka_ref[0, h_idx] = jnp.where(lane < HEAD_DIM, kh, onehot).astype(BF16)


def _inproj(x, mod3, g_mix, wqk_hi, wqk_lo, wvu, cos_t, sin_t):
    bsz, seq, d = x.shape
    tm = MOBA_BLOCK
    nb = seq // tm
    assert nb % SUBLANES == 0 and nb <= LANES - HEAD_DIM
    const = lambda b, t: (0, 0)
    return pl.pallas_call(
        _inproj_kernel,
        grid=(bsz, nb),
        in_specs=[pl.BlockSpec((1, tm, d), lambda b, t: (b, t, 0)),
                  pl.BlockSpec((1, 1, mod3.shape[-1]), lambda b, t: (b, 0, 0)),
                  pl.BlockSpec((1, d), const),
                  pl.BlockSpec(wqk_hi.shape, const),
                  pl.BlockSpec(wqk_lo.shape, const),
                  pl.BlockSpec(wvu.shape, const),
                  pl.BlockSpec((tm, LANES), lambda b, t: (t, 0)),
                  pl.BlockSpec((tm, LANES), lambda b, t: (t, 0))],
        out_specs=[pl.BlockSpec((1, ATTN_HEADS, 1, LANES, tm), lambda b, t: (b, 0, t, 0, 0)),
                   pl.BlockSpec((1, ATTN_HEADS, tm, LANES), lambda b, t: (b, 0, t, 0)),
                   pl.BlockSpec((1, ATTN_HEADS, 1, VT_ROWS, tm), lambda b, t: (b, 0, t, 0, 0)),
                   pl.BlockSpec((1, tm, SSM_WIDTH), lambda b, t: (b, t, 0)),
                   pl.BlockSpec((1, SSM_WIDTH, tm), lambda b, t: (b, 0, t))],
        out_shape=[jax.ShapeDtypeStruct((bsz, ATTN_HEADS, nb, LANES, tm), BF16),
                   jax.ShapeDtypeStruct((bsz, ATTN_HEADS, seq, LANES), BF16),
                   jax.ShapeDtypeStruct((bsz, ATTN_HEADS, nb, VT_ROWS, tm), BF16),
                   jax.ShapeDtypeStruct((bsz, seq, SSM_WIDTH), F32),
                   jax.ShapeDtypeStruct((bsz, SSM_WIDTH, seq), BF16)],
        scratch_shapes=[pltpu.VMEM((nb, ATTN_WIDTH), F32)],
        compiler_params=_params("arbitrary", "arbitrary"),
        name="inproj",
    )(x, mod3, g_mix, wqk_hi, wqk_lo, wvu, cos_t, sin_t)


ATTN_QBLOCKS = 2
RUNNING_MAX_INIT = -3.0e38


def _attn_kernel(qt_ref, ka_ref, vt_ref, o_ref, sa_scr, sb_scr, m_scr, acc_scr):
    pair = pl.program_id(1)
    i = ATTN_QBLOCKS * pair
    nh = qt_ref.shape[1]
    tq = qt_ref.shape[4]
    key = lax.broadcasted_iota(jnp.int32, (tq, tq), 0)
    qry = lax.broadcasted_iota(jnp.int32, (tq, tq), 1)
    causal = key <= qry

    def scores(qb, h, j, s_scr, mask=None):
        start = pl.multiple_of(j * tq, tq)
        s = _dot(ka_ref[0, h, pl.ds(start, tq), :], qt_ref[0, h, qb])
        if mask is not None:
            s = jnp.where(mask, s, MASK_VALUE)
        s_scr[qb * nh + h] = s

    def consume(qb, h, j, s_scr):
        c = qb * nh + h
        s = s_scr[c]
        m_old = m_scr[c]
        m_new = jnp.maximum(m_old, jnp.max(s, axis=0, keepdims=True))
        alpha = jnp.exp2(m_old - m_new)
        p = jnp.exp2(s - m_new).astype(BF16)
        acc_scr[c] = alpha * acc_scr[c] + _dot(vt_ref[0, h, j], p)
        m_scr[c] = m_new

    m_scr[...] = jnp.full(m_scr.shape, RUNNING_MAX_INIT, F32)
    acc_scr[...] = jnp.zeros(acc_scr.shape, F32)
    for h in range(nh):
        scores(0, h, i, sa_scr, causal)
        scores(1, h, i + 1, sa_scr, causal)
    for h in range(nh):
        scores(1, h, i, sb_scr)
        consume(1, h, i + 1, sa_scr)

    def body(t, carry):
        j0, j1 = 2 * t, 2 * t + 1
        prev = jnp.where(t == 0, i, j0 - 1)
        for h in range(nh):
            scores(0, h, j0, sb_scr)
            consume(0, h, prev, sa_scr)
            scores(1, h, j0, sa_scr)
            consume(1, h, prev, sb_scr)
        for h in range(nh):
            scores(0, h, j1, sa_scr)
            consume(0, h, j0, sb_scr)
            scores(1, h, j1, sb_scr)
            consume(1, h, j0, sa_scr)
        return carry

    lax.fori_loop(0, pair, body, 0)
    last = jnp.where(pair == 0, i, i - 1)
    for h in range(nh):
        consume(0, h, last, sa_scr)
        consume(1, h, last, sb_scr)

    for qb in range(ATTN_QBLOCKS):
        for vt in range(nh // HEADS_PER_TILE):
            rows = []
            for hh in range(HEADS_PER_TILE):
                acc = acc_scr[qb * nh + vt * HEADS_PER_TILE + hh]
                rows.append(acc[0:HEAD_DIM] / acc[HEAD_DIM:HEAD_DIM + 1])
            o_ref[0, qb * tq:(qb + 1) * tq, vt * LANES:(vt + 1) * LANES] = jnp.concatenate(rows, axis=0).T


def _attention(q_t, k_aug, v_t):
    bsz, nh, nb, _, tq = q_t.shape
    seq = nb * tq
    assert nb % ATTN_QBLOCKS == 0
    chains = ATTN_QBLOCKS * nh
    resident = pl.Buffered(1)
    return pl.pallas_call(
        _attn_kernel,
        grid=(bsz, nb // ATTN_QBLOCKS),
        in_specs=[pl.BlockSpec((1, nh, ATTN_QBLOCKS, LANES, tq), lambda b, g: (b, 0, g, 0, 0)),
                  pl.BlockSpec((1, nh, seq, LANES), lambda b, g: (b, 0, 0, 0), pipeline_mode=resident),
                  pl.BlockSpec((1, nh, nb, VT_ROWS, tq), lambda b, g: (b, 0, 0, 0, 0), pipeline_mode=resident)],
        out_specs=pl.BlockSpec((1, ATTN_QBLOCKS * tq, nh * HEAD_DIM), lambda b, g: (b, g, 0)),
        out_shape=jax.ShapeDtypeStruct((bsz, seq, nh * HEAD_DIM), F32),
        scratch_shapes=[pltpu.VMEM((chains, tq, tq), F32), pltpu.VMEM((chains, tq, tq), F32),
                        pltpu.VMEM((chains, 1, tq), F32), pltpu.VMEM((chains, VT_ROWS, tq), F32)],
        compiler_params=_params("arbitrary", "arbitrary"),
        name="attn",
    )(q_t, k_aug, v_t)


S5_STEPS = 64
S5_CHUNKS = 4
S5_CHUNK_GROUPS = SSM_GROUPS // S5_CHUNKS
S5_CHUNK_STATES = S5_CHUNK_GROUPS * SSM_STATE
S5_CHUNK_LANES = S5_CHUNK_GROUPS * SSM_GROUP_CH


S5_CARRY_GROUPS = LANES // SSM_STATE


def _cmul(ar, ai, br, bi):
    return ar * br - ai * bi, ar * bi + ai * br


def _s5_carry_kernel(ut_ref, lam_ref, bt_ref, o_ref):
    bsz, nrows, seq = ut_ref.shape
    nch = nrows // S5_CARRY_GROUPS
    ls = seq // N_SEG
    reps = bsz * N_SEG
    sub = lax.broadcasted_iota(jnp.int32, (SUBLANES, LANES), 0)
    lr = jnp.broadcast_to(lam_ref[0, 0:1], (SUBLANES, LANES))
    li = jnp.broadcast_to(lam_ref[0, 1:2], (SUBLANES, LANES))
    pr = jnp.ones((SUBLANES, LANES), F32)
    pi = jnp.zeros((SUBLANES, LANES), F32)
    tr = jnp.zeros((SUBLANES, LANES), F32)
    ti = jnp.zeros((SUBLANES, LANES), F32)
    for r in range(SUBLANES):
        tr = jnp.where(sub == SUBLANES - 1 - r, pr, tr)
        ti = jnp.where(sub == SUBLANES - 1 - r, pi, ti)
        pr, pi = _cmul(pr, pi, lr, li)
    span = SUBLANES
    while span < ls:
        nr, ni = _cmul(tr, ti, pr[0:1], pi[0:1])
        tr = jnp.concatenate([nr, tr], axis=0)
        ti = jnp.concatenate([ni, ti], axis=0)
        pr, pi = _cmul(pr, pi, pr, pi)
        span *= 2
    lhs = jnp.concatenate([ut_ref[b, g * nch:(g + 1) * nch, s * ls:(s + 1) * ls]
                           for g in range(S5_CARRY_GROUPS) for b in range(bsz) for s in range(N_SEG)],
                          axis=0)
    gr = _dot(lhs, tr.astype(BF16))
    gi = _dot(lhs, ti.astype(BF16))
    btr = jnp.concatenate([bt_ref[0, 0]] * (reps * S5_CARRY_GROUPS), axis=0)
    bti = jnp.concatenate([bt_ref[0, 1]] * (reps * S5_CARRY_GROUPS), axis=0)
    wr, wi = _cmul(gr, gi, btr, bti)
    fr = wr.reshape(reps * S5_CARRY_GROUPS, nch, LANES).sum(axis=1)
    fi = wi.reshape(reps * S5_CARRY_GROUPS, nch, LANES).sum(axis=1)
    grp = lax.broadcasted_iota(jnp.int32, (reps, LANES), 1) // SSM_STATE
    fin_r, fin_i = fr[0:reps], fi[0:reps]
    for g in range(1, S5_CARRY_GROUPS):
        fin_r = jnp.where(grp == g, fr[g * reps:(g + 1) * reps], fin_r)
        fin_i = jnp.where(grp == g, fi[g * reps:(g + 1) * reps], fin_i)
    rows_r, rows_i = [], []
    for b in range(bsz):
        sr = jnp.zeros((1, LANES), F32)
        si = jnp.zeros((1, LANES), F32)
        rows_r.append(sr)
        rows_i.append(si)
        for seg in range(1, N_SEG):
            r = b * N_SEG + seg - 1
            mr, mi = _cmul(sr, si, pr[0:1], pi[0:1])
            sr, si = mr + fin_r[r:r + 1], mi + fin_i[r:r + 1]
            rows_r.append(sr)
            rows_i.append(si)
    o_ref[0, 0] = jnp.concatenate(rows_r, axis=0)
    o_ref[0, 1] = jnp.concatenate(rows_i, axis=0)


def _s5_main_kernel(u_ref, bc_ref, cre_ref, cim_ref, lam_ref, sinit_ref, d_ref, wglu_ref, bglu_ref, o_ref,
                    il_scr, bu_scr, st_scr, y_scr):
    jb = pl.program_id(1)
    nseg, steps, width = u_ref.shape
    cs = S5_CHUNK_STATES

    @pl.when(jb == 0)
    def _():
        st_scr[...] = sinit_ref[...]

    for j in range(steps):
        il_scr[j * nseg:(j + 1) * nseg, :] = u_ref[:, j, :]
    u = il_scr[...]
    u_bf = u.astype(BF16)
    for c in range(S5_CHUNKS):
        bu_scr[c] = _dot(u_bf[:, c * S5_CHUNK_LANES:(c + 1) * S5_CHUNK_LANES], bc_ref[c])
    ys = []
    for c in range(S5_CHUNKS):
        lr = jnp.broadcast_to(lam_ref[c, 0:1, :], (nseg, cs))
        li = jnp.broadcast_to(lam_ref[c, 1:2, :], (nseg, cs))
        xr = st_scr[c, :, 0:cs]
        xi = st_scr[c, :, cs:2 * cs]
        for j in range(steps):
            rows = slice(j * nseg, (j + 1) * nseg)
            xr, xi = (lr * xr - li * xi + bu_scr[c, rows, 0:cs],
                      lr * xi + li * xr + bu_scr[c, rows, cs:2 * cs])
            bu_scr[c, rows, 0:cs] = xr
            bu_scr[c, rows, cs:2 * cs] = xi
        st_scr[c, :, 0:cs] = xr
        st_scr[c, :, cs:2 * cs] = xi
        ys.append(_dot(bu_scr[c, :, 0:cs].astype(BF16), cre_ref[c])
                  + _dot(bu_scr[c, :, cs:2 * cs].astype(BF16), cim_ref[c]))
    y = jnp.concatenate(ys, axis=1) + d_ref[...] * u
    y = y * (0.5 * (1.0 + jnp.tanh(math.sqrt(2.0 / math.pi) * (y + 0.044715 * (y * y * y)))))
    z = _dot(y.astype(BF16), wglu_ref[...]) + bglu_ref[...]
    y = y * jax.nn.sigmoid(z)
    for t in range(width // LANES):
        y_scr[t] = y[:, t * LANES:(t + 1) * LANES]
    for s in range(nseg):
        for t in range(width // LANES):
            o_ref[s, :, t * LANES:(t + 1) * LANES] = y_scr[t, pl.ds(s, steps, stride=nseg), :]


def _s5(u, u_t, prm, d_skip, w_glu, b_glu):
    bsz, seq, _ = u.shape
    ls = seq // N_SEG
    steps = S5_STEPS
    rows = steps * N_SEG
    ngrp = SSM_GROUPS // S5_CARRY_GROUPS
    carry = pl.pallas_call(
        _s5_carry_kernel,
        grid=(ngrp,),
        in_specs=[pl.BlockSpec((bsz, S5_CARRY_GROUPS * SSM_GROUP_CH, seq), lambda g: (0, g, 0)),
                  pl.BlockSpec((1, 2, LANES), lambda g: (g, 0, 0)),
                  pl.BlockSpec((1, 2, SSM_GROUP_CH, LANES), lambda g: (g, 0, 0, 0))],
        out_specs=pl.BlockSpec((1, 2, bsz * N_SEG, LANES), lambda g: (g, 0, 0, 0)),
        out_shape=jax.ShapeDtypeStruct((ngrp, 2, bsz * N_SEG, LANES), F32),
        compiler_params=_params("arbitrary"),
        name="s5_carry",
    )(u_t, prm["lam_carry"], prm["bt_carry"])
    sinit = carry.reshape(S5_CHUNKS, ngrp // S5_CHUNKS, 2, bsz, N_SEG, LANES)
    sinit = sinit.transpose(3, 0, 4, 2, 1, 5).reshape(bsz, S5_CHUNKS, N_SEG, 2 * S5_CHUNK_STATES)

    def const(shape):
        return pl.BlockSpec(shape, lambda b, j: (0,) * len(shape))

    blk = pl.BlockSpec((None, N_SEG, steps, SSM_WIDTH), lambda b, j: (b, 0, j, 0))
    out = pl.pallas_call(
        _s5_main_kernel,
        grid=(bsz, ls // steps),
        in_specs=[blk, const(prm["bc"].shape), const(prm["cre"].shape), const(prm["cim"].shape),
                  const(prm["lam_chunk"].shape),
                  pl.BlockSpec((None, S5_CHUNKS, N_SEG, 2 * S5_CHUNK_STATES), lambda b, j: (b, 0, 0, 0)),
                  const(d_skip.shape), const(w_glu.shape), const(b_glu.shape)],
        out_specs=blk,
        out_shape=jax.ShapeDtypeStruct((bsz, N_SEG, ls, SSM_WIDTH), F32),
        scratch_shapes=[pltpu.VMEM((rows, SSM_WIDTH), F32),
                        pltpu.VMEM((S5_CHUNKS, rows, 2 * S5_CHUNK_STATES), F32),
                        pltpu.VMEM((S5_CHUNKS, N_SEG, 2 * S5_CHUNK_STATES), F32),
                        pltpu.VMEM((SSM_WIDTH // LANES, rows, LANES), F32)],
        compiler_params=_params("arbitrary", "arbitrary"),
        name="s5_main",
    )(u.reshape(bsz, N_SEG, ls, SSM_WIDTH), prm["bc"], prm["cre"], prm["cim"], prm["lam_chunk"], sinit,
      d_skip, w_glu, b_glu)
    return out.reshape(bsz, seq, SSM_WIDTH)


FF_CHUNK = 1024


def _outmlp_kernel(x_ref, attn_ref, ssm_ref, mod_ref, ga_ref, gs_ref, gm_ref, gf_ref,
                   wout_ref, w1_ref, w2_ref, o_ref, *, final_norm):
    x = x_ref[0]
    tm, d = x.shape
    gt1 = mod_ref[0, :, 2 * d:3 * d]
    sh2 = mod_ref[0, :, 3 * d:4 * d]
    sc2 = mod_ref[0, :, 4 * d:5 * d]
    gt2 = mod_ref[0, :, 5 * d:6 * d]
    a = _rmsnorm(attn_ref[0], ga_ref[...]).astype(BF16)
    s = _rmsnorm(ssm_ref[0], gs_ref[...]).astype(BF16)
    mixed = _dot(a, wout_ref[0:ATTN_WIDTH, :]) + _dot(s, wout_ref[ATTN_WIDTH:, :])
    x1 = x + gt1 * mixed
    h = (_rmsnorm(x1, gm_ref[...]) * (1.0 + sc2) + sh2).astype(BF16)
    ff = w1_ref.shape[1]
    acc = jnp.zeros((tm, d), F32)
    for c in range(ff // FF_CHUNK):
        t = jnp.maximum(_dot(h, w1_ref[:, c * FF_CHUNK:(c + 1) * FF_CHUNK]), 0.0)
        acc = acc + _dot((t * t).astype(BF16), w2_ref[c * FF_CHUNK:(c + 1) * FF_CHUNK, :])
    x2 = x1 + gt2 * acc
    o_ref[0] = _rmsnorm(x2, gf_ref[...]) if final_norm else x2


def _outmlp(x, attn, ssm, mod3, g_attn, g_ssm, g_mlp, g_final, w_out, w_fc1, w_fc2, final_norm):
    bsz, seq, d = x.shape
    tm = 256
    const = lambda b, t: (0, 0)
    return pl.pallas_call(
        functools.partial(_outmlp_kernel, final_norm=final_norm),
        grid=(bsz, seq // tm),
        in_specs=[pl.BlockSpec((1, tm, d), lambda b, t: (b, t, 0)),
                  pl.BlockSpec((1, tm, ATTN_WIDTH), lambda b, t: (b, t, 0)),
                  pl.BlockSpec((1, tm, SSM_WIDTH), lambda b, t: (b, t, 0)),
                  pl.BlockSpec((1, 1, mod3.shape[-1]), lambda b, t: (b, 0, 0)),
                  pl.BlockSpec(g_attn.shape, const),
                  pl.BlockSpec(g_ssm.shape, const),
                  pl.BlockSpec(g_mlp.shape, const),
                  pl.BlockSpec(g_final.shape, const),
                  pl.BlockSpec(w_out.shape, const),
                  pl.BlockSpec(w_fc1.shape, const),
                  pl.BlockSpec(w_fc2.shape, const)],
        out_specs=pl.BlockSpec((1, tm, d), lambda b, t: (b, t, 0)),
        out_shape=jax.ShapeDtypeStruct((bsz, seq, d), F32),
        compiler_params=_params("arbitrary", "arbitrary"),
        name="outmlp",
    )(x, attn, ssm, mod3, g_attn, g_ssm, g_mlp, g_final, w_out, w_fc1, w_fc2)


def _rope_tables(seq):
    half = HEAD_DIM // 2
    inv_freq = ROPE_THETA ** (-jnp.arange(half, dtype=F32) / half)
    ang = jnp.arange(seq, dtype=F32)[:, None] * inv_freq[None, :]
    cos, sin = jnp.cos(ang), jnp.sin(ang)
    reps = LANES // HEAD_DIM
    cos_t = jnp.tile(cos, (1, 2 * reps))
    sin_t = jnp.tile(jnp.concatenate([-sin, sin], axis=1), (1, reps))
    return cos_t, sin_t


def _s5_params(lam_re, lam_im, log_dt, b_re, b_im, c_re, c_im):
    lr, li = lam_re.astype(F32), lam_im.astype(F32)
    dt = jnp.exp(log_dt.astype(F32))[:, None]
    mag = jnp.exp(lr * dt)
    ar, ai = mag * jnp.cos(li * dt), mag * jnp.sin(li * dt)
    den = lr * lr + li * li
    cr = ((ar - 1.0) * lr + ai * li) / den
    ci = (ai * lr - (ar - 1.0) * li) / den
    bbar_re = cr[..., None] * b_re.astype(F32) - ci[..., None] * b_im.astype(F32)
    bbar_im = cr[..., None] * b_im.astype(F32) + ci[..., None] * b_re.astype(F32)
    eye = jnp.eye(S5_CHUNK_GROUPS, dtype=F32)

    def chunked(m):
        return m.reshape((S5_CHUNKS, S5_CHUNK_GROUPS) + m.shape[1:])

    def diag_in(m):
        return jnp.einsum('cgpn,gh->cgnhp', chunked(m), eye).reshape(S5_CHUNKS, S5_CHUNK_LANES, S5_CHUNK_STATES)

    def diag_out(m):
        return jnp.einsum('cgnp,gh->cgphn', chunked(m), eye).reshape(S5_CHUNKS, S5_CHUNK_STATES, S5_CHUNK_LANES)

    ngrp = SSM_GROUPS // S5_CARRY_GROUPS

    def lane_tiles(m):
        m = m.reshape(ngrp, S5_CARRY_GROUPS, SSM_GROUP_CH, SSM_STATE)
        return m.transpose(0, 2, 1, 3).reshape(ngrp, SSM_GROUP_CH, LANES)

    return {
        "bc": jnp.concatenate([diag_in(bbar_re), diag_in(bbar_im)], axis=2).astype(BF16),
        "cre": diag_out(c_re.astype(F32)).astype(BF16),
        "cim": diag_out(-c_im.astype(F32)).astype(BF16),
        "lam_chunk": jnp.stack([ar.reshape(S5_CHUNKS, S5_CHUNK_STATES),
                                ai.reshape(S5_CHUNKS, S5_CHUNK_STATES)], axis=1),
        "lam_carry": jnp.stack([ar.reshape(ngrp, LANES), ai.reshape(ngrp, LANES)], axis=1),
        "bt_carry": jnp.stack([lane_tiles(bbar_re.transpose(0, 2, 1)),
                               lane_tiles(bbar_im.transpose(0, 2, 1))], axis=1),
    }


def kernel(x, c, w_ada, b_ada, g_mix, w_in, g_attn_out, lam_re, lam_im, log_dt, b_re, b_im, c_re, c_im,
           d_skip, w_glu, b_glu, g_ssm_out, w_out, g_mlp, w_fc1, w_fc2, g_final):
    bsz, seq, d = x.shape
    depth = w_ada.shape[0]
    assert seq % (N_SEG * MOBA_BLOCK) == 0 and (seq // N_SEG) % S5_STEPS == 0 and bsz <= SUBLANES
    cos_t, sin_t = _rope_tables(seq)
    c_pad = jnp.zeros((SUBLANES, d), F32).at[:bsz].set(c.astype(F32))

    for l in range(depth):
        mod = _adaln(c_pad, w_ada[l], b_ada[l][None, :])
        mod3 = mod[:bsz, None, :]

        wqk_hi, wqk_lo = _split_bf16(w_in[l][:, :2 * ATTN_WIDTH])
        wvu = w_in[l][:, 2 * ATTN_WIDTH:].astype(BF16)
        q_t, k_aug, v_t, u, u_t = _inproj(x, mod3, g_mix[l][None, :], wqk_hi, wqk_lo, wvu, cos_t, sin_t)
        attn = _attention(q_t, k_aug, v_t)

        prm = _s5_params(lam_re[l], lam_im[l], log_dt[l], b_re[l], b_im[l], c_re[l], c_im[l])
        ssm = _s5(u, u_t, prm, d_skip[l].reshape(1, SSM_WIDTH), w_glu[l].astype(BF16), b_glu[l][None, :])

        x = _outmlp(x, attn, ssm, mod3, g_attn_out[l][None, :], g_ssm_out[l][None, :], g_mlp[l][None, :],
                    g_final[None, :], w_out[l].astype(BF16), w_fc1[l].astype(BF16), w_fc2[l].astype(BF16),
                    final_norm=(l == depth - 1))
    return x
```

```python
import functools
import math

import jax
import jax.numpy as jnp
from jax import lax
from jax.experimental import pallas as pl
from jax.experimental.pallas import tpu as pltpu

F32 = jnp.float32
BF16 = jnp.bfloat16

HEAD_DIM = 64
ATTN_HEADS = 8
ATTN_WIDTH = ATTN_HEADS * HEAD_DIM
SSM_GROUPS = 32
SSM_GROUP_CH = 16
SSM_WIDTH = SSM_GROUPS * SSM_GROUP_CH
SSM_STATE = 64
SSM_NSTATE = SSM_GROUPS * SSM_STATE
MOBA_BLOCK = 256
MOBA_TOPK = 3
ROPE_THETA = 10000.0
EPS = 1e-6

LANES = 128
SUBLANES = 8
N_SEG = SUBLANES
HEADS_PER_TILE = LANES // HEAD_DIM
MASK_VALUE = -(2.0 ** 100)
VMEM_LIMIT_BYTES = 56 * 1024 * 1024


def _split_bf16(a):
    hi = a.astype(BF16)
    lo = (a - hi.astype(F32)).astype(BF16)
    return hi, lo


def _dot(a, b):
    return jnp.dot(a, b, preferred_element_type=F32)


def _rmsnorm(x, g):
    return x * lax.rsqrt(jnp.mean(x * x, axis=-1, keepdims=True) + EPS) * g


def _params(*semantics):
    return pltpu.CompilerParams(dimension_semantics=semantics, vmem_limit_bytes=VMEM_LIMIT_BYTES)


def _adaln_kernel(c_ref, w_ref, b_ref, o_ref):
    c = c_ref[...]
    s_hi, s_lo = _split_bf16(c * jax.nn.sigmoid(c))
    w_hi, w_lo = _split_bf16(w_ref[...])
    o_ref[...] = _dot(s_hi, w_hi) + _dot(s_lo, w_hi) + _dot(s_hi, w_lo) + b_ref[...]


def _adaln(c_pad, w, b):
    rows, d = c_pad.shape
    n = w.shape[1]
    tn = n // 4
    return pl.pallas_call(
        _adaln_kernel,
        grid=(n // tn,),
        in_specs=[pl.BlockSpec((rows, d), lambda j: (0, 0)),
                  pl.BlockSpec((d, tn), lambda j: (0, j)),
                  pl.BlockSpec((1, tn), lambda j: (0, j))],
        out_specs=pl.BlockSpec((rows, tn), lambda j: (0, j)),
        out_shape=jax.ShapeDtypeStruct((rows, n), F32),
        compiler_params=_params("arbitrary"),
        name="adaln",
    )(c_pad, w, b)


VT_ROWS = HEAD_DIM + 16


def _top_blocks(gate, blk, past, own):
    nb = gate.shape[0]
    g = jnp.where(past, gate, -jnp.inf)
    chosen = None
    for _ in range(MOBA_TOPK):
        best = jnp.max(g, axis=0, keepdims=True)
        first = jnp.min(jnp.where(g == best, blk, float(nb)), axis=0, keepdims=True)
        pick = blk == first
        chosen = pick if chosen is None else (chosen | pick)
        g = jnp.where(pick, -jnp.inf, g)
    return (chosen & past) | own


def _inproj_norm(x_ref, mod_ref, g_ref, h_scr):
    x = x_ref[0]
    d = x.shape[1]
    sh1 = mod_ref[0, :, 0:d]
    sc1 = mod_ref[0, :, d:2 * d]
    h = _rmsnorm(x, g_ref[...]) * (1.0 + sc1) + sh1
    h_hi, h_lo = _split_bf16(h)
    h_scr[0] = h_hi
    h_scr[1] = h_lo


def _inproj_matmul(h_scr, wqk_hi_ref, wqk_lo_ref, wvu_ref, qk_scr, vu_scr):
    h_hi, h_lo = h_scr[0], h_scr[1]
    qk_scr[...] = (_dot(h_hi, wqk_hi_ref[...]) + _dot(h_lo, wqk_hi_ref[...])
                   + _dot(h_hi, wqk_lo_ref[...]))
    vu_scr[...] = _dot(h_hi, wvu_ref[...])


def _inproj_route(i, qk_scr, vu_scr, cos_ref, sin_ref, qt_ref, ka_ref, vt_ref, u_ref, ut_ref, km_scr):
    tm = qk_scr.shape[0]
    nb = km_scr.shape[0]
    u = vu_scr[:, ATTN_WIDTH:]
    u_ref[0] = u
    ut_ref[0] = u.T.astype(BF16)

    reps = ATTN_WIDTH // LANES
    cos = jnp.concatenate([cos_ref[...]] * reps, axis=1)
    sin = jnp.concatenate([sin_ref[...]] * reps, axis=1)
    wide_lane = lax.broadcasted_iota(jnp.int32, (tm, ATTN_WIDTH), 1)
    first_half = (wide_lane & (HEAD_DIM // 2)) == 0

    def rope(t):
        partner = jnp.where(first_half,
                            pltpu.roll(t, ATTN_WIDTH - HEAD_DIM // 2, 1),
                            pltpu.roll(t, HEAD_DIM // 2, 1))
        return t * cos + partner * sin

    q = rope(qk_scr[:, :ATTN_WIDTH])
    k = rope(qk_scr[:, ATTN_WIDTH:])
    v = vu_scr[:, :ATTN_WIDTH]

    km_scr[pl.ds(i, 1), :] = jnp.sum(k, axis=0, keepdims=True) * (1.0 / tm)
    km_all = km_scr[...]

    lane = lax.broadcasted_iota(jnp.int32, (tm, LANES), 1)
    km_lane = lax.broadcasted_iota(jnp.int32, (nb, LANES), 1)
    blk_i = lax.broadcasted_iota(jnp.int32, (nb, tm), 0)
    blk = blk_i.astype(F32)
    past = blk_i < i
    own = blk_i == i
    ones_rows = jnp.where(lax.broadcasted_iota(jnp.int32, (VT_ROWS - HEAD_DIM, tm), 0) == 0, 1.0, 0.0)
    onehot = jnp.where(lane == HEAD_DIM + i, 1.0, 0.0)
    scale = HEAD_DIM ** -0.5 * math.log2(math.e)
    for tile in range(ATTN_WIDTH // LANES):
        cols = slice(tile * LANES, (tile + 1) * LANES)
        q_t = q[:, cols].T
        v_t = v[:, cols].T
        k_tile = k[:, cols]
        km = km_all[:, cols]
        qt_hi, qt_lo = _split_bf16(q_t)
        for hh in range(HEADS_PER_TILE):
            h_idx = tile * HEADS_PER_TILE + hh
            kmh_hi, kmh_lo = _split_bf16(jnp.where(km_lane // HEAD_DIM == hh, km, 0.0))
            gate = _dot(kmh_hi, qt_hi) + _dot(kmh_lo, qt_hi) + _dot(kmh_hi, qt_lo)
            bias_t = jnp.where(_top_blocks(gate, blk, past, own), 0.0, MASK_VALUE)
            head_rows = slice(hh * HEAD_DIM, (hh + 1) * HEAD_DIM)
            qt_ref[0, h_idx, 0] = jnp.concatenate(
                [q_t[head_rows] * scale, bias_t, jnp.zeros((LANES - HEAD_DIM - nb, tm), F32)],
                axis=0).astype(BF16)
            vt_ref[0, h_idx, 0] = jnp.concatenate([v_t[head_rows], ones_rows], axis=0).astype(BF16)
            kh = jnp.where(lane // HEAD_DIM == hh, k_tile, 0.0)
            if hh:
                kh = pltpu.roll(kh, LANES - hh * HEAD_DIM, 1)
            ka_ref[0, h_idx] = jnp.where(lane < HEAD_DIM, kh, onehot).astype(BF16)


INPROJ_STAGES = 3


def _inproj_kernel(x_ref, mod_ref, g_ref, wqk_hi_ref, wqk_lo_ref, wvu_ref, cos_ref, sin_ref,
                   qt_ref, ka_ref, vt_ref, u_ref, ut_ref, h_scr, qk_scr, vu_scr, km_scr):
    s = pl.program_id(1)

    @pl.when(s == 0)
    def _():
        h_scr[...] = jnp.zeros_like(h_scr)
        qk_scr[...] = jnp.zeros_like(qk_scr)
        vu_scr[...] = jnp.zeros_like(vu_scr)
        km_scr[...] = jnp.zeros_like(km_scr)

    block = jnp.maximum(s - (INPROJ_STAGES - 1), 0)
    for parity in range(2):
        @pl.when(s % 2 == parity)
        def _(parity=parity):
            other = 1 - parity
            _inproj_matmul(h_scr.at[other], wqk_hi_ref, wqk_lo_ref, wvu_ref, qk_scr.at[other], vu_scr.at[other])
            _inproj_norm(x_ref, mod_ref, g_ref, h_scr.at[parity])
            _inproj_route(block, qk_scr.at[parity], vu_scr.at[parity], cos_ref, sin_ref,
                          qt_ref, ka_ref, vt_ref, u_ref, ut_ref, km_scr)


def _inproj(x, mod3, g_mix, wqk_hi, wqk_lo, wvu, cos_t, sin_t):
    bsz, seq, d = x.shape
    tm = MOBA_BLOCK
    nb = seq // tm
    assert nb % SUBLANES == 0 and nb <= LANES - HEAD_DIM
    lag = INPROJ_STAGES - 1
    const = lambda b, s: (0, 0)
    done = lambda s: jnp.maximum(s - lag, 0)
    return pl.pallas_call(
        _inproj_kernel,
        grid=(bsz, nb + lag),
        in_specs=[pl.BlockSpec((1, tm, d), lambda b, s: (b, jnp.minimum(s, nb - 1), 0)),
                  pl.BlockSpec((1, 1, mod3.shape[-1]), lambda b, s: (b, 0, 0)),
                  pl.BlockSpec((1, d), const),
                  pl.BlockSpec(wqk_hi.shape, const),
                  pl.BlockSpec(wqk_lo.shape, const),
                  pl.BlockSpec(wvu.shape, const),
                  pl.BlockSpec((tm, LANES), lambda b, s: (done(s), 0)),
                  pl.BlockSpec((tm, LANES), lambda b, s: (done(s), 0))],
        out_specs=[pl.BlockSpec((1, ATTN_HEADS, 1, LANES, tm), lambda b, s: (b, 0, done(s), 0, 0)),
                   pl.BlockSpec((1, ATTN_HEADS, tm, LANES), lambda b, s: (b, 0, done(s), 0)),
                   pl.BlockSpec((1, ATTN_HEADS, 1, VT_ROWS, tm), lambda b, s: (b, 0, done(s), 0, 0)),
                   pl.BlockSpec((1, tm, SSM_WIDTH), lambda b, s: (b, done(s), 0)),
                   pl.BlockSpec((1, SSM_WIDTH, tm), lambda b, s: (b, 0, done(s)))],
        out_shape=[jax.ShapeDtypeStruct((bsz, ATTN_HEADS, nb, LANES, tm), BF16),
                   jax.ShapeDtypeStruct((bsz, ATTN_HEADS, seq, LANES), BF16),
                   jax.ShapeDtypeStruct((bsz, ATTN_HEADS, nb, VT_ROWS, tm), BF16),
                   jax.ShapeDtypeStruct((bsz, seq, SSM_WIDTH), F32),
                   jax.ShapeDtypeStruct((bsz, SSM_WIDTH, seq), BF16)],
        scratch_shapes=[pltpu.VMEM((2, 2, tm, d), BF16),
                        pltpu.VMEM((2, tm, 2 * ATTN_WIDTH), F32),
                        pltpu.VMEM((2, tm, ATTN_WIDTH + SSM_WIDTH), F32),
                        pltpu.VMEM((nb, ATTN_WIDTH), F32)],
        compiler_params=_params("arbitrary", "arbitrary"),
        name="inproj",
    )(x, mod3, g_mix, wqk_hi, wqk_lo, wvu, cos_t, sin_t)


ATTN_QBLOCKS = 2
RUNNING_MAX_INIT = -3.0e38


def _attn_kernel(qt_ref, ka_ref, vt_ref, o_ref, sa_scr, sb_scr, m_scr, acc_scr):
    pair = pl.program_id(1)
    i = ATTN_QBLOCKS * pair
    nh = qt_ref.shape[1]
    tq = qt_ref.shape[4]
    key = lax.broadcasted_iota(jnp.int32, (tq, tq), 0)
    qry = lax.broadcasted_iota(jnp.int32, (tq, tq), 1)
    causal = key <= qry

    def scores(qb, h, j, s_scr, mask=None):
        start = pl.multiple_of(j * tq, tq)
        s = _dot(ka_ref[0, h, pl.ds(start, tq), :], qt_ref[0, h, qb])
        if mask is not None:
            s = jnp.where(mask, s, MASK_VALUE)
        s_scr[qb * nh + h] = s

    def consume(qb, h, j, s_scr):
        c = qb * nh + h
        s = s_scr[c]
        m_old = m_scr[c]
        m_new = jnp.maximum(m_old, jnp.max(s, axis=0, keepdims=True))
        alpha = jnp.exp2(m_old - m_new)
        p = jnp.exp2(s - m_new).astype(BF16)
        acc_scr[c] = alpha * acc_scr[c] + _dot(vt_ref[0, h, j], p)
        m_scr[c] = m_new

    m_scr[...] = jnp.full(m_scr.shape, RUNNING_MAX_INIT, F32)
    acc_scr[...] = jnp.zeros(acc_scr.shape, F32)
    for h in range(nh):
        scores(0, h, i, sa_scr, causal)
        scores(1, h, i + 1, sa_scr, causal)
    for h in range(nh):
        scores(1, h, i, sb_scr)
        consume(1, h, i + 1, sa_scr)

    def body(t, carry):
        j0, j1 = 2 * t, 2 * t + 1
        prev = jnp.where(t == 0, i, j0 - 1)
        for h in range(nh):
            scores(0, h, j0, sb_scr)
            consume(0, h, prev, sa_scr)
            scores(1, h, j0, sa_scr)
            consume(1, h, prev, sb_scr)
        for h in range(nh):
            scores(0, h, j1, sa_scr)
            consume(0, h, j0, sb_scr)
            scores(1, h, j1, sb_scr)
            consume(1, h, j0, sa_scr)
        return carry

    lax.fori_loop(0, pair, body, 0)
    last = jnp.where(pair == 0, i, i - 1)
    for h in range(nh):
        consume(0, h, last, sa_scr)
        consume(1, h, last, sb_scr)

    for qb in range(ATTN_QBLOCKS):
        for vt in range(nh // HEADS_PER_TILE):
            rows = []
            for hh in range(HEADS_PER_TILE):
                acc = acc_scr[qb * nh + vt * HEADS_PER_TILE + hh]
                rows.append(acc[0:HEAD_DIM] / acc[HEAD_DIM:HEAD_DIM + 1])
            o_ref[0, qb * tq:(qb + 1) * tq, vt * LANES:(vt + 1) * LANES] = jnp.concatenate(rows, axis=0).T


def _attention(q_t, k_aug, v_t):
    bsz, nh, nb, _, tq = q_t.shape
    seq = nb * tq
    assert nb % ATTN_QBLOCKS == 0
    chains = ATTN_QBLOCKS * nh
    resident = pl.Buffered(1)
    return pl.pallas_call(
        _attn_kernel,
        grid=(bsz, nb // ATTN_QBLOCKS),
        in_specs=[pl.BlockSpec((1, nh, ATTN_QBLOCKS, LANES, tq), lambda b, g: (b, 0, g, 0, 0)),
                  pl.BlockSpec((1, nh, seq, LANES), lambda b, g: (b, 0, 0, 0), pipeline_mode=resident),
                  pl.BlockSpec((1, nh, nb, VT_ROWS, tq), lambda b, g: (b, 0, 0, 0, 0), pipeline_mode=resident)],
        out_specs=pl.BlockSpec((1, ATTN_QBLOCKS * tq, nh * HEAD_DIM), lambda b, g: (b, g, 0)),
        out_shape=jax.ShapeDtypeStruct((bsz, seq, nh * HEAD_DIM), F32),
        scratch_shapes=[pltpu.VMEM((chains, tq, tq), F32), pltpu.VMEM((chains, tq, tq), F32),
                        pltpu.VMEM((chains, 1, tq), F32), pltpu.VMEM((chains, VT_ROWS, tq), F32)],
        compiler_params=_params("arbitrary", "arbitrary"),
        name="attn",
    )(q_t, k_aug, v_t)


S5_STEPS = 64
S5_CHUNKS = 4
S5_CHUNK_GROUPS = SSM_GROUPS // S5_CHUNKS
S5_CHUNK_STATES = S5_CHUNK_GROUPS * SSM_STATE
S5_CHUNK_LANES = S5_CHUNK_GROUPS * SSM_GROUP_CH


S5_CARRY_GROUPS = LANES // SSM_STATE


def _cmul(ar, ai, br, bi):
    return ar * br - ai * bi, ar * bi + ai * br


def _s5_carry_kernel(ut_ref, lam_ref, bt_ref, o_ref):
    bsz, nrows, seq = ut_ref.shape
    nch = nrows // S5_CARRY_GROUPS
    ls = seq // N_SEG
    reps = bsz * N_SEG
    sub = lax.broadcasted_iota(jnp.int32, (SUBLANES, LANES), 0)
    lr = jnp.broadcast_to(lam_ref[0, 0:1], (SUBLANES, LANES))
    li = jnp.broadcast_to(lam_ref[0, 1:2], (SUBLANES, LANES))
    pr = jnp.ones((SUBLANES, LANES), F32)
    pi = jnp.zeros((SUBLANES, LANES), F32)
    tr = jnp.zeros((SUBLANES, LANES), F32)
    ti = jnp.zeros((SUBLANES, LANES), F32)
    for r in range(SUBLANES):
        tr = jnp.where(sub == SUBLANES - 1 - r, pr, tr)
        ti = jnp.where(sub == SUBLANES - 1 - r, pi, ti)
        pr, pi = _cmul(pr, pi, lr, li)
    span = SUBLANES
    while span < ls:
        nr, ni = _cmul(tr, ti, pr[0:1], pi[0:1])
        tr = jnp.concatenate([nr, tr], axis=0)
        ti = jnp.concatenate([ni, ti], axis=0)
        pr, pi = _cmul(pr, pi, pr, pi)
        span *= 2
    lhs = jnp.concatenate([ut_ref[b, g * nch:(g + 1) * nch, s * ls:(s + 1) * ls]
                           for g in range(S5_CARRY_GROUPS) for b in range(bsz) for s in range(N_SEG)],
                          axis=0)
    gr = _dot(lhs, tr.astype(BF16))
    gi = _dot(lhs, ti.astype(BF16))
    btr = jnp.concatenate([bt_ref[0, 0]] * (reps * S5_CARRY_GROUPS), axis=0)
    bti = jnp.concatenate([bt_ref[0, 1]] * (reps * S5_CARRY_GROUPS), axis=0)
    wr, wi = _cmul(gr, gi, btr, bti)
    fr = wr.reshape(reps * S5_CARRY_GROUPS, nch, LANES).sum(axis=1)
    fi = wi.reshape(reps * S5_CARRY_GROUPS, nch, LANES).sum(axis=1)
    grp = lax.broadcasted_iota(jnp.int32, (reps, LANES), 1) // SSM_STATE
    fin_r, fin_i = fr[0:reps], fi[0:reps]
    for g in range(1, S5_CARRY_GROUPS):
        fin_r = jnp.where(grp == g, fr[g * reps:(g + 1) * reps], fin_r)
        fin_i = jnp.where(grp == g, fi[g * reps:(g + 1) * reps], fin_i)
    rows_r, rows_i = [], []
    for b in range(bsz):
        sr = jnp.zeros((1, LANES), F32)
        si = jnp.zeros((1, LANES), F32)
        rows_r.append(sr)
        rows_i.append(si)
        for seg in range(1, N_SEG):
            r = b * N_SEG + seg - 1
            mr, mi = _cmul(sr, si, pr[0:1], pi[0:1])
            sr, si = mr + fin_r[r:r + 1], mi + fin_i[r:r + 1]
            rows_r.append(sr)
            rows_i.append(si)
    o_ref[0, 0] = jnp.concatenate(rows_r, axis=0)
    o_ref[0, 1] = jnp.concatenate(rows_i, axis=0)


def _s5_main_kernel(u_ref, bc_ref, cre_ref, cim_ref, lam_ref, sinit_ref, d_ref, wglu_ref, bglu_ref, o_ref,
                    il_scr, bu_scr, st_scr, y_scr):
    jb = pl.program_id(1)
    nseg, steps, width = u_ref.shape
    cs = S5_CHUNK_STATES

    @pl.when(jb == 0)
    def _():
        st_scr[...] = sinit_ref[...]

    for j in range(steps):
        il_scr[j * nseg:(j + 1) * nseg, :] = u_ref[:, j, :]
    u = il_scr[...]
    u_bf = u.astype(BF16)
    for c in range(S5_CHUNKS):
        bu_scr[c] = _dot(u_bf[:, c * S5_CHUNK_LANES:(c + 1) * S5_CHUNK_LANES], bc_ref[c])
    ys = []
    for c in range(S5_CHUNKS):
        lr = jnp.broadcast_to(lam_ref[c, 0:1, :], (nseg, cs))
        li = jnp.broadcast_to(lam_ref[c, 1:2, :], (nseg, cs))
        xr = st_scr[c, :, 0:cs]
        xi = st_scr[c, :, cs:2 * cs]
        for j in range(steps):
            rows = slice(j * nseg, (j + 1) * nseg)
            xr, xi = (lr * xr - li * xi + bu_scr[c, rows, 0:cs],
                      lr * xi + li * xr + bu_scr[c, rows, cs:2 * cs])
            bu_scr[c, rows, 0:cs] = xr
            bu_scr[c, rows, cs:2 * cs] = xi
        st_scr[c, :, 0:cs] = xr
        st_scr[c, :, cs:2 * cs] = xi
        ys.append(_dot(bu_scr[c, :, 0:cs].astype(BF16), cre_ref[c])
                  + _dot(bu_scr[c, :, cs:2 * cs].astype(BF16), cim_ref[c]))
    y = jnp.concatenate(ys, axis=1) + d_ref[...] * u
    y = y * (0.5 * (1.0 + jnp.tanh(math.sqrt(2.0 / math.pi) * (y + 0.044715 * (y * y * y)))))
    z = _dot(y.astype(BF16), wglu_ref[...]) + bglu_ref[...]
    y = y * jax.nn.sigmoid(z)
    for t in range(width // LANES):
        y_scr[t] = y[:, t * LANES:(t + 1) * LANES]
    for s in range(nseg):
        for t in range(width // LANES):
            o_ref[s, :, t * LANES:(t + 1) * LANES] = y_scr[t, pl.ds(s, steps, stride=nseg), :]


def _s5(u, u_t, prm, d_skip, w_glu, b_glu):
    bsz, seq, _ = u.shape
    ls = seq // N_SEG
    steps = S5_STEPS
    rows = steps * N_SEG
    ngrp = SSM_GROUPS // S5_CARRY_GROUPS
    carry = pl.pallas_call(
        _s5_carry_kernel,
        grid=(ngrp,),
        in_specs=[pl.BlockSpec((bsz, S5_CARRY_GROUPS * SSM_GROUP_CH, seq), lambda g: (0, g, 0)),
                  pl.BlockSpec((1, 2, LANES), lambda g: (g, 0, 0)),
                  pl.BlockSpec((1, 2, SSM_GROUP_CH, LANES), lambda g: (g, 0, 0, 0))],
        out_specs=pl.BlockSpec((1, 2, bsz * N_SEG, LANES), lambda g: (g, 0, 0, 0)),
        out_shape=jax.ShapeDtypeStruct((ngrp, 2, bsz * N_SEG, LANES), F32),
        compiler_params=_params("arbitrary"),
        name="s5_carry",
    )(u_t, prm["lam_carry"], prm["bt_carry"])
    sinit = carry.reshape(S5_CHUNKS, ngrp // S5_CHUNKS, 2, bsz, N_SEG, LANES)
    sinit = sinit.transpose(3, 0, 4, 2, 1, 5).reshape(bsz, S5_CHUNKS, N_SEG, 2 * S5_CHUNK_STATES)

    def const(shape):
        return pl.BlockSpec(shape, lambda b, j: (0,) * len(shape))

    blk = pl.BlockSpec((None, N_SEG, steps, SSM_WIDTH), lambda b, j: (b, 0, j, 0))
    out = pl.pallas_call(
        _s5_main_kernel,
        grid=(bsz, ls // steps),
        in_specs=[blk, const(prm["bc"].shape), const(prm["cre"].shape), const(prm["cim"].shape),
                  const(prm["lam_chunk"].shape),
                  pl.BlockSpec((None, S5_CHUNKS, N_SEG, 2 * S5_CHUNK_STATES), lambda b, j: (b, 0, 0, 0)),
                  const(d_skip.shape), const(w_glu.shape), const(b_glu.shape)],
        out_specs=blk,
        out_shape=jax.ShapeDtypeStruct((bsz, N_SEG, ls, SSM_WIDTH), F32),
        scratch_shapes=[pltpu.VMEM((rows, SSM_WIDTH), F32),
                        pltpu.VMEM((S5_CHUNKS, rows, 2 * S5_CHUNK_STATES), F32),
                        pltpu.VMEM((S5_CHUNKS, N_SEG, 2 * S5_CHUNK_STATES), F32),
                        pltpu.VMEM((SSM_WIDTH // LANES, rows, LANES), F32)],
        compiler_params=_params("arbitrary", "arbitrary"),
        name="s5_main",
    )(u.reshape(bsz, N_SEG, ls, SSM_WIDTH), prm["bc"], prm["cre"], prm["cim"], prm["lam_chunk"], sinit,
      d_skip, w_glu, b_glu)
    return out.reshape(bsz, seq, SSM_WIDTH)


FF_CHUNK = 1024


def _outmlp_kernel(x_ref, attn_ref, ssm_ref, mod_ref, ga_ref, gs_ref, gm_ref, gf_ref,
                   wout_ref, w1_ref, w2_ref, o_ref, *, final_norm):
    x = x_ref[0]
    tm, d = x.shape
    gt1 = mod_ref[0, :, 2 * d:3 * d]
    sh2 = mod_ref[0, :, 3 * d:4 * d]
    sc2 = mod_ref[0, :, 4 * d:5 * d]
    gt2 = mod_ref[0, :, 5 * d:6 * d]
    a = _rmsnorm(attn_ref[0], ga_ref[...]).astype(BF16)
    s = _rmsnorm(ssm_ref[0], gs_ref[...]).astype(BF16)
    mixed = _dot(a, wout_ref[0:ATTN_WIDTH, :]) + _dot(s, wout_ref[ATTN_WIDTH:, :])
    x1 = x + gt1 * mixed
    h = (_rmsnorm(x1, gm_ref[...]) * (1.0 + sc2) + sh2).astype(BF16)
    ff = w1_ref.shape[1]
    acc = jnp.zeros((tm, d), F32)
    for c in range(ff // FF_CHUNK):
        t = jnp.maximum(_dot(h, w1_ref[:, c * FF_CHUNK:(c + 1) * FF_CHUNK]), 0.0)
        acc = acc + _dot((t * t).astype(BF16), w2_ref[c * FF_CHUNK:(c + 1) * FF_CHUNK, :])
    x2 = x1 + gt2 * acc
    o_ref[0] = _rmsnorm(x2, gf_ref[...]) if final_norm else x2


OUTMLP_ROWS = 512


def _outmlp(x, attn, ssm, mod3, g_attn, g_ssm, g_mlp, g_final, w_out, w_fc1, w_fc2, final_norm):
    bsz, seq, d = x.shape
    tm = OUTMLP_ROWS
    const = lambda b, t: (0, 0)
    return pl.pallas_call(
        functools.partial(_outmlp_kernel, final_norm=final_norm),
        grid=(bsz, seq // tm),
        in_specs=[pl.BlockSpec((1, tm, d), lambda b, t: (b, t, 0)),
                  pl.BlockSpec((1, tm, ATTN_WIDTH), lambda b, t: (b, t, 0)),
                  pl.BlockSpec((1, tm, SSM_WIDTH), lambda b, t: (b, t, 0)),
                  pl.BlockSpec((1, 1, mod3.shape[-1]), lambda b, t: (b, 0, 0)),
                  pl.BlockSpec(g_attn.shape, const),
                  pl.BlockSpec(g_ssm.shape, const),
                  pl.BlockSpec(g_mlp.shape, const),
                  pl.BlockSpec(g_final.shape, const),
                  pl.BlockSpec(w_out.shape, const),
                  pl.BlockSpec(w_fc1.shape, const),
                  pl.BlockSpec(w_fc2.shape, const)],
        out_specs=pl.BlockSpec((1, tm, d), lambda b, t: (b, t, 0)),
        out_shape=jax.ShapeDtypeStruct((bsz, seq, d), F32),
        compiler_params=_params("arbitrary", "arbitrary"),
        name="outmlp",
    )(x, attn, ssm, mod3, g_attn, g_ssm, g_mlp, g_final, w_out, w_fc1, w_fc2)


def _rope_tables(seq):
    half = HEAD_DIM // 2
    inv_freq = ROPE_THETA ** (-jnp.arange(half, dtype=F32) / half)
    ang = jnp.arange(seq, dtype=F32)[:, None] * inv_freq[None, :]
    cos, sin = jnp.cos(ang), jnp.sin(ang)
    reps = LANES // HEAD_DIM
    cos_t = jnp.tile(cos, (1, 2 * reps))
    sin_t = jnp.tile(jnp.concatenate([-sin, sin], axis=1), (1, reps))
    return cos_t, sin_t


def _s5_params(lam_re, lam_im, log_dt, b_re, b_im, c_re, c_im):
    lr, li = lam_re.astype(F32), lam_im.astype(F32)
    dt = jnp.exp(log_dt.astype(F32))[:, None]
    mag = jnp.exp(lr * dt)
    ar, ai = mag * jnp.cos(li * dt), mag * jnp.sin(li * dt)
    den = lr * lr + li * li
    cr = ((ar - 1.0) * lr + ai * li) / den
    ci = (ai * lr - (ar - 1.0) * li) / den
    bbar_re = cr[..., None] * b_re.astype(F32) - ci[..., None] * b_im.astype(F32)
    bbar_im = cr[..., None] * b_im.astype(F32) + ci[..., None] * b_re.astype(F32)
    eye = jnp.eye(S5_CHUNK_GROUPS, dtype=F32)

    def chunked(m):
        return m.reshape((S5_CHUNKS, S5_CHUNK_GROUPS) + m.shape[1:])

    def diag_in(m):
        return jnp.einsum('cgpn,gh->cgnhp', chunked(m), eye).reshape(S5_CHUNKS, S5_CHUNK_LANES, S5_CHUNK_STATES)

    def diag_out(m):
        return jnp.einsum('cgnp,gh->cgphn', chunked(m), eye).reshape(S5_CHUNKS, S5_CHUNK_STATES, S5_CHUNK_LANES)

    ngrp = SSM_GROUPS // S5_CARRY_GROUPS

    def lane_tiles(m):
        m = m.reshape(ngrp, S5_CARRY_GROUPS, SSM_GROUP_CH, SSM_STATE)
        return m.transpose(0, 2, 1, 3).reshape(ngrp, SSM_GROUP_CH, LANES)

    return {
        "bc": jnp.concatenate([diag_in(bbar_re), diag_in(bbar_im)], axis=2).astype(BF16),
        "cre": diag_out(c_re.astype(F32)).astype(BF16),
        "cim": diag_out(-c_im.astype(F32)).astype(BF16),
        "lam_chunk": jnp.stack([ar.reshape(S5_CHUNKS, S5_CHUNK_STATES),
                                ai.reshape(S5_CHUNKS, S5_CHUNK_STATES)], axis=1),
        "lam_carry": jnp.stack([ar.reshape(ngrp, LANES), ai.reshape(ngrp, LANES)], axis=1),
        "bt_carry": jnp.stack([lane_tiles(bbar_re.transpose(0, 2, 1)),
                               lane_tiles(bbar_im.transpose(0, 2, 1))], axis=1),
    }


def kernel(x, c, w_ada, b_ada, g_mix, w_in, g_attn_out, lam_re, lam_im, log_dt, b_re, b_im, c_re, c_im,
           d_skip, w_glu, b_glu, g_ssm_out, w_out, g_mlp, w_fc1, w_fc2, g_final):
    bsz, seq, d = x.shape
    depth = w_ada.shape[0]
    assert seq % (N_SEG * MOBA_BLOCK) == 0 and (seq // N_SEG) % S5_STEPS == 0 and bsz <= SUBLANES
    cos_t, sin_t = _rope_tables(seq)
    c_pad = jnp.zeros((SUBLANES, d), F32).at[:bsz].set(c.astype(F32))

    for l in range(depth):
        mod = _adaln(c_pad, w_ada[l], b_ada[l][None, :])
        mod3 = mod[:bsz, None, :]

        wqk_hi, wqk_lo = _split_bf16(w_in[l][:, :2 * ATTN_WIDTH])
        wvu = w_in[l][:, 2 * ATTN_WIDTH:].astype(BF16)
        q_t, k_aug, v_t, u, u_t = _inproj(x, mod3, g_mix[l][None, :], wqk_hi, wqk_lo, wvu, cos_t, sin_t)
        attn = _attention(q_t, k_aug, v_t)

        prm = _s5_params(lam_re[l], lam_im[l], log_dt[l], b_re[l], b_im[l], c_re[l], c_im[l])
        ssm = _s5(u, u_t, prm, d_skip[l].reshape(1, SSM_WIDTH), w_glu[l].astype(BF16), b_glu[l][None, :])

        x = _outmlp(x, attn, ssm, mod3, g_attn_out[l][None, :], g_ssm_out[l][None, :], g_mlp[l][None, :],
                    g_final[None, :], w_out[l].astype(BF16), w_fc1[l].astype(BF16), w_fc2[l].astype(BF16),
                    final_norm=(l == depth - 1))
    return x
```

```python
import functools
import math

import jax
import jax.numpy as jnp
from jax import lax
from jax.experimental import pallas as pl
from jax.experimental.pallas import tpu as pltpu

F32 = jnp.float32
BF16 = jnp.bfloat16

HEAD_DIM = 64
ATTN_HEADS = 8
ATTN_WIDTH = ATTN_HEADS * HEAD_DIM
SSM_GROUPS = 32
SSM_GROUP_CH = 16
SSM_WIDTH = SSM_GROUPS * SSM_GROUP_CH
SSM_STATE = 64
SSM_NSTATE = SSM_GROUPS * SSM_STATE
MOBA_BLOCK = 256
MOBA_TOPK = 3
ROPE_THETA = 10000.0
EPS = 1e-6

LANES = 128
SUBLANES = 8
N_SEG = SUBLANES
HEADS_PER_TILE = LANES // HEAD_DIM
MASK_VALUE = -(2.0 ** 100)
VMEM_LIMIT_BYTES = 56 * 1024 * 1024


def _split_bf16(a):
    hi = a.astype(BF16)
    lo = (a - hi.astype(F32)).astype(BF16)
    return hi, lo


def _dot(a, b):
    return jnp.dot(a, b, preferred_element_type=F32)


def _rmsnorm(x, g):
    return x * lax.rsqrt(jnp.mean(x * x, axis=-1, keepdims=True) + EPS) * g


def _params(*semantics):
    return pltpu.CompilerParams(dimension_semantics=semantics, vmem_limit_bytes=VMEM_LIMIT_BYTES)


def _adaln_kernel(c_ref, w_ref, b_ref, o_ref):
    c = c_ref[...]
    s_hi, s_lo = _split_bf16(c * jax.nn.sigmoid(c))
    w_hi, w_lo = _split_bf16(w_ref[...])
    o_ref[...] = _dot(s_hi, w_hi) + _dot(s_lo, w_hi) + _dot(s_hi, w_lo) + b_ref[...]


def _adaln(c_pad, w, b):
    rows, d = c_pad.shape
    n = w.shape[1]
    tn = n // 4
    return pl.pallas_call(
        _adaln_kernel,
        grid=(n // tn,),
        in_specs=[pl.BlockSpec((rows, d), lambda j: (0, 0)),
                  pl.BlockSpec((d, tn), lambda j: (0, j)),
                  pl.BlockSpec((1, tn), lambda j: (0, j))],
        out_specs=pl.BlockSpec((rows, tn), lambda j: (0, j)),
        out_shape=jax.ShapeDtypeStruct((rows, n), F32),
        compiler_params=_params("arbitrary"),
        name="adaln",
    )(c_pad, w, b)


VT_ROWS = HEAD_DIM + 16


def _top_blocks(gate, blk, past, own):
    nb = gate.shape[0]
    g = jnp.where(past, gate, -jnp.inf)
    chosen = None
    for _ in range(MOBA_TOPK):
        best = jnp.max(g, axis=0, keepdims=True)
        first = jnp.min(jnp.where(g == best, blk, float(nb)), axis=0, keepdims=True)
        pick = blk == first
        chosen = pick if chosen is None else (chosen | pick)
        g = jnp.where(pick, -jnp.inf, g)
    return (chosen & past) | own


def _inproj_kernel(x_ref, mod_ref, g_ref, wqk_hi_ref, wqk_lo_ref, wvu_ref, cos_ref, sin_ref,
                   qt_ref, ka_ref, vt_ref, u_ref, ut_ref, km_scr):
    i = pl.program_id(1)
    x = x_ref[0]
    tm, d = x.shape
    nb = km_scr.shape[0]
    sh1 = mod_ref[0, :, 0:d]
    sc1 = mod_ref[0, :, d:2 * d]
    h = _rmsnorm(x, g_ref[...]) * (1.0 + sc1) + sh1
    h_hi, h_lo = _split_bf16(h)
    qk = (_dot(h_hi, wqk_hi_ref[...]) + _dot(h_lo, wqk_hi_ref[...])
          + _dot(h_hi, wqk_lo_ref[...]))
    vu = _dot(h_hi, wvu_ref[...])
    u = vu[:, ATTN_WIDTH:]
    u_ref[0] = u
    ut_ref[0] = u.T.astype(BF16)

    reps = ATTN_WIDTH // LANES
    cos = jnp.concatenate([cos_ref[...]] * reps, axis=1)
    sin = jnp.concatenate([sin_ref[...]] * reps, axis=1)
    wide_lane = lax.broadcasted_iota(jnp.int32, (tm, ATTN_WIDTH), 1)
    first_half = (wide_lane & (HEAD_DIM // 2)) == 0

    def rope(t):
        partner = jnp.where(first_half,
                            pltpu.roll(t, ATTN_WIDTH - HEAD_DIM // 2, 1),
                            pltpu.roll(t, HEAD_DIM // 2, 1))
        return t * cos + partner * sin

    q = rope(qk[:, :ATTN_WIDTH])
    k = rope(qk[:, ATTN_WIDTH:])
    v = vu[:, :ATTN_WIDTH]

    @pl.when(i == 0)
    def _():
        km_scr[...] = jnp.zeros_like(km_scr)

    km_scr[pl.ds(i, 1), :] = jnp.sum(k, axis=0, keepdims=True) * (1.0 / tm)
    km_all = km_scr[...]

    lane = lax.broadcasted_iota(jnp.int32, (tm, LANES), 1)
    km_lane = lax.broadcasted_iota(jnp.int32, (nb, LANES), 1)
    blk_i = lax.broadcasted_iota(jnp.int32, (nb, tm), 0)
    blk = blk_i.astype(F32)
    past = blk_i < i
    own = blk_i == i
    ones_rows = jnp.where(lax.broadcasted_iota(jnp.int32, (VT_ROWS - HEAD_DIM, tm), 0) == 0, 1.0, 0.0)
    onehot = jnp.where(lane == HEAD_DIM + i, 1.0, 0.0)
    scale = HEAD_DIM ** -0.5 * math.log2(math.e)
    for tile in range(ATTN_WIDTH // LANES):
        cols = slice(tile * LANES, (tile + 1) * LANES)
        q_t = q[:, cols].T
        v_t = v[:, cols].T
        k_tile = k[:, cols]
        km = km_all[:, cols]
        qt_hi, qt_lo = _split_bf16(q_t)
        for hh in range(HEADS_PER_TILE):
            h_idx = tile * HEADS_PER_TILE + hh
            kmh_hi, kmh_lo = _split_bf16(jnp.where(km_lane // HEAD_DIM == hh, km, 0.0))
            gate = _dot(kmh_hi, qt_hi) + _dot(kmh_lo, qt_hi) + _dot(kmh_hi, qt_lo)
            bias_t = jnp.where(_top_blocks(gate, blk, past, own), 0.0, MASK_VALUE)
            head_rows = slice(hh * HEAD_DIM, (hh + 1) * HEAD_DIM)
            qt_ref[0, h_idx, 0] = jnp.concatenate(
                [q_t[head_rows] * scale, bias_t, jnp.zeros((LANES - HEAD_DIM - nb, tm), F32)],
                axis=0).astype(BF16)
            vt_ref[0, h_idx, 0] = jnp.concatenate([v_t[head_rows], ones_rows], axis=0).astype(BF16)
            kh = jnp.where(lane // HEAD_DIM == hh, k_tile, 0.0)
            if hh:
                kh = pltpu.roll(kh, LANES - hh * HEAD_DIM, 1)
            ka_ref[0, h_idx] = jnp.where(lane < HEAD_DIM, kh, onehot).astype(BF16)


def _inproj(x, mod3, g_mix, wqk_hi, wqk_lo, wvu, cos_t, sin_t):
    bsz, seq, d = x.shape
    tm = MOBA_BLOCK
    nb = seq // tm
    assert nb % SUBLANES == 0 and nb <= LANES - HEAD_DIM
    const = lambda b, t: (0, 0)
    return pl.pallas_call(
        _inproj_kernel,
        grid=(bsz, nb),
        in_specs=[pl.BlockSpec((1, tm, d), lambda b, t: (b, t, 0)),
                  pl.BlockSpec((1, 1, mod3.shape[-1]), lambda b, t: (b, 0, 0)),
                  pl.BlockSpec((1, d), const),
                  pl.BlockSpec(wqk_hi.shape, const),
                  pl.BlockSpec(wqk_lo.shape, const),
                  pl.BlockSpec(wvu.shape, const),
                  pl.BlockSpec((tm, LANES), lambda b, t: (t, 0)),
                  pl.BlockSpec((tm, LANES), lambda b, t: (t, 0))],
        out_specs=[pl.BlockSpec((1, ATTN_HEADS, 1, LANES, tm), lambda b, t: (b, 0, t, 0, 0)),
                   pl.BlockSpec((1, ATTN_HEADS, tm, LANES), lambda b, t: (b, 0, t, 0)),
                   pl.BlockSpec((1, ATTN_HEADS, 1, VT_ROWS, tm), lambda b, t: (b, 0, t, 0, 0)),
                   pl.BlockSpec((1, tm, SSM_WIDTH), lambda b, t: (b, t, 0)),
                   pl.BlockSpec((1, SSM_WIDTH, tm), lambda b, t: (b, 0, t))],
        out_shape=[jax.ShapeDtypeStruct((bsz, ATTN_HEADS, nb, LANES, tm), BF16),
                   jax.ShapeDtypeStruct((bsz, ATTN_HEADS, seq, LANES), BF16),
                   jax.ShapeDtypeStruct((bsz, ATTN_HEADS, nb, VT_ROWS, tm), BF16),
                   jax.ShapeDtypeStruct((bsz, seq, SSM_WIDTH), F32),
                   jax.ShapeDtypeStruct((bsz, SSM_WIDTH, seq), BF16)],
        scratch_shapes=[pltpu.VMEM((nb, ATTN_WIDTH), F32)],
        compiler_params=_params("arbitrary", "arbitrary"),
        name="inproj",
    )(x, mod3, g_mix, wqk_hi, wqk_lo, wvu, cos_t, sin_t)


ATTN_QBLOCKS = 2
RUNNING_MAX_INIT = -3.0e38


def _attn_kernel(qt_ref, ka_ref, vt_ref, o_ref, sa_scr, sb_scr, m_scr, acc_scr):
    pair = pl.program_id(1)
    i = ATTN_QBLOCKS * pair
    nh = qt_ref.shape[1]
    tq = qt_ref.shape[4]
    key = lax.broadcasted_iota(jnp.int32, (tq, tq), 0)
    qry = lax.broadcasted_iota(jnp.int32, (tq, tq), 1)
    causal = key <= qry

    def scores(qb, h, j, s_scr, mask=None):
        start = pl.multiple_of(j * tq, tq)
        s = _dot(ka_ref[0, h, pl.ds(start, tq), :], qt_ref[0, h, qb])
        if mask is not None:
            s = jnp.where(mask, s, MASK_VALUE)
        s_scr[qb * nh + h] = s

    def consume(qb, h, j, s_scr):
        c = qb * nh + h
        s = s_scr[c]
        m_old = m_scr[c]
        m_new = jnp.maximum(m_old, jnp.max(s, axis=0, keepdims=True))
        alpha = jnp.exp2(m_old - m_new)
        p = jnp.exp2(s - m_new).astype(BF16)
        acc_scr[c] = alpha * acc_scr[c] + _dot(vt_ref[0, h, j], p)
        m_scr[c] = m_new

    m_scr[...] = jnp.full(m_scr.shape, RUNNING_MAX_INIT, F32)
    acc_scr[...] = jnp.zeros(acc_scr.shape, F32)
    for h in range(nh):
        scores(0, h, i, sa_scr, causal)
        scores(1, h, i + 1, sa_scr, causal)
    for h in range(nh):
        scores(1, h, i, sb_scr)
        consume(1, h, i + 1, sa_scr)

    def body(t, carry):
        j0, j1 = 2 * t, 2 * t + 1
        prev = jnp.where(t == 0, i, j0 - 1)
        for h in range(nh):
            scores(0, h, j0, sb_scr)
            consume(0, h, prev, sa_scr)
            scores(1, h, j0, sa_scr)
            consume(1, h, prev, sb_scr)
        for h in range(nh):
            scores(0, h, j1, sa_scr)
            consume(0, h, j0, sb_scr)
            scores(1, h, j1, sb_scr)
            consume(1, h, j0, sa_scr)
        return carry

    lax.fori_loop(0, pair, body, 0)
    last = jnp.where(pair == 0, i, i - 1)
    for h in range(nh):
        consume(0, h, last, sa_scr)
        consume(1, h, last, sb_scr)

    for qb in range(ATTN_QBLOCKS):
        for vt in range(nh // HEADS_PER_TILE):
            rows = []
            for hh in range(HEADS_PER_TILE):
                acc = acc_scr[qb * nh + vt * HEADS_PER_TILE + hh]
                rows.append(acc[0:HEAD_DIM] / acc[HEAD_DIM:HEAD_DIM + 1])
            o_ref[0, qb * tq:(qb + 1) * tq, vt * LANES:(vt + 1) * LANES] = jnp.concatenate(rows, axis=0).T


def _attention(q_t, k_aug, v_t):
    bsz, nh, nb, _, tq = q_t.shape
    seq = nb * tq
    assert nb % ATTN_QBLOCKS == 0
    chains = ATTN_QBLOCKS * nh
    resident = pl.Buffered(1)
    return pl.pallas_call(
        _attn_kernel,
        grid=(bsz, nb // ATTN_QBLOCKS),
        in_specs=[pl.BlockSpec((1, nh, ATTN_QBLOCKS, LANES, tq), lambda b, g: (b, 0, g, 0, 0)),
                  pl.BlockSpec((1, nh, seq, LANES), lambda b, g: (b, 0, 0, 0), pipeline_mode=resident),
                  pl.BlockSpec((1, nh, nb, VT_ROWS, tq), lambda b, g: (b, 0, 0, 0, 0), pipeline_mode=resident)],
        out_specs=pl.BlockSpec((1, ATTN_QBLOCKS * tq, nh * HEAD_DIM), lambda b, g: (b, g, 0)),
        out_shape=jax.ShapeDtypeStruct((bsz, seq, nh * HEAD_DIM), F32),
        scratch_shapes=[pltpu.VMEM((chains, tq, tq), F32), pltpu.VMEM((chains, tq, tq), F32),
                        pltpu.VMEM((chains, 1, tq), F32), pltpu.VMEM((chains, VT_ROWS, tq), F32)],
        compiler_params=_params("arbitrary", "arbitrary"),
        name="attn",
    )(q_t, k_aug, v_t)


S5_STEPS = 64
S5_CHUNKS = 4
S5_CHUNK_GROUPS = SSM_GROUPS // S5_CHUNKS
S5_CHUNK_STATES = S5_CHUNK_GROUPS * SSM_STATE
S5_CHUNK_LANES = S5_CHUNK_GROUPS * SSM_GROUP_CH


S5_CARRY_GROUPS = LANES // SSM_STATE


def _cmul(ar, ai, br, bi):
    return ar * br - ai * bi, ar * bi + ai * br


def _s5_carry_kernel(ut_ref, lam_ref, bt_ref, o_ref):
    bsz, nrows, seq = ut_ref.shape
    nch = nrows // S5_CARRY_GROUPS
    ls = seq // N_SEG
    reps = bsz * N_SEG
    sub = lax.broadcasted_iota(jnp.int32, (SUBLANES, LANES), 0)
    lr = jnp.broadcast_to(lam_ref[0, 0:1], (SUBLANES, LANES))
    li = jnp.broadcast_to(lam_ref[0, 1:2], (SUBLANES, LANES))
    pr = jnp.ones((SUBLANES, LANES), F32)
    pi = jnp.zeros((SUBLANES, LANES), F32)
    tr = jnp.zeros((SUBLANES, LANES), F32)
    ti = jnp.zeros((SUBLANES, LANES), F32)
    for r in range(SUBLANES):
        tr = jnp.where(sub == SUBLANES - 1 - r, pr, tr)
        ti = jnp.where(sub == SUBLANES - 1 - r, pi, ti)
        pr, pi = _cmul(pr, pi, lr, li)
    span = SUBLANES
    while span < ls:
        nr, ni = _cmul(tr, ti, pr[0:1], pi[0:1])
        tr = jnp.concatenate([nr, tr], axis=0)
        ti = jnp.concatenate([ni, ti], axis=0)
        pr, pi = _cmul(pr, pi, pr, pi)
        span *= 2
    lhs = jnp.concatenate([ut_ref[b, g * nch:(g + 1) * nch, s * ls:(s + 1) * ls]
                           for g in range(S5_CARRY_GROUPS) for b in range(bsz) for s in range(N_SEG)],
                          axis=0)
    gr = _dot(lhs, tr.astype(BF16))
    gi = _dot(lhs, ti.astype(BF16))
    btr = jnp.concatenate([bt_ref[0, 0]] * (reps * S5_CARRY_GROUPS), axis=0)
    bti = jnp.concatenate([bt_ref[0, 1]] * (reps * S5_CARRY_GROUPS), axis=0)
    wr, wi = _cmul(gr, gi, btr, bti)
    fr = wr.reshape(reps * S5_CARRY_GROUPS, nch, LANES).sum(axis=1)
    fi = wi.reshape(reps * S5_CARRY_GROUPS, nch, LANES).sum(axis=1)
    grp = lax.broadcasted_iota(jnp.int32, (reps, LANES), 1) // SSM_STATE
    fin_r, fin_i = fr[0:reps], fi[0:reps]
    for g in range(1, S5_CARRY_GROUPS):
        fin_r = jnp.where(grp == g, fr[g * reps:(g + 1) * reps], fin_r)
        fin_i = jnp.where(grp == g, fi[g * reps:(g + 1) * reps], fin_i)
    rows_r, rows_i = [], []
    for b in range(bsz):
        sr = jnp.zeros((1, LANES), F32)
        si = jnp.zeros((1, LANES), F32)
        rows_r.append(sr)
        rows_i.append(si)
        for seg in range(1, N_SEG):
            r = b * N_SEG + seg - 1
            mr, mi = _cmul(sr, si, pr[0:1], pi[0:1])
            sr, si = mr + fin_r[r:r + 1], mi + fin_i[r:r + 1]
            rows_r.append(sr)
            rows_i.append(si)
    o_ref[0, 0] = jnp.concatenate(rows_r, axis=0)
    o_ref[0, 1] = jnp.concatenate(rows_i, axis=0)


def _s5_main_kernel(u_ref, bc_ref, cre_ref, cim_ref, lam_ref, sinit_ref, d_ref, wglu_ref, bglu_ref, o_ref,
                    il_scr, bu_scr, st_scr, y_scr):
    jb = pl.program_id(1)
    nseg, steps, width = u_ref.shape
    cs = S5_CHUNK_STATES

    @pl.when(jb == 0)
    def _():
        st_scr[...] = sinit_ref[...]

    for j in range(steps):
        il_scr[j * nseg:(j + 1) * nseg, :] = u_ref[:, j, :]
    u = il_scr[...]
    u_bf = u.astype(BF16)
    for c in range(S5_CHUNKS):
        bu_scr[c] = _dot(u_bf[:, c * S5_CHUNK_LANES:(c + 1) * S5_CHUNK_LANES], bc_ref[c])
    ys = []
    for c in range(S5_CHUNKS):
        lr = jnp.broadcast_to(lam_ref[c, 0:1, :], (nseg, cs))
        li = jnp.broadcast_to(lam_ref[c, 1:2, :], (nseg, cs))
        xr = st_scr[c, :, 0:cs]
        xi = st_scr[c, :, cs:2 * cs]
        for j in range(steps):
            rows = slice(j * nseg, (j + 1) * nseg)
            xr, xi = (lr * xr - li * xi + bu_scr[c, rows, 0:cs],
                      lr * xi + li * xr + bu_scr[c, rows, cs:2 * cs])
            bu_scr[c, rows, 0:cs] = xr
            bu_scr[c, rows, cs:2 * cs] = xi
        st_scr[c, :, 0:cs] = xr
        st_scr[c, :, cs:2 * cs] = xi
        ys.append(_dot(bu_scr[c, :, 0:cs].astype(BF16), cre_ref[c])
                  + _dot(bu_scr[c, :, cs:2 * cs].astype(BF16), cim_ref[c]))
    y = jnp.concatenate(ys, axis=1) + d_ref[...] * u
    y = y * (0.5 * (1.0 + jnp.tanh(math.sqrt(2.0 / math.pi) * (y + 0.044715 * (y * y * y)))))
    z = _dot(y.astype(BF16), wglu_ref[...]) + bglu_ref[...]
    y = y * jax.nn.sigmoid(z)
    for t in range(width // LANES):
        y_scr[t] = y[:, t * LANES:(t + 1) * LANES]
    for s in range(nseg):
        for t in range(width // LANES):
            o_ref[s, :, t * LANES:(t + 1) * LANES] = y_scr[t, pl.ds(s, steps, stride=nseg), :]


def _s5(u, u_t, prm, d_skip, w_glu, b_glu):
    bsz, seq, _ = u.shape
    ls = seq // N_SEG
    steps = S5_STEPS
    rows = steps * N_SEG
    ngrp = SSM_GROUPS // S5_CARRY_GROUPS
    carry = pl.pallas_call(
        _s5_carry_kernel,
        grid=(ngrp,),
        in_specs=[pl.BlockSpec((bsz, S5_CARRY_GROUPS * SSM_GROUP_CH, seq), lambda g: (0, g, 0)),
                  pl.BlockSpec((1, 2, LANES), lambda g: (g, 0, 0)),
                  pl.BlockSpec((1, 2, SSM_GROUP_CH, LANES), lambda g: (g, 0, 0, 0))],
        out_specs=pl.BlockSpec((1, 2, bsz * N_SEG, LANES), lambda g: (g, 0, 0, 0)),
        out_shape=jax.ShapeDtypeStruct((ngrp, 2, bsz * N_SEG, LANES), F32),
        compiler_params=_params("arbitrary"),
        name="s5_carry",
    )(u_t, prm["lam_carry"], prm["bt_carry"])
    sinit = carry.reshape(S5_CHUNKS, ngrp // S5_CHUNKS, 2, bsz, N_SEG, LANES)
    sinit = sinit.transpose(3, 0, 4, 2, 1, 5).reshape(bsz, S5_CHUNKS, N_SEG, 2 * S5_CHUNK_STATES)

    def const(shape):
        return pl.BlockSpec(shape, lambda b, j: (0,) * len(shape))

    blk = pl.BlockSpec((None, N_SEG, steps, SSM_WIDTH), lambda b, j: (b, 0, j, 0))
    out = pl.pallas_call(
        _s5_main_kernel,
        grid=(bsz, ls // steps),
        in_specs=[blk, const(prm["bc"].shape), const(prm["cre"].shape), const(prm["cim"].shape),
                  const(prm["lam_chunk"].shape),
                  pl.BlockSpec((None, S5_CHUNKS, N_SEG, 2 * S5_CHUNK_STATES), lambda b, j: (b, 0, 0, 0)),
                  const(d_skip.shape), const(w_glu.shape), const(b_glu.shape)],
        out_specs=blk,
        out_shape=jax.ShapeDtypeStruct((bsz, N_SEG, ls, SSM_WIDTH), F32),
        scratch_shapes=[pltpu.VMEM((rows, SSM_WIDTH), F32),
                        pltpu.VMEM((S5_CHUNKS, rows, 2 * S5_CHUNK_STATES), F32),
                        pltpu.VMEM((S5_CHUNKS, N_SEG, 2 * S5_CHUNK_STATES), F32),
                        pltpu.VMEM((SSM_WIDTH // LANES, rows, LANES), F32)],
        compiler_params=_params("arbitrary", "arbitrary"),
        name="s5_main",
    )(u.reshape(bsz, N_SEG, ls, SSM_WIDTH), prm["bc"], prm["cre"], prm["cim"], prm["lam_chunk"], sinit,
      d_skip, w_glu, b_glu)
    return out.reshape(bsz, seq, SSM_WIDTH)


FF_CHUNK = 1024


def _outmlp_kernel(x_ref, attn_ref, ssm_ref, mod_ref, ga_ref, gs_ref, gm_ref, gf_ref,
                   wout_ref, w1_ref, w2_ref, o_ref, *, final_norm):
    x = x_ref[0]
    tm, d = x.shape
    gt1 = mod_ref[0, :, 2 * d:3 * d]
    sh2 = mod_ref[0, :, 3 * d:4 * d]
    sc2 = mod_ref[0, :, 4 * d:5 * d]
    gt2 = mod_ref[0, :, 5 * d:6 * d]
    a = _rmsnorm(attn_ref[0], ga_ref[...]).astype(BF16)
    s = _rmsnorm(ssm_ref[0], gs_ref[...]).astype(BF16)
    mixed = _dot(a, wout_ref[0:ATTN_WIDTH, :]) + _dot(s, wout_ref[ATTN_WIDTH:, :])
    x1 = x + gt1 * mixed
    h = (_rmsnorm(x1, gm_ref[...]) * (1.0 + sc2) + sh2).astype(BF16)
    ff = w1_ref.shape[1]
    acc = jnp.zeros((tm, d), F32)
    for c in range(ff // FF_CHUNK):
        t = jnp.maximum(_dot(h, w1_ref[:, c * FF_CHUNK:(c + 1) * FF_CHUNK]), 0.0)
        acc = acc + _dot((t * t).astype(BF16), w2_ref[c * FF_CHUNK:(c + 1) * FF_CHUNK, :])
    x2 = x1 + gt2 * acc
    o_ref[0] = _rmsnorm(x2, gf_ref[...]) if final_norm else x2


OUTMLP_ROWS = 512


def _outmlp(x, attn, ssm, mod3, g_attn, g_ssm, g_mlp, g_final, w_out, w_fc1, w_fc2, final_norm):
    bsz, seq, d = x.shape
    tm = OUTMLP_ROWS
    const = lambda b, t: (0, 0)
    return pl.pallas_call(
        functools.partial(_outmlp_kernel, final_norm=final_norm),
        grid=(bsz, seq // tm),
        in_specs=[pl.BlockSpec((1, tm, d), lambda b, t: (b, t, 0)),
                  pl.BlockSpec((1, tm, ATTN_WIDTH), lambda b, t: (b, t, 0)),
                  pl.BlockSpec((1, tm, SSM_WIDTH), lambda b, t: (b, t, 0)),
                  pl.BlockSpec((1, 1, mod3.shape[-1]), lambda b, t: (b, 0, 0)),
                  pl.BlockSpec(g_attn.shape, const),
                  pl.BlockSpec(g_ssm.shape, const),
                  pl.BlockSpec(g_mlp.shape, const),
                  pl.BlockSpec(g_final.shape, const),
                  pl.BlockSpec(w_out.shape, const),
                  pl.BlockSpec(w_fc1.shape, const),
                  pl.BlockSpec(w_fc2.shape, const)],
        out_specs=pl.BlockSpec((1, tm, d), lambda b, t: (b, t, 0)),
        out_shape=jax.ShapeDtypeStruct((bsz, seq, d), F32),
        compiler_params=_params("arbitrary", "arbitrary"),
        name="outmlp",
    )(x, attn, ssm, mod3, g_attn, g_ssm, g_mlp, g_final, w_out, w_fc1, w_fc2)


def _rope_tables(seq):
    half = HEAD_DIM // 2
    inv_freq = ROPE_THETA ** (-jnp.arange(half, dtype=F32) / half)
    ang = jnp.arange(seq, dtype=F32)[:, None] * inv_freq[None, :]
    cos, sin = jnp.cos(ang), jnp.sin(ang)
    reps = LANES // HEAD_DIM
    cos_t = jnp.tile(cos, (1, 2 * reps))
    sin_t = jnp.tile(jnp.concatenate([-sin, sin], axis=1), (1, reps))
    return cos_t, sin_t


def _s5_params(lam_re, lam_im, log_dt, b_re, b_im, c_re, c_im):
    lr, li = lam_re.astype(F32), lam_im.astype(F32)
    dt = jnp.exp(log_dt.astype(F32))[:, None]
    mag = jnp.exp(lr * dt)
    ar, ai = mag * jnp.cos(li * dt), mag * jnp.sin(li * dt)
    den = lr * lr + li * li
    cr = ((ar - 1.0) * lr + ai * li) / den
    ci = (ai * lr - (ar - 1.0) * li) / den
    bbar_re = cr[..., None] * b_re.astype(F32) - ci[..., None] * b_im.astype(F32)
    bbar_im = cr[..., None] * b_im.astype(F32) + ci[..., None] * b_re.astype(F32)
    eye = jnp.eye(S5_CHUNK_GROUPS, dtype=F32)

    def chunked(m):
        return m.reshape((S5_CHUNKS, S5_CHUNK_GROUPS) + m.shape[1:])

    def diag_in(m):
        return jnp.einsum('cgpn,gh->cgnhp', chunked(m), eye).reshape(S5_CHUNKS, S5_CHUNK_LANES, S5_CHUNK_STATES)

    def diag_out(m):
        return jnp.einsum('cgnp,gh->cgphn', chunked(m), eye).reshape(S5_CHUNKS, S5_CHUNK_STATES, S5_CHUNK_LANES)

    ngrp = SSM_GROUPS // S5_CARRY_GROUPS

    def lane_tiles(m):
        m = m.reshape(ngrp, S5_CARRY_GROUPS, SSM_GROUP_CH, SSM_STATE)
        return m.transpose(0, 2, 1, 3).reshape(ngrp, SSM_GROUP_CH, LANES)

    return {
        "bc": jnp.concatenate([diag_in(bbar_re), diag_in(bbar_im)], axis=2).astype(BF16),
        "cre": diag_out(c_re.astype(F32)).astype(BF16),
        "cim": diag_out(-c_im.astype(F32)).astype(BF16),
        "lam_chunk": jnp.stack([ar.reshape(S5_CHUNKS, S5_CHUNK_STATES),
                                ai.reshape(S5_CHUNKS, S5_CHUNK_STATES)], axis=1),
        "lam_carry": jnp.stack([ar.reshape(ngrp, LANES), ai.reshape(ngrp, LANES)], axis=1),
        "bt_carry": jnp.stack([lane_tiles(bbar_re.transpose(0, 2, 1)),
                               lane_tiles(bbar_im.transpose(0, 2, 1))], axis=1),
    }


def kernel(x, c, w_ada, b_ada, g_mix, w_in, g_attn_out, lam_re, lam_im, log_dt, b_re, b_im, c_re, c_im,
           d_skip, w_glu, b_glu, g_ssm_out, w_out, g_mlp, w_fc1, w_fc2, g_final):
    bsz, seq, d = x.shape
    depth = w_ada.shape[0]
    assert seq % (N_SEG * MOBA_BLOCK) == 0 and (seq // N_SEG) % S5_STEPS == 0 and bsz <= SUBLANES
    cos_t, sin_t = _rope_tables(seq)
    c_pad = jnp.zeros((SUBLANES, d), F32).at[:bsz].set(c.astype(F32))

    for l in range(depth):
        mod = _adaln(c_pad, w_ada[l], b_ada[l][None, :])
        mod3 = mod[:bsz, None, :]

        wqk_hi, wqk_lo = _split_bf16(w_in[l][:, :2 * ATTN_WIDTH])
        wvu = w_in[l][:, 2 * ATTN_WIDTH:].astype(BF16)
        q_t, k_aug, v_t, u, u_t = _inproj(x, mod3, g_mix[l][None, :], wqk_hi, wqk_lo, wvu, cos_t, sin_t)
        attn = _attention(q_t, k_aug, v_t)

        prm = _s5_params(lam_re[l], lam_im[l], log_dt[l], b_re[l], b_im[l], c_re[l], c_im[l])
        ssm = _s5(u, u_t, prm, d_skip[l].reshape(1, SSM_WIDTH), w_glu[l].astype(BF16), b_glu[l][None, :])

        x = _outmlp(x, attn, ssm, mod3, g_attn_out[l][None, :], g_ssm_out[l][None, :], g_mlp[l][None, :],
                    g_final[None, :], w_out[l].astype(BF16), w_fc1[l].astype(BF16), w_fc2[l].astype(BF16),
                    final_norm=(l == depth - 1))
    return x
```

```python
import functools
import math

import jax
import jax.numpy as jnp
from jax import lax
from jax.experimental import pallas as pl
from jax.experimental.pallas import tpu as pltpu

F32 = jnp.float32
BF16 = jnp.bfloat16

HEAD_DIM = 64
ATTN_HEADS = 8
ATTN_WIDTH = ATTN_HEADS * HEAD_DIM
SSM_GROUPS = 32
SSM_GROUP_CH = 16
SSM_WIDTH = SSM_GROUPS * SSM_GROUP_CH
SSM_STATE = 64
SSM_NSTATE = SSM_GROUPS * SSM_STATE
MOBA_BLOCK = 256
MOBA_TOPK = 3
ROPE_THETA = 10000.0
EPS = 1e-6

LANES = 128
SUBLANES = 8
N_SEG = SUBLANES
HEADS_PER_TILE = LANES // HEAD_DIM
MASK_VALUE = -(2.0 ** 100)
VMEM_LIMIT_BYTES = 56 * 1024 * 1024


def _split_bf16(a):
    hi = a.astype(BF16)
    lo = (a - hi.astype(F32)).astype(BF16)
    return hi, lo


def _dot(a, b):
    return jnp.dot(a, b, preferred_element_type=F32)


def _rmsnorm(x, g):
    return x * lax.rsqrt(jnp.mean(x * x, axis=-1, keepdims=True) + EPS) * g


def _params(*semantics):
    return pltpu.CompilerParams(dimension_semantics=semantics, vmem_limit_bytes=VMEM_LIMIT_BYTES)


ADALN_STEPS = 4


def _adaln_kernel(c_ref, w_ref, b_ref, win_ref, o_ref, win_hi_ref, win_lo_ref):
    c = c_ref[...]
    s_hi, s_lo = _split_bf16(c * jax.nn.sigmoid(c))
    w_hi, w_lo = _split_bf16(w_ref[...])
    o_ref[...] = _dot(s_hi, w_hi) + _dot(s_lo, w_hi) + _dot(s_hi, w_lo) + b_ref[...]
    win_hi_ref[...], win_lo_ref[...] = _split_bf16(win_ref[...])


def _adaln(c_pad, w, b, w_in):
    rows, d = c_pad.shape
    n = w.shape[1]
    tn = n // ADALN_STEPS
    tw = w_in.shape[1] // ADALN_STEPS
    slab = pl.BlockSpec((d, tw), lambda j: (0, j))
    return pl.pallas_call(
        _adaln_kernel,
        grid=(ADALN_STEPS,),
        in_specs=[pl.BlockSpec((rows, d), lambda j: (0, 0)),
                  pl.BlockSpec((d, tn), lambda j: (0, j)),
                  pl.BlockSpec((1, tn), lambda j: (0, j)),
                  slab],
        out_specs=[pl.BlockSpec((rows, tn), lambda j: (0, j)), slab, slab],
        out_shape=[jax.ShapeDtypeStruct((rows, n), F32),
                   jax.ShapeDtypeStruct(w_in.shape, BF16),
                   jax.ShapeDtypeStruct(w_in.shape, BF16)],
        compiler_params=_params("arbitrary"),
        name="adaln",
    )(c_pad, w, b, w_in)


VT_ROWS = HEAD_DIM + 16


def _top_blocks(gate, blk, past, own):
    nb = gate.shape[0]
    g = jnp.where(past, gate, -jnp.inf)
    chosen = None
    for _ in range(MOBA_TOPK):
        best = jnp.max(g, axis=0, keepdims=True)
        first = jnp.min(jnp.where(g == best, blk, float(nb)), axis=0, keepdims=True)
        pick = blk == first
        chosen = pick if chosen is None else (chosen | pick)
        g = jnp.where(pick, -jnp.inf, g)
    return (chosen & past) | own


def _inproj_kernel(x_ref, mod_ref, g_ref, wqk_hi_ref, wqk_lo_ref, wvu_ref, cos_ref, sin_ref,
                   wout_ref, wfc1_ref, wfc2_ref,
                   qt_ref, ka_ref, vt_ref, u_ref, ut_ref, wout_bf_ref, wfc1_bf_ref, wfc2_bf_ref, km_scr):
    i = pl.program_id(1)
    wout_bf_ref[...] = wout_ref[...].astype(BF16)
    wfc1_bf_ref[...] = wfc1_ref[...].astype(BF16)
    wfc2_bf_ref[...] = wfc2_ref[...].astype(BF16)
    x = x_ref[0]
    tm, d = x.shape
    nb = km_scr.shape[0]
    sh1 = mod_ref[0, :, 0:d]
    sc1 = mod_ref[0, :, d:2 * d]
    h = _rmsnorm(x, g_ref[...]) * (1.0 + sc1) + sh1
    h_hi, h_lo = _split_bf16(h)
    qk = (_dot(h_hi, wqk_hi_ref[...]) + _dot(h_lo, wqk_hi_ref[...])
          + _dot(h_hi, wqk_lo_ref[...]))
    vu = _dot(h_hi, wvu_ref[...])
    u = vu[:, ATTN_WIDTH:]
    u_ref[0] = u
    ut_ref[0] = u.T.astype(BF16)

    reps = ATTN_WIDTH // LANES
    cos = jnp.concatenate([cos_ref[...]] * reps, axis=1)
    sin = jnp.concatenate([sin_ref[...]] * reps, axis=1)
    wide_lane = lax.broadcasted_iota(jnp.int32, (tm, ATTN_WIDTH), 1)
    first_half = (wide_lane & (HEAD_DIM // 2)) == 0

    def rope(t):
        partner = jnp.where(first_half,
                            pltpu.roll(t, ATTN_WIDTH - HEAD_DIM // 2, 1),
                            pltpu.roll(t, HEAD_DIM // 2, 1))
        return t * cos + partner * sin

    q = rope(qk[:, :ATTN_WIDTH])
    k = rope(qk[:, ATTN_WIDTH:])
    v = vu[:, :ATTN_WIDTH]

    @pl.when(i == 0)
    def _():
        km_scr[...] = jnp.zeros_like(km_scr)

    km_scr[pl.ds(i, 1), :] = jnp.sum(k, axis=0, keepdims=True) * (1.0 / tm)
    km_all = km_scr[...]

    lane = lax.broadcasted_iota(jnp.int32, (tm, LANES), 1)
    km_lane = lax.broadcasted_iota(jnp.int32, (nb, LANES), 1)
    blk_i = lax.broadcasted_iota(jnp.int32, (nb, tm), 0)
    blk = blk_i.astype(F32)
    past = blk_i < i
    own = blk_i == i
    ones_rows = jnp.where(lax.broadcasted_iota(jnp.int32, (VT_ROWS - HEAD_DIM, tm), 0) == 0, 1.0, 0.0)
    onehot = jnp.where(lane == HEAD_DIM + i, 1.0, 0.0)
    scale = HEAD_DIM ** -0.5 * math.log2(math.e)
    for tile in range(ATTN_WIDTH // LANES):
        cols = slice(tile * LANES, (tile + 1) * LANES)
        q_t = q[:, cols].T
        v_t = v[:, cols].T
        k_tile = k[:, cols]
        km = km_all[:, cols]
        qt_hi, qt_lo = _split_bf16(q_t)
        for hh in range(HEADS_PER_TILE):
            h_idx = tile * HEADS_PER_TILE + hh
            kmh_hi, kmh_lo = _split_bf16(jnp.where(km_lane // HEAD_DIM == hh, km, 0.0))
            gate = _dot(kmh_hi, qt_hi) + _dot(kmh_lo, qt_hi) + _dot(kmh_hi, qt_lo)
            bias_t = jnp.where(_top_blocks(gate, blk, past, own), 0.0, MASK_VALUE)
            head_rows = slice(hh * HEAD_DIM, (hh + 1) * HEAD_DIM)
            qt_ref[0, h_idx, 0] = jnp.concatenate(
                [q_t[head_rows] * scale, bias_t, jnp.zeros((LANES - HEAD_DIM - nb, tm), F32)],
                axis=0).astype(BF16)
            vt_ref[0, h_idx, 0] = jnp.concatenate([v_t[head_rows], ones_rows], axis=0).astype(BF16)
            kh = jnp.where(lane // HEAD_DIM == hh, k_tile, 0.0)
            if hh:
                kh = pltpu.roll(kh, LANES - hh * HEAD_DIM, 1)
            ka_ref[0, h_idx] = jnp.where(lane < HEAD_DIM, kh, onehot).astype(BF16)


def _inproj(x, mod3, g_mix, win_hi, win_lo, cos_t, sin_t, w_out, w_fc1, w_fc2):
    bsz, seq, d = x.shape
    tm = MOBA_BLOCK
    nb = seq // tm
    assert nb % SUBLANES == 0 and nb <= LANES - HEAD_DIM
    const = lambda b, t: (0, 0)
    wqk = 2 * ATTN_WIDTH
    assert win_hi.shape[1] == 2 * wqk
    steps = bsz * nb

    def slab(w):
        rows = w.shape[0] // steps
        assert rows * steps == w.shape[0] and rows % (2 * SUBLANES) == 0
        return pl.BlockSpec((rows, w.shape[1]), lambda b, t: (b * nb + t, 0))

    casts = [w_out, w_fc1, w_fc2]
    return pl.pallas_call(
        _inproj_kernel,
        grid=(bsz, nb),
        in_specs=[pl.BlockSpec((1, tm, d), lambda b, t: (b, t, 0)),
                  pl.BlockSpec((1, 1, mod3.shape[-1]), lambda b, t: (b, 0, 0)),
                  pl.BlockSpec((1, d), const),
                  pl.BlockSpec((d, wqk), const),
                  pl.BlockSpec((d, wqk), const),
                  pl.BlockSpec((d, wqk), lambda b, t: (0, 1)),
                  pl.BlockSpec((tm, LANES), lambda b, t: (t, 0)),
                  pl.BlockSpec((tm, LANES), lambda b, t: (t, 0))] + [slab(w) for w in casts],
        out_specs=[pl.BlockSpec((1, ATTN_HEADS, 1, LANES, tm), lambda b, t: (b, 0, t, 0, 0)),
                   pl.BlockSpec((1, ATTN_HEADS, tm, LANES), lambda b, t: (b, 0, t, 0)),
                   pl.BlockSpec((1, ATTN_HEADS, 1, VT_ROWS, tm), lambda b, t: (b, 0, t, 0, 0)),
                   pl.BlockSpec((1, tm, SSM_WIDTH), lambda b, t: (b, t, 0)),
                   pl.BlockSpec((1, SSM_WIDTH, tm), lambda b, t: (b, 0, t))] + [slab(w) for w in casts],
        out_shape=[jax.ShapeDtypeStruct((bsz, ATTN_HEADS, nb, LANES, tm), BF16),
                   jax.ShapeDtypeStruct((bsz, ATTN_HEADS, seq, LANES), BF16),
                   jax.ShapeDtypeStruct((bsz, ATTN_HEADS, nb, VT_ROWS, tm), BF16),
                   jax.ShapeDtypeStruct((bsz, seq, SSM_WIDTH), F32),
                   jax.ShapeDtypeStruct((bsz, SSM_WIDTH, seq), BF16)]
                  + [jax.ShapeDtypeStruct(w.shape, BF16) for w in casts],
        scratch_shapes=[pltpu.VMEM((nb, ATTN_WIDTH), F32)],
        compiler_params=_params("arbitrary", "arbitrary"),
        name="inproj",
    )(x, mod3, g_mix, win_hi, win_lo, win_hi, cos_t, sin_t, *casts)


ATTN_QBLOCKS = 2
RUNNING_MAX_INIT = -3.0e38


def _attn_kernel(qt_ref, ka_ref, vt_ref, o_ref, sa_scr, sb_scr, m_scr, acc_scr):
    pair = pl.program_id(1)
    i = ATTN_QBLOCKS * pair
    nh = qt_ref.shape[1]
    tq = qt_ref.shape[4]
    key = lax.broadcasted_iota(jnp.int32, (tq, tq), 0)
    qry = lax.broadcasted_iota(jnp.int32, (tq, tq), 1)
    causal = key <= qry

    def scores(qb, h, j, s_scr, mask=None):
        start = pl.multiple_of(j * tq, tq)
        s = _dot(ka_ref[0, h, pl.ds(start, tq), :], qt_ref[0, h, qb])
        if mask is not None:
            s = jnp.where(mask, s, MASK_VALUE)
        s_scr[qb * nh + h] = s

    def consume(qb, h, j, s_scr):
        c = qb * nh + h
        s = s_scr[c]
        m_old = m_scr[c]
        m_new = jnp.maximum(m_old, jnp.max(s, axis=0, keepdims=True))
        alpha = jnp.exp2(m_old - m_new)
        p = jnp.exp2(s - m_new).astype(BF16)
        acc_scr[c] = alpha * acc_scr[c] + _dot(vt_ref[0, h, j], p)
        m_scr[c] = m_new

    m_scr[...] = jnp.full(m_scr.shape, RUNNING_MAX_INIT, F32)
    acc_scr[...] = jnp.zeros(acc_scr.shape, F32)
    for h in range(nh):
        scores(0, h, i, sa_scr, causal)
        scores(1, h, i + 1, sa_scr, causal)
    for h in range(nh):
        scores(1, h, i, sb_scr)
        consume(1, h, i + 1, sa_scr)

    def body(t, carry):
        j0, j1 = 2 * t, 2 * t + 1
        prev = jnp.where(t == 0, i, j0 - 1)
        for h in range(nh):
            scores(0, h, j0, sb_scr)
            consume(0, h, prev, sa_scr)
            scores(1, h, j0, sa_scr)
            consume(1, h, prev, sb_scr)
        for h in range(nh):
            scores(0, h, j1, sa_scr)
            consume(0, h, j0, sb_scr)
            scores(1, h, j1, sb_scr)
            consume(1, h, j0, sa_scr)
        return carry

    lax.fori_loop(0, pair, body, 0)
    last = jnp.where(pair == 0, i, i - 1)
    for h in range(nh):
        consume(0, h, last, sa_scr)
        consume(1, h, last, sb_scr)

    for qb in range(ATTN_QBLOCKS):
        for vt in range(nh // HEADS_PER_TILE):
            rows = []
            for hh in range(HEADS_PER_TILE):
                acc = acc_scr[qb * nh + vt * HEADS_PER_TILE + hh]
                rows.append(acc[0:HEAD_DIM] / acc[HEAD_DIM:HEAD_DIM + 1])
            o_ref[0, qb * tq:(qb + 1) * tq, vt * LANES:(vt + 1) * LANES] = jnp.concatenate(rows, axis=0).T


def _attention(q_t, k_aug, v_t):
    bsz, nh, nb, _, tq = q_t.shape
    seq = nb * tq
    assert nb % ATTN_QBLOCKS == 0
    chains = ATTN_QBLOCKS * nh
    resident = pl.Buffered(1)
    return pl.pallas_call(
        _attn_kernel,
        grid=(bsz, nb // ATTN_QBLOCKS),
        in_specs=[pl.BlockSpec((1, nh, ATTN_QBLOCKS, LANES, tq), lambda b, g: (b, 0, g, 0, 0)),
                  pl.BlockSpec((1, nh, seq, LANES), lambda b, g: (b, 0, 0, 0), pipeline_mode=resident),
                  pl.BlockSpec((1, nh, nb, VT_ROWS, tq), lambda b, g: (b, 0, 0, 0, 0), pipeline_mode=resident)],
        out_specs=pl.BlockSpec((1, ATTN_QBLOCKS * tq, nh * HEAD_DIM), lambda b, g: (b, g, 0)),
        out_shape=jax.ShapeDtypeStruct((bsz, seq, nh * HEAD_DIM), F32),
        scratch_shapes=[pltpu.VMEM((chains, tq, tq), F32), pltpu.VMEM((chains, tq, tq), F32),
                        pltpu.VMEM((chains, 1, tq), F32), pltpu.VMEM((chains, VT_ROWS, tq), F32)],
        compiler_params=_params("arbitrary", "arbitrary"),
        name="attn",
    )(q_t, k_aug, v_t)


S5_STEPS = 64
S5_CHUNKS = 4
S5_CHUNK_GROUPS = SSM_GROUPS // S5_CHUNKS
S5_CHUNK_STATES = S5_CHUNK_GROUPS * SSM_STATE
S5_CHUNK_LANES = S5_CHUNK_GROUPS * SSM_GROUP_CH


S5_CARRY_GROUPS = LANES // SSM_STATE


def _cmul(ar, ai, br, bi):
    return ar * br - ai * bi, ar * bi + ai * br


def _s5_carry_kernel(ut_ref, lam_ref, bt_ref, o_ref):
    bsz, nrows, seq = ut_ref.shape
    nch = nrows // S5_CARRY_GROUPS
    ls = seq // N_SEG
    reps = bsz * N_SEG
    sub = lax.broadcasted_iota(jnp.int32, (SUBLANES, LANES), 0)
    lr = jnp.broadcast_to(lam_ref[0, 0:1], (SUBLANES, LANES))
    li = jnp.broadcast_to(lam_ref[0, 1:2], (SUBLANES, LANES))
    pr = jnp.ones((SUBLANES, LANES), F32)
    pi = jnp.zeros((SUBLANES, LANES), F32)
    tr = jnp.zeros((SUBLANES, LANES), F32)
    ti = jnp.zeros((SUBLANES, LANES), F32)
    for r in range(SUBLANES):
        tr = jnp.where(sub == SUBLANES - 1 - r, pr, tr)
        ti = jnp.where(sub == SUBLANES - 1 - r, pi, ti)
        pr, pi = _cmul(pr, pi, lr, li)
    span = SUBLANES
    while span < ls:
        nr, ni = _cmul(tr, ti, pr[0:1], pi[0:1])
        tr = jnp.concatenate([nr, tr], axis=0)
        ti = jnp.concatenate([ni, ti], axis=0)
        pr, pi = _cmul(pr, pi, pr, pi)
        span *= 2
    lhs = jnp.concatenate([ut_ref[b, g * nch:(g + 1) * nch, s * ls:(s + 1) * ls]
                           for g in range(S5_CARRY_GROUPS) for b in range(bsz) for s in range(N_SEG)],
                          axis=0)
    gr = _dot(lhs, tr.astype(BF16))
    gi = _dot(lhs, ti.astype(BF16))
    btr = jnp.concatenate([bt_ref[0, 0]] * (reps * S5_CARRY_GROUPS), axis=0)
    bti = jnp.concatenate([bt_ref[0, 1]] * (reps * S5_CARRY_GROUPS), axis=0)
    wr, wi = _cmul(gr, gi, btr, bti)
    fr = wr.reshape(reps * S5_CARRY_GROUPS, nch, LANES).sum(axis=1)
    fi = wi.reshape(reps * S5_CARRY_GROUPS, nch, LANES).sum(axis=1)
    grp = lax.broadcasted_iota(jnp.int32, (reps, LANES), 1) // SSM_STATE
    fin_r, fin_i = fr[0:reps], fi[0:reps]
    for g in range(1, S5_CARRY_GROUPS):
        fin_r = jnp.where(grp == g, fr[g * reps:(g + 1) * reps], fin_r)
        fin_i = jnp.where(grp == g, fi[g * reps:(g + 1) * reps], fin_i)
    rows_r, rows_i = [], []
    for b in range(bsz):
        sr = jnp.zeros((1, LANES), F32)
        si = jnp.zeros((1, LANES), F32)
        rows_r.append(sr)
        rows_i.append(si)
        for seg in range(1, N_SEG):
            r = b * N_SEG + seg - 1
            mr, mi = _cmul(sr, si, pr[0:1], pi[0:1])
            sr, si = mr + fin_r[r:r + 1], mi + fin_i[r:r + 1]
            rows_r.append(sr)
            rows_i.append(si)
    o_ref[0, 0] = jnp.concatenate(rows_r, axis=0)
    o_ref[0, 1] = jnp.concatenate(rows_i, axis=0)


def _s5_main_kernel(u_ref, bc_ref, cre_ref, cim_ref, lam_ref, sinit_ref, d_ref, wglu_ref, bglu_ref, o_ref,
                    il_scr, bu_scr, st_scr, y_scr):
    jb = pl.program_id(1)
    nseg, steps, width = u_ref.shape
    cs = S5_CHUNK_STATES

    @pl.when(jb == 0)
    def _():
        st_scr[...] = sinit_ref[...]

    for j in range(steps):
        il_scr[j * nseg:(j + 1) * nseg, :] = u_ref[:, j, :]
    u = il_scr[...]
    u_bf = u.astype(BF16)
    for c in range(S5_CHUNKS):
        bu_scr[c] = _dot(u_bf[:, c * S5_CHUNK_LANES:(c + 1) * S5_CHUNK_LANES], bc_ref[c])
    ys = []
    for c in range(S5_CHUNKS):
        lr = jnp.broadcast_to(lam_ref[c, 0:1, :], (nseg, cs))
        li = jnp.broadcast_to(lam_ref[c, 1:2, :], (nseg, cs))
        xr = st_scr[c, :, 0:cs]
        xi = st_scr[c, :, cs:2 * cs]
        for j in range(steps):
            rows = slice(j * nseg, (j + 1) * nseg)
            xr, xi = (lr * xr - li * xi + bu_scr[c, rows, 0:cs],
                      lr * xi + li * xr + bu_scr[c, rows, cs:2 * cs])
            bu_scr[c, rows, 0:cs] = xr
            bu_scr[c, rows, cs:2 * cs] = xi
        st_scr[c, :, 0:cs] = xr
        st_scr[c, :, cs:2 * cs] = xi
        ys.append(_dot(bu_scr[c, :, 0:cs].astype(BF16), cre_ref[c])
                  + _dot(bu_scr[c, :, cs:2 * cs].astype(BF16), cim_ref[c]))
    y = jnp.concatenate(ys, axis=1) + d_ref[...] * u
    y = y * (0.5 * (1.0 + jnp.tanh(math.sqrt(2.0 / math.pi) * (y + 0.044715 * (y * y * y)))))
    z = _dot(y.astype(BF16), wglu_ref[...]) + bglu_ref[...]
    y = y * jax.nn.sigmoid(z)
    for t in range(width // LANES):
        y_scr[t] = y[:, t * LANES:(t + 1) * LANES]
    for s in range(nseg):
        for t in range(width // LANES):
            o_ref[s, :, t * LANES:(t + 1) * LANES] = y_scr[t, pl.ds(s, steps, stride=nseg), :]


def _s5(u, u_t, prm, d_skip, w_glu, b_glu):
    bsz, seq, _ = u.shape
    ls = seq // N_SEG
    steps = S5_STEPS
    rows = steps * N_SEG
    ngrp = SSM_GROUPS // S5_CARRY_GROUPS
    carry = pl.pallas_call(
        _s5_carry_kernel,
        grid=(ngrp,),
        in_specs=[pl.BlockSpec((bsz, S5_CARRY_GROUPS * SSM_GROUP_CH, seq), lambda g: (0, g, 0)),
                  pl.BlockSpec((1, 2, LANES), lambda g: (g, 0, 0)),
                  pl.BlockSpec((1, 2, SSM_GROUP_CH, LANES), lambda g: (g, 0, 0, 0))],
        out_specs=pl.BlockSpec((1, 2, bsz * N_SEG, LANES), lambda g: (g, 0, 0, 0)),
        out_shape=jax.ShapeDtypeStruct((ngrp, 2, bsz * N_SEG, LANES), F32),
        compiler_params=_params("arbitrary"),
        name="s5_carry",
    )(u_t, prm["lam_carry"], prm["bt_carry"])
    sinit = carry.reshape(S5_CHUNKS, ngrp // S5_CHUNKS, 2, bsz, N_SEG, LANES)
    sinit = sinit.transpose(3, 0, 4, 2, 1, 5).reshape(bsz, S5_CHUNKS, N_SEG, 2 * S5_CHUNK_STATES)

    def const(shape):
        return pl.BlockSpec(shape, lambda b, j: (0,) * len(shape))

    blk = pl.BlockSpec((None, N_SEG, steps, SSM_WIDTH), lambda b, j: (b, 0, j, 0))
    out = pl.pallas_call(
        _s5_main_kernel,
        grid=(bsz, ls // steps),
        in_specs=[blk, const(prm["bc"].shape), const(prm["cre"].shape), const(prm["cim"].shape),
                  const(prm["lam_chunk"].shape),
                  pl.BlockSpec((None, S5_CHUNKS, N_SEG, 2 * S5_CHUNK_STATES), lambda b, j: (b, 0, 0, 0)),
                  const(d_skip.shape), const(w_glu.shape), const(b_glu.shape)],
        out_specs=blk,
        out_shape=jax.ShapeDtypeStruct((bsz, N_SEG, ls, SSM_WIDTH), F32),
        scratch_shapes=[pltpu.VMEM((rows, SSM_WIDTH), F32),
                        pltpu.VMEM((S5_CHUNKS, rows, 2 * S5_CHUNK_STATES), F32),
                        pltpu.VMEM((S5_CHUNKS, N_SEG, 2 * S5_CHUNK_STATES), F32),
                        pltpu.VMEM((SSM_WIDTH // LANES, rows, LANES), F32)],
        compiler_params=_params("arbitrary", "arbitrary"),
        name="s5_main",
    )(u.reshape(bsz, N_SEG, ls, SSM_WIDTH), prm["bc"], prm["cre"], prm["cim"], prm["lam_chunk"], sinit,
      d_skip, w_glu, b_glu)
    return out.reshape(bsz, seq, SSM_WIDTH)


FF_CHUNK = 1024


def _outmlp_kernel(x_ref, attn_ref, ssm_ref, mod_ref, ga_ref, gs_ref, gm_ref, gf_ref,
                   wout_ref, w1_ref, w2_ref, o_ref, *, final_norm):
    x = x_ref[0]
    tm, d = x.shape
    gt1 = mod_ref[0, :, 2 * d:3 * d]
    sh2 = mod_ref[0, :, 3 * d:4 * d]
    sc2 = mod_ref[0, :, 4 * d:5 * d]
    gt2 = mod_ref[0, :, 5 * d:6 * d]
    a = _rmsnorm(attn_ref[0], ga_ref[...]).astype(BF16)
    s = _rmsnorm(ssm_ref[0], gs_ref[...]).astype(BF16)
    mixed = _dot(a, wout_ref[0:ATTN_WIDTH, :]) + _dot(s, wout_ref[ATTN_WIDTH:, :])
    x1 = x + gt1 * mixed
    h = (_rmsnorm(x1, gm_ref[...]) * (1.0 + sc2) + sh2).astype(BF16)
    ff = w1_ref.shape[1]
    acc = jnp.zeros((tm, d), F32)
    for c in range(ff // FF_CHUNK):
        t = jnp.maximum(_dot(h, w1_ref[:, c * FF_CHUNK:(c + 1) * FF_CHUNK]), 0.0)
        acc = acc + _dot((t * t).astype(BF16), w2_ref[c * FF_CHUNK:(c + 1) * FF_CHUNK, :])
    x2 = x1 + gt2 * acc
    o_ref[0] = _rmsnorm(x2, gf_ref[...]) if final_norm else x2


OUTMLP_ROWS = 512


def _outmlp(x, attn, ssm, mod3, g_attn, g_ssm, g_mlp, g_final, w_out, w_fc1, w_fc2, final_norm):
    bsz, seq, d = x.shape
    tm = OUTMLP_ROWS
    const = lambda b, t: (0, 0)
    return pl.pallas_call(
        functools.partial(_outmlp_kernel, final_norm=final_norm),
        grid=(bsz, seq // tm),
        in_specs=[pl.BlockSpec((1, tm, d), lambda b, t: (b, t, 0)),
                  pl.BlockSpec((1, tm, ATTN_WIDTH), lambda b, t: (b, t, 0)),
                  pl.BlockSpec((1, tm, SSM_WIDTH), lambda b, t: (b, t, 0)),
                  pl.BlockSpec((1, 1, mod3.shape[-1]), lambda b, t: (b, 0, 0)),
                  pl.BlockSpec(g_attn.shape, const),
                  pl.BlockSpec(g_ssm.shape, const),
                  pl.BlockSpec(g_mlp.shape, const),
                  pl.BlockSpec(g_final.shape, const),
                  pl.BlockSpec(w_out.shape, const),
                  pl.BlockSpec(w_fc1.shape, const),
                  pl.BlockSpec(w_fc2.shape, const)],
        out_specs=pl.BlockSpec((1, tm, d), lambda b, t: (b, t, 0)),
        out_shape=jax.ShapeDtypeStruct((bsz, seq, d), F32),
        compiler_params=_params("arbitrary", "arbitrary"),
        name="outmlp",
    )(x, attn, ssm, mod3, g_attn, g_ssm, g_mlp, g_final, w_out, w_fc1, w_fc2)


def _rope_tables(seq):
    half = HEAD_DIM // 2
    inv_freq = ROPE_THETA ** (-jnp.arange(half, dtype=F32) / half)
    ang = jnp.arange(seq, dtype=F32)[:, None] * inv_freq[None, :]
    cos, sin = jnp.cos(ang), jnp.sin(ang)
    reps = LANES // HEAD_DIM
    cos_t = jnp.tile(cos, (1, 2 * reps))
    sin_t = jnp.tile(jnp.concatenate([-sin, sin], axis=1), (1, reps))
    return cos_t, sin_t


def _s5_params(lam_re, lam_im, log_dt, b_re, b_im, c_re, c_im):
    lr, li = lam_re.astype(F32), lam_im.astype(F32)
    dt = jnp.exp(log_dt.astype(F32))[:, None]
    mag = jnp.exp(lr * dt)
    ar, ai = mag * jnp.cos(li * dt), mag * jnp.sin(li * dt)
    den = lr * lr + li * li
    cr = ((ar - 1.0) * lr + ai * li) / den
    ci = (ai * lr - (ar - 1.0) * li) / den
    bbar_re = cr[..., None] * b_re.astype(F32) - ci[..., None] * b_im.astype(F32)
    bbar_im = cr[..., None] * b_im.astype(F32) + ci[..., None] * b_re.astype(F32)
    eye = jnp.eye(S5_CHUNK_GROUPS, dtype=F32)

    def chunked(m):
        return m.reshape((S5_CHUNKS, S5_CHUNK_GROUPS) + m.shape[1:])

    def diag_in(m):
        return jnp.einsum('cgpn,gh->cgnhp', chunked(m), eye).reshape(S5_CHUNKS, S5_CHUNK_LANES, S5_CHUNK_STATES)

    def diag_out(m):
        return jnp.einsum('cgnp,gh->cgphn', chunked(m), eye).reshape(S5_CHUNKS, S5_CHUNK_STATES, S5_CHUNK_LANES)

    ngrp = SSM_GROUPS // S5_CARRY_GROUPS

    def lane_tiles(m):
        m = m.reshape(ngrp, S5_CARRY_GROUPS, SSM_GROUP_CH, SSM_STATE)
        return m.transpose(0, 2, 1, 3).reshape(ngrp, SSM_GROUP_CH, LANES)

    return {
        "bc": jnp.concatenate([diag_in(bbar_re), diag_in(bbar_im)], axis=2).astype(BF16),
        "cre": diag_out(c_re.astype(F32)).astype(BF16),
        "cim": diag_out(-c_im.astype(F32)).astype(BF16),
        "lam_chunk": jnp.stack([ar.reshape(S5_CHUNKS, S5_CHUNK_STATES),
                                ai.reshape(S5_CHUNKS, S5_CHUNK_STATES)], axis=1),
        "lam_carry": jnp.stack([ar.reshape(ngrp, LANES), ai.reshape(ngrp, LANES)], axis=1),
        "bt_carry": jnp.stack([lane_tiles(bbar_re.transpose(0, 2, 1)),
                               lane_tiles(bbar_im.transpose(0, 2, 1))], axis=1),
    }


def kernel(x, c, w_ada, b_ada, g_mix, w_in, g_attn_out, lam_re, lam_im, log_dt, b_re, b_im, c_re, c_im,
           d_skip, w_glu, b_glu, g_ssm_out, w_out, g_mlp, w_fc1, w_fc2, g_final):
    bsz, seq, d = x.shape
    depth = w_ada.shape[0]
    assert seq % (N_SEG * MOBA_BLOCK) == 0 and (seq // N_SEG) % S5_STEPS == 0 and bsz <= SUBLANES
    cos_t, sin_t = _rope_tables(seq)
    c_pad = jnp.zeros((SUBLANES, d), F32).at[:bsz].set(c.astype(F32))

    for l in range(depth):
        mod, win_hi, win_lo = _adaln(c_pad, w_ada[l], b_ada[l][None, :], w_in[l])
        mod3 = mod[:bsz, None, :]

        q_t, k_aug, v_t, u, u_t, wout_bf, wfc1_bf, wfc2_bf = _inproj(
            x, mod3, g_mix[l][None, :], win_hi, win_lo, cos_t, sin_t, w_out[l], w_fc1[l], w_fc2[l])
        attn = _attention(q_t, k_aug, v_t)

        prm = _s5_params(lam_re[l], lam_im[l], log_dt[l], b_re[l], b_im[l], c_re[l], c_im[l])
        ssm = _s5(u, u_t, prm, d_skip[l].reshape(1, SSM_WIDTH), w_glu[l].astype(BF16), b_glu[l][None, :])

        x = _outmlp(x, attn, ssm, mod3, g_attn_out[l][None, :], g_ssm_out[l][None, :], g_mlp[l][None, :],
                    g_final[None, :], wout_bf, wfc1_bf, wfc2_bf, final_norm=(l == depth - 1))
    return x
```

```python
import functools
import math

import jax
import jax.numpy as jnp
from jax import lax
from jax.experimental import pallas as pl
from jax.experimental.pallas import tpu as pltpu

F32 = jnp.float32
BF16 = jnp.bfloat16

HEAD_DIM = 64
ATTN_HEADS = 8
ATTN_WIDTH = ATTN_HEADS * HEAD_DIM
SSM_GROUPS = 32
SSM_GROUP_CH = 16
SSM_WIDTH = SSM_GROUPS * SSM_GROUP_CH
SSM_STATE = 64
SSM_NSTATE = SSM_GROUPS * SSM_STATE
MOBA_BLOCK = 256
MOBA_TOPK = 3
ROPE_THETA = 10000.0
EPS = 1e-6

LANES = 128
SUBLANES = 8
N_SEG = SUBLANES
HEADS_PER_TILE = LANES // HEAD_DIM
MASK_VALUE = -(2.0 ** 100)
VMEM_LIMIT_BYTES = 56 * 1024 * 1024


def _split_bf16(a):
    hi = a.astype(BF16)
    lo = (a - hi.astype(F32)).astype(BF16)
    return hi, lo


def _dot(a, b):
    return jnp.dot(a, b, preferred_element_type=F32)


def _rmsnorm(x, g):
    return x * lax.rsqrt(jnp.mean(x * x, axis=-1, keepdims=True) + EPS) * g


def _params(*semantics):
    return pltpu.CompilerParams(dimension_semantics=semantics, vmem_limit_bytes=VMEM_LIMIT_BYTES)


ADALN_STEPS = 4


def _adaln_kernel(c_ref, w_ref, b_ref, win_ref, o_ref, win_hi_ref, win_lo_ref):
    c = c_ref[...]
    s_hi, s_lo = _split_bf16(c * jax.nn.sigmoid(c))
    w_hi, w_lo = _split_bf16(w_ref[...])
    o_ref[...] = _dot(s_hi, w_hi) + _dot(s_lo, w_hi) + _dot(s_hi, w_lo) + b_ref[...]
    win_hi_ref[...], win_lo_ref[...] = _split_bf16(win_ref[...])


def _adaln(c_pad, w, b, w_in):
    rows, d = c_pad.shape
    n = w.shape[1]
    tn = n // ADALN_STEPS
    tw = w_in.shape[1] // ADALN_STEPS
    slab = pl.BlockSpec((d, tw), lambda j: (0, j))
    return pl.pallas_call(
        _adaln_kernel,
        grid=(ADALN_STEPS,),
        in_specs=[pl.BlockSpec((rows, d), lambda j: (0, 0)),
                  pl.BlockSpec((d, tn), lambda j: (0, j)),
                  pl.BlockSpec((1, tn), lambda j: (0, j)),
                  slab],
        out_specs=[pl.BlockSpec((rows, tn), lambda j: (0, j)), slab, slab],
        out_shape=[jax.ShapeDtypeStruct((rows, n), F32),
                   jax.ShapeDtypeStruct(w_in.shape, BF16),
                   jax.ShapeDtypeStruct(w_in.shape, BF16)],
        compiler_params=_params("arbitrary"),
        name="adaln",
    )(c_pad, w, b, w_in)


VT_ROWS = HEAD_DIM + 16


def _top_blocks(gate, blk, past, own):
    nb = gate.shape[0]
    g = jnp.where(past, gate, -jnp.inf)
    chosen = None
    for _ in range(MOBA_TOPK):
        best = jnp.max(g, axis=0, keepdims=True)
        first = jnp.min(jnp.where(g == best, blk, float(nb)), axis=0, keepdims=True)
        pick = blk == first
        chosen = pick if chosen is None else (chosen | pick)
        g = jnp.where(pick, -jnp.inf, g)
    return (chosen & past) | own


def _inproj_kernel(x_ref, mod_ref, g_ref, wqk_hi_ref, wqk_lo_ref, wvu_ref, cos_ref, sin_ref,
                   wout_ref, wfc1_ref, wfc2_ref,
                   qt_ref, ka_ref, vt_ref, u_ref, ut_ref, wout_bf_ref, wfc1_bf_ref, wfc2_bf_ref, km_scr):
    i = pl.program_id(1)
    wout_bf_ref[...] = wout_ref[...].astype(BF16)
    wfc1_bf_ref[...] = wfc1_ref[...].astype(BF16)
    wfc2_bf_ref[...] = wfc2_ref[...].astype(BF16)
    x = x_ref[0]
    tm, d = x.shape
    nb = km_scr.shape[0]
    sh1 = mod_ref[0, :, 0:d]
    sc1 = mod_ref[0, :, d:2 * d]
    h = _rmsnorm(x, g_ref[...]) * (1.0 + sc1) + sh1
    h_hi, h_lo = _split_bf16(h)
    qk = (_dot(h_hi, wqk_hi_ref[...]) + _dot(h_lo, wqk_hi_ref[...])
          + _dot(h_hi, wqk_lo_ref[...]))
    vu = _dot(h_hi, wvu_ref[...])
    u = vu[:, ATTN_WIDTH:]
    u_ref[0] = u
    ut_ref[0] = u.T.astype(BF16)

    reps = ATTN_WIDTH // LANES
    cos = jnp.concatenate([cos_ref[...]] * reps, axis=1)
    sin = jnp.concatenate([sin_ref[...]] * reps, axis=1)
    wide_lane = lax.broadcasted_iota(jnp.int32, (tm, ATTN_WIDTH), 1)
    first_half = (wide_lane & (HEAD_DIM // 2)) == 0

    def rope(t):
        partner = jnp.where(first_half,
                            pltpu.roll(t, ATTN_WIDTH - HEAD_DIM // 2, 1),
                            pltpu.roll(t, HEAD_DIM // 2, 1))
        return t * cos + partner * sin

    q = rope(qk[:, :ATTN_WIDTH])
    k = rope(qk[:, ATTN_WIDTH:])
    v = vu[:, :ATTN_WIDTH]

    @pl.when(i == 0)
    def _():
        km_scr[...] = jnp.zeros_like(km_scr)

    km_scr[pl.ds(i, 1), :] = jnp.sum(k, axis=0, keepdims=True) * (1.0 / tm)
    km_all = km_scr[...]

    lane = lax.broadcasted_iota(jnp.int32, (tm, LANES), 1)
    km_lane = lax.broadcasted_iota(jnp.int32, (nb, LANES), 1)
    blk_i = lax.broadcasted_iota(jnp.int32, (nb, tm), 0)
    blk = blk_i.astype(F32)
    past = blk_i < i
    own = blk_i == i
    ones_rows = jnp.where(lax.broadcasted_iota(jnp.int32, (VT_ROWS - HEAD_DIM, tm), 0) == 0, 1.0, 0.0)
    onehot = jnp.where(lane == HEAD_DIM + i, 1.0, 0.0)
    scale = HEAD_DIM ** -0.5 * math.log2(math.e)
    for tile in range(ATTN_WIDTH // LANES):
        cols = slice(tile * LANES, (tile + 1) * LANES)
        q_t = q[:, cols].T
        v_t = v[:, cols].T
        k_tile = k[:, cols]
        km = km_all[:, cols]
        qt_hi, qt_lo = _split_bf16(q_t)
        for hh in range(HEADS_PER_TILE):
            h_idx = tile * HEADS_PER_TILE + hh
            kmh_hi, kmh_lo = _split_bf16(jnp.where(km_lane // HEAD_DIM == hh, km, 0.0))
            gate = _dot(kmh_hi, qt_hi) + _dot(kmh_lo, qt_hi) + _dot(kmh_hi, qt_lo)
            bias_t = jnp.where(_top_blocks(gate, blk, past, own), 0.0, MASK_VALUE)
            head_rows = slice(hh * HEAD_DIM, (hh + 1) * HEAD_DIM)
            qt_ref[0, h_idx, 0] = jnp.concatenate(
                [q_t[head_rows] * scale, bias_t, jnp.zeros((LANES - HEAD_DIM - nb, tm), F32)],
                axis=0).astype(BF16)
            vt_ref[0, h_idx, 0] = jnp.concatenate([v_t[head_rows], ones_rows], axis=0).astype(BF16)
            kh = jnp.where(lane // HEAD_DIM == hh, k_tile, 0.0)
            if hh:
                kh = pltpu.roll(kh, LANES - hh * HEAD_DIM, 1)
            ka_ref[0, h_idx] = jnp.where(lane < HEAD_DIM, kh, onehot).astype(BF16)


def _inproj(x, mod3, g_mix, win_hi, win_lo, cos_t, sin_t, w_out, w_fc1, w_fc2):
    bsz, seq, d = x.shape
    tm = MOBA_BLOCK
    nb = seq // tm
    assert nb % SUBLANES == 0 and nb <= LANES - HEAD_DIM
    const = lambda b, t: (0, 0)
    wqk = 2 * ATTN_WIDTH
    assert win_hi.shape[1] == 2 * wqk
    steps = bsz * nb

    def slab(w):
        rows = w.shape[0] // steps
        assert rows * steps == w.shape[0] and rows % (2 * SUBLANES) == 0
        return pl.BlockSpec((rows, w.shape[1]), lambda b, t: (b * nb + t, 0))

    casts = [w_out, w_fc1, w_fc2]
    return pl.pallas_call(
        _inproj_kernel,
        grid=(bsz, nb),
        in_specs=[pl.BlockSpec((1, tm, d), lambda b, t: (b, t, 0)),
                  pl.BlockSpec((1, 1, mod3.shape[-1]), lambda b, t: (b, 0, 0)),
                  pl.BlockSpec((1, d), const),
                  pl.BlockSpec((d, wqk), const),
                  pl.BlockSpec((d, wqk), const),
                  pl.BlockSpec((d, wqk), lambda b, t: (0, 1)),
                  pl.BlockSpec((tm, LANES), lambda b, t: (t, 0)),
                  pl.BlockSpec((tm, LANES), lambda b, t: (t, 0))] + [slab(w) for w in casts],
        out_specs=[pl.BlockSpec((1, ATTN_HEADS, 1, LANES, tm), lambda b, t: (b, 0, t, 0, 0)),
                   pl.BlockSpec((1, ATTN_HEADS, tm, LANES), lambda b, t: (b, 0, t, 0)),
                   pl.BlockSpec((1, ATTN_HEADS, 1, VT_ROWS, tm), lambda b, t: (b, 0, t, 0, 0)),
                   pl.BlockSpec((1, tm, SSM_WIDTH), lambda b, t: (b, t, 0)),
                   pl.BlockSpec((1, SSM_WIDTH, tm), lambda b, t: (b, 0, t))] + [slab(w) for w in casts],
        out_shape=[jax.ShapeDtypeStruct((bsz, ATTN_HEADS, nb, LANES, tm), BF16),
                   jax.ShapeDtypeStruct((bsz, ATTN_HEADS, seq, LANES), BF16),
                   jax.ShapeDtypeStruct((bsz, ATTN_HEADS, nb, VT_ROWS, tm), BF16),
                   jax.ShapeDtypeStruct((bsz, seq, SSM_WIDTH), F32),
                   jax.ShapeDtypeStruct((bsz, SSM_WIDTH, seq), BF16)]
                  + [jax.ShapeDtypeStruct(w.shape, BF16) for w in casts],
        scratch_shapes=[pltpu.VMEM((nb, ATTN_WIDTH), F32)],
        compiler_params=_params("arbitrary", "arbitrary"),
        name="inproj",
    )(x, mod3, g_mix, win_hi, win_lo, win_hi, cos_t, sin_t, *casts)


ATTN_QBLOCKS = 2
RUNNING_MAX_INIT = -3.0e38


def _attn_kernel(qt_ref, ka_ref, vt_ref, o_ref, sa_scr, sb_scr, m_scr, acc_scr):
    pair = pl.program_id(1)
    i = ATTN_QBLOCKS * pair
    nh = qt_ref.shape[1]
    tq = qt_ref.shape[4]
    key = lax.broadcasted_iota(jnp.int32, (tq, tq), 0)
    qry = lax.broadcasted_iota(jnp.int32, (tq, tq), 1)
    causal = key <= qry

    def scores(qb, h, j, s_scr, mask=None):
        start = pl.multiple_of(j * tq, tq)
        s = _dot(ka_ref[0, h, pl.ds(start, tq), :], qt_ref[0, h, qb])
        if mask is not None:
            s = jnp.where(mask, s, MASK_VALUE)
        s_scr[qb * nh + h] = s

    def consume(qb, h, j, s_scr):
        c = qb * nh + h
        s = s_scr[c]
        m_old = m_scr[c]
        m_new = jnp.maximum(m_old, jnp.max(s, axis=0, keepdims=True))
        alpha = jnp.exp2(m_old - m_new)
        p = jnp.exp2(s - m_new).astype(BF16)
        acc_scr[c] = alpha * acc_scr[c] + _dot(vt_ref[0, h, j], p)
        m_scr[c] = m_new

    m_scr[...] = jnp.full(m_scr.shape, RUNNING_MAX_INIT, F32)
    acc_scr[...] = jnp.zeros(acc_scr.shape, F32)
    for h in range(nh):
        scores(0, h, i, sa_scr, causal)
        scores(1, h, i + 1, sa_scr, causal)
    for h in range(nh):
        scores(1, h, i, sb_scr)
        consume(1, h, i + 1, sa_scr)

    def two_blocks(j0):
        j1 = j0 + 1
        prev = jnp.where(j0 == 0, i, j0 - 1)
        for h in range(nh):
            scores(0, h, j0, sb_scr)
            consume(0, h, prev, sa_scr)
            scores(1, h, j0, sa_scr)
            consume(1, h, prev, sb_scr)
        for h in range(nh):
            scores(0, h, j1, sa_scr)
            consume(0, h, j0, sb_scr)
            scores(1, h, j1, sb_scr)
            consume(1, h, j0, sa_scr)

    def body(t, carry):
        two_blocks(4 * t)
        two_blocks(4 * t + 2)
        return carry

    lax.fori_loop(0, pair // 2, body, 0)

    @pl.when(pair % 2 == 1)
    def _():
        two_blocks(i - 2)

    last = jnp.where(pair == 0, i, i - 1)
    for h in range(nh):
        consume(0, h, last, sa_scr)
        consume(1, h, last, sb_scr)

    for qb in range(ATTN_QBLOCKS):
        for vt in range(nh // HEADS_PER_TILE):
            rows = []
            for hh in range(HEADS_PER_TILE):
                acc = acc_scr[qb * nh + vt * HEADS_PER_TILE + hh]
                rows.append(acc[0:HEAD_DIM] / acc[HEAD_DIM:HEAD_DIM + 1])
            o_ref[0, qb * tq:(qb + 1) * tq, vt * LANES:(vt + 1) * LANES] = jnp.concatenate(rows, axis=0).T


def _attention(q_t, k_aug, v_t):
    bsz, nh, nb, _, tq = q_t.shape
    seq = nb * tq
    assert nb % ATTN_QBLOCKS == 0
    chains = ATTN_QBLOCKS * nh
    resident = pl.Buffered(1)
    return pl.pallas_call(
        _attn_kernel,
        grid=(bsz, nb // ATTN_QBLOCKS),
        in_specs=[pl.BlockSpec((1, nh, ATTN_QBLOCKS, LANES, tq), lambda b, g: (b, 0, g, 0, 0)),
                  pl.BlockSpec((1, nh, seq, LANES), lambda b, g: (b, 0, 0, 0), pipeline_mode=resident),
                  pl.BlockSpec((1, nh, nb, VT_ROWS, tq), lambda b, g: (b, 0, 0, 0, 0), pipeline_mode=resident)],
        out_specs=pl.BlockSpec((1, ATTN_QBLOCKS * tq, nh * HEAD_DIM), lambda b, g: (b, g, 0)),
        out_shape=jax.ShapeDtypeStruct((bsz, seq, nh * HEAD_DIM), F32),
        scratch_shapes=[pltpu.VMEM((chains, tq, tq), F32), pltpu.VMEM((chains, tq, tq), F32),
                        pltpu.VMEM((chains, 1, tq), F32), pltpu.VMEM((chains, VT_ROWS, tq), F32)],
        compiler_params=_params("arbitrary", "arbitrary"),
        name="attn",
    )(q_t, k_aug, v_t)


S5_STEPS = 64
S5_CHUNKS = 4
S5_CHUNK_GROUPS = SSM_GROUPS // S5_CHUNKS
S5_CHUNK_STATES = S5_CHUNK_GROUPS * SSM_STATE
S5_CHUNK_LANES = S5_CHUNK_GROUPS * SSM_GROUP_CH


S5_CARRY_GROUPS = LANES // SSM_STATE


def _cmul(ar, ai, br, bi):
    return ar * br - ai * bi, ar * bi + ai * br


def _s5_carry_kernel(ut_ref, lam_ref, bt_ref, o_ref):
    bsz, nrows, seq = ut_ref.shape
    nch = nrows // S5_CARRY_GROUPS
    ls = seq // N_SEG
    reps = bsz * N_SEG
    sub = lax.broadcasted_iota(jnp.int32, (SUBLANES, LANES), 0)
    lr = jnp.broadcast_to(lam_ref[0, 0:1], (SUBLANES, LANES))
    li = jnp.broadcast_to(lam_ref[0, 1:2], (SUBLANES, LANES))
    pr = jnp.ones((SUBLANES, LANES), F32)
    pi = jnp.zeros((SUBLANES, LANES), F32)
    tr = jnp.zeros((SUBLANES, LANES), F32)
    ti = jnp.zeros((SUBLANES, LANES), F32)
    for r in range(SUBLANES):
        tr = jnp.where(sub == SUBLANES - 1 - r, pr, tr)
        ti = jnp.where(sub == SUBLANES - 1 - r, pi, ti)
        pr, pi = _cmul(pr, pi, lr, li)
    span = SUBLANES
    while span < ls:
        nr, ni = _cmul(tr, ti, pr[0:1], pi[0:1])
        tr = jnp.concatenate([nr, tr], axis=0)
        ti = jnp.concatenate([ni, ti], axis=0)
        pr, pi = _cmul(pr, pi, pr, pi)
        span *= 2
    lhs = jnp.concatenate([ut_ref[b, g * nch:(g + 1) * nch, s * ls:(s + 1) * ls]
                           for g in range(S5_CARRY_GROUPS) for b in range(bsz) for s in range(N_SEG)],
                          axis=0)
    gr = _dot(lhs, tr.astype(BF16))
    gi = _dot(lhs, ti.astype(BF16))
    btr = jnp.concatenate([bt_ref[0, 0]] * (reps * S5_CARRY_GROUPS), axis=0)
    bti = jnp.concatenate([bt_ref[0, 1]] * (reps * S5_CARRY_GROUPS), axis=0)
    wr, wi = _cmul(gr, gi, btr, bti)
    fr = wr.reshape(reps * S5_CARRY_GROUPS, nch, LANES).sum(axis=1)
    fi = wi.reshape(reps * S5_CARRY_GROUPS, nch, LANES).sum(axis=1)
    grp = lax.broadcasted_iota(jnp.int32, (reps, LANES), 1) // SSM_STATE
    fin_r, fin_i = fr[0:reps], fi[0:reps]
    for g in range(1, S5_CARRY_GROUPS):
        fin_r = jnp.where(grp == g, fr[g * reps:(g + 1) * reps], fin_r)
        fin_i = jnp.where(grp == g, fi[g * reps:(g + 1) * reps], fin_i)
    rows_r, rows_i = [], []
    for b in range(bsz):
        sr = jnp.zeros((1, LANES), F32)
        si = jnp.zeros((1, LANES), F32)
        rows_r.append(sr)
        rows_i.append(si)
        for seg in range(1, N_SEG):
            r = b * N_SEG + seg - 1
            mr, mi = _cmul(sr, si, pr[0:1], pi[0:1])
            sr, si = mr + fin_r[r:r + 1], mi + fin_i[r:r + 1]
            rows_r.append(sr)
            rows_i.append(si)
    o_ref[0, 0] = jnp.concatenate(rows_r, axis=0)
    o_ref[0, 1] = jnp.concatenate(rows_i, axis=0)


def _s5_main_kernel(u_ref, bc_ref, cre_ref, cim_ref, lam_ref, sinit_ref, d_ref, wglu_ref, bglu_ref, o_ref,
                    il_scr, bu_scr, st_scr, y_scr):
    jb = pl.program_id(1)
    nseg, steps, width = u_ref.shape
    cs = S5_CHUNK_STATES

    @pl.when(jb == 0)
    def _():
        st_scr[...] = sinit_ref[...]

    for j in range(steps):
        il_scr[j * nseg:(j + 1) * nseg, :] = u_ref[:, j, :]
    u = il_scr[...]
    u_bf = u.astype(BF16)
    for c in range(S5_CHUNKS):
        bu_scr[c] = _dot(u_bf[:, c * S5_CHUNK_LANES:(c + 1) * S5_CHUNK_LANES], bc_ref[c])
    ys = []
    for c in range(S5_CHUNKS):
        lr = jnp.broadcast_to(lam_ref[c, 0:1, :], (nseg, cs))
        li = jnp.broadcast_to(lam_ref[c, 1:2, :], (nseg, cs))
        xr = st_scr[c, :, 0:cs]
        xi = st_scr[c, :, cs:2 * cs]
        for j in range(steps):
            rows = slice(j * nseg, (j + 1) * nseg)
            xr, xi = (lr * xr - li * xi + bu_scr[c, rows, 0:cs],
                      lr * xi + li * xr + bu_scr[c, rows, cs:2 * cs])
            bu_scr[c, rows, 0:cs] = xr
            bu_scr[c, rows, cs:2 * cs] = xi
        st_scr[c, :, 0:cs] = xr
        st_scr[c, :, cs:2 * cs] = xi
        ys.append(_dot(bu_scr[c, :, 0:cs].astype(BF16), cre_ref[c])
                  + _dot(bu_scr[c, :, cs:2 * cs].astype(BF16), cim_ref[c]))
    y = jnp.concatenate(ys, axis=1) + d_ref[...] * u
    y = y * (0.5 * (1.0 + jnp.tanh(math.sqrt(2.0 / math.pi) * (y + 0.044715 * (y * y * y)))))
    z = _dot(y.astype(BF16), wglu_ref[...]) + bglu_ref[...]
    y = y * jax.nn.sigmoid(z)
    for t in range(width // LANES):
        y_scr[t] = y[:, t * LANES:(t + 1) * LANES]
    for s in range(nseg):
        for t in range(width // LANES):
            o_ref[s, :, t * LANES:(t + 1) * LANES] = y_scr[t, pl.ds(s, steps, stride=nseg), :]


def _s5(u, u_t, prm, d_skip, w_glu, b_glu):
    bsz, seq, _ = u.shape
    ls = seq // N_SEG
    steps = S5_STEPS
    rows = steps * N_SEG
    ngrp = SSM_GROUPS // S5_CARRY_GROUPS
    carry = pl.pallas_call(
        _s5_carry_kernel,
        grid=(ngrp,),
        in_specs=[pl.BlockSpec((bsz, S5_CARRY_GROUPS * SSM_GROUP_CH, seq), lambda g: (0, g, 0)),
                  pl.BlockSpec((1, 2, LANES), lambda g: (g, 0, 0)),
                  pl.BlockSpec((1, 2, SSM_GROUP_CH, LANES), lambda g: (g, 0, 0, 0))],
        out_specs=pl.BlockSpec((1, 2, bsz * N_SEG, LANES), lambda g: (g, 0, 0, 0)),
        out_shape=jax.ShapeDtypeStruct((ngrp, 2, bsz * N_SEG, LANES), F32),
        compiler_params=_params("arbitrary"),
        name="s5_carry",
    )(u_t, prm["lam_carry"], prm["bt_carry"])
    sinit = carry.reshape(S5_CHUNKS, ngrp // S5_CHUNKS, 2, bsz, N_SEG, LANES)
    sinit = sinit.transpose(3, 0, 4, 2, 1, 5).reshape(bsz, S5_CHUNKS, N_SEG, 2 * S5_CHUNK_STATES)

    def const(shape):
        return pl.BlockSpec(shape, lambda b, j: (0,) * len(shape))

    blk = pl.BlockSpec((None, N_SEG, steps, SSM_WIDTH), lambda b, j: (b, 0, j, 0))
    out = pl.pallas_call(
        _s5_main_kernel,
        grid=(bsz, ls // steps),
        in_specs=[blk, const(prm["bc"].shape), const(prm["cre"].shape), const(prm["cim"].shape),
                  const(prm["lam_chunk"].shape),
                  pl.BlockSpec((None, S5_CHUNKS, N_SEG, 2 * S5_CHUNK_STATES), lambda b, j: (b, 0, 0, 0)),
                  const(d_skip.shape), const(w_glu.shape), const(b_glu.shape)],
        out_specs=blk,
        out_shape=jax.ShapeDtypeStruct((bsz, N_SEG, ls, SSM_WIDTH), F32),
        scratch_shapes=[pltpu.VMEM((rows, SSM_WIDTH), F32),
                        pltpu.VMEM((S5_CHUNKS, rows, 2 * S5_CHUNK_STATES), F32),
                        pltpu.VMEM((S5_CHUNKS, N_SEG, 2 * S5_CHUNK_STATES), F32),
                        pltpu.VMEM((SSM_WIDTH // LANES, rows, LANES), F32)],
        compiler_params=_params("arbitrary", "arbitrary"),
        name="s5_main",
    )(u.reshape(bsz, N_SEG, ls, SSM_WIDTH), prm["bc"], prm["cre"], prm["cim"], prm["lam_chunk"], sinit,
      d_skip, w_glu, b_glu)
    return out.reshape(bsz, seq, SSM_WIDTH)


FF_CHUNK = 1024


def _outmlp_kernel(x_ref, attn_ref, ssm_ref, mod_ref, ga_ref, gs_ref, gm_ref, gf_ref,
                   wout_ref, w1_ref, w2_ref, o_ref, *, final_norm):
    x = x_ref[0]
    tm, d = x.shape
    gt1 = mod_ref[0, :, 2 * d:3 * d]
    sh2 = mod_ref[0, :, 3 * d:4 * d]
    sc2 = mod_ref[0, :, 4 * d:5 * d]
    gt2 = mod_ref[0, :, 5 * d:6 * d]
    a = _rmsnorm(attn_ref[0], ga_ref[...]).astype(BF16)
    s = _rmsnorm(ssm_ref[0], gs_ref[...]).astype(BF16)
    mixed = _dot(a, wout_ref[0:ATTN_WIDTH, :]) + _dot(s, wout_ref[ATTN_WIDTH:, :])
    x1 = x + gt1 * mixed
    h = (_rmsnorm(x1, gm_ref[...]) * (1.0 + sc2) + sh2).astype(BF16)
    ff = w1_ref.shape[1]
    acc = jnp.zeros((tm, d), F32)
    for c in range(ff // FF_CHUNK):
        t = jnp.maximum(_dot(h, w1_ref[:, c * FF_CHUNK:(c + 1) * FF_CHUNK]), 0.0)
        acc = acc + _dot((t * t).astype(BF16), w2_ref[c * FF_CHUNK:(c + 1) * FF_CHUNK, :])
    x2 = x1 + gt2 * acc
    o_ref[0] = _rmsnorm(x2, gf_ref[...]) if final_norm else x2


OUTMLP_ROWS = 512


def _outmlp(x, attn, ssm, mod3, g_attn, g_ssm, g_mlp, g_final, w_out, w_fc1, w_fc2, final_norm):
    bsz, seq, d = x.shape
    tm = OUTMLP_ROWS
    const = lambda b, t: (0, 0)
    return pl.pallas_call(
        functools.partial(_outmlp_kernel, final_norm=final_norm),
        grid=(bsz, seq // tm),
        in_specs=[pl.BlockSpec((1, tm, d), lambda b, t: (b, t, 0)),
                  pl.BlockSpec((1, tm, ATTN_WIDTH), lambda b, t: (b, t, 0)),
                  pl.BlockSpec((1, tm, SSM_WIDTH), lambda b, t: (b, t, 0)),
                  pl.BlockSpec((1, 1, mod3.shape[-1]), lambda b, t: (b, 0, 0)),
                  pl.BlockSpec(g_attn.shape, const),
                  pl.BlockSpec(g_ssm.shape, const),
                  pl.BlockSpec(g_mlp.shape, const),
                  pl.BlockSpec(g_final.shape, const),
                  pl.BlockSpec(w_out.shape, const),
                  pl.BlockSpec(w_fc1.shape, const),
                  pl.BlockSpec(w_fc2.shape, const)],
        out_specs=pl.BlockSpec((1, tm, d), lambda b, t: (b, t, 0)),
        out_shape=jax.ShapeDtypeStruct((bsz, seq, d), F32),
        compiler_params=_params("arbitrary", "arbitrary"),
        name="outmlp",
    )(x, attn, ssm, mod3, g_attn, g_ssm, g_mlp, g_final, w_out, w_fc1, w_fc2)


def _rope_tables(seq):
    half = HEAD_DIM // 2
    inv_freq = ROPE_THETA ** (-jnp.arange(half, dtype=F32) / half)
    ang = jnp.arange(seq, dtype=F32)[:, None] * inv_freq[None, :]
    cos, sin = jnp.cos(ang), jnp.sin(ang)
    reps = LANES // HEAD_DIM
    cos_t = jnp.tile(cos, (1, 2 * reps))
    sin_t = jnp.tile(jnp.concatenate([-sin, sin], axis=1), (1, reps))
    return cos_t, sin_t


def _s5_params(lam_re, lam_im, log_dt, b_re, b_im, c_re, c_im):
    lr, li = lam_re.astype(F32), lam_im.astype(F32)
    dt = jnp.exp(log_dt.astype(F32))[:, None]
    mag = jnp.exp(lr * dt)
    ar, ai = mag * jnp.cos(li * dt), mag * jnp.sin(li * dt)
    den = lr * lr + li * li
    cr = ((ar - 1.0) * lr + ai * li) / den
    ci = (ai * lr - (ar - 1.0) * li) / den
    bbar_re = cr[..., None] * b_re.astype(F32) - ci[..., None] * b_im.astype(F32)
    bbar_im = cr[..., None] * b_im.astype(F32) + ci[..., None] * b_re.astype(F32)
    eye = jnp.eye(S5_CHUNK_GROUPS, dtype=F32)

    def chunked(m):
        return m.reshape((S5_CHUNKS, S5_CHUNK_GROUPS) + m.shape[1:])

    def diag_in(m):
        return jnp.einsum('cgpn,gh->cgnhp', chunked(m), eye).reshape(S5_CHUNKS, S5_CHUNK_LANES, S5_CHUNK_STATES)

    def diag_out(m):
        return jnp.einsum('cgnp,gh->cgphn', chunked(m), eye).reshape(S5_CHUNKS, S5_CHUNK_STATES, S5_CHUNK_LANES)

    ngrp = SSM_GROUPS // S5_CARRY_GROUPS

    def lane_tiles(m):
        m = m.reshape(ngrp, S5_CARRY_GROUPS, SSM_GROUP_CH, SSM_STATE)
        return m.transpose(0, 2, 1, 3).reshape(ngrp, SSM_GROUP_CH, LANES)

    return {
        "bc": jnp.concatenate([diag_in(bbar_re), diag_in(bbar_im)], axis=2).astype(BF16),
        "cre": diag_out(c_re.astype(F32)).astype(BF16),
        "cim": diag_out(-c_im.astype(F32)).astype(BF16),
        "lam_chunk": jnp.stack([ar.reshape(S5_CHUNKS, S5_CHUNK_STATES),
                                ai.reshape(S5_CHUNKS, S5_CHUNK_STATES)], axis=1),
        "lam_carry": jnp.stack([ar.reshape(ngrp, LANES), ai.reshape(ngrp, LANES)], axis=1),
        "bt_carry": jnp.stack([lane_tiles(bbar_re.transpose(0, 2, 1)),
                               lane_tiles(bbar_im.transpose(0, 2, 1))], axis=1),
    }


def kernel(x, c, w_ada, b_ada, g_mix, w_in, g_attn_out, lam_re, lam_im, log_dt, b_re, b_im, c_re, c_im,
           d_skip, w_glu, b_glu, g_ssm_out, w_out, g_mlp, w_fc1, w_fc2, g_final):
    bsz, seq, d = x.shape
    depth = w_ada.shape[0]
    assert seq % (N_SEG * MOBA_BLOCK) == 0 and (seq // N_SEG) % S5_STEPS == 0 and bsz <= SUBLANES
    cos_t, sin_t = _rope_tables(seq)
    c_pad = jnp.zeros((SUBLANES, d), F32).at[:bsz].set(c.astype(F32))

    for l in range(depth):
        mod, win_hi, win_lo = _adaln(c_pad, w_ada[l], b_ada[l][None, :], w_in[l])
        mod3 = mod[:bsz, None, :]

        q_t, k_aug, v_t, u, u_t, wout_bf, wfc1_bf, wfc2_bf = _inproj(
            x, mod3, g_mix[l][None, :], win_hi, win_lo, cos_t, sin_t, w_out[l], w_fc1[l], w_fc2[l])
        attn = _attention(q_t, k_aug, v_t)

        prm = _s5_params(lam_re[l], lam_im[l], log_dt[l], b_re[l], b_im[l], c_re[l], c_im[l])
        ssm = _s5(u, u_t, prm, d_skip[l].reshape(1, SSM_WIDTH), w_glu[l].astype(BF16), b_glu[l][None, :])

        x = _outmlp(x, attn, ssm, mod3, g_attn_out[l][None, :], g_ssm_out[l][None, :], g_mlp[l][None, :],
                    g_final[None, :], wout_bf, wfc1_bf, wfc2_bf, final_norm=(l == depth - 1))
    return x
```

```python
import functools
import math

import jax
import jax.numpy as jnp
from jax import lax
from jax.experimental import pallas as pl
from jax.experimental.pallas import tpu as pltpu

F32 = jnp.float32
BF16 = jnp.bfloat16

HEAD_DIM = 64
ATTN_HEADS = 8
ATTN_WIDTH = ATTN_HEADS * HEAD_DIM
SSM_GROUPS = 32
SSM_GROUP_CH = 16
SSM_WIDTH = SSM_GROUPS * SSM_GROUP_CH
SSM_STATE = 64
SSM_NSTATE = SSM_GROUPS * SSM_STATE
MOBA_BLOCK = 256
MOBA_TOPK = 3
ROPE_THETA = 10000.0
EPS = 1e-6

LANES = 128
SUBLANES = 8
N_SEG = SUBLANES
HEADS_PER_TILE = LANES // HEAD_DIM
MASK_VALUE = -(2.0 ** 100)
VMEM_LIMIT_BYTES = 56 * 1024 * 1024


def _split_bf16(a):
    hi = a.astype(BF16)
    lo = (a - hi.astype(F32)).astype(BF16)
    return hi, lo


def _dot(a, b):
    return jnp.dot(a, b, preferred_element_type=F32)


def _rmsnorm(x, g):
    return x * lax.rsqrt(jnp.mean(x * x, axis=-1, keepdims=True) + EPS) * g


def _params(*semantics):
    return pltpu.CompilerParams(dimension_semantics=semantics, vmem_limit_bytes=VMEM_LIMIT_BYTES)


ADALN_STEPS = 4


def _adaln_kernel(c_ref, w_ref, b_ref, win_ref, o_ref, win_hi_ref, win_lo_ref):
    c = c_ref[...]
    s_hi, s_lo = _split_bf16(c * jax.nn.sigmoid(c))
    w_hi, w_lo = _split_bf16(w_ref[...])
    o_ref[...] = _dot(s_hi, w_hi) + _dot(s_lo, w_hi) + _dot(s_hi, w_lo) + b_ref[...]
    win_hi_ref[...], win_lo_ref[...] = _split_bf16(win_ref[...])


def _adaln(c_pad, w, b, w_in):
    rows, d = c_pad.shape
    n = w.shape[1]
    tn = n // ADALN_STEPS
    tw = w_in.shape[1] // ADALN_STEPS
    slab = pl.BlockSpec((d, tw), lambda j: (0, j))
    return pl.pallas_call(
        _adaln_kernel,
        grid=(ADALN_STEPS,),
        in_specs=[pl.BlockSpec((rows, d), lambda j: (0, 0)),
                  pl.BlockSpec((d, tn), lambda j: (0, j)),
                  pl.BlockSpec((1, tn), lambda j: (0, j)),
                  slab],
        out_specs=[pl.BlockSpec((rows, tn), lambda j: (0, j)), slab, slab],
        out_shape=[jax.ShapeDtypeStruct((rows, n), F32),
                   jax.ShapeDtypeStruct(w_in.shape, BF16),
                   jax.ShapeDtypeStruct(w_in.shape, BF16)],
        compiler_params=_params("arbitrary"),
        name="adaln",
    )(c_pad, w, b, w_in)


VT_ROWS = HEAD_DIM + 16


def _top_blocks(gate, blk, past, own):
    nb = gate.shape[0]
    g = jnp.where(past, gate, -jnp.inf)
    chosen = None
    for _ in range(MOBA_TOPK):
        best = jnp.max(g, axis=0, keepdims=True)
        first = jnp.min(jnp.where(g == best, blk, float(nb)), axis=0, keepdims=True)
        pick = blk == first
        chosen = pick if chosen is None else (chosen | pick)
        g = jnp.where(pick, -jnp.inf, g)
    return (chosen & past) | own


def _inproj_kernel(x_ref, mod_ref, g_ref, wqk_hi_ref, wqk_lo_ref, wvu_ref, cos_ref, sin_ref,
                   wout_ref, wfc1_ref, wfc2_ref,
                   qt_ref, ka_ref, vt_ref, u_ref, ut_ref, wout_bf_ref, wfc1_bf_ref, wfc2_bf_ref, km_scr):
    i = pl.program_id(1)
    wout_bf_ref[...] = wout_ref[...].astype(BF16)
    wfc1_bf_ref[...] = wfc1_ref[...].astype(BF16)
    wfc2_bf_ref[...] = wfc2_ref[...].astype(BF16)
    x = x_ref[0]
    tm, d = x.shape
    nb = km_scr.shape[0]
    sh1 = mod_ref[0, :, 0:d]
    sc1 = mod_ref[0, :, d:2 * d]
    h = _rmsnorm(x, g_ref[...]) * (1.0 + sc1) + sh1
    h_hi, h_lo = _split_bf16(h)
    qk = (_dot(h_hi, wqk_hi_ref[...]) + _dot(h_lo, wqk_hi_ref[...])
          + _dot(h_hi, wqk_lo_ref[...]))
    vu = _dot(h_hi, wvu_ref[...])
    u = vu[:, ATTN_WIDTH:]
    u_ref[0] = u
    ut_ref[0, 0] = u.T.astype(BF16)

    reps = ATTN_WIDTH // LANES
    cos = jnp.concatenate([cos_ref[...]] * reps, axis=1)
    sin = jnp.concatenate([sin_ref[...]] * reps, axis=1)
    wide_lane = lax.broadcasted_iota(jnp.int32, (tm, ATTN_WIDTH), 1)
    first_half = (wide_lane & (HEAD_DIM // 2)) == 0

    def rope(t):
        partner = jnp.where(first_half,
                            pltpu.roll(t, ATTN_WIDTH - HEAD_DIM // 2, 1),
                            pltpu.roll(t, HEAD_DIM // 2, 1))
        return t * cos + partner * sin

    q = rope(qk[:, :ATTN_WIDTH])
    k = rope(qk[:, ATTN_WIDTH:])
    v = vu[:, :ATTN_WIDTH]

    @pl.when(i == 0)
    def _():
        km_scr[...] = jnp.zeros_like(km_scr)

    km_scr[pl.ds(i, 1), :] = jnp.sum(k, axis=0, keepdims=True) * (1.0 / tm)
    km_all = km_scr[...]

    lane = lax.broadcasted_iota(jnp.int32, (tm, LANES), 1)
    km_lane = lax.broadcasted_iota(jnp.int32, (nb, LANES), 1)
    blk_i = lax.broadcasted_iota(jnp.int32, (nb, tm), 0)
    blk = blk_i.astype(F32)
    past = blk_i < i
    own = blk_i == i
    ones_rows = jnp.where(lax.broadcasted_iota(jnp.int32, (VT_ROWS - HEAD_DIM, tm), 0) == 0, 1.0, 0.0)
    onehot = jnp.where(lane == HEAD_DIM + i, 1.0, 0.0)
    scale = HEAD_DIM ** -0.5 * math.log2(math.e)
    for tile in range(ATTN_WIDTH // LANES):
        cols = slice(tile * LANES, (tile + 1) * LANES)
        q_t = q[:, cols].T
        v_t = v[:, cols].T
        k_tile = k[:, cols]
        km = km_all[:, cols]
        qt_hi, qt_lo = _split_bf16(q_t)
        for hh in range(HEADS_PER_TILE):
            h_idx = tile * HEADS_PER_TILE + hh
            kmh_hi, kmh_lo = _split_bf16(jnp.where(km_lane // HEAD_DIM == hh, km, 0.0))
            gate = _dot(kmh_hi, qt_hi) + _dot(kmh_lo, qt_hi) + _dot(kmh_hi, qt_lo)
            bias_t = jnp.where(_top_blocks(gate, blk, past, own), 0.0, MASK_VALUE)
            head_rows = slice(hh * HEAD_DIM, (hh + 1) * HEAD_DIM)
            qt_ref[0, h_idx, 0] = jnp.concatenate(
                [q_t[head_rows] * scale, bias_t, jnp.zeros((LANES - HEAD_DIM - nb, tm), F32)],
                axis=0).astype(BF16)
            vt_ref[0, h_idx, 0] = jnp.concatenate([v_t[head_rows], ones_rows], axis=0).astype(BF16)
            kh = jnp.where(lane // HEAD_DIM == hh, k_tile, 0.0)
            if hh:
                kh = pltpu.roll(kh, LANES - hh * HEAD_DIM, 1)
            ka_ref[0, h_idx] = jnp.where(lane < HEAD_DIM, kh, onehot).astype(BF16)


def _inproj(x, mod3, g_mix, win_hi, win_lo, cos_t, sin_t, w_out, w_fc1, w_fc2):
    bsz, seq, d = x.shape
    tm = MOBA_BLOCK
    nb = seq // tm
    assert nb % SUBLANES == 0 and nb <= LANES - HEAD_DIM
    const = lambda b, t: (0, 0)
    wqk = 2 * ATTN_WIDTH
    assert win_hi.shape[1] == 2 * wqk
    steps = bsz * nb

    def slab(w):
        rows = w.shape[0] // steps
        assert rows * steps == w.shape[0] and rows % (2 * SUBLANES) == 0
        return pl.BlockSpec((rows, w.shape[1]), lambda b, t: (b * nb + t, 0))

    casts = [w_out, w_fc1, w_fc2]
    return pl.pallas_call(
        _inproj_kernel,
        grid=(bsz, nb),
        in_specs=[pl.BlockSpec((1, tm, d), lambda b, t: (b, t, 0)),
                  pl.BlockSpec((1, 1, mod3.shape[-1]), lambda b, t: (b, 0, 0)),
                  pl.BlockSpec((1, d), const),
                  pl.BlockSpec((d, wqk), const),
                  pl.BlockSpec((d, wqk), const),
                  pl.BlockSpec((d, wqk), lambda b, t: (0, 1)),
                  pl.BlockSpec((tm, LANES), lambda b, t: (t, 0)),
                  pl.BlockSpec((tm, LANES), lambda b, t: (t, 0))] + [slab(w) for w in casts],
        out_specs=[pl.BlockSpec((1, ATTN_HEADS, 1, LANES, tm), lambda b, t: (b, 0, t, 0, 0)),
                   pl.BlockSpec((1, ATTN_HEADS, tm, LANES), lambda b, t: (b, 0, t, 0)),
                   pl.BlockSpec((1, ATTN_HEADS, 1, VT_ROWS, tm), lambda b, t: (b, 0, t, 0, 0)),
                   pl.BlockSpec((1, tm, SSM_WIDTH), lambda b, t: (b, t, 0)),
                   pl.BlockSpec((1, 1, SSM_WIDTH, tm), lambda b, t: (b, t, 0, 0))] + [slab(w) for w in casts],
        out_shape=[jax.ShapeDtypeStruct((bsz, ATTN_HEADS, nb, LANES, tm), BF16),
                   jax.ShapeDtypeStruct((bsz, ATTN_HEADS, seq, LANES), BF16),
                   jax.ShapeDtypeStruct((bsz, ATTN_HEADS, nb, VT_ROWS, tm), BF16),
                   jax.ShapeDtypeStruct((bsz, seq, SSM_WIDTH), F32),
                   jax.ShapeDtypeStruct((bsz, nb, SSM_WIDTH, tm), BF16)]
                  + [jax.ShapeDtypeStruct(w.shape, BF16) for w in casts],
        scratch_shapes=[pltpu.VMEM((nb, ATTN_WIDTH), F32)],
        compiler_params=_params("arbitrary", "arbitrary"),
        name="inproj",
    )(x, mod3, g_mix, win_hi, win_lo, win_hi, cos_t, sin_t, *casts)


ATTN_QBLOCKS = 2
RUNNING_MAX_INIT = -3.0e38


def _attn_kernel(qt_ref, ka_ref, vt_ref, o_ref, sa_scr, sb_scr, m_scr, acc_scr):
    pair = pl.program_id(1)
    i = ATTN_QBLOCKS * pair
    nh = qt_ref.shape[1]
    tq = qt_ref.shape[4]
    key = lax.broadcasted_iota(jnp.int32, (tq, tq), 0)
    qry = lax.broadcasted_iota(jnp.int32, (tq, tq), 1)
    causal = key <= qry

    def scores(qb, h, j, s_scr, mask=None):
        start = pl.multiple_of(j * tq, tq)
        s = _dot(ka_ref[0, h, pl.ds(start, tq), :], qt_ref[0, h, qb])
        if mask is not None:
            s = jnp.where(mask, s, MASK_VALUE)
        s_scr[qb * nh + h] = s

    def consume(qb, h, j, s_scr):
        c = qb * nh + h
        s = s_scr[c]
        m_old = m_scr[c]
        m_new = jnp.maximum(m_old, jnp.max(s, axis=0, keepdims=True))
        alpha = jnp.exp2(m_old - m_new)
        p = jnp.exp2(s - m_new).astype(BF16)
        acc_scr[c] = alpha * acc_scr[c] + _dot(vt_ref[0, h, j], p)
        m_scr[c] = m_new

    m_scr[...] = jnp.full(m_scr.shape, RUNNING_MAX_INIT, F32)
    acc_scr[...] = jnp.zeros(acc_scr.shape, F32)
    for h in range(nh):
        scores(0, h, i, sa_scr, causal)
        scores(1, h, i + 1, sa_scr, causal)
    for h in range(nh):
        scores(1, h, i, sb_scr)
        consume(1, h, i + 1, sa_scr)

    def two_blocks(j0):
        j1 = j0 + 1
        prev = jnp.where(j0 == 0, i, j0 - 1)
        for h in range(nh):
            scores(0, h, j0, sb_scr)
            consume(0, h, prev, sa_scr)
            scores(1, h, j0, sa_scr)
            consume(1, h, prev, sb_scr)
        for h in range(nh):
            scores(0, h, j1, sa_scr)
            consume(0, h, j0, sb_scr)
            scores(1, h, j1, sb_scr)
            consume(1, h, j0, sa_scr)

    def body(t, carry):
        two_blocks(4 * t)
        two_blocks(4 * t + 2)
        return carry

    lax.fori_loop(0, pair // 2, body, 0)

    @pl.when(pair % 2 == 1)
    def _():
        two_blocks(i - 2)

    last = jnp.where(pair == 0, i, i - 1)
    for h in range(nh):
        consume(0, h, last, sa_scr)
        consume(1, h, last, sb_scr)

    for qb in range(ATTN_QBLOCKS):
        for vt in range(nh // HEADS_PER_TILE):
            rows = []
            for hh in range(HEADS_PER_TILE):
                acc = acc_scr[qb * nh + vt * HEADS_PER_TILE + hh]
                rows.append(acc[0:HEAD_DIM] / acc[HEAD_DIM:HEAD_DIM + 1])
            o_ref[0, qb * tq:(qb + 1) * tq, vt * LANES:(vt + 1) * LANES] = jnp.concatenate(rows, axis=0).T


def _attention(q_t, k_aug, v_t):
    bsz, nh, nb, _, tq = q_t.shape
    seq = nb * tq
    assert nb % ATTN_QBLOCKS == 0
    chains = ATTN_QBLOCKS * nh
    resident = pl.Buffered(1)
    return pl.pallas_call(
        _attn_kernel,
        grid=(bsz, nb // ATTN_QBLOCKS),
        in_specs=[pl.BlockSpec((1, nh, ATTN_QBLOCKS, LANES, tq), lambda b, g: (b, 0, g, 0, 0)),
                  pl.BlockSpec((1, nh, seq, LANES), lambda b, g: (b, 0, 0, 0), pipeline_mode=resident),
                  pl.BlockSpec((1, nh, nb, VT_ROWS, tq), lambda b, g: (b, 0, 0, 0, 0), pipeline_mode=resident)],
        out_specs=pl.BlockSpec((1, ATTN_QBLOCKS * tq, nh * HEAD_DIM), lambda b, g: (b, g, 0)),
        out_shape=jax.ShapeDtypeStruct((bsz, seq, nh * HEAD_DIM), F32),
        scratch_shapes=[pltpu.VMEM((chains, tq, tq), F32), pltpu.VMEM((chains, tq, tq), F32),
                        pltpu.VMEM((chains, 1, tq), F32), pltpu.VMEM((chains, VT_ROWS, tq), F32)],
        compiler_params=_params("arbitrary", "arbitrary"),
        name="attn",
    )(q_t, k_aug, v_t)


S5_STEPS = 64
S5_CHUNKS = 4
S5_CHUNK_GROUPS = SSM_GROUPS // S5_CHUNKS
S5_CHUNK_STATES = S5_CHUNK_GROUPS * SSM_STATE
S5_CHUNK_LANES = S5_CHUNK_GROUPS * SSM_GROUP_CH


S5_CARRY_GROUPS = LANES // SSM_STATE


def _cmul(ar, ai, br, bi):
    return ar * br - ai * bi, ar * bi + ai * br


def _s5_carry_kernel(ut_ref, lam_ref, bt_ref, o_ref):
    bsz, nblk, nrows, tm = ut_ref.shape
    nch = nrows // S5_CARRY_GROUPS
    seg_blocks = nblk // N_SEG
    ls = seg_blocks * tm
    reps = bsz * N_SEG
    sub = lax.broadcasted_iota(jnp.int32, (SUBLANES, LANES), 0)
    lr = jnp.broadcast_to(lam_ref[0, 0:1], (SUBLANES, LANES))
    li = jnp.broadcast_to(lam_ref[0, 1:2], (SUBLANES, LANES))
    pr = jnp.ones((SUBLANES, LANES), F32)
    pi = jnp.zeros((SUBLANES, LANES), F32)
    tr = jnp.zeros((SUBLANES, LANES), F32)
    ti = jnp.zeros((SUBLANES, LANES), F32)
    for r in range(SUBLANES):
        tr = jnp.where(sub == SUBLANES - 1 - r, pr, tr)
        ti = jnp.where(sub == SUBLANES - 1 - r, pi, ti)
        pr, pi = _cmul(pr, pi, lr, li)
    span = SUBLANES
    while span < ls:
        nr, ni = _cmul(tr, ti, pr[0:1], pi[0:1])
        tr = jnp.concatenate([nr, tr], axis=0)
        ti = jnp.concatenate([ni, ti], axis=0)
        pr, pi = _cmul(pr, pi, pr, pi)
        span *= 2
    def seg_rows(g, b, s):
        return jnp.concatenate([ut_ref[b, s * seg_blocks + k, g * nch:(g + 1) * nch, :]
                                for k in range(seg_blocks)], axis=1)

    lhs = jnp.concatenate([seg_rows(g, b, s)
                           for g in range(S5_CARRY_GROUPS) for b in range(bsz) for s in range(N_SEG)],
                          axis=0)
    gr = _dot(lhs, tr.astype(BF16))
    gi = _dot(lhs, ti.astype(BF16))
    btr = jnp.concatenate([bt_ref[0, 0]] * (reps * S5_CARRY_GROUPS), axis=0)
    bti = jnp.concatenate([bt_ref[0, 1]] * (reps * S5_CARRY_GROUPS), axis=0)
    wr, wi = _cmul(gr, gi, btr, bti)
    fr = wr.reshape(reps * S5_CARRY_GROUPS, nch, LANES).sum(axis=1)
    fi = wi.reshape(reps * S5_CARRY_GROUPS, nch, LANES).sum(axis=1)
    grp = lax.broadcasted_iota(jnp.int32, (reps, LANES), 1) // SSM_STATE
    fin_r, fin_i = fr[0:reps], fi[0:reps]
    for g in range(1, S5_CARRY_GROUPS):
        fin_r = jnp.where(grp == g, fr[g * reps:(g + 1) * reps], fin_r)
        fin_i = jnp.where(grp == g, fi[g * reps:(g + 1) * reps], fin_i)
    rows_r, rows_i = [], []
    for b in range(bsz):
        sr = jnp.zeros((1, LANES), F32)
        si = jnp.zeros((1, LANES), F32)
        rows_r.append(sr)
        rows_i.append(si)
        for seg in range(1, N_SEG):
            r = b * N_SEG + seg - 1
            mr, mi = _cmul(sr, si, pr[0:1], pi[0:1])
            sr, si = mr + fin_r[r:r + 1], mi + fin_i[r:r + 1]
            rows_r.append(sr)
            rows_i.append(si)
    o_ref[0, 0] = jnp.concatenate(rows_r, axis=0)
    o_ref[0, 1] = jnp.concatenate(rows_i, axis=0)


def _s5_main_kernel(u_ref, bc_ref, cre_ref, cim_ref, lam_ref, sinit_ref, d_ref, wglu_ref, bglu_ref, o_ref,
                    il_scr, bu_scr, st_scr, y_scr):
    jb = pl.program_id(1)
    nseg, steps, width = u_ref.shape
    cs = S5_CHUNK_STATES

    @pl.when(jb == 0)
    def _():
        st_scr[...] = sinit_ref[...]

    for j in range(steps):
        il_scr[j * nseg:(j + 1) * nseg, :] = u_ref[:, j, :]
    u = il_scr[...]
    u_bf = u.astype(BF16)
    for c in range(S5_CHUNKS):
        bu_scr[c] = _dot(u_bf[:, c * S5_CHUNK_LANES:(c + 1) * S5_CHUNK_LANES], bc_ref[c])
    ys = []
    for c in range(S5_CHUNKS):
        lr = jnp.broadcast_to(lam_ref[c, 0:1, :], (nseg, cs))
        li = jnp.broadcast_to(lam_ref[c, 1:2, :], (nseg, cs))
        xr = st_scr[c, :, 0:cs]
        xi = st_scr[c, :, cs:2 * cs]
        for j in range(steps):
            rows = slice(j * nseg, (j + 1) * nseg)
            xr, xi = (lr * xr - li * xi + bu_scr[c, rows, 0:cs],
                      lr * xi + li * xr + bu_scr[c, rows, cs:2 * cs])
            bu_scr[c, rows, 0:cs] = xr
            bu_scr[c, rows, cs:2 * cs] = xi
        st_scr[c, :, 0:cs] = xr
        st_scr[c, :, cs:2 * cs] = xi
        ys.append(_dot(bu_scr[c, :, 0:cs].astype(BF16), cre_ref[c])
                  + _dot(bu_scr[c, :, cs:2 * cs].astype(BF16), cim_ref[c]))
    y = jnp.concatenate(ys, axis=1) + d_ref[...] * u
    y = y * (0.5 * (1.0 + jnp.tanh(math.sqrt(2.0 / math.pi) * (y + 0.044715 * (y * y * y)))))
    z = _dot(y.astype(BF16), wglu_ref[...]) + bglu_ref[...]
    y = y * jax.nn.sigmoid(z)
    for t in range(width // LANES):
        y_scr[t] = y[:, t * LANES:(t + 1) * LANES]
    for s in range(nseg):
        for t in range(width // LANES):
            o_ref[s, :, t * LANES:(t + 1) * LANES] = y_scr[t, pl.ds(s, steps, stride=nseg), :]


def _s5(u, u_t, prm, d_skip, w_glu, b_glu):
    bsz, seq, _ = u.shape
    ls = seq // N_SEG
    steps = S5_STEPS
    rows = steps * N_SEG
    ngrp = SSM_GROUPS // S5_CARRY_GROUPS
    carry = pl.pallas_call(
        _s5_carry_kernel,
        grid=(ngrp,),
        in_specs=[pl.BlockSpec((bsz, u_t.shape[1], S5_CARRY_GROUPS * SSM_GROUP_CH, u_t.shape[3]),
                               lambda g: (0, 0, g, 0)),
                  pl.BlockSpec((1, 2, LANES), lambda g: (g, 0, 0)),
                  pl.BlockSpec((1, 2, SSM_GROUP_CH, LANES), lambda g: (g, 0, 0, 0))],
        out_specs=pl.BlockSpec((1, 2, bsz * N_SEG, LANES), lambda g: (g, 0, 0, 0)),
        out_shape=jax.ShapeDtypeStruct((ngrp, 2, bsz * N_SEG, LANES), F32),
        compiler_params=_params("arbitrary"),
        name="s5_carry",
    )(u_t, prm["lam_carry"], prm["bt_carry"])
    sinit = carry.reshape(S5_CHUNKS, ngrp // S5_CHUNKS, 2, bsz, N_SEG, LANES)
    sinit = sinit.transpose(3, 0, 4, 2, 1, 5).reshape(bsz, S5_CHUNKS, N_SEG, 2 * S5_CHUNK_STATES)

    def const(shape):
        return pl.BlockSpec(shape, lambda b, j: (0,) * len(shape))

    blk = pl.BlockSpec((None, N_SEG, steps, SSM_WIDTH), lambda b, j: (b, 0, j, 0))
    out = pl.pallas_call(
        _s5_main_kernel,
        grid=(bsz, ls // steps),
        in_specs=[blk, const(prm["bc"].shape), const(prm["cre"].shape), const(prm["cim"].shape),
                  const(prm["lam_chunk"].shape),
                  pl.BlockSpec((None, S5_CHUNKS, N_SEG, 2 * S5_CHUNK_STATES), lambda b, j: (b, 0, 0, 0)),
                  const(d_skip.shape), const(w_glu.shape), const(b_glu.shape)],
        out_specs=blk,
        out_shape=jax.ShapeDtypeStruct((bsz, N_SEG, ls, SSM_WIDTH), F32),
        scratch_shapes=[pltpu.VMEM((rows, SSM_WIDTH), F32),
                        pltpu.VMEM((S5_CHUNKS, rows, 2 * S5_CHUNK_STATES), F32),
                        pltpu.VMEM((S5_CHUNKS, N_SEG, 2 * S5_CHUNK_STATES), F32),
                        pltpu.VMEM((SSM_WIDTH // LANES, rows, LANES), F32)],
        compiler_params=_params("arbitrary", "arbitrary"),
        name="s5_main",
    )(u.reshape(bsz, N_SEG, ls, SSM_WIDTH), prm["bc"], prm["cre"], prm["cim"], prm["lam_chunk"], sinit,
      d_skip, w_glu, b_glu)
    return out.reshape(bsz, seq, SSM_WIDTH)


FF_CHUNK = 1024


def _outmlp_kernel(x_ref, attn_ref, ssm_ref, mod_ref, ga_ref, gs_ref, gm_ref, gf_ref,
                   wout_ref, w1_ref, w2_ref, o_ref, *, final_norm):
    x = x_ref[0]
    tm, d = x.shape
    gt1 = mod_ref[0, :, 2 * d:3 * d]
    sh2 = mod_ref[0, :, 3 * d:4 * d]
    sc2 = mod_ref[0, :, 4 * d:5 * d]
    gt2 = mod_ref[0, :, 5 * d:6 * d]
    a = _rmsnorm(attn_ref[0], ga_ref[...]).astype(BF16)
    s = _rmsnorm(ssm_ref[0], gs_ref[...]).astype(BF16)
    mixed = _dot(a, wout_ref[0:ATTN_WIDTH, :]) + _dot(s, wout_ref[ATTN_WIDTH:, :])
    x1 = x + gt1 * mixed
    h = (_rmsnorm(x1, gm_ref[...]) * (1.0 + sc2) + sh2).astype(BF16)
    ff = w1_ref.shape[1]
    acc = jnp.zeros((tm, d), F32)
    for c in range(ff // FF_CHUNK):
        t = jnp.maximum(_dot(h, w1_ref[:, c * FF_CHUNK:(c + 1) * FF_CHUNK]), 0.0)
        acc = acc + _dot((t * t).astype(BF16), w2_ref[c * FF_CHUNK:(c + 1) * FF_CHUNK, :])
    x2 = x1 + gt2 * acc
    o_ref[0] = _rmsnorm(x2, gf_ref[...]) if final_norm else x2


OUTMLP_ROWS = 512


def _outmlp(x, attn, ssm, mod3, g_attn, g_ssm, g_mlp, g_final, w_out, w_fc1, w_fc2, final_norm):
    bsz, seq, d = x.shape
    tm = OUTMLP_ROWS
    const = lambda b, t: (0, 0)
    return pl.pallas_call(
        functools.partial(_outmlp_kernel, final_norm=final_norm),
        grid=(bsz, seq // tm),
        in_specs=[pl.BlockSpec((1, tm, d), lambda b, t: (b, t, 0)),
                  pl.BlockSpec((1, tm, ATTN_WIDTH), lambda b, t: (b, t, 0)),
                  pl.BlockSpec((1, tm, SSM_WIDTH), lambda b, t: (b, t, 0)),
                  pl.BlockSpec((1, 1, mod3.shape[-1]), lambda b, t: (b, 0, 0)),
                  pl.BlockSpec(g_attn.shape, const),
                  pl.BlockSpec(g_ssm.shape, const),
                  pl.BlockSpec(g_mlp.shape, const),
                  pl.BlockSpec(g_final.shape, const),
                  pl.BlockSpec(w_out.shape, const),
                  pl.BlockSpec(w_fc1.shape, const),
                  pl.BlockSpec(w_fc2.shape, const)],
        out_specs=pl.BlockSpec((1, tm, d), lambda b, t: (b, t, 0)),
        out_shape=jax.ShapeDtypeStruct((bsz, seq, d), F32),
        compiler_params=_params("arbitrary", "arbitrary"),
        name="outmlp",
    )(x, attn, ssm, mod3, g_attn, g_ssm, g_mlp, g_final, w_out, w_fc1, w_fc2)


def _rope_tables(seq):
    half = HEAD_DIM // 2
    inv_freq = ROPE_THETA ** (-jnp.arange(half, dtype=F32) / half)
    ang = jnp.arange(seq, dtype=F32)[:, None] * inv_freq[None, :]
    cos, sin = jnp.cos(ang), jnp.sin(ang)
    reps = LANES // HEAD_DIM
    cos_t = jnp.tile(cos, (1, 2 * reps))
    sin_t = jnp.tile(jnp.concatenate([-sin, sin], axis=1), (1, reps))
    return cos_t, sin_t


def _s5_params(lam_re, lam_im, log_dt, b_re, b_im, c_re, c_im):
    lr, li = lam_re.astype(F32), lam_im.astype(F32)
    dt = jnp.exp(log_dt.astype(F32))[:, None]
    mag = jnp.exp(lr * dt)
    ar, ai = mag * jnp.cos(li * dt), mag * jnp.sin(li * dt)
    den = lr * lr + li * li
    cr = ((ar - 1.0) * lr + ai * li) / den
    ci = (ai * lr - (ar - 1.0) * li) / den
    bbar_re = cr[..., None] * b_re.astype(F32) - ci[..., None] * b_im.astype(F32)
    bbar_im = cr[..., None] * b_im.astype(F32) + ci[..., None] * b_re.astype(F32)
    eye = jnp.eye(S5_CHUNK_GROUPS, dtype=F32)

    def chunked(m):
        return m.reshape((S5_CHUNKS, S5_CHUNK_GROUPS) + m.shape[1:])

    def diag_in(m):
        return jnp.einsum('cgpn,gh->cgnhp', chunked(m), eye).reshape(S5_CHUNKS, S5_CHUNK_LANES, S5_CHUNK_STATES)

    def diag_out(m):
        return jnp.einsum('cgnp,gh->cgphn', chunked(m), eye).reshape(S5_CHUNKS, S5_CHUNK_STATES, S5_CHUNK_LANES)

    ngrp = SSM_GROUPS // S5_CARRY_GROUPS

    def lane_tiles(m):
        m = m.reshape(ngrp, S5_CARRY_GROUPS, SSM_GROUP_CH, SSM_STATE)
        return m.transpose(0, 2, 1, 3).reshape(ngrp, SSM_GROUP_CH, LANES)

    return {
        "bc": jnp.concatenate([diag_in(bbar_re), diag_in(bbar_im)], axis=2).astype(BF16),
        "cre": diag_out(c_re.astype(F32)).astype(BF16),
        "cim": diag_out(-c_im.astype(F32)).astype(BF16),
        "lam_chunk": jnp.stack([ar.reshape(S5_CHUNKS, S5_CHUNK_STATES),
                                ai.reshape(S5_CHUNKS, S5_CHUNK_STATES)], axis=1),
        "lam_carry": jnp.stack([ar.reshape(ngrp, LANES), ai.reshape(ngrp, LANES)], axis=1),
        "bt_carry": jnp.stack([lane_tiles(bbar_re.transpose(0, 2, 1)),
                               lane_tiles(bbar_im.transpose(0, 2, 1))], axis=1),
    }


def kernel(x, c, w_ada, b_ada, g_mix, w_in, g_attn_out, lam_re, lam_im, log_dt, b_re, b_im, c_re, c_im,
           d_skip, w_glu, b_glu, g_ssm_out, w_out, g_mlp, w_fc1, w_fc2, g_final):
    bsz, seq, d = x.shape
    depth = w_ada.shape[0]
    assert seq % (N_SEG * MOBA_BLOCK) == 0 and (seq // N_SEG) % S5_STEPS == 0 and bsz <= SUBLANES
    cos_t, sin_t = _rope_tables(seq)
    c_pad = jnp.zeros((SUBLANES, d), F32).at[:bsz].set(c.astype(F32))

    for l in range(depth):
        mod, win_hi, win_lo = _adaln(c_pad, w_ada[l], b_ada[l][None, :], w_in[l])
        mod3 = mod[:bsz, None, :]

        q_t, k_aug, v_t, u, u_t, wout_bf, wfc1_bf, wfc2_bf = _inproj(
            x, mod3, g_mix[l][None, :], win_hi, win_lo, cos_t, sin_t, w_out[l], w_fc1[l], w_fc2[l])
        attn = _attention(q_t, k_aug, v_t)

        prm = _s5_params(lam_re[l], lam_im[l], log_dt[l], b_re[l], b_im[l], c_re[l], c_im[l])
        ssm = _s5(u, u_t, prm, d_skip[l].reshape(1, SSM_WIDTH), w_glu[l].astype(BF16), b_glu[l][None, :])

        x = _outmlp(x, attn, ssm, mod3, g_attn_out[l][None, :], g_ssm_out[l][None, :], g_mlp[l][None, :],
                    g_final[None, :], wout_bf, wfc1_bf, wfc2_bf, final_norm=(l == depth - 1))
    return x
```

```python
import functools
import math

import jax
import jax.numpy as jnp
from jax import lax
from jax.experimental import pallas as pl
from jax.experimental.pallas import tpu as pltpu

F32 = jnp.float32
BF16 = jnp.bfloat16

HEAD_DIM = 64
ATTN_HEADS = 8
ATTN_WIDTH = ATTN_HEADS * HEAD_DIM
SSM_GROUPS = 32
SSM_GROUP_CH = 16
SSM_WIDTH = SSM_GROUPS * SSM_GROUP_CH
SSM_STATE = 64
SSM_NSTATE = SSM_GROUPS * SSM_STATE
MOBA_BLOCK = 256
MOBA_TOPK = 3
ROPE_THETA = 10000.0
EPS = 1e-6

LANES = 128
SUBLANES = 8
N_SEG = SUBLANES
HEADS_PER_TILE = LANES // HEAD_DIM
MASK_VALUE = -(2.0 ** 100)
VMEM_LIMIT_BYTES = 56 * 1024 * 1024


def _split_bf16(a):
    hi = a.astype(BF16)
    lo = (a - hi.astype(F32)).astype(BF16)
    return hi, lo


def _dot(a, b):
    return jnp.dot(a, b, preferred_element_type=F32)


def _rmsnorm(x, g):
    return x * lax.rsqrt(jnp.mean(x * x, axis=-1, keepdims=True) + EPS) * g


def _params(*semantics):
    return pltpu.CompilerParams(dimension_semantics=semantics, vmem_limit_bytes=VMEM_LIMIT_BYTES)


ADALN_STEPS = 4


def _adaln_kernel(c_ref, w_ref, b_ref, win_ref, o_ref, win_hi_ref, win_lo_ref):
    c = c_ref[...]
    s_hi, s_lo = _split_bf16(c * jax.nn.sigmoid(c))
    w_hi, w_lo = _split_bf16(w_ref[...])
    o_ref[...] = _dot(s_hi, w_hi) + _dot(s_lo, w_hi) + _dot(s_hi, w_lo) + b_ref[...]
    win_hi_ref[...], win_lo_ref[...] = _split_bf16(win_ref[...])


def _adaln(c_pad, w, b, w_in):
    rows, d = c_pad.shape
    n = w.shape[1]
    tn = n // ADALN_STEPS
    tw = w_in.shape[1] // ADALN_STEPS
    slab = pl.BlockSpec((d, tw), lambda j: (0, j))
    return pl.pallas_call(
        _adaln_kernel,
        grid=(ADALN_STEPS,),
        in_specs=[pl.BlockSpec((rows, d), lambda j: (0, 0)),
                  pl.BlockSpec((d, tn), lambda j: (0, j)),
                  pl.BlockSpec((1, tn), lambda j: (0, j)),
                  slab],
        out_specs=[pl.BlockSpec((rows, tn), lambda j: (0, j)), slab, slab],
        out_shape=[jax.ShapeDtypeStruct((rows, n), F32),
                   jax.ShapeDtypeStruct(w_in.shape, BF16),
                   jax.ShapeDtypeStruct(w_in.shape, BF16)],
        compiler_params=_params("arbitrary"),
        name="adaln",
    )(c_pad, w, b, w_in)


VT_ROWS = HEAD_DIM + 16


def _top_blocks(gate, blk, past, own):
    nb = gate.shape[0]
    g = jnp.where(past, gate, -jnp.inf)
    chosen = None
    for _ in range(MOBA_TOPK):
        best = jnp.max(g, axis=0, keepdims=True)
        first = jnp.min(jnp.where(g == best, blk, float(nb)), axis=0, keepdims=True)
        pick = blk == first
        chosen = pick if chosen is None else (chosen | pick)
        g = jnp.where(pick, -jnp.inf, g)
    return (chosen & past) | own


INPROJ_BLOCKS = 2


def _inproj_route(i, slot, qk, vu, cos_ref, sin_ref, qt_ref, ka_ref, vt_ref, u_ref, ut_ref, km_scr):
    tm = qk.shape[0]
    nb = km_scr.shape[0]
    rows = slice(slot * tm, (slot + 1) * tm)
    u = vu[:, ATTN_WIDTH:]
    u_ref[0, rows] = u
    ut_ref[0, slot] = u.T.astype(BF16)

    reps = ATTN_WIDTH // LANES
    cos = jnp.concatenate([cos_ref[rows]] * reps, axis=1)
    sin = jnp.concatenate([sin_ref[rows]] * reps, axis=1)
    wide_lane = lax.broadcasted_iota(jnp.int32, (tm, ATTN_WIDTH), 1)
    first_half = (wide_lane & (HEAD_DIM // 2)) == 0

    def rope(t):
        partner = jnp.where(first_half,
                            pltpu.roll(t, ATTN_WIDTH - HEAD_DIM // 2, 1),
                            pltpu.roll(t, HEAD_DIM // 2, 1))
        return t * cos + partner * sin

    q = rope(qk[:, :ATTN_WIDTH])
    k = rope(qk[:, ATTN_WIDTH:])
    v = vu[:, :ATTN_WIDTH]

    km_scr[pl.ds(i, 1), :] = jnp.sum(k, axis=0, keepdims=True) * (1.0 / tm)
    km_all = km_scr[...]

    lane = lax.broadcasted_iota(jnp.int32, (tm, LANES), 1)
    km_lane = lax.broadcasted_iota(jnp.int32, (nb, LANES), 1)
    blk_i = lax.broadcasted_iota(jnp.int32, (nb, tm), 0)
    blk = blk_i.astype(F32)
    past = blk_i < i
    own = blk_i == i
    ones_rows = jnp.where(lax.broadcasted_iota(jnp.int32, (VT_ROWS - HEAD_DIM, tm), 0) == 0, 1.0, 0.0)
    onehot = jnp.where(lane == HEAD_DIM + i, 1.0, 0.0)
    scale = HEAD_DIM ** -0.5 * math.log2(math.e)
    for tile in range(ATTN_WIDTH // LANES):
        cols = slice(tile * LANES, (tile + 1) * LANES)
        q_t = q[:, cols].T
        v_t = v[:, cols].T
        k_tile = k[:, cols]
        km = km_all[:, cols]
        qt_hi, qt_lo = _split_bf16(q_t)
        for hh in range(HEADS_PER_TILE):
            h_idx = tile * HEADS_PER_TILE + hh
            kmh_hi, kmh_lo = _split_bf16(jnp.where(km_lane // HEAD_DIM == hh, km, 0.0))
            gate = _dot(kmh_hi, qt_hi) + _dot(kmh_lo, qt_hi) + _dot(kmh_hi, qt_lo)
            bias_t = jnp.where(_top_blocks(gate, blk, past, own), 0.0, MASK_VALUE)
            head_rows = slice(hh * HEAD_DIM, (hh + 1) * HEAD_DIM)
            qt_ref[0, h_idx, slot] = jnp.concatenate(
                [q_t[head_rows] * scale, bias_t, jnp.zeros((LANES - HEAD_DIM - nb, tm), F32)],
                axis=0).astype(BF16)
            vt_ref[0, h_idx, slot] = jnp.concatenate([v_t[head_rows], ones_rows], axis=0).astype(BF16)
            kh = jnp.where(lane // HEAD_DIM == hh, k_tile, 0.0)
            if hh:
                kh = pltpu.roll(kh, LANES - hh * HEAD_DIM, 1)
            ka_ref[0, h_idx, rows] = jnp.where(lane < HEAD_DIM, kh, onehot).astype(BF16)


def _inproj_kernel(x_ref, mod_ref, g_ref, wqk_hi_ref, wqk_lo_ref, wvu_ref, cos_ref, sin_ref,
                   wout_ref, wfc1_ref, wfc2_ref,
                   qt_ref, ka_ref, vt_ref, u_ref, ut_ref, wout_bf_ref, wfc1_bf_ref, wfc2_bf_ref, km_scr):
    step = pl.program_id(1)

    @pl.when(step == 0)
    def _():
        km_scr[...] = jnp.zeros_like(km_scr)

    wout_bf_ref[...] = wout_ref[...].astype(BF16)
    wfc1_bf_ref[...] = wfc1_ref[...].astype(BF16)
    wfc2_bf_ref[...] = wfc2_ref[...].astype(BF16)
    d = x_ref.shape[2]
    tm = x_ref.shape[1] // INPROJ_BLOCKS
    sh1 = mod_ref[0, :, 0:d]
    sc1 = mod_ref[0, :, d:2 * d]
    hs = []
    for slot in range(INPROJ_BLOCKS):
        x = x_ref[0, slot * tm:(slot + 1) * tm, :]
        hs.append(_split_bf16(_rmsnorm(x, g_ref[...]) * (1.0 + sc1) + sh1))
    prj = []
    for h_hi, h_lo in hs:
        qk = (_dot(h_hi, wqk_hi_ref[...]) + _dot(h_lo, wqk_hi_ref[...])
              + _dot(h_hi, wqk_lo_ref[...]))
        prj.append((qk, _dot(h_hi, wvu_ref[...])))

    for slot, (qk, vu) in enumerate(prj):
        _inproj_route(step * INPROJ_BLOCKS + slot, slot, qk, vu, cos_ref, sin_ref,
                      qt_ref, ka_ref, vt_ref, u_ref, ut_ref, km_scr)


def _inproj(x, mod3, g_mix, win_hi, win_lo, cos_t, sin_t, w_out, w_fc1, w_fc2):
    bsz, seq, d = x.shape
    tm = MOBA_BLOCK
    nb = seq // tm
    assert nb % SUBLANES == 0 and nb <= LANES - HEAD_DIM and nb % INPROJ_BLOCKS == 0
    nblk = INPROJ_BLOCKS
    rows = nblk * tm
    const = lambda b, t: (0, 0)
    wqk = 2 * ATTN_WIDTH
    assert win_hi.shape[1] == 2 * wqk
    per_batch = nb // nblk
    steps = bsz * per_batch

    def slab(w):
        n = w.shape[0] // steps
        assert n * steps == w.shape[0] and n % (2 * SUBLANES) == 0
        return pl.BlockSpec((n, w.shape[1]), lambda b, t: (b * per_batch + t, 0))

    casts = [w_out, w_fc1, w_fc2]
    return pl.pallas_call(
        _inproj_kernel,
        grid=(bsz, per_batch),
        in_specs=[pl.BlockSpec((1, rows, d), lambda b, t: (b, t, 0)),
                  pl.BlockSpec((1, 1, mod3.shape[-1]), lambda b, t: (b, 0, 0)),
                  pl.BlockSpec((1, d), const),
                  pl.BlockSpec((d, wqk), const),
                  pl.BlockSpec((d, wqk), const),
                  pl.BlockSpec((d, wqk), lambda b, t: (0, 1)),
                  pl.BlockSpec((rows, LANES), lambda b, t: (t, 0)),
                  pl.BlockSpec((rows, LANES), lambda b, t: (t, 0))] + [slab(w) for w in casts],
        out_specs=[pl.BlockSpec((1, ATTN_HEADS, nblk, LANES, tm), lambda b, t: (b, 0, t, 0, 0)),
                   pl.BlockSpec((1, ATTN_HEADS, rows, LANES), lambda b, t: (b, 0, t, 0)),
                   pl.BlockSpec((1, ATTN_HEADS, nblk, VT_ROWS, tm), lambda b, t: (b, 0, t, 0, 0)),
                   pl.BlockSpec((1, rows, SSM_WIDTH), lambda b, t: (b, t, 0)),
                   pl.BlockSpec((1, nblk, SSM_WIDTH, tm), lambda b, t: (b, t, 0, 0))] + [slab(w) for w in casts],
        out_shape=[jax.ShapeDtypeStruct((bsz, ATTN_HEADS, nb, LANES, tm), BF16),
                   jax.ShapeDtypeStruct((bsz, ATTN_HEADS, seq, LANES), BF16),
                   jax.ShapeDtypeStruct((bsz, ATTN_HEADS, nb, VT_ROWS, tm), BF16),
                   jax.ShapeDtypeStruct((bsz, seq, SSM_WIDTH), F32),
                   jax.ShapeDtypeStruct((bsz, nb, SSM_WIDTH, tm), BF16)]
                  + [jax.ShapeDtypeStruct(w.shape, BF16) for w in casts],
        scratch_shapes=[pltpu.VMEM((nb, ATTN_WIDTH), F32)],
        compiler_params=_params("arbitrary", "arbitrary"),
        name="inproj",
    )(x, mod3, g_mix, win_hi, win_lo, win_hi, cos_t, sin_t, *casts)


ATTN_QBLOCKS = 2
RUNNING_MAX_INIT = -3.0e38


def _attn_kernel(qt_ref, ka_ref, vt_ref, o_ref, sa_scr, sb_scr, m_scr, acc_scr):
    pair = pl.program_id(1)
    i = ATTN_QBLOCKS * pair
    nh = qt_ref.shape[1]
    tq = qt_ref.shape[4]
    key = lax.broadcasted_iota(jnp.int32, (tq, tq), 0)
    qry = lax.broadcasted_iota(jnp.int32, (tq, tq), 1)
    causal = key <= qry

    def scores(qb, h, j, s_scr, mask=None):
        start = pl.multiple_of(j * tq, tq)
        s = _dot(ka_ref[0, h, pl.ds(start, tq), :], qt_ref[0, h, qb])
        if mask is not None:
            s = jnp.where(mask, s, MASK_VALUE)
        s_scr[qb * nh + h] = s

    def consume(qb, h, j, s_scr):
        c = qb * nh + h
        s = s_scr[c]
        m_old = m_scr[c]
        m_new = jnp.maximum(m_old, jnp.max(s, axis=0, keepdims=True))
        alpha = jnp.exp2(m_old - m_new)
        p = jnp.exp2(s - m_new).astype(BF16)
        acc_scr[c] = alpha * acc_scr[c] + _dot(vt_ref[0, h, j], p)
        m_scr[c] = m_new

    m_scr[...] = jnp.full(m_scr.shape, RUNNING_MAX_INIT, F32)
    acc_scr[...] = jnp.zeros(acc_scr.shape, F32)
    for h in range(nh):
        scores(0, h, i, sa_scr, causal)
        scores(1, h, i + 1, sa_scr, causal)
    for h in range(nh):
        scores(1, h, i, sb_scr)
        consume(1, h, i + 1, sa_scr)

    def two_blocks(j0):
        j1 = j0 + 1
        prev = jnp.where(j0 == 0, i, j0 - 1)
        for h in range(nh):
            scores(0, h, j0, sb_scr)
            consume(0, h, prev, sa_scr)
            scores(1, h, j0, sa_scr)
            consume(1, h, prev, sb_scr)
        for h in range(nh):
            scores(0, h, j1, sa_scr)
            consume(0, h, j0, sb_scr)
            scores(1, h, j1, sb_scr)
            consume(1, h, j0, sa_scr)

    def body(t, carry):
        two_blocks(4 * t)
        two_blocks(4 * t + 2)
        return carry

    lax.fori_loop(0, pair // 2, body, 0)

    @pl.when(pair % 2 == 1)
    def _():
        two_blocks(i - 2)

    last = jnp.where(pair == 0, i, i - 1)
    for h in range(nh):
        consume(0, h, last, sa_scr)
        consume(1, h, last, sb_scr)

    for qb in range(ATTN_QBLOCKS):
        for vt in range(nh // HEADS_PER_TILE):
            rows = []
            for hh in range(HEADS_PER_TILE):
                acc = acc_scr[qb * nh + vt * HEADS_PER_TILE + hh]
                rows.append(acc[0:HEAD_DIM] / acc[HEAD_DIM:HEAD_DIM + 1])
            o_ref[0, qb * tq:(qb + 1) * tq, vt * LANES:(vt + 1) * LANES] = jnp.concatenate(rows, axis=0).T


def _attention(q_t, k_aug, v_t):
    bsz, nh, nb, _, tq = q_t.shape
    seq = nb * tq
    assert nb % ATTN_QBLOCKS == 0
    chains = ATTN_QBLOCKS * nh
    resident = pl.Buffered(1)
    return pl.pallas_call(
        _attn_kernel,
        grid=(bsz, nb // ATTN_QBLOCKS),
        in_specs=[pl.BlockSpec((1, nh, ATTN_QBLOCKS, LANES, tq), lambda b, g: (b, 0, g, 0, 0)),
                  pl.BlockSpec((1, nh, seq, LANES), lambda b, g: (b, 0, 0, 0), pipeline_mode=resident),
                  pl.BlockSpec((1, nh, nb, VT_ROWS, tq), lambda b, g: (b, 0, 0, 0, 0), pipeline_mode=resident)],
        out_specs=pl.BlockSpec((1, ATTN_QBLOCKS * tq, nh * HEAD_DIM), lambda b, g: (b, g, 0)),
        out_shape=jax.ShapeDtypeStruct((bsz, seq, nh * HEAD_DIM), F32),
        scratch_shapes=[pltpu.VMEM((chains, tq, tq), F32), pltpu.VMEM((chains, tq, tq), F32),
                        pltpu.VMEM((chains, 1, tq), F32), pltpu.VMEM((chains, VT_ROWS, tq), F32)],
        compiler_params=_params("arbitrary", "arbitrary"),
        name="attn",
    )(q_t, k_aug, v_t)


S5_STEPS = 64
S5_CHUNKS = 4
S5_CHUNK_GROUPS = SSM_GROUPS // S5_CHUNKS
S5_CHUNK_STATES = S5_CHUNK_GROUPS * SSM_STATE
S5_CHUNK_LANES = S5_CHUNK_GROUPS * SSM_GROUP_CH


S5_CARRY_GROUPS = LANES // SSM_STATE


def _cmul(ar, ai, br, bi):
    return ar * br - ai * bi, ar * bi + ai * br


def _s5_carry_kernel(ut_ref, lam_ref, bt_ref, o_ref):
    bsz, nblk, nrows, tm = ut_ref.shape
    nch = nrows // S5_CARRY_GROUPS
    seg_blocks = nblk // N_SEG
    ls = seg_blocks * tm
    reps = bsz * N_SEG
    sub = lax.broadcasted_iota(jnp.int32, (SUBLANES, LANES), 0)
    lr = jnp.broadcast_to(lam_ref[0, 0:1], (SUBLANES, LANES))
    li = jnp.broadcast_to(lam_ref[0, 1:2], (SUBLANES, LANES))
    pr = jnp.ones((SUBLANES, LANES), F32)
    pi = jnp.zeros((SUBLANES, LANES), F32)
    tr = jnp.zeros((SUBLANES, LANES), F32)
    ti = jnp.zeros((SUBLANES, LANES), F32)
    for r in range(SUBLANES):
        tr = jnp.where(sub == SUBLANES - 1 - r, pr, tr)
        ti = jnp.where(sub == SUBLANES - 1 - r, pi, ti)
        pr, pi = _cmul(pr, pi, lr, li)
    span = SUBLANES
    while span < ls:
        nr, ni = _cmul(tr, ti, pr[0:1], pi[0:1])
        tr = jnp.concatenate([nr, tr], axis=0)
        ti = jnp.concatenate([ni, ti], axis=0)
        pr, pi = _cmul(pr, pi, pr, pi)
        span *= 2
    def seg_rows(g, b, s):
        return jnp.concatenate([ut_ref[b, s * seg_blocks + k, g * nch:(g + 1) * nch, :]
                                for k in range(seg_blocks)], axis=1)

    lhs = jnp.concatenate([seg_rows(g, b, s)
                           for g in range(S5_CARRY_GROUPS) for b in range(bsz) for s in range(N_SEG)],
                          axis=0)
    gr = _dot(lhs, tr.astype(BF16))
    gi = _dot(lhs, ti.astype(BF16))
    btr = jnp.concatenate([bt_ref[0, 0]] * (reps * S5_CARRY_GROUPS), axis=0)
    bti = jnp.concatenate([bt_ref[0, 1]] * (reps * S5_CARRY_GROUPS), axis=0)
    wr, wi = _cmul(gr, gi, btr, bti)
    fr = wr.reshape(reps * S5_CARRY_GROUPS, nch, LANES).sum(axis=1)
    fi = wi.reshape(reps * S5_CARRY_GROUPS, nch, LANES).sum(axis=1)
    grp = lax.broadcasted_iota(jnp.int32, (reps, LANES), 1) // SSM_STATE
    fin_r, fin_i = fr[0:reps], fi[0:reps]
    for g in range(1, S5_CARRY_GROUPS):
        fin_r = jnp.where(grp == g, fr[g * reps:(g + 1) * reps], fin_r)
        fin_i = jnp.where(grp == g, fi[g * reps:(g + 1) * reps], fin_i)
    rows_r, rows_i = [], []
    for b in range(bsz):
        sr = jnp.zeros((1, LANES), F32)
        si = jnp.zeros((1, LANES), F32)
        rows_r.append(sr)
        rows_i.append(si)
        for seg in range(1, N_SEG):
            r = b * N_SEG + seg - 1
            mr, mi = _cmul(sr, si, pr[0:1], pi[0:1])
            sr, si = mr + fin_r[r:r + 1], mi + fin_i[r:r + 1]
            rows_r.append(sr)
            rows_i.append(si)
    o_ref[0, 0] = jnp.concatenate(rows_r, axis=0)
    o_ref[0, 1] = jnp.concatenate(rows_i, axis=0)


def _s5_main_kernel(u_ref, bc_ref, cre_ref, cim_ref, lam_ref, sinit_ref, d_ref, wglu_ref, bglu_ref, o_ref,
                    il_scr, bu_scr, st_scr, y_scr):
    jb = pl.program_id(1)
    nseg, steps, width = u_ref.shape
    cs = S5_CHUNK_STATES

    @pl.when(jb == 0)
    def _():
        st_scr[...] = sinit_ref[...]

    for j in range(steps):
        il_scr[j * nseg:(j + 1) * nseg, :] = u_ref[:, j, :]
    u = il_scr[...]
    u_bf = u.astype(BF16)
    for c in range(S5_CHUNKS):
        bu_scr[c] = _dot(u_bf[:, c * S5_CHUNK_LANES:(c + 1) * S5_CHUNK_LANES], bc_ref[c])
    ys = []
    for c in range(S5_CHUNKS):
        lr = jnp.broadcast_to(lam_ref[c, 0:1, :], (nseg, cs))
        li = jnp.broadcast_to(lam_ref[c, 1:2, :], (nseg, cs))
        xr = st_scr[c, :, 0:cs]
        xi = st_scr[c, :, cs:2 * cs]
        for j in range(steps):
            rows = slice(j * nseg, (j + 1) * nseg)
            xr, xi = (lr * xr - li * xi + bu_scr[c, rows, 0:cs],
                      lr * xi + li * xr + bu_scr[c, rows, cs:2 * cs])
            bu_scr[c, rows, 0:cs] = xr
            bu_scr[c, rows, cs:2 * cs] = xi
        st_scr[c, :, 0:cs] = xr
        st_scr[c, :, cs:2 * cs] = xi
        ys.append(_dot(bu_scr[c, :, 0:cs].astype(BF16), cre_ref[c])
                  + _dot(bu_scr[c, :, cs:2 * cs].astype(BF16), cim_ref[c]))
    y = jnp.concatenate(ys, axis=1) + d_ref[...] * u
    y = y * (0.5 * (1.0 + jnp.tanh(math.sqrt(2.0 / math.pi) * (y + 0.044715 * (y * y * y)))))
    z = _dot(y.astype(BF16), wglu_ref[...]) + bglu_ref[...]
    y = y * jax.nn.sigmoid(z)
    for t in range(width // LANES):
        y_scr[t] = y[:, t * LANES:(t + 1) * LANES]
    for s in range(nseg):
        for t in range(width // LANES):
            o_ref[s, :, t * LANES:(t + 1) * LANES] = y_scr[t, pl.ds(s, steps, stride=nseg), :]


def _s5(u, u_t, prm, d_skip, w_glu, b_glu):
    bsz, seq, _ = u.shape
    ls = seq // N_SEG
    steps = S5_STEPS
    rows = steps * N_SEG
    ngrp = SSM_GROUPS // S5_CARRY_GROUPS
    carry = pl.pallas_call(
        _s5_carry_kernel,
        grid=(ngrp,),
        in_specs=[pl.BlockSpec((bsz, u_t.shape[1], S5_CARRY_GROUPS * SSM_GROUP_CH, u_t.shape[3]),
                               lambda g: (0, 0, g, 0)),
                  pl.BlockSpec((1, 2, LANES), lambda g: (g, 0, 0)),
                  pl.BlockSpec((1, 2, SSM_GROUP_CH, LANES), lambda g: (g, 0, 0, 0))],
        out_specs=pl.BlockSpec((1, 2, bsz * N_SEG, LANES), lambda g: (g, 0, 0, 0)),
        out_shape=jax.ShapeDtypeStruct((ngrp, 2, bsz * N_SEG, LANES), F32),
        compiler_params=_params("arbitrary"),
        name="s5_carry",
    )(u_t, prm["lam_carry"], prm["bt_carry"])
    sinit = carry.reshape(S5_CHUNKS, ngrp // S5_CHUNKS, 2, bsz, N_SEG, LANES)
    sinit = sinit.transpose(3, 0, 4, 2, 1, 5).reshape(bsz, S5_CHUNKS, N_SEG, 2 * S5_CHUNK_STATES)

    def const(shape):
        return pl.BlockSpec(shape, lambda b, j: (0,) * len(shape))

    blk = pl.BlockSpec((None, N_SEG, steps, SSM_WIDTH), lambda b, j: (b, 0, j, 0))
    out = pl.pallas_call(
        _s5_main_kernel,
        grid=(bsz, ls // steps),
        in_specs=[blk, const(prm["bc"].shape), const(prm["cre"].shape), const(prm["cim"].shape),
                  const(prm["lam_chunk"].shape),
                  pl.BlockSpec((None, S5_CHUNKS, N_SEG, 2 * S5_CHUNK_STATES), lambda b, j: (b, 0, 0, 0)),
                  const(d_skip.shape), const(w_glu.shape), const(b_glu.shape)],
        out_specs=blk,
        out_shape=jax.ShapeDtypeStruct((bsz, N_SEG, ls, SSM_WIDTH), F32),
        scratch_shapes=[pltpu.VMEM((rows, SSM_WIDTH), F32),
                        pltpu.VMEM((S5_CHUNKS, rows, 2 * S5_CHUNK_STATES), F32),
                        pltpu.VMEM((S5_CHUNKS, N_SEG, 2 * S5_CHUNK_STATES), F32),
                        pltpu.VMEM((SSM_WIDTH // LANES, rows, LANES), F32)],
        compiler_params=_params("arbitrary", "arbitrary"),
        name="s5_main",
    )(u.reshape(bsz, N_SEG, ls, SSM_WIDTH), prm["bc"], prm["cre"], prm["cim"], prm["lam_chunk"], sinit,
      d_skip, w_glu, b_glu)
    return out.reshape(bsz, seq, SSM_WIDTH)


FF_CHUNK = 1024


def _outmlp_kernel(x_ref, attn_ref, ssm_ref, mod_ref, ga_ref, gs_ref, gm_ref, gf_ref,
                   wout_ref, w1_ref, w2_ref, o_ref, *, final_norm):
    x = x_ref[0]
    tm, d = x.shape
    gt1 = mod_ref[0, :, 2 * d:3 * d]
    sh2 = mod_ref[0, :, 3 * d:4 * d]
    sc2 = mod_ref[0, :, 4 * d:5 * d]
    gt2 = mod_ref[0, :, 5 * d:6 * d]
    a = _rmsnorm(attn_ref[0], ga_ref[...]).astype(BF16)
    s = _rmsnorm(ssm_ref[0], gs_ref[...]).astype(BF16)
    mixed = _dot(a, wout_ref[0:ATTN_WIDTH, :]) + _dot(s, wout_ref[ATTN_WIDTH:, :])
    x1 = x + gt1 * mixed
    h = (_rmsnorm(x1, gm_ref[...]) * (1.0 + sc2) + sh2).astype(BF16)
    ff = w1_ref.shape[1]
    acc = jnp.zeros((tm, d), F32)
    for c in range(ff // FF_CHUNK):
        t = jnp.maximum(_dot(h, w1_ref[:, c * FF_CHUNK:(c + 1) * FF_CHUNK]), 0.0)
        acc = acc + _dot((t * t).astype(BF16), w2_ref[c * FF_CHUNK:(c + 1) * FF_CHUNK, :])
    x2 = x1 + gt2 * acc
    o_ref[0] = _rmsnorm(x2, gf_ref[...]) if final_norm else x2


OUTMLP_ROWS = 512


def _outmlp(x, attn, ssm, mod3, g_attn, g_ssm, g_mlp, g_final, w_out, w_fc1, w_fc2, final_norm):
    bsz, seq, d = x.shape
    tm = OUTMLP_ROWS
    const = lambda b, t: (0, 0)
    return pl.pallas_call(
        functools.partial(_outmlp_kernel, final_norm=final_norm),
        grid=(bsz, seq // tm),
        in_specs=[pl.BlockSpec((1, tm, d), lambda b, t: (b, t, 0)),
                  pl.BlockSpec((1, tm, ATTN_WIDTH), lambda b, t: (b, t, 0)),
                  pl.BlockSpec((1, tm, SSM_WIDTH), lambda b, t: (b, t, 0)),
                  pl.BlockSpec((1, 1, mod3.shape[-1]), lambda b, t: (b, 0, 0)),
                  pl.BlockSpec(g_attn.shape, const),
                  pl.BlockSpec(g_ssm.shape, const),
                  pl.BlockSpec(g_mlp.shape, const),
                  pl.BlockSpec(g_final.shape, const),
                  pl.BlockSpec(w_out.shape, const),
                  pl.BlockSpec(w_fc1.shape, const),
                  pl.BlockSpec(w_fc2.shape, const)],
        out_specs=pl.BlockSpec((1, tm, d), lambda b, t: (b, t, 0)),
        out_shape=jax.ShapeDtypeStruct((bsz, seq, d), F32),
        compiler_params=_params("arbitrary", "arbitrary"),
        name="outmlp",
    )(x, attn, ssm, mod3, g_attn, g_ssm, g_mlp, g_final, w_out, w_fc1, w_fc2)


def _rope_tables(seq):
    half = HEAD_DIM // 2
    inv_freq = ROPE_THETA ** (-jnp.arange(half, dtype=F32) / half)
    ang = jnp.arange(seq, dtype=F32)[:, None] * inv_freq[None, :]
    cos, sin = jnp.cos(ang), jnp.sin(ang)
    reps = LANES // HEAD_DIM
    cos_t = jnp.tile(cos, (1, 2 * reps))
    sin_t = jnp.tile(jnp.concatenate([-sin, sin], axis=1), (1, reps))
    return cos_t, sin_t


def _s5_params(lam_re, lam_im, log_dt, b_re, b_im, c_re, c_im):
    lr, li = lam_re.astype(F32), lam_im.astype(F32)
    dt = jnp.exp(log_dt.astype(F32))[:, None]
    mag = jnp.exp(lr * dt)
    ar, ai = mag * jnp.cos(li * dt), mag * jnp.sin(li * dt)
    den = lr * lr + li * li
    cr = ((ar - 1.0) * lr + ai * li) / den
    ci = (ai * lr - (ar - 1.0) * li) / den
    bbar_re = cr[..., None] * b_re.astype(F32) - ci[..., None] * b_im.astype(F32)
    bbar_im = cr[..., None] * b_im.astype(F32) + ci[..., None] * b_re.astype(F32)
    eye = jnp.eye(S5_CHUNK_GROUPS, dtype=F32)

    def chunked(m):
        return m.reshape((S5_CHUNKS, S5_CHUNK_GROUPS) + m.shape[1:])

    def diag_in(m):
        return jnp.einsum('cgpn,gh->cgnhp', chunked(m), eye).reshape(S5_CHUNKS, S5_CHUNK_LANES, S5_CHUNK_STATES)

    def diag_out(m):
        return jnp.einsum('cgnp,gh->cgphn', chunked(m), eye).reshape(S5_CHUNKS, S5_CHUNK_STATES, S5_CHUNK_LANES)

    ngrp = SSM_GROUPS // S5_CARRY_GROUPS

    def lane_tiles(m):
        m = m.reshape(ngrp, S5_CARRY_GROUPS, SSM_GROUP_CH, SSM_STATE)
        return m.transpose(0, 2, 1, 3).reshape(ngrp, SSM_GROUP_CH, LANES)

    return {
        "bc": jnp.concatenate([diag_in(bbar_re), diag_in(bbar_im)], axis=2).astype(BF16),
        "cre": diag_out(c_re.astype(F32)).astype(BF16),
        "cim": diag_out(-c_im.astype(F32)).astype(BF16),
        "lam_chunk": jnp.stack([ar.reshape(S5_CHUNKS, S5_CHUNK_STATES),
                                ai.reshape(S5_CHUNKS, S5_CHUNK_STATES)], axis=1),
        "lam_carry": jnp.stack([ar.reshape(ngrp, LANES), ai.reshape(ngrp, LANES)], axis=1),
        "bt_carry": jnp.stack([lane_tiles(bbar_re.transpose(0, 2, 1)),
                               lane_tiles(bbar_im.transpose(0, 2, 1))], axis=1),
    }


def kernel(x, c, w_ada, b_ada, g_mix, w_in, g_attn_out, lam_re, lam_im, log_dt, b_re, b_im, c_re, c_im,
           d_skip, w_glu, b_glu, g_ssm_out, w_out, g_mlp, w_fc1, w_fc2, g_final):
    bsz, seq, d = x.shape
    depth = w_ada.shape[0]
    assert seq % (N_SEG * MOBA_BLOCK) == 0 and (seq // N_SEG) % S5_STEPS == 0 and bsz <= SUBLANES
    cos_t, sin_t = _rope_tables(seq)
    c_pad = jnp.zeros((SUBLANES, d), F32).at[:bsz].set(c.astype(F32))

    for l in range(depth):
        mod, win_hi, win_lo = _adaln(c_pad, w_ada[l], b_ada[l][None, :], w_in[l])
        mod3 = mod[:bsz, None, :]

        q_t, k_aug, v_t, u, u_t, wout_bf, wfc1_bf, wfc2_bf = _inproj(
            x, mod3, g_mix[l][None, :], win_hi, win_lo, cos_t, sin_t, w_out[l], w_fc1[l], w_fc2[l])
        attn = _attention(q_t, k_aug, v_t)

        prm = _s5_params(lam_re[l], lam_im[l], log_dt[l], b_re[l], b_im[l], c_re[l], c_im[l])
        ssm = _s5(u, u_t, prm, d_skip[l].reshape(1, SSM_WIDTH), w_glu[l].astype(BF16), b_glu[l][None, :])

        x = _outmlp(x, attn, ssm, mod3, g_attn_out[l][None, :], g_ssm_out[l][None, :], g_mlp[l][None, :],
                    g_final[None, :], wout_bf, wfc1_bf, wfc2_bf, final_norm=(l == depth - 1))
    return x
```

```python
import functools
import math

import jax
import jax.numpy as jnp
from jax import lax
from jax.experimental import pallas as pl
from jax.experimental.pallas import tpu as pltpu

F32 = jnp.float32
BF16 = jnp.bfloat16

HEAD_DIM = 64
ATTN_HEADS = 8
ATTN_WIDTH = ATTN_HEADS * HEAD_DIM
SSM_GROUPS = 32
SSM_GROUP_CH = 16
SSM_WIDTH = SSM_GROUPS * SSM_GROUP_CH
SSM_STATE = 64
SSM_NSTATE = SSM_GROUPS * SSM_STATE
MOBA_BLOCK = 256
MOBA_TOPK = 3
ROPE_THETA = 10000.0
EPS = 1e-6

LANES = 128
SUBLANES = 8
N_SEG = SUBLANES
HEADS_PER_TILE = LANES // HEAD_DIM
MASK_VALUE = -(2.0 ** 100)
VMEM_LIMIT_BYTES = 56 * 1024 * 1024


def _split_bf16(a):
    hi = a.astype(BF16)
    lo = (a - hi.astype(F32)).astype(BF16)
    return hi, lo


def _dot(a, b):
    return jnp.dot(a, b, preferred_element_type=F32)


def _rmsnorm(x, g):
    return x * lax.rsqrt(jnp.mean(x * x, axis=-1, keepdims=True) + EPS) * g


def _params(*semantics):
    return pltpu.CompilerParams(dimension_semantics=semantics, vmem_limit_bytes=VMEM_LIMIT_BYTES)


ADALN_STEPS = 4


def _adaln_kernel(c_ref, w_ref, b_ref, win_ref, o_ref, win_hi_ref, win_lo_ref):
    c = c_ref[...]
    s_hi, s_lo = _split_bf16(c * jax.nn.sigmoid(c))
    w_hi, w_lo = _split_bf16(w_ref[...])
    o_ref[...] = _dot(s_hi, w_hi) + _dot(s_lo, w_hi) + _dot(s_hi, w_lo) + b_ref[...]
    win_hi_ref[...], win_lo_ref[...] = _split_bf16(win_ref[...])


def _adaln(c_pad, w, b, w_in):
    rows, d = c_pad.shape
    n = w.shape[1]
    tn = n // ADALN_STEPS
    tw = w_in.shape[1] // ADALN_STEPS
    slab = pl.BlockSpec((d, tw), lambda j: (0, j))
    return pl.pallas_call(
        _adaln_kernel,
        grid=(ADALN_STEPS,),
        in_specs=[pl.BlockSpec((rows, d), lambda j: (0, 0)),
                  pl.BlockSpec((d, tn), lambda j: (0, j)),
                  pl.BlockSpec((1, tn), lambda j: (0, j)),
                  slab],
        out_specs=[pl.BlockSpec((rows, tn), lambda j: (0, j)), slab, slab],
        out_shape=[jax.ShapeDtypeStruct((rows, n), F32),
                   jax.ShapeDtypeStruct(w_in.shape, BF16),
                   jax.ShapeDtypeStruct(w_in.shape, BF16)],
        compiler_params=_params("arbitrary"),
        name="adaln",
    )(c_pad, w, b, w_in)


VT_ROWS = HEAD_DIM + 16


def _top_blocks(gate, blk, past, own):
    nb = gate.shape[0]
    g = jnp.where(past, gate, -jnp.inf)
    chosen = None
    for _ in range(MOBA_TOPK):
        best = jnp.max(g, axis=0, keepdims=True)
        first = jnp.min(jnp.where(g == best, blk, float(nb)), axis=0, keepdims=True)
        pick = blk == first
        chosen = pick if chosen is None else (chosen | pick)
        g = jnp.where(pick, -jnp.inf, g)
    return (chosen & past) | own


INPROJ_BLOCKS = 4


def _inproj_route(i, slot, qk, vu, cos_ref, sin_ref, qt_ref, ka_ref, vt_ref, u_ref, ut_ref, km_scr):
    tm = qk.shape[0]
    nb = km_scr.shape[0]
    rows = slice(slot * tm, (slot + 1) * tm)
    u = vu[:, ATTN_WIDTH:]
    u_ref[0, rows] = u
    ut_ref[0, slot] = u.T.astype(BF16)

    reps = ATTN_WIDTH // LANES
    cos = jnp.concatenate([cos_ref[rows]] * reps, axis=1)
    sin = jnp.concatenate([sin_ref[rows]] * reps, axis=1)
    wide_lane = lax.broadcasted_iota(jnp.int32, (tm, ATTN_WIDTH), 1)
    first_half = (wide_lane & (HEAD_DIM // 2)) == 0

    def rope(t):
        partner = jnp.where(first_half,
                            pltpu.roll(t, ATTN_WIDTH - HEAD_DIM // 2, 1),
                            pltpu.roll(t, HEAD_DIM // 2, 1))
        return t * cos + partner * sin

    q = rope(qk[:, :ATTN_WIDTH])
    k = rope(qk[:, ATTN_WIDTH:])
    v = vu[:, :ATTN_WIDTH]

    km_scr[pl.ds(i, 1), :] = jnp.sum(k, axis=0, keepdims=True) * (1.0 / tm)
    km_all = km_scr[...]

    lane = lax.broadcasted_iota(jnp.int32, (tm, LANES), 1)
    km_lane = lax.broadcasted_iota(jnp.int32, (nb, LANES), 1)
    blk_i = lax.broadcasted_iota(jnp.int32, (nb, tm), 0)
    blk = blk_i.astype(F32)
    past = blk_i < i
    own = blk_i == i
    ones_rows = jnp.where(lax.broadcasted_iota(jnp.int32, (VT_ROWS - HEAD_DIM, tm), 0) == 0, 1.0, 0.0)
    onehot = jnp.where(lane == HEAD_DIM + i, 1.0, 0.0)
    scale = HEAD_DIM ** -0.5 * math.log2(math.e)
    for tile in range(ATTN_WIDTH // LANES):
        cols = slice(tile * LANES, (tile + 1) * LANES)
        q_t = q[:, cols].T
        v_t = v[:, cols].T
        k_tile = k[:, cols]
        km = km_all[:, cols]
        qt_hi, qt_lo = _split_bf16(q_t)
        for hh in range(HEADS_PER_TILE):
            h_idx = tile * HEADS_PER_TILE + hh
            kmh_hi, kmh_lo = _split_bf16(jnp.where(km_lane // HEAD_DIM == hh, km, 0.0))
            gate = _dot(kmh_hi, qt_hi) + _dot(kmh_lo, qt_hi) + _dot(kmh_hi, qt_lo)
            bias_t = jnp.where(_top_blocks(gate, blk, past, own), 0.0, MASK_VALUE)
            head_rows = slice(hh * HEAD_DIM, (hh + 1) * HEAD_DIM)
            qt_ref[0, h_idx, slot] = jnp.concatenate(
                [q_t[head_rows] * scale, bias_t, jnp.zeros((LANES - HEAD_DIM - nb, tm), F32)],
                axis=0).astype(BF16)
            vt_ref[0, h_idx, slot] = jnp.concatenate([v_t[head_rows], ones_rows], axis=0).astype(BF16)
            kh = jnp.where(lane // HEAD_DIM == hh, k_tile, 0.0)
            if hh:
                kh = pltpu.roll(kh, LANES - hh * HEAD_DIM, 1)
            ka_ref[0, h_idx, rows] = jnp.where(lane < HEAD_DIM, kh, onehot).astype(BF16)


def _inproj_kernel(x_ref, mod_ref, g_ref, wqk_hi_ref, wqk_lo_ref, wvu_ref, cos_ref, sin_ref,
                   wout_ref, wfc1_ref, wfc2_ref,
                   qt_ref, ka_ref, vt_ref, u_ref, ut_ref, wout_bf_ref, wfc1_bf_ref, wfc2_bf_ref, km_scr):
    step = pl.program_id(1)

    @pl.when(step == 0)
    def _():
        km_scr[...] = jnp.zeros_like(km_scr)

    wout_bf_ref[...] = wout_ref[...].astype(BF16)
    wfc1_bf_ref[...] = wfc1_ref[...].astype(BF16)
    wfc2_bf_ref[...] = wfc2_ref[...].astype(BF16)
    d = x_ref.shape[2]
    tm = x_ref.shape[1] // INPROJ_BLOCKS
    sh1 = mod_ref[0, :, 0:d]
    sc1 = mod_ref[0, :, d:2 * d]
    hs = []
    for slot in range(INPROJ_BLOCKS):
        x = x_ref[0, slot * tm:(slot + 1) * tm, :]
        hs.append(_split_bf16(_rmsnorm(x, g_ref[...]) * (1.0 + sc1) + sh1))
    prj = []
    for h_hi, h_lo in hs:
        qk = (_dot(h_hi, wqk_hi_ref[...]) + _dot(h_lo, wqk_hi_ref[...])
              + _dot(h_hi, wqk_lo_ref[...]))
        prj.append((qk, _dot(h_hi, wvu_ref[...])))

    for slot, (qk, vu) in enumerate(prj):
        _inproj_route(step * INPROJ_BLOCKS + slot, slot, qk, vu, cos_ref, sin_ref,
                      qt_ref, ka_ref, vt_ref, u_ref, ut_ref, km_scr)


def _inproj(x, mod3, g_mix, win_hi, win_lo, cos_t, sin_t, w_out, w_fc1, w_fc2):
    bsz, seq, d = x.shape
    tm = MOBA_BLOCK
    nb = seq // tm
    assert nb % SUBLANES == 0 and nb <= LANES - HEAD_DIM and nb % INPROJ_BLOCKS == 0
    nblk = INPROJ_BLOCKS
    rows = nblk * tm
    const = lambda b, t: (0, 0)
    wqk = 2 * ATTN_WIDTH
    assert win_hi.shape[1] == 2 * wqk
    per_batch = nb // nblk
    steps = bsz * per_batch

    def slab(w):
        n = w.shape[0] // steps
        assert n * steps == w.shape[0] and n % (2 * SUBLANES) == 0
        return pl.BlockSpec((n, w.shape[1]), lambda b, t: (b * per_batch + t, 0))

    casts = [w_out, w_fc1, w_fc2]
    return pl.pallas_call(
        _inproj_kernel,
        grid=(bsz, per_batch),
        in_specs=[pl.BlockSpec((1, rows, d), lambda b, t: (b, t, 0)),
                  pl.BlockSpec((1, 1, mod3.shape[-1]), lambda b, t: (b, 0, 0)),
                  pl.BlockSpec((1, d), const),
                  pl.BlockSpec((d, wqk), const),
                  pl.BlockSpec((d, wqk), const),
                  pl.BlockSpec((d, wqk), lambda b, t: (0, 1)),
                  pl.BlockSpec((rows, LANES), lambda b, t: (t, 0)),
                  pl.BlockSpec((rows, LANES), lambda b, t: (t, 0))] + [slab(w) for w in casts],
        out_specs=[pl.BlockSpec((1, ATTN_HEADS, nblk, LANES, tm), lambda b, t: (b, 0, t, 0, 0)),
                   pl.BlockSpec((1, ATTN_HEADS, rows, LANES), lambda b, t: (b, 0, t, 0)),
                   pl.BlockSpec((1, ATTN_HEADS, nblk, VT_ROWS, tm), lambda b, t: (b, 0, t, 0, 0)),
                   pl.BlockSpec((1, rows, SSM_WIDTH), lambda b, t: (b, t, 0)),
                   pl.BlockSpec((1, nblk, SSM_WIDTH, tm), lambda b, t: (b, t, 0, 0))] + [slab(w) for w in casts],
        out_shape=[jax.ShapeDtypeStruct((bsz, ATTN_HEADS, nb, LANES, tm), BF16),
                   jax.ShapeDtypeStruct((bsz, ATTN_HEADS, seq, LANES), BF16),
                   jax.ShapeDtypeStruct((bsz, ATTN_HEADS, nb, VT_ROWS, tm), BF16),
                   jax.ShapeDtypeStruct((bsz, seq, SSM_WIDTH), F32),
                   jax.ShapeDtypeStruct((bsz, nb, SSM_WIDTH, tm), BF16)]
                  + [jax.ShapeDtypeStruct(w.shape, BF16) for w in casts],
        scratch_shapes=[pltpu.VMEM((nb, ATTN_WIDTH), F32)],
        compiler_params=_params("arbitrary", "arbitrary"),
        name="inproj",
    )(x, mod3, g_mix, win_hi, win_lo, win_hi, cos_t, sin_t, *casts)


ATTN_QBLOCKS = 2
RUNNING_MAX_INIT = -3.0e38


def _attn_kernel(qt_ref, ka_ref, vt_ref, o_ref, sa_scr, sb_scr, m_scr, acc_scr):
    pair = pl.program_id(1)
    i = ATTN_QBLOCKS * pair
    nh = qt_ref.shape[1]
    tq = qt_ref.shape[4]
    key = lax.broadcasted_iota(jnp.int32, (tq, tq), 0)
    qry = lax.broadcasted_iota(jnp.int32, (tq, tq), 1)
    causal = key <= qry

    def scores(qb, h, j, s_scr, mask=None):
        start = pl.multiple_of(j * tq, tq)
        s = _dot(ka_ref[0, h, pl.ds(start, tq), :], qt_ref[0, h, qb])
        if mask is not None:
            s = jnp.where(mask, s, MASK_VALUE)
        s_scr[qb * nh + h] = s

    def consume(qb, h, j, s_scr):
        c = qb * nh + h
        s = s_scr[c]
        m_old = m_scr[c]
        m_new = jnp.maximum(m_old, jnp.max(s, axis=0, keepdims=True))
        alpha = jnp.exp2(m_old - m_new)
        p = jnp.exp2(s - m_new).astype(BF16)
        acc_scr[c] = alpha * acc_scr[c] + _dot(vt_ref[0, h, j], p)
        m_scr[c] = m_new

    m_scr[...] = jnp.full(m_scr.shape, RUNNING_MAX_INIT, F32)
    acc_scr[...] = jnp.zeros(acc_scr.shape, F32)
    for h in range(nh):
        scores(0, h, i, sa_scr, causal)
        scores(1, h, i + 1, sa_scr, causal)
    for h in range(nh):
        scores(1, h, i, sb_scr)
        consume(1, h, i + 1, sa_scr)

    def two_blocks(j0):
        j1 = j0 + 1
        prev = jnp.where(j0 == 0, i, j0 - 1)
        for h in range(nh):
            scores(0, h, j0, sb_scr)
            consume(0, h, prev, sa_scr)
            scores(1, h, j0, sa_scr)
            consume(1, h, prev, sb_scr)
        for h in range(nh):
            scores(0, h, j1, sa_scr)
            consume(0, h, j0, sb_scr)
            scores(1, h, j1, sb_scr)
            consume(1, h, j0, sa_scr)

    def body(t, carry):
        two_blocks(4 * t)
        two_blocks(4 * t + 2)
        return carry

    lax.fori_loop(0, pair // 2, body, 0)

    @pl.when(pair % 2 == 1)
    def _():
        two_blocks(i - 2)

    last = jnp.where(pair == 0, i, i - 1)
    for h in range(nh):
        consume(0, h, last, sa_scr)
        consume(1, h, last, sb_scr)

    for qb in range(ATTN_QBLOCKS):
        for vt in range(nh // HEADS_PER_TILE):
            rows = []
            for hh in range(HEADS_PER_TILE):
                acc = acc_scr[qb * nh + vt * HEADS_PER_TILE + hh]
                rows.append(acc[0:HEAD_DIM] / acc[HEAD_DIM:HEAD_DIM + 1])
            o_ref[0, qb * tq:(qb + 1) * tq, vt * LANES:(vt + 1) * LANES] = jnp.concatenate(rows, axis=0).T


def _attention(q_t, k_aug, v_t):
    bsz, nh, nb, _, tq = q_t.shape
    seq = nb * tq
    assert nb % ATTN_QBLOCKS == 0
    chains = ATTN_QBLOCKS * nh
    resident = pl.Buffered(1)
    return pl.pallas_call(
        _attn_kernel,
        grid=(bsz, nb // ATTN_QBLOCKS),
        in_specs=[pl.BlockSpec((1, nh, ATTN_QBLOCKS, LANES, tq), lambda b, g: (b, 0, g, 0, 0)),
                  pl.BlockSpec((1, nh, seq, LANES), lambda b, g: (b, 0, 0, 0), pipeline_mode=resident),
                  pl.BlockSpec((1, nh, nb, VT_ROWS, tq), lambda b, g: (b, 0, 0, 0, 0), pipeline_mode=resident)],
        out_specs=pl.BlockSpec((1, ATTN_QBLOCKS * tq, nh * HEAD_DIM), lambda b, g: (b, g, 0)),
        out_shape=jax.ShapeDtypeStruct((bsz, seq, nh * HEAD_DIM), F32),
        scratch_shapes=[pltpu.VMEM((chains, tq, tq), F32), pltpu.VMEM((chains, tq, tq), F32),
                        pltpu.VMEM((chains, 1, tq), F32), pltpu.VMEM((chains, VT_ROWS, tq), F32)],
        compiler_params=_params("arbitrary", "arbitrary"),
        name="attn",
    )(q_t, k_aug, v_t)


S5_STEPS = 64
S5_CHUNKS = 4
S5_CHUNK_GROUPS = SSM_GROUPS // S5_CHUNKS
S5_CHUNK_STATES = S5_CHUNK_GROUPS * SSM_STATE
S5_CHUNK_LANES = S5_CHUNK_GROUPS * SSM_GROUP_CH


S5_CARRY_GROUPS = LANES // SSM_STATE


def _cmul(ar, ai, br, bi):
    return ar * br - ai * bi, ar * bi + ai * br


def _s5_carry_kernel(ut_ref, lam_ref, bt_ref, o_ref):
    bsz, nblk, nrows, tm = ut_ref.shape
    nch = nrows // S5_CARRY_GROUPS
    seg_blocks = nblk // N_SEG
    ls = seg_blocks * tm
    reps = bsz * N_SEG
    sub = lax.broadcasted_iota(jnp.int32, (SUBLANES, LANES), 0)
    lr = jnp.broadcast_to(lam_ref[0, 0:1], (SUBLANES, LANES))
    li = jnp.broadcast_to(lam_ref[0, 1:2], (SUBLANES, LANES))
    pr = jnp.ones((SUBLANES, LANES), F32)
    pi = jnp.zeros((SUBLANES, LANES), F32)
    tr = jnp.zeros((SUBLANES, LANES), F32)
    ti = jnp.zeros((SUBLANES, LANES), F32)
    for r in range(SUBLANES):
        tr = jnp.where(sub == SUBLANES - 1 - r, pr, tr)
        ti = jnp.where(sub == SUBLANES - 1 - r, pi, ti)
        pr, pi = _cmul(pr, pi, lr, li)
    span = SUBLANES
    while span < ls:
        nr, ni = _cmul(tr, ti, pr[0:1], pi[0:1])
        tr = jnp.concatenate([nr, tr], axis=0)
        ti = jnp.concatenate([ni, ti], axis=0)
        pr, pi = _cmul(pr, pi, pr, pi)
        span *= 2
    def seg_rows(g, b, s):
        return jnp.concatenate([ut_ref[b, s * seg_blocks + k, g * nch:(g + 1) * nch, :]
                                for k in range(seg_blocks)], axis=1)

    lhs = jnp.concatenate([seg_rows(g, b, s)
                           for g in range(S5_CARRY_GROUPS) for b in range(bsz) for s in range(N_SEG)],
                          axis=0)
    gr = _dot(lhs, tr.astype(BF16))
    gi = _dot(lhs, ti.astype(BF16))
    btr = jnp.concatenate([bt_ref[0, 0]] * (reps * S5_CARRY_GROUPS), axis=0)
    bti = jnp.concatenate([bt_ref[0, 1]] * (reps * S5_CARRY_GROUPS), axis=0)
    wr, wi = _cmul(gr, gi, btr, bti)
    fr = wr.reshape(reps * S5_CARRY_GROUPS, nch, LANES).sum(axis=1)
    fi = wi.reshape(reps * S5_CARRY_GROUPS, nch, LANES).sum(axis=1)
    grp = lax.broadcasted_iota(jnp.int32, (reps, LANES), 1) // SSM_STATE
    fin_r, fin_i = fr[0:reps], fi[0:reps]
    for g in range(1, S5_CARRY_GROUPS):
        fin_r = jnp.where(grp == g, fr[g * reps:(g + 1) * reps], fin_r)
        fin_i = jnp.where(grp == g, fi[g * reps:(g + 1) * reps], fin_i)
    rows_r, rows_i = [], []
    for b in range(bsz):
        sr = jnp.zeros((1, LANES), F32)
        si = jnp.zeros((1, LANES), F32)
        rows_r.append(sr)
        rows_i.append(si)
        for seg in range(1, N_SEG):
            r = b * N_SEG + seg - 1
            mr, mi = _cmul(sr, si, pr[0:1], pi[0:1])
            sr, si = mr + fin_r[r:r + 1], mi + fin_i[r:r + 1]
            rows_r.append(sr)
            rows_i.append(si)
    o_ref[0, 0] = jnp.concatenate(rows_r, axis=0)
    o_ref[0, 1] = jnp.concatenate(rows_i, axis=0)


def _s5_main_kernel(u_ref, bc_ref, cre_ref, cim_ref, lam_ref, sinit_ref, d_ref, wglu_ref, bglu_ref, o_ref,
                    il_scr, bu_scr, st_scr, y_scr):
    jb = pl.program_id(1)
    nseg, steps, width = u_ref.shape
    cs = S5_CHUNK_STATES

    @pl.when(jb == 0)
    def _():
        st_scr[...] = sinit_ref[...]

    for j in range(steps):
        il_scr[j * nseg:(j + 1) * nseg, :] = u_ref[:, j, :]
    u = il_scr[...]
    u_bf = u.astype(BF16)
    for c in range(S5_CHUNKS):
        bu_scr[c] = _dot(u_bf[:, c * S5_CHUNK_LANES:(c + 1) * S5_CHUNK_LANES], bc_ref[c])
    ys = []
    for c in range(S5_CHUNKS):
        lr = jnp.broadcast_to(lam_ref[c, 0:1, :], (nseg, cs))
        li = jnp.broadcast_to(lam_ref[c, 1:2, :], (nseg, cs))
        xr = st_scr[c, :, 0:cs]
        xi = st_scr[c, :, cs:2 * cs]
        for j in range(steps):
            rows = slice(j * nseg, (j + 1) * nseg)
            xr, xi = (lr * xr - li * xi + bu_scr[c, rows, 0:cs],
                      lr * xi + li * xr + bu_scr[c, rows, cs:2 * cs])
            bu_scr[c, rows, 0:cs] = xr
            bu_scr[c, rows, cs:2 * cs] = xi
        st_scr[c, :, 0:cs] = xr
        st_scr[c, :, cs:2 * cs] = xi
        ys.append(_dot(bu_scr[c, :, 0:cs].astype(BF16), cre_ref[c])
                  + _dot(bu_scr[c, :, cs:2 * cs].astype(BF16), cim_ref[c]))
    y = jnp.concatenate(ys, axis=1) + d_ref[...] * u
    y = y * (0.5 * (1.0 + jnp.tanh(math.sqrt(2.0 / math.pi) * (y + 0.044715 * (y * y * y)))))
    z = _dot(y.astype(BF16), wglu_ref[...]) + bglu_ref[...]
    y = y * jax.nn.sigmoid(z)
    for t in range(width // LANES):
        y_scr[t] = y[:, t * LANES:(t + 1) * LANES]
    for s in range(nseg):
        for t in range(width // LANES):
            o_ref[s, :, t * LANES:(t + 1) * LANES] = y_scr[t, pl.ds(s, steps, stride=nseg), :]


def _s5(u, u_t, prm, d_skip, w_glu, b_glu):
    bsz, seq, _ = u.shape
    ls = seq // N_SEG
    steps = S5_STEPS
    rows = steps * N_SEG
    ngrp = SSM_GROUPS // S5_CARRY_GROUPS
    carry = pl.pallas_call(
        _s5_carry_kernel,
        grid=(ngrp,),
        in_specs=[pl.BlockSpec((bsz, u_t.shape[1], S5_CARRY_GROUPS * SSM_GROUP_CH, u_t.shape[3]),
                               lambda g: (0, 0, g, 0)),
                  pl.BlockSpec((1, 2, LANES), lambda g: (g, 0, 0)),
                  pl.BlockSpec((1, 2, SSM_GROUP_CH, LANES), lambda g: (g, 0, 0, 0))],
        out_specs=pl.BlockSpec((1, 2, bsz * N_SEG, LANES), lambda g: (g, 0, 0, 0)),
        out_shape=jax.ShapeDtypeStruct((ngrp, 2, bsz * N_SEG, LANES), F32),
        compiler_params=_params("arbitrary"),
        name="s5_carry",
    )(u_t, prm["lam_carry"], prm["bt_carry"])
    sinit = carry.reshape(S5_CHUNKS, ngrp // S5_CHUNKS, 2, bsz, N_SEG, LANES)
    sinit = sinit.transpose(3, 0, 4, 2, 1, 5).reshape(bsz, S5_CHUNKS, N_SEG, 2 * S5_CHUNK_STATES)

    def const(shape):
        return pl.BlockSpec(shape, lambda b, j: (0,) * len(shape))

    blk = pl.BlockSpec((None, N_SEG, steps, SSM_WIDTH), lambda b, j: (b, 0, j, 0))
    out = pl.pallas_call(
        _s5_main_kernel,
        grid=(bsz, ls // steps),
        in_specs=[blk, const(prm["bc"].shape), const(prm["cre"].shape), const(prm["cim"].shape),
                  const(prm["lam_chunk"].shape),
                  pl.BlockSpec((None, S5_CHUNKS, N_SEG, 2 * S5_CHUNK_STATES), lambda b, j: (b, 0, 0, 0)),
                  const(d_skip.shape), const(w_glu.shape), const(b_glu.shape)],
        out_specs=blk,
        out_shape=jax.ShapeDtypeStruct((bsz, N_SEG, ls, SSM_WIDTH), F32),
        scratch_shapes=[pltpu.VMEM((rows, SSM_WIDTH), F32),
                        pltpu.VMEM((S5_CHUNKS, rows, 2 * S5_CHUNK_STATES), F32),
                        pltpu.VMEM((S5_CHUNKS, N_SEG, 2 * S5_CHUNK_STATES), F32),
                        pltpu.VMEM((SSM_WIDTH // LANES, rows, LANES), F32)],
        compiler_params=_params("arbitrary", "arbitrary"),
        name="s5_main",
    )(u.reshape(bsz, N_SEG, ls, SSM_WIDTH), prm["bc"], prm["cre"], prm["cim"], prm["lam_chunk"], sinit,
      d_skip, w_glu, b_glu)
    return out.reshape(bsz, seq, SSM_WIDTH)


FF_CHUNK = 1024


def _outmlp_kernel(x_ref, attn_ref, ssm_ref, mod_ref, ga_ref, gs_ref, gm_ref, gf_ref,
                   wout_ref, w1_ref, w2_ref, o_ref, *, final_norm):
    x = x_ref[0]
    tm, d = x.shape
    gt1 = mod_ref[0, :, 2 * d:3 * d]
    sh2 = mod_ref[0, :, 3 * d:4 * d]
    sc2 = mod_ref[0, :, 4 * d:5 * d]
    gt2 = mod_ref[0, :, 5 * d:6 * d]
    a = _rmsnorm(attn_ref[0], ga_ref[...]).astype(BF16)
    s = _rmsnorm(ssm_ref[0], gs_ref[...]).astype(BF16)
    mixed = _dot(a, wout_ref[0:ATTN_WIDTH, :]) + _dot(s, wout_ref[ATTN_WIDTH:, :])
    x1 = x + gt1 * mixed
    h = (_rmsnorm(x1, gm_ref[...]) * (1.0 + sc2) + sh2).astype(BF16)
    ff = w1_ref.shape[1]
    acc = jnp.zeros((tm, d), F32)
    for c in range(ff // FF_CHUNK):
        t = jnp.maximum(_dot(h, w1_ref[:, c * FF_CHUNK:(c + 1) * FF_CHUNK]), 0.0)
        acc = acc + _dot((t * t).astype(BF16), w2_ref[c * FF_CHUNK:(c + 1) * FF_CHUNK, :])
    x2 = x1 + gt2 * acc
    o_ref[0] = _rmsnorm(x2, gf_ref[...]) if final_norm else x2


OUTMLP_ROWS = 512


def _outmlp(x, attn, ssm, mod3, g_attn, g_ssm, g_mlp, g_final, w_out, w_fc1, w_fc2, final_norm):
    bsz, seq, d = x.shape
    tm = OUTMLP_ROWS
    const = lambda b, t: (0, 0)
    return pl.pallas_call(
        functools.partial(_outmlp_kernel, final_norm=final_norm),
        grid=(bsz, seq // tm),
        in_specs=[pl.BlockSpec((1, tm, d), lambda b, t: (b, t, 0)),
                  pl.BlockSpec((1, tm, ATTN_WIDTH), lambda b, t: (b, t, 0)),
                  pl.BlockSpec((1, tm, SSM_WIDTH), lambda b, t: (b, t, 0)),
                  pl.BlockSpec((1, 1, mod3.shape[-1]), lambda b, t: (b, 0, 0)),
                  pl.BlockSpec(g_attn.shape, const),
                  pl.BlockSpec(g_ssm.shape, const),
                  pl.BlockSpec(g_mlp.shape, const),
                  pl.BlockSpec(g_final.shape, const),
                  pl.BlockSpec(w_out.shape, const),
                  pl.BlockSpec(w_fc1.shape, const),
                  pl.BlockSpec(w_fc2.shape, const)],
        out_specs=pl.BlockSpec((1, tm, d), lambda b, t: (b, t, 0)),
        out_shape=jax.ShapeDtypeStruct((bsz, seq, d), F32),
        compiler_params=_params("arbitrary", "arbitrary"),
        name="outmlp",
    )(x, attn, ssm, mod3, g_attn, g_ssm, g_mlp, g_final, w_out, w_fc1, w_fc2)


def _rope_tables(seq):
    half = HEAD_DIM // 2
    inv_freq = ROPE_THETA ** (-jnp.arange(half, dtype=F32) / half)
    ang = jnp.arange(seq, dtype=F32)[:, None] * inv_freq[None, :]
    cos, sin = jnp.cos(ang), jnp.sin(ang)
    reps = LANES // HEAD_DIM
    cos_t = jnp.tile(cos, (1, 2 * reps))
    sin_t = jnp.tile(jnp.concatenate([-sin, sin], axis=1), (1, reps))
    return cos_t, sin_t


def _s5_params(lam_re, lam_im, log_dt, b_re, b_im, c_re, c_im):
    lr, li = lam_re.astype(F32), lam_im.astype(F32)
    dt = jnp.exp(log_dt.astype(F32))[:, None]
    mag = jnp.exp(lr * dt)
    ar, ai = mag * jnp.cos(li * dt), mag * jnp.sin(li * dt)
    den = lr * lr + li * li
    cr = ((ar - 1.0) * lr + ai * li) / den
    ci = (ai * lr - (ar - 1.0) * li) / den
    bbar_re = cr[..., None] * b_re.astype(F32) - ci[..., None] * b_im.astype(F32)
    bbar_im = cr[..., None] * b_im.astype(F32) + ci[..., None] * b_re.astype(F32)
    eye = jnp.eye(S5_CHUNK_GROUPS, dtype=F32)

    def chunked(m):
        return m.reshape((S5_CHUNKS, S5_CHUNK_GROUPS) + m.shape[1:])

    def diag_in(m):
        return jnp.einsum('cgpn,gh->cgnhp', chunked(m), eye).reshape(S5_CHUNKS, S5_CHUNK_LANES, S5_CHUNK_STATES)

    def diag_out(m):
        return jnp.einsum('cgnp,gh->cgphn', chunked(m), eye).reshape(S5_CHUNKS, S5_CHUNK_STATES, S5_CHUNK_LANES)

    ngrp = SSM_GROUPS // S5_CARRY_GROUPS

    def lane_tiles(m):
        m = m.reshape(ngrp, S5_CARRY_GROUPS, SSM_GROUP_CH, SSM_STATE)
        return m.transpose(0, 2, 1, 3).reshape(ngrp, SSM_GROUP_CH, LANES)

    return {
        "bc": jnp.concatenate([diag_in(bbar_re), diag_in(bbar_im)], axis=2).astype(BF16),
        "cre": diag_out(c_re.astype(F32)).astype(BF16),
        "cim": diag_out(-c_im.astype(F32)).astype(BF16),
        "lam_chunk": jnp.stack([ar.reshape(S5_CHUNKS, S5_CHUNK_STATES),
                                ai.reshape(S5_CHUNKS, S5_CHUNK_STATES)], axis=1),
        "lam_carry": jnp.stack([ar.reshape(ngrp, LANES), ai.reshape(ngrp, LANES)], axis=1),
        "bt_carry": jnp.stack([lane_tiles(bbar_re.transpose(0, 2, 1)),
                               lane_tiles(bbar_im.transpose(0, 2, 1))], axis=1),
    }


def kernel(x, c, w_ada, b_ada, g_mix, w_in, g_attn_out, lam_re, lam_im, log_dt, b_re, b_im, c_re, c_im,
           d_skip, w_glu, b_glu, g_ssm_out, w_out, g_mlp, w_fc1, w_fc2, g_final):
    bsz, seq, d = x.shape
    depth = w_ada.shape[0]
    assert seq % (N_SEG * MOBA_BLOCK) == 0 and (seq // N_SEG) % S5_STEPS == 0 and bsz <= SUBLANES
    cos_t, sin_t = _rope_tables(seq)
    c_pad = jnp.zeros((SUBLANES, d), F32).at[:bsz].set(c.astype(F32))

    for l in range(depth):
        mod, win_hi, win_lo = _adaln(c_pad, w_ada[l], b_ada[l][None, :], w_in[l])
        mod3 = mod[:bsz, None, :]

        q_t, k_aug, v_t, u, u_t, wout_bf, wfc1_bf, wfc2_bf = _inproj(
            x, mod3, g_mix[l][None, :], win_hi, win_lo, cos_t, sin_t, w_out[l], w_fc1[l], w_fc2[l])
        attn = _attention(q_t, k_aug, v_t)

        prm = _s5_params(lam_re[l], lam_im[l], log_dt[l], b_re[l], b_im[l], c_re[l], c_im[l])
        ssm = _s5(u, u_t, prm, d_skip[l].reshape(1, SSM_WIDTH), w_glu[l].astype(BF16), b_glu[l][None, :])

        x = _outmlp(x, attn, ssm, mod3, g_attn_out[l][None, :], g_ssm_out[l][None, :], g_mlp[l][None, :],
                    g_final[None, :], wout_bf, wfc1_bf, wfc2_bf, final_norm=(l == depth - 1))
    return x
```

```python
import functools
import math

import jax
import jax.numpy as jnp
from jax import lax
from jax.experimental import pallas as pl
from jax.experimental.pallas import tpu as pltpu

F32 = jnp.float32
BF16 = jnp.bfloat16

HEAD_DIM = 64
ATTN_HEADS = 8
ATTN_WIDTH = ATTN_HEADS * HEAD_DIM
SSM_GROUPS = 32
SSM_GROUP_CH = 16
SSM_WIDTH = SSM_GROUPS * SSM_GROUP_CH
SSM_STATE = 64
SSM_NSTATE = SSM_GROUPS * SSM_STATE
MOBA_BLOCK = 256
MOBA_TOPK = 3
ROPE_THETA = 10000.0
EPS = 1e-6

LANES = 128
SUBLANES = 8
N_SEG = SUBLANES
HEADS_PER_TILE = LANES // HEAD_DIM
MASK_VALUE = -(2.0 ** 100)
VMEM_LIMIT_BYTES = 56 * 1024 * 1024


def _split_bf16(a):
    hi = a.astype(BF16)
    lo = (a - hi.astype(F32)).astype(BF16)
    return hi, lo


def _dot(a, b):
    return jnp.dot(a, b, preferred_element_type=F32)


def _rmsnorm(x, g):
    return x * lax.rsqrt(jnp.mean(x * x, axis=-1, keepdims=True) + EPS) * g


def _params(*semantics):
    return pltpu.CompilerParams(dimension_semantics=semantics, vmem_limit_bytes=VMEM_LIMIT_BYTES)


ADALN_STEPS = 4


def _adaln_kernel(c_ref, w_ref, b_ref, win_ref, o_ref, win_hi_ref, win_lo_ref):
    c = c_ref[...]
    s_hi, s_lo = _split_bf16(c * jax.nn.sigmoid(c))
    w_hi, w_lo = _split_bf16(w_ref[...])
    o_ref[...] = _dot(s_hi, w_hi) + _dot(s_lo, w_hi) + _dot(s_hi, w_lo) + b_ref[...]
    win_hi_ref[...], win_lo_ref[...] = _split_bf16(win_ref[...])


def _adaln(c_pad, w, b, w_in):
    rows, d = c_pad.shape
    n = w.shape[1]
    tn = n // ADALN_STEPS
    tw = w_in.shape[1] // ADALN_STEPS
    slab = pl.BlockSpec((d, tw), lambda j: (0, j))
    return pl.pallas_call(
        _adaln_kernel,
        grid=(ADALN_STEPS,),
        in_specs=[pl.BlockSpec((rows, d), lambda j: (0, 0)),
                  pl.BlockSpec((d, tn), lambda j: (0, j)),
                  pl.BlockSpec((1, tn), lambda j: (0, j)),
                  slab],
        out_specs=[pl.BlockSpec((rows, tn), lambda j: (0, j)), slab, slab],
        out_shape=[jax.ShapeDtypeStruct((rows, n), F32),
                   jax.ShapeDtypeStruct(w_in.shape, BF16),
                   jax.ShapeDtypeStruct(w_in.shape, BF16)],
        compiler_params=_params("arbitrary"),
        name="adaln",
    )(c_pad, w, b, w_in)


VT_ROWS = HEAD_DIM + 16


def _top_blocks(gate, blk, past, own):
    nb = gate.shape[0]
    g = jnp.where(past, gate, -jnp.inf)
    chosen = None
    for _ in range(MOBA_TOPK):
        best = jnp.max(g, axis=0, keepdims=True)
        first = jnp.min(jnp.where(g == best, blk, float(nb)), axis=0, keepdims=True)
        pick = blk == first
        chosen = pick if chosen is None else (chosen | pick)
        g = jnp.where(pick, -jnp.inf, g)
    return (chosen & past) | own


INPROJ_BLOCKS = 4


def _inproj_route(i, slot, qk, vu, cos_ref, sin_ref, qt_ref, ka_ref, vt_ref, u_ref, ut_ref, km_scr):
    tm = qk.shape[0]
    nb = km_scr.shape[0]
    rows = slice(slot * tm, (slot + 1) * tm)
    u = vu[:, ATTN_WIDTH:]
    u_ref[0, rows] = u
    ut_ref[0, slot] = u.T.astype(BF16)

    reps = ATTN_WIDTH // LANES
    cos = jnp.concatenate([cos_ref[rows]] * reps, axis=1)
    sin = jnp.concatenate([sin_ref[rows]] * reps, axis=1)
    wide_lane = lax.broadcasted_iota(jnp.int32, (tm, ATTN_WIDTH), 1)
    first_half = (wide_lane & (HEAD_DIM // 2)) == 0

    def rope(t):
        partner = jnp.where(first_half,
                            pltpu.roll(t, ATTN_WIDTH - HEAD_DIM // 2, 1),
                            pltpu.roll(t, HEAD_DIM // 2, 1))
        return t * cos + partner * sin

    q = rope(qk[:, :ATTN_WIDTH])
    k = rope(qk[:, ATTN_WIDTH:])
    v = vu[:, :ATTN_WIDTH]

    km_scr[pl.ds(i, 1), :] = jnp.sum(k, axis=0, keepdims=True) * (1.0 / tm)
    km_all = km_scr[...]

    lane = lax.broadcasted_iota(jnp.int32, (tm, LANES), 1)
    km_lane = lax.broadcasted_iota(jnp.int32, (nb, LANES), 1)
    blk_i = lax.broadcasted_iota(jnp.int32, (nb, tm), 0)
    blk = blk_i.astype(F32)
    past = blk_i < i
    own = blk_i == i
    ones_rows = jnp.where(lax.broadcasted_iota(jnp.int32, (VT_ROWS - HEAD_DIM, tm), 0) == 0, 1.0, 0.0)
    onehot = jnp.where(lane == HEAD_DIM + i, 1.0, 0.0)
    scale = HEAD_DIM ** -0.5 * math.log2(math.e)
    for tile in range(ATTN_WIDTH // LANES):
        cols = slice(tile * LANES, (tile + 1) * LANES)
        q_t = q[:, cols].T
        v_t = v[:, cols].T
        k_tile = k[:, cols]
        km = km_all[:, cols]
        qt_hi, qt_lo = _split_bf16(q_t)
        for hh in range(HEADS_PER_TILE):
            h_idx = tile * HEADS_PER_TILE + hh
            kmh_hi, kmh_lo = _split_bf16(jnp.where(km_lane // HEAD_DIM == hh, km, 0.0))
            gate = _dot(kmh_hi, qt_hi) + _dot(kmh_lo, qt_hi) + _dot(kmh_hi, qt_lo)
            bias_t = jnp.where(_top_blocks(gate, blk, past, own), 0.0, MASK_VALUE)
            head_rows = slice(hh * HEAD_DIM, (hh + 1) * HEAD_DIM)
            qt_ref[0, h_idx, slot] = jnp.concatenate(
                [q_t[head_rows] * scale, bias_t, jnp.zeros((LANES - HEAD_DIM - nb, tm), F32)],
                axis=0).astype(BF16)
            vt_ref[0, h_idx, slot] = jnp.concatenate([v_t[head_rows], ones_rows], axis=0).astype(BF16)
            kh = jnp.where(lane // HEAD_DIM == hh, k_tile, 0.0)
            if hh:
                kh = pltpu.roll(kh, LANES - hh * HEAD_DIM, 1)
            ka_ref[0, h_idx, rows] = jnp.where(lane < HEAD_DIM, kh, onehot).astype(BF16)


def _inproj_kernel(x_ref, mod_ref, g_ref, wqk_hi_ref, wqk_lo_ref, wvu_ref, cos_ref, sin_ref,
                   wout_ref, wfc1_ref, wfc2_ref,
                   qt_ref, ka_ref, vt_ref, u_ref, ut_ref, wout_bf_ref, wfc1_bf_ref, wfc2_bf_ref, km_scr):
    step = pl.program_id(1)

    @pl.when(step == 0)
    def _():
        km_scr[...] = jnp.zeros_like(km_scr)

    wout_bf_ref[...] = wout_ref[...].astype(BF16)
    wfc1_bf_ref[...] = wfc1_ref[...].astype(BF16)
    wfc2_bf_ref[...] = wfc2_ref[...].astype(BF16)
    d = x_ref.shape[2]
    tm = x_ref.shape[1] // INPROJ_BLOCKS
    sh1 = mod_ref[0, :, 0:d]
    sc1 = mod_ref[0, :, d:2 * d]
    hs = []
    for slot in range(INPROJ_BLOCKS):
        x = x_ref[0, slot * tm:(slot + 1) * tm, :]
        hs.append(_split_bf16(_rmsnorm(x, g_ref[...]) * (1.0 + sc1) + sh1))
    prj = []
    for h_hi, h_lo in hs:
        qk = (_dot(h_hi, wqk_hi_ref[...]) + _dot(h_lo, wqk_hi_ref[...])
              + _dot(h_hi, wqk_lo_ref[...]))
        prj.append((qk, _dot(h_hi, wvu_ref[...])))

    for slot, (qk, vu) in enumerate(prj):
        _inproj_route(step * INPROJ_BLOCKS + slot, slot, qk, vu, cos_ref, sin_ref,
                      qt_ref, ka_ref, vt_ref, u_ref, ut_ref, km_scr)


def _inproj(x, mod3, g_mix, win_hi, win_lo, cos_t, sin_t, w_out, w_fc1, w_fc2):
    bsz, seq, d = x.shape
    tm = MOBA_BLOCK
    nb = seq // tm
    assert nb % SUBLANES == 0 and nb <= LANES - HEAD_DIM and nb % INPROJ_BLOCKS == 0
    nblk = INPROJ_BLOCKS
    rows = nblk * tm
    const = lambda b, t: (0, 0)
    wqk = 2 * ATTN_WIDTH
    assert win_hi.shape[1] == 2 * wqk
    per_batch = nb // nblk
    steps = bsz * per_batch

    def slab(w):
        n = w.shape[0] // steps
        assert n * steps == w.shape[0] and n % (2 * SUBLANES) == 0
        return pl.BlockSpec((n, w.shape[1]), lambda b, t: (b * per_batch + t, 0))

    casts = [w_out, w_fc1, w_fc2]
    return pl.pallas_call(
        _inproj_kernel,
        grid=(bsz, per_batch),
        in_specs=[pl.BlockSpec((1, rows, d), lambda b, t: (b, t, 0)),
                  pl.BlockSpec((1, 1, mod3.shape[-1]), lambda b, t: (b, 0, 0)),
                  pl.BlockSpec((1, d), const),
                  pl.BlockSpec((d, wqk), const),
                  pl.BlockSpec((d, wqk), const),
                  pl.BlockSpec((d, wqk), lambda b, t: (0, 1)),
                  pl.BlockSpec((rows, LANES), lambda b, t: (t, 0)),
                  pl.BlockSpec((rows, LANES), lambda b, t: (t, 0))] + [slab(w) for w in casts],
        out_specs=[pl.BlockSpec((1, ATTN_HEADS, nblk, LANES, tm), lambda b, t: (b, 0, t, 0, 0)),
                   pl.BlockSpec((1, ATTN_HEADS, rows, LANES), lambda b, t: (b, 0, t, 0)),
                   pl.BlockSpec((1, ATTN_HEADS, nblk, VT_ROWS, tm), lambda b, t: (b, 0, t, 0, 0)),
                   pl.BlockSpec((1, rows, SSM_WIDTH), lambda b, t: (b, t, 0)),
                   pl.BlockSpec((1, nblk, SSM_WIDTH, tm), lambda b, t: (b, t, 0, 0))] + [slab(w) for w in casts],
        out_shape=[jax.ShapeDtypeStruct((bsz, ATTN_HEADS, nb, LANES, tm), BF16),
                   jax.ShapeDtypeStruct((bsz, ATTN_HEADS, seq, LANES), BF16),
                   jax.ShapeDtypeStruct((bsz, ATTN_HEADS, nb, VT_ROWS, tm), BF16),
                   jax.ShapeDtypeStruct((bsz, seq, SSM_WIDTH), F32),
                   jax.ShapeDtypeStruct((bsz, nb, SSM_WIDTH, tm), BF16)]
                  + [jax.ShapeDtypeStruct(w.shape, BF16) for w in casts],
        scratch_shapes=[pltpu.VMEM((nb, ATTN_WIDTH), F32)],
        compiler_params=_params("arbitrary", "arbitrary"),
        name="inproj",
    )(x, mod3, g_mix, win_hi, win_lo, win_hi, cos_t, sin_t, *casts)


ATTN_QBLOCKS = 2
RUNNING_MAX_INIT = -3.0e38


def _attn_kernel(qt_ref, ka_ref, vt_ref, o_ref, sa_scr, sb_scr, m_scr, acc_scr):
    pair = pl.program_id(1)
    i = ATTN_QBLOCKS * pair
    nh = qt_ref.shape[1]
    tq = qt_ref.shape[4]
    key = lax.broadcasted_iota(jnp.int32, (tq, tq), 0)
    qry = lax.broadcasted_iota(jnp.int32, (tq, tq), 1)
    causal = key <= qry

    def scores(qb, h, j, s_scr, mask=None):
        start = pl.multiple_of(j * tq, tq)
        s = _dot(ka_ref[0, h, pl.ds(start, tq), :], qt_ref[0, h, qb])
        if mask is not None:
            s = jnp.where(mask, s, MASK_VALUE)
        s_scr[qb * nh + h] = s

    def consume(qb, h, j, s_scr):
        c = qb * nh + h
        s = s_scr[c]
        m_old = m_scr[c]
        m_new = jnp.maximum(m_old, jnp.max(s, axis=0, keepdims=True))
        alpha = jnp.exp2(m_old - m_new)
        p = jnp.exp2(s - m_new).astype(BF16)
        acc_scr[c] = alpha * acc_scr[c] + _dot(vt_ref[0, h, j], p)
        m_scr[c] = m_new

    m_scr[...] = jnp.full(m_scr.shape, RUNNING_MAX_INIT, F32)
    acc_scr[...] = jnp.zeros(acc_scr.shape, F32)
    for h in range(nh):
        scores(0, h, i, sa_scr, causal)
        scores(1, h, i + 1, sa_scr, causal)
    for h in range(nh):
        scores(1, h, i, sb_scr)
        consume(1, h, i + 1, sa_scr)

    def two_blocks(j0):
        j1 = j0 + 1
        prev = jnp.where(j0 == 0, i, j0 - 1)
        for h in range(nh):
            scores(0, h, j0, sb_scr)
            consume(0, h, prev, sa_scr)
            scores(1, h, j0, sa_scr)
            consume(1, h, prev, sb_scr)
        for h in range(nh):
            scores(0, h, j1, sa_scr)
            consume(0, h, j0, sb_scr)
            scores(1, h, j1, sb_scr)
            consume(1, h, j0, sa_scr)

    def body(t, carry):
        two_blocks(4 * t)
        two_blocks(4 * t + 2)
        return carry

    lax.fori_loop(0, pair // 2, body, 0)

    @pl.when(pair % 2 == 1)
    def _():
        two_blocks(i - 2)

    last = jnp.where(pair == 0, i, i - 1)
    for h in range(nh):
        consume(0, h, last, sa_scr)
        consume(1, h, last, sb_scr)

    for qb in range(ATTN_QBLOCKS):
        for vt in range(nh // HEADS_PER_TILE):
            rows = []
            for hh in range(HEADS_PER_TILE):
                acc = acc_scr[qb * nh + vt * HEADS_PER_TILE + hh]
                rows.append(acc[0:HEAD_DIM] / acc[HEAD_DIM:HEAD_DIM + 1])
            o_ref[0, qb * tq:(qb + 1) * tq, vt * LANES:(vt + 1) * LANES] = jnp.concatenate(rows, axis=0).T


def _attention(q_t, k_aug, v_t):
    bsz, nh, nb, _, tq = q_t.shape
    seq = nb * tq
    assert nb % ATTN_QBLOCKS == 0
    chains = ATTN_QBLOCKS * nh
    resident = pl.Buffered(1)
    return pl.pallas_call(
        _attn_kernel,
        grid=(bsz, nb // ATTN_QBLOCKS),
        in_specs=[pl.BlockSpec((1, nh, ATTN_QBLOCKS, LANES, tq), lambda b, g: (b, 0, g, 0, 0)),
                  pl.BlockSpec((1, nh, seq, LANES), lambda b, g: (b, 0, 0, 0), pipeline_mode=resident),
                  pl.BlockSpec((1, nh, nb, VT_ROWS, tq), lambda b, g: (b, 0, 0, 0, 0), pipeline_mode=resident)],
        out_specs=pl.BlockSpec((1, ATTN_QBLOCKS * tq, nh * HEAD_DIM), lambda b, g: (b, g, 0)),
        out_shape=jax.ShapeDtypeStruct((bsz, seq, nh * HEAD_DIM), F32),
        scratch_shapes=[pltpu.VMEM((chains, tq, tq), F32), pltpu.VMEM((chains, tq, tq), F32),
                        pltpu.VMEM((chains, 1, tq), F32), pltpu.VMEM((chains, VT_ROWS, tq), F32)],
        compiler_params=_params("arbitrary", "arbitrary"),
        name="attn",
    )(q_t, k_aug, v_t)


S5_STEPS = 128
S5_SUB_STEPS = 64
S5_CHUNKS = 4
S5_CHUNK_GROUPS = SSM_GROUPS // S5_CHUNKS
S5_CHUNK_STATES = S5_CHUNK_GROUPS * SSM_STATE
S5_CHUNK_LANES = S5_CHUNK_GROUPS * SSM_GROUP_CH


S5_CARRY_GROUPS = LANES // SSM_STATE


def _cmul(ar, ai, br, bi):
    return ar * br - ai * bi, ar * bi + ai * br


def _s5_carry_kernel(ut_ref, lam_ref, bt_ref, o_ref):
    bsz, nblk, nrows, tm = ut_ref.shape
    nch = nrows // S5_CARRY_GROUPS
    seg_blocks = nblk // N_SEG
    ls = seg_blocks * tm
    reps = bsz * N_SEG
    sub = lax.broadcasted_iota(jnp.int32, (SUBLANES, LANES), 0)
    lr = jnp.broadcast_to(lam_ref[0, 0:1], (SUBLANES, LANES))
    li = jnp.broadcast_to(lam_ref[0, 1:2], (SUBLANES, LANES))
    pr = jnp.ones((SUBLANES, LANES), F32)
    pi = jnp.zeros((SUBLANES, LANES), F32)
    tr = jnp.zeros((SUBLANES, LANES), F32)
    ti = jnp.zeros((SUBLANES, LANES), F32)
    for r in range(SUBLANES):
        tr = jnp.where(sub == SUBLANES - 1 - r, pr, tr)
        ti = jnp.where(sub == SUBLANES - 1 - r, pi, ti)
        pr, pi = _cmul(pr, pi, lr, li)
    span = SUBLANES
    while span < ls:
        nr, ni = _cmul(tr, ti, pr[0:1], pi[0:1])
        tr = jnp.concatenate([nr, tr], axis=0)
        ti = jnp.concatenate([ni, ti], axis=0)
        pr, pi = _cmul(pr, pi, pr, pi)
        span *= 2
    def seg_rows(g, b, s):
        return jnp.concatenate([ut_ref[b, s * seg_blocks + k, g * nch:(g + 1) * nch, :]
                                for k in range(seg_blocks)], axis=1)

    lhs = jnp.concatenate([seg_rows(g, b, s)
                           for g in range(S5_CARRY_GROUPS) for b in range(bsz) for s in range(N_SEG)],
                          axis=0)
    gr = _dot(lhs, tr.astype(BF16))
    gi = _dot(lhs, ti.astype(BF16))
    btr = jnp.concatenate([bt_ref[0, 0]] * (reps * S5_CARRY_GROUPS), axis=0)
    bti = jnp.concatenate([bt_ref[0, 1]] * (reps * S5_CARRY_GROUPS), axis=0)
    wr, wi = _cmul(gr, gi, btr, bti)
    fr = wr.reshape(reps * S5_CARRY_GROUPS, nch, LANES).sum(axis=1)
    fi = wi.reshape(reps * S5_CARRY_GROUPS, nch, LANES).sum(axis=1)
    grp = lax.broadcasted_iota(jnp.int32, (reps, LANES), 1) // SSM_STATE
    fin_r, fin_i = fr[0:reps], fi[0:reps]
    for g in range(1, S5_CARRY_GROUPS):
        fin_r = jnp.where(grp == g, fr[g * reps:(g + 1) * reps], fin_r)
        fin_i = jnp.where(grp == g, fi[g * reps:(g + 1) * reps], fin_i)
    rows_r, rows_i = [], []
    for b in range(bsz):
        sr = jnp.zeros((1, LANES), F32)
        si = jnp.zeros((1, LANES), F32)
        rows_r.append(sr)
        rows_i.append(si)
        for seg in range(1, N_SEG):
            r = b * N_SEG + seg - 1
            mr, mi = _cmul(sr, si, pr[0:1], pi[0:1])
            sr, si = mr + fin_r[r:r + 1], mi + fin_i[r:r + 1]
            rows_r.append(sr)
            rows_i.append(si)
    o_ref[0, 0] = jnp.concatenate(rows_r, axis=0)
    o_ref[0, 1] = jnp.concatenate(rows_i, axis=0)


def _s5_main_kernel(u_ref, bc_ref, cre_ref, cim_ref, lam_ref, sinit_ref, d_ref, wglu_ref, bglu_ref, o_ref,
                    il_scr, bu_scr, st_scr, y_scr):
    jb = pl.program_id(1)
    nseg, steps, width = u_ref.shape
    cs = S5_CHUNK_STATES
    nsub = steps // S5_SUB_STEPS
    srows = S5_SUB_STEPS * nseg

    @pl.when(jb == 0)
    def _():
        st_scr[...] = sinit_ref[...]

    for j in range(steps):
        il_scr[j * nseg:(j + 1) * nseg, :] = u_ref[:, j, :]
    us = [il_scr[sb * srows:(sb + 1) * srows, :] for sb in range(nsub)]
    for sb in range(nsub):
        u_bf = us[sb].astype(BF16)
        for c in range(S5_CHUNKS):
            bu_scr[c, sb * srows:(sb + 1) * srows, :] = _dot(
                u_bf[:, c * S5_CHUNK_LANES:(c + 1) * S5_CHUNK_LANES], bc_ref[c])
    state = [(st_scr[c, :, 0:cs], st_scr[c, :, cs:2 * cs]) for c in range(S5_CHUNKS)]
    for sb in range(nsub):
        sub_rows = slice(sb * srows, (sb + 1) * srows)
        ys = []
        for c in range(S5_CHUNKS):
            lr = jnp.broadcast_to(lam_ref[c, 0:1, :], (nseg, cs))
            li = jnp.broadcast_to(lam_ref[c, 1:2, :], (nseg, cs))
            xr, xi = state[c]
            for j in range(sb * S5_SUB_STEPS, (sb + 1) * S5_SUB_STEPS):
                rows = slice(j * nseg, (j + 1) * nseg)
                xr, xi = (lr * xr - li * xi + bu_scr[c, rows, 0:cs],
                          lr * xi + li * xr + bu_scr[c, rows, cs:2 * cs])
                bu_scr[c, rows, 0:cs] = xr
                bu_scr[c, rows, cs:2 * cs] = xi
            state[c] = (xr, xi)
            ys.append(_dot(bu_scr[c, sub_rows, 0:cs].astype(BF16), cre_ref[c])
                      + _dot(bu_scr[c, sub_rows, cs:2 * cs].astype(BF16), cim_ref[c]))
        y = jnp.concatenate(ys, axis=1) + d_ref[...] * us[sb]
        y = y * (0.5 * (1.0 + jnp.tanh(math.sqrt(2.0 / math.pi) * (y + 0.044715 * (y * y * y)))))
        z = _dot(y.astype(BF16), wglu_ref[...]) + bglu_ref[...]
        y = y * jax.nn.sigmoid(z)
        for t in range(width // LANES):
            y_scr[t, sub_rows, :] = y[:, t * LANES:(t + 1) * LANES]
    for c in range(S5_CHUNKS):
        st_scr[c, :, 0:cs] = state[c][0]
        st_scr[c, :, cs:2 * cs] = state[c][1]
    for s in range(nseg):
        for t in range(width // LANES):
            o_ref[s, :, t * LANES:(t + 1) * LANES] = y_scr[t, pl.ds(s, steps, stride=nseg), :]


def _s5(u, u_t, prm, d_skip, w_glu, b_glu):
    bsz, seq, _ = u.shape
    ls = seq // N_SEG
    steps = S5_STEPS
    rows = steps * N_SEG
    ngrp = SSM_GROUPS // S5_CARRY_GROUPS
    carry = pl.pallas_call(
        _s5_carry_kernel,
        grid=(ngrp,),
        in_specs=[pl.BlockSpec((bsz, u_t.shape[1], S5_CARRY_GROUPS * SSM_GROUP_CH, u_t.shape[3]),
                               lambda g: (0, 0, g, 0)),
                  pl.BlockSpec((1, 2, LANES), lambda g: (g, 0, 0)),
                  pl.BlockSpec((1, 2, SSM_GROUP_CH, LANES), lambda g: (g, 0, 0, 0))],
        out_specs=pl.BlockSpec((1, 2, bsz * N_SEG, LANES), lambda g: (g, 0, 0, 0)),
        out_shape=jax.ShapeDtypeStruct((ngrp, 2, bsz * N_SEG, LANES), F32),
        compiler_params=_params("arbitrary"),
        name="s5_carry",
    )(u_t, prm["lam_carry"], prm["bt_carry"])
    sinit = carry.reshape(S5_CHUNKS, ngrp // S5_CHUNKS, 2, bsz, N_SEG, LANES)
    sinit = sinit.transpose(3, 0, 4, 2, 1, 5).reshape(bsz, S5_CHUNKS, N_SEG, 2 * S5_CHUNK_STATES)

    def const(shape):
        return pl.BlockSpec(shape, lambda b, j: (0,) * len(shape))

    blk = pl.BlockSpec((None, N_SEG, steps, SSM_WIDTH), lambda b, j: (b, 0, j, 0))
    out = pl.pallas_call(
        _s5_main_kernel,
        grid=(bsz, ls // steps),
        in_specs=[blk, const(prm["bc"].shape), const(prm["cre"].shape), const(prm["cim"].shape),
                  const(prm["lam_chunk"].shape),
                  pl.BlockSpec((None, S5_CHUNKS, N_SEG, 2 * S5_CHUNK_STATES), lambda b, j: (b, 0, 0, 0)),
                  const(d_skip.shape), const(w_glu.shape), const(b_glu.shape)],
        out_specs=blk,
        out_shape=jax.ShapeDtypeStruct((bsz, N_SEG, ls, SSM_WIDTH), F32),
        scratch_shapes=[pltpu.VMEM((rows, SSM_WIDTH), F32),
                        pltpu.VMEM((S5_CHUNKS, rows, 2 * S5_CHUNK_STATES), F32),
                        pltpu.VMEM((S5_CHUNKS, N_SEG, 2 * S5_CHUNK_STATES), F32),
                        pltpu.VMEM((SSM_WIDTH // LANES, rows, LANES), F32)],
        compiler_params=_params("arbitrary", "arbitrary"),
        name="s5_main",
    )(u.reshape(bsz, N_SEG, ls, SSM_WIDTH), prm["bc"], prm["cre"], prm["cim"], prm["lam_chunk"], sinit,
      d_skip, w_glu, b_glu)
    return out.reshape(bsz, seq, SSM_WIDTH)


FF_CHUNK = 1024


def _outmlp_kernel(x_ref, attn_ref, ssm_ref, mod_ref, ga_ref, gs_ref, gm_ref, gf_ref,
                   wout_ref, w1_ref, w2_ref, o_ref, *, final_norm):
    x = x_ref[0]
    tm, d = x.shape
    gt1 = mod_ref[0, :, 2 * d:3 * d]
    sh2 = mod_ref[0, :, 3 * d:4 * d]
    sc2 = mod_ref[0, :, 4 * d:5 * d]
    gt2 = mod_ref[0, :, 5 * d:6 * d]
    a = _rmsnorm(attn_ref[0], ga_ref[...]).astype(BF16)
    s = _rmsnorm(ssm_ref[0], gs_ref[...]).astype(BF16)
    mixed = _dot(a, wout_ref[0:ATTN_WIDTH, :]) + _dot(s, wout_ref[ATTN_WIDTH:, :])
    x1 = x + gt1 * mixed
    h = (_rmsnorm(x1, gm_ref[...]) * (1.0 + sc2) + sh2).astype(BF16)
    ff = w1_ref.shape[1]
    acc = jnp.zeros((tm, d), F32)
    for c in range(ff // FF_CHUNK):
        t = jnp.maximum(_dot(h, w1_ref[:, c * FF_CHUNK:(c + 1) * FF_CHUNK]), 0.0)
        acc = acc + _dot((t * t).astype(BF16), w2_ref[c * FF_CHUNK:(c + 1) * FF_CHUNK, :])
    x2 = x1 + gt2 * acc
    o_ref[0] = _rmsnorm(x2, gf_ref[...]) if final_norm else x2


OUTMLP_ROWS = 512


def _outmlp(x, attn, ssm, mod3, g_attn, g_ssm, g_mlp, g_final, w_out, w_fc1, w_fc2, final_norm):
    bsz, seq, d = x.shape
    tm = OUTMLP_ROWS
    const = lambda b, t: (0, 0)
    return pl.pallas_call(
        functools.partial(_outmlp_kernel, final_norm=final_norm),
        grid=(bsz, seq // tm),
        in_specs=[pl.BlockSpec((1, tm, d), lambda b, t: (b, t, 0)),
                  pl.BlockSpec((1, tm, ATTN_WIDTH), lambda b, t: (b, t, 0)),
                  pl.BlockSpec((1, tm, SSM_WIDTH), lambda b, t: (b, t, 0)),
                  pl.BlockSpec((1, 1, mod3.shape[-1]), lambda b, t: (b, 0, 0)),
                  pl.BlockSpec(g_attn.shape, const),
                  pl.BlockSpec(g_ssm.shape, const),
                  pl.BlockSpec(g_mlp.shape, const),
                  pl.BlockSpec(g_final.shape, const),
                  pl.BlockSpec(w_out.shape, const),
                  pl.BlockSpec(w_fc1.shape, const),
                  pl.BlockSpec(w_fc2.shape, const)],
        out_specs=pl.BlockSpec((1, tm, d), lambda b, t: (b, t, 0)),
        out_shape=jax.ShapeDtypeStruct((bsz, seq, d), F32),
        compiler_params=_params("arbitrary", "arbitrary"),
        name="outmlp",
    )(x, attn, ssm, mod3, g_attn, g_ssm, g_mlp, g_final, w_out, w_fc1, w_fc2)


def _rope_tables(seq):
    half = HEAD_DIM // 2
    inv_freq = ROPE_THETA ** (-jnp.arange(half, dtype=F32) / half)
    ang = jnp.arange(seq, dtype=F32)[:, None] * inv_freq[None, :]
    cos, sin = jnp.cos(ang), jnp.sin(ang)
    reps = LANES // HEAD_DIM
    cos_t = jnp.tile(cos, (1, 2 * reps))
    sin_t = jnp.tile(jnp.concatenate([-sin, sin], axis=1), (1, reps))
    return cos_t, sin_t


def _s5_params(lam_re, lam_im, log_dt, b_re, b_im, c_re, c_im):
    lr, li = lam_re.astype(F32), lam_im.astype(F32)
    dt = jnp.exp(log_dt.astype(F32))[:, None]
    mag = jnp.exp(lr * dt)
    ar, ai = mag * jnp.cos(li * dt), mag * jnp.sin(li * dt)
    den = lr * lr + li * li
    cr = ((ar - 1.0) * lr + ai * li) / den
    ci = (ai * lr - (ar - 1.0) * li) / den
    bbar_re = cr[..., None] * b_re.astype(F32) - ci[..., None] * b_im.astype(F32)
    bbar_im = cr[..., None] * b_im.astype(F32) + ci[..., None] * b_re.astype(F32)
    eye = jnp.eye(S5_CHUNK_GROUPS, dtype=F32)

    def chunked(m):
        return m.reshape((S5_CHUNKS, S5_CHUNK_GROUPS) + m.shape[1:])

    def diag_in(m):
        return jnp.einsum('cgpn,gh->cgnhp', chunked(m), eye).reshape(S5_CHUNKS, S5_CHUNK_LANES, S5_CHUNK_STATES)

    def diag_out(m):
        return jnp.einsum('cgnp,gh->cgphn', chunked(m), eye).reshape(S5_CHUNKS, S5_CHUNK_STATES, S5_CHUNK_LANES)

    ngrp = SSM_GROUPS // S5_CARRY_GROUPS

    def lane_tiles(m):
        m = m.reshape(ngrp, S5_CARRY_GROUPS, SSM_GROUP_CH, SSM_STATE)
        return m.transpose(0, 2, 1, 3).reshape(ngrp, SSM_GROUP_CH, LANES)

    return {
        "bc": jnp.concatenate([diag_in(bbar_re), diag_in(bbar_im)], axis=2).astype(BF16),
        "cre": diag_out(c_re.astype(F32)).astype(BF16),
        "cim": diag_out(-c_im.astype(F32)).astype(BF16),
        "lam_chunk": jnp.stack([ar.reshape(S5_CHUNKS, S5_CHUNK_STATES),
                                ai.reshape(S5_CHUNKS, S5_CHUNK_STATES)], axis=1),
        "lam_carry": jnp.stack([ar.reshape(ngrp, LANES), ai.reshape(ngrp, LANES)], axis=1),
        "bt_carry": jnp.stack([lane_tiles(bbar_re.transpose(0, 2, 1)),
                               lane_tiles(bbar_im.transpose(0, 2, 1))], axis=1),
    }


def kernel(x, c, w_ada, b_ada, g_mix, w_in, g_attn_out, lam_re, lam_im, log_dt, b_re, b_im, c_re, c_im,
           d_skip, w_glu, b_glu, g_ssm_out, w_out, g_mlp, w_fc1, w_fc2, g_final):
    bsz, seq, d = x.shape
    depth = w_ada.shape[0]
    assert seq % (N_SEG * MOBA_BLOCK) == 0 and (seq // N_SEG) % S5_STEPS == 0 and bsz <= SUBLANES
    cos_t, sin_t = _rope_tables(seq)
    c_pad = jnp.zeros((SUBLANES, d), F32).at[:bsz].set(c.astype(F32))

    for l in range(depth):
        mod, win_hi, win_lo = _adaln(c_pad, w_ada[l], b_ada[l][None, :], w_in[l])
        mod3 = mod[:bsz, None, :]

        q_t, k_aug, v_t, u, u_t, wout_bf, wfc1_bf, wfc2_bf = _inproj(
            x, mod3, g_mix[l][None, :], win_hi, win_lo, cos_t, sin_t, w_out[l], w_fc1[l], w_fc2[l])
        attn = _attention(q_t, k_aug, v_t)

        prm = _s5_params(lam_re[l], lam_im[l], log_dt[l], b_re[l], b_im[l], c_re[l], c_im[l])
        ssm = _s5(u, u_t, prm, d_skip[l].reshape(1, SSM_WIDTH), w_glu[l].astype(BF16), b_glu[l][None, :])

        x = _outmlp(x, attn, ssm, mod3, g_attn_out[l][None, :], g_ssm_out[l][None, :], g_mlp[l][None, :],
                    g_final[None, :], wout_bf, wfc1_bf, wfc2_bf, final_norm=(l == depth - 1))
    return x
```

```python
import functools
import math

import jax
import jax.numpy as jnp
from jax import lax
from jax.experimental import pallas as pl
from jax.experimental.pallas import tpu as pltpu

F32 = jnp.float32
BF16 = jnp.bfloat16

HEAD_DIM = 64
ATTN_HEADS = 8
ATTN_WIDTH = ATTN_HEADS * HEAD_DIM
SSM_GROUPS = 32
SSM_GROUP_CH = 16
SSM_WIDTH = SSM_GROUPS * SSM_GROUP_CH
SSM_STATE = 64
MOBA_BLOCK = 256
MOBA_TOPK = 3
ROPE_THETA = 10000.0
EPS = 1e-6

LANES = 128
SUBLANES = 8
N_SEG = SUBLANES
HEADS_PER_TILE = LANES // HEAD_DIM
MASK_VALUE = -(2.0 ** 100)
VMEM_LIMIT_BYTES = 56 * 1024 * 1024


def _split_bf16(a):
    hi = a.astype(BF16)
    lo = (a - hi.astype(F32)).astype(BF16)
    return hi, lo


def _dot(a, b):
    return jnp.dot(a, b, preferred_element_type=F32)


def _rmsnorm(x, g):
    return x * lax.rsqrt(jnp.mean(x * x, axis=-1, keepdims=True) + EPS) * g


def _params(*semantics):
    return pltpu.CompilerParams(dimension_semantics=semantics, vmem_limit_bytes=VMEM_LIMIT_BYTES)


ADALN_STEPS = 4


def _adaln_kernel(c_ref, w_ref, b_ref, win_ref, o_ref, win_hi_ref, win_lo_ref):
    c = c_ref[...]
    s_hi, s_lo = _split_bf16(c * jax.nn.sigmoid(c))
    w_hi, w_lo = _split_bf16(w_ref[...])
    o_ref[...] = _dot(s_hi, w_hi) + _dot(s_lo, w_hi) + _dot(s_hi, w_lo) + b_ref[...]
    win_hi_ref[...], win_lo_ref[...] = _split_bf16(win_ref[...])


def _adaln(c_pad, w, b, w_in):
    rows, d = c_pad.shape
    n = w.shape[1]
    tn = n // ADALN_STEPS
    tw = w_in.shape[1] // ADALN_STEPS
    slab = pl.BlockSpec((d, tw), lambda j: (0, j))
    return pl.pallas_call(
        _adaln_kernel,
        grid=(ADALN_STEPS,),
        in_specs=[pl.BlockSpec((rows, d), lambda j: (0, 0)),
                  pl.BlockSpec((d, tn), lambda j: (0, j)),
                  pl.BlockSpec((1, tn), lambda j: (0, j)),
                  slab],
        out_specs=[pl.BlockSpec((rows, tn), lambda j: (0, j)), slab, slab],
        out_shape=[jax.ShapeDtypeStruct((rows, n), F32),
                   jax.ShapeDtypeStruct(w_in.shape, BF16),
                   jax.ShapeDtypeStruct(w_in.shape, BF16)],
        compiler_params=_params("arbitrary"),
        name="adaln",
    )(c_pad, w, b, w_in)


VT_ROWS = HEAD_DIM + 16


def _top_blocks(gate, blk, past, own):
    nb = gate.shape[0]
    g = jnp.where(past, gate, -jnp.inf)
    chosen = None
    for _ in range(MOBA_TOPK):
        best = jnp.max(g, axis=0, keepdims=True)
        first = jnp.min(jnp.where(g == best, blk, float(nb)), axis=0, keepdims=True)
        pick = blk == first
        chosen = pick if chosen is None else (chosen | pick)
        g = jnp.where(pick, -jnp.inf, g)
    return (chosen & past) | own


INPROJ_BLOCKS = 4


def _inproj_route(i, slot, qk, vu, cos_ref, sin_ref, qt_ref, ka_ref, vt_ref, u_ref, ut_ref, km_scr):
    tm = qk.shape[0]
    nb = km_scr.shape[0]
    rows = slice(slot * tm, (slot + 1) * tm)
    u = vu[:, ATTN_WIDTH:]
    u_ref[0, rows] = u
    ut_ref[0, slot] = u.T.astype(BF16)

    reps = ATTN_WIDTH // LANES
    cos = jnp.concatenate([cos_ref[rows]] * reps, axis=1)
    sin = jnp.concatenate([sin_ref[rows]] * reps, axis=1)
    wide_lane = lax.broadcasted_iota(jnp.int32, (tm, ATTN_WIDTH), 1)
    first_half = (wide_lane & (HEAD_DIM // 2)) == 0

    def rope(t):
        partner = jnp.where(first_half,
                            pltpu.roll(t, ATTN_WIDTH - HEAD_DIM // 2, 1),
                            pltpu.roll(t, HEAD_DIM // 2, 1))
        return t * cos + partner * sin

    q = rope(qk[:, :ATTN_WIDTH])
    k = rope(qk[:, ATTN_WIDTH:])
    v = vu[:, :ATTN_WIDTH]

    km_scr[pl.ds(i, 1), :] = jnp.sum(k, axis=0, keepdims=True) * (1.0 / tm)
    km_all = km_scr[...]

    lane = lax.broadcasted_iota(jnp.int32, (tm, LANES), 1)
    km_lane = lax.broadcasted_iota(jnp.int32, (nb, LANES), 1)
    blk_i = lax.broadcasted_iota(jnp.int32, (nb, tm), 0)
    blk = blk_i.astype(F32)
    past = blk_i < i
    own = blk_i == i
    ones_rows = jnp.where(lax.broadcasted_iota(jnp.int32, (VT_ROWS - HEAD_DIM, tm), 0) == 0, 1.0, 0.0)
    onehot = jnp.where(lane == HEAD_DIM + i, 1.0, 0.0)
    scale = HEAD_DIM ** -0.5 * math.log2(math.e)
    for tile in range(ATTN_WIDTH // LANES):
        cols = slice(tile * LANES, (tile + 1) * LANES)
        q_t = q[:, cols].T
        v_t = v[:, cols].T
        k_tile = k[:, cols]
        km = km_all[:, cols]
        qt_hi, qt_lo = _split_bf16(q_t)
        for hh in range(HEADS_PER_TILE):
            h_idx = tile * HEADS_PER_TILE + hh
            kmh_hi, kmh_lo = _split_bf16(jnp.where(km_lane // HEAD_DIM == hh, km, 0.0))
            gate = _dot(kmh_hi, qt_hi) + _dot(kmh_lo, qt_hi) + _dot(kmh_hi, qt_lo)
            bias_t = jnp.where(_top_blocks(gate, blk, past, own), 0.0, MASK_VALUE)
            head_rows = slice(hh * HEAD_DIM, (hh + 1) * HEAD_DIM)
            qt_ref[0, h_idx, slot] = jnp.concatenate(
                [q_t[head_rows] * scale, bias_t, jnp.zeros((LANES - HEAD_DIM - nb, tm), F32)],
                axis=0).astype(BF16)
            vt_ref[0, h_idx, slot] = jnp.concatenate([v_t[head_rows], ones_rows], axis=0).astype(BF16)
            kh = jnp.where(lane // HEAD_DIM == hh, k_tile, 0.0)
            if hh:
                kh = pltpu.roll(kh, LANES - hh * HEAD_DIM, 1)
            ka_ref[0, h_idx, rows] = jnp.where(lane < HEAD_DIM, kh, onehot).astype(BF16)


def _inproj_kernel(x_ref, mod_ref, g_ref, wqk_hi_ref, wqk_lo_ref, wvu_ref, cos_ref, sin_ref,
                   wout_ref, wfc1_ref, wfc2_ref,
                   qt_ref, ka_ref, vt_ref, u_ref, ut_ref, wout_bf_ref, wfc1_bf_ref, wfc2_bf_ref, km_scr):
    step = pl.program_id(1)

    @pl.when(step == 0)
    def _():
        km_scr[...] = jnp.zeros_like(km_scr)

    wout_bf_ref[...] = wout_ref[...].astype(BF16)
    wfc1_bf_ref[...] = wfc1_ref[...].astype(BF16)
    wfc2_bf_ref[...] = wfc2_ref[...].astype(BF16)
    d = x_ref.shape[2]
    tm = x_ref.shape[1] // INPROJ_BLOCKS
    sh1 = mod_ref[0, :, 0:d]
    sc1 = mod_ref[0, :, d:2 * d]
    hs = []
    for slot in range(INPROJ_BLOCKS):
        x = x_ref[0, slot * tm:(slot + 1) * tm, :]
        hs.append(_split_bf16(_rmsnorm(x, g_ref[...]) * (1.0 + sc1) + sh1))
    prj = []
    for h_hi, h_lo in hs:
        qk = (_dot(h_hi, wqk_hi_ref[...]) + _dot(h_lo, wqk_hi_ref[...])
              + _dot(h_hi, wqk_lo_ref[...]))
        prj.append((qk, _dot(h_hi, wvu_ref[...])))

    for slot, (qk, vu) in enumerate(prj):
        _inproj_route(step * INPROJ_BLOCKS + slot, slot, qk, vu, cos_ref, sin_ref,
                      qt_ref, ka_ref, vt_ref, u_ref, ut_ref, km_scr)


def _inproj(x, mod3, g_mix, win_hi, win_lo, cos_t, sin_t, w_out, w_fc1, w_fc2):
    bsz, seq, d = x.shape
    tm = MOBA_BLOCK
    nb = seq // tm
    assert nb % SUBLANES == 0 and nb <= LANES - HEAD_DIM and nb % INPROJ_BLOCKS == 0
    nblk = INPROJ_BLOCKS
    rows = nblk * tm
    const = lambda b, t: (0, 0)
    wqk = 2 * ATTN_WIDTH
    assert win_hi.shape[1] == 2 * wqk
    per_batch = nb // nblk
    steps = bsz * per_batch

    def slab(w):
        n = w.shape[0] // steps
        assert n * steps == w.shape[0] and n % (2 * SUBLANES) == 0
        return pl.BlockSpec((n, w.shape[1]), lambda b, t: (b * per_batch + t, 0))

    casts = [w_out, w_fc1, w_fc2]
    return pl.pallas_call(
        _inproj_kernel,
        grid=(bsz, per_batch),
        in_specs=[pl.BlockSpec((1, rows, d), lambda b, t: (b, t, 0)),
                  pl.BlockSpec((1, 1, mod3.shape[-1]), lambda b, t: (b, 0, 0)),
                  pl.BlockSpec((1, d), const),
                  pl.BlockSpec((d, wqk), const),
                  pl.BlockSpec((d, wqk), const),
                  pl.BlockSpec((d, wqk), lambda b, t: (0, 1)),
                  pl.BlockSpec((rows, LANES), lambda b, t: (t, 0)),
                  pl.BlockSpec((rows, LANES), lambda b, t: (t, 0))] + [slab(w) for w in casts],
        out_specs=[pl.BlockSpec((1, ATTN_HEADS, nblk, LANES, tm), lambda b, t: (b, 0, t, 0, 0)),
                   pl.BlockSpec((1, ATTN_HEADS, rows, LANES), lambda b, t: (b, 0, t, 0)),
                   pl.BlockSpec((1, ATTN_HEADS, nblk, VT_ROWS, tm), lambda b, t: (b, 0, t, 0, 0)),
                   pl.BlockSpec((1, rows, SSM_WIDTH), lambda b, t: (b, t, 0)),
                   pl.BlockSpec((1, nblk, SSM_WIDTH, tm), lambda b, t: (b, t, 0, 0))] + [slab(w) for w in casts],
        out_shape=[jax.ShapeDtypeStruct((bsz, ATTN_HEADS, nb, LANES, tm), BF16),
                   jax.ShapeDtypeStruct((bsz, ATTN_HEADS, seq, LANES), BF16),
                   jax.ShapeDtypeStruct((bsz, ATTN_HEADS, nb, VT_ROWS, tm), BF16),
                   jax.ShapeDtypeStruct((bsz, seq, SSM_WIDTH), F32),
                   jax.ShapeDtypeStruct((bsz, nb, SSM_WIDTH, tm), BF16)]
                  + [jax.ShapeDtypeStruct(w.shape, BF16) for w in casts],
        scratch_shapes=[pltpu.VMEM((nb, ATTN_WIDTH), F32)],
        compiler_params=_params("arbitrary", "arbitrary"),
        name="inproj",
    )(x, mod3, g_mix, win_hi, win_lo, win_hi, cos_t, sin_t, *casts)


ATTN_QBLOCKS = 2
RUNNING_MAX_INIT = -3.0e38


def _attn_kernel(qt_ref, ka_ref, vt_ref, o_ref, sa_scr, sb_scr, m_scr, acc_scr):
    pair = pl.program_id(1)
    i = ATTN_QBLOCKS * pair
    nh = qt_ref.shape[1]
    tq = qt_ref.shape[4]
    key = lax.broadcasted_iota(jnp.int32, (tq, tq), 0)
    qry = lax.broadcasted_iota(jnp.int32, (tq, tq), 1)
    causal = key <= qry

    def scores(qb, h, j, s_scr, mask=None):
        start = pl.multiple_of(j * tq, tq)
        s = _dot(ka_ref[0, h, pl.ds(start, tq), :], qt_ref[0, h, qb])
        if mask is not None:
            s = jnp.where(mask, s, MASK_VALUE)
        s_scr[qb * nh + h] = s

    def consume(qb, h, j, s_scr):
        c = qb * nh + h
        s = s_scr[c]
        m_old = m_scr[c]
        m_new = jnp.maximum(m_old, jnp.max(s, axis=0, keepdims=True))
        alpha = jnp.exp2(m_old - m_new)
        p = jnp.exp2(s - m_new).astype(BF16)
        acc_scr[c] = alpha * acc_scr[c] + _dot(vt_ref[0, h, j], p)
        m_scr[c] = m_new

    m_scr[...] = jnp.full(m_scr.shape, RUNNING_MAX_INIT, F32)
    acc_scr[...] = jnp.zeros(acc_scr.shape, F32)
    for h in range(nh):
        scores(0, h, i, sa_scr, causal)
        scores(1, h, i + 1, sa_scr, causal)
    for h in range(nh):
        scores(1, h, i, sb_scr)
        consume(1, h, i + 1, sa_scr)

    def two_blocks(j0):
        j1 = j0 + 1
        prev = jnp.where(j0 == 0, i, j0 - 1)
        for h in range(nh):
            scores(0, h, j0, sb_scr)
            consume(0, h, prev, sa_scr)
            scores(1, h, j0, sa_scr)
            consume(1, h, prev, sb_scr)
        for h in range(nh):
            scores(0, h, j1, sa_scr)
            consume(0, h, j0, sb_scr)
            scores(1, h, j1, sb_scr)
            consume(1, h, j0, sa_scr)

    def body(t, carry):
        for k in range(0, 8, 2):
            two_blocks(8 * t + k)
        return carry

    lax.fori_loop(0, pair // 4, body, 0)

    @pl.when(pair % 4 >= 2)
    def _():
        two_blocks(8 * (pair // 4))
        two_blocks(8 * (pair // 4) + 2)

    @pl.when(pair % 2 == 1)
    def _():
        two_blocks(i - 2)

    last = jnp.where(pair == 0, i, i - 1)
    for h in range(nh):
        consume(0, h, last, sa_scr)
        consume(1, h, last, sb_scr)

    for qb in range(ATTN_QBLOCKS):
        for vt in range(nh // HEADS_PER_TILE):
            rows = []
            for hh in range(HEADS_PER_TILE):
                acc = acc_scr[qb * nh + vt * HEADS_PER_TILE + hh]
                rows.append(acc[0:HEAD_DIM] / acc[HEAD_DIM:HEAD_DIM + 1])
            o_ref[0, qb * tq:(qb + 1) * tq, vt * LANES:(vt + 1) * LANES] = jnp.concatenate(rows, axis=0).T


def _attention(q_t, k_aug, v_t):
    bsz, nh, nb, _, tq = q_t.shape
    seq = nb * tq
    assert nb % ATTN_QBLOCKS == 0
    chains = ATTN_QBLOCKS * nh
    resident = pl.Buffered(1)
    return pl.pallas_call(
        _attn_kernel,
        grid=(bsz, nb // ATTN_QBLOCKS),
        in_specs=[pl.BlockSpec((1, nh, ATTN_QBLOCKS, LANES, tq), lambda b, g: (b, 0, g, 0, 0)),
                  pl.BlockSpec((1, nh, seq, LANES), lambda b, g: (b, 0, 0, 0), pipeline_mode=resident),
                  pl.BlockSpec((1, nh, nb, VT_ROWS, tq), lambda b, g: (b, 0, 0, 0, 0), pipeline_mode=resident)],
        out_specs=pl.BlockSpec((1, ATTN_QBLOCKS * tq, nh * HEAD_DIM), lambda b, g: (b, g, 0)),
        out_shape=jax.ShapeDtypeStruct((bsz, seq, nh * HEAD_DIM), F32),
        scratch_shapes=[pltpu.VMEM((chains, tq, tq), F32), pltpu.VMEM((chains, tq, tq), F32),
                        pltpu.VMEM((chains, 1, tq), F32), pltpu.VMEM((chains, VT_ROWS, tq), F32)],
        compiler_params=_params("arbitrary", "arbitrary"),
        name="attn",
    )(q_t, k_aug, v_t)


S5_STEPS = 128
S5_SUB_STEPS = 64
S5_CHUNKS = 4
S5_CHUNK_GROUPS = SSM_GROUPS // S5_CHUNKS
S5_CHUNK_STATES = S5_CHUNK_GROUPS * SSM_STATE
S5_CHUNK_LANES = S5_CHUNK_GROUPS * SSM_GROUP_CH


S5_CARRY_GROUPS = LANES // SSM_STATE


def _cmul(ar, ai, br, bi):
    return ar * br - ai * bi, ar * bi + ai * br


def _s5_carry_kernel(ut_ref, lam_ref, bt_ref, o_ref):
    bsz, nblk, nrows, tm = ut_ref.shape
    nch = nrows // S5_CARRY_GROUPS
    seg_blocks = nblk // N_SEG
    ls = seg_blocks * tm
    reps = bsz * N_SEG
    sub = lax.broadcasted_iota(jnp.int32, (SUBLANES, LANES), 0)
    lr = jnp.broadcast_to(lam_ref[0, 0:1], (SUBLANES, LANES))
    li = jnp.broadcast_to(lam_ref[0, 1:2], (SUBLANES, LANES))
    pr = jnp.ones((SUBLANES, LANES), F32)
    pi = jnp.zeros((SUBLANES, LANES), F32)
    tr = jnp.zeros((SUBLANES, LANES), F32)
    ti = jnp.zeros((SUBLANES, LANES), F32)
    for r in range(SUBLANES):
        tr = jnp.where(sub == SUBLANES - 1 - r, pr, tr)
        ti = jnp.where(sub == SUBLANES - 1 - r, pi, ti)
        pr, pi = _cmul(pr, pi, lr, li)
    span = SUBLANES
    while span < ls:
        nr, ni = _cmul(tr, ti, pr[0:1], pi[0:1])
        tr = jnp.concatenate([nr, tr], axis=0)
        ti = jnp.concatenate([ni, ti], axis=0)
        pr, pi = _cmul(pr, pi, pr, pi)
        span *= 2
    def seg_rows(g, b, s):
        return jnp.concatenate([ut_ref[b, s * seg_blocks + k, g * nch:(g + 1) * nch, :]
                                for k in range(seg_blocks)], axis=1)

    lhs = jnp.concatenate([seg_rows(g, b, s)
                           for g in range(S5_CARRY_GROUPS) for b in range(bsz) for s in range(N_SEG)],
                          axis=0)
    gr = _dot(lhs, tr.astype(BF16))
    gi = _dot(lhs, ti.astype(BF16))
    btr = jnp.concatenate([bt_ref[0, 0]] * (reps * S5_CARRY_GROUPS), axis=0)
    bti = jnp.concatenate([bt_ref[0, 1]] * (reps * S5_CARRY_GROUPS), axis=0)
    wr, wi = _cmul(gr, gi, btr, bti)
    fr = wr.reshape(reps * S5_CARRY_GROUPS, nch, LANES).sum(axis=1)
    fi = wi.reshape(reps * S5_CARRY_GROUPS, nch, LANES).sum(axis=1)
    grp = lax.broadcasted_iota(jnp.int32, (reps, LANES), 1) // SSM_STATE
    fin_r, fin_i = fr[0:reps], fi[0:reps]
    for g in range(1, S5_CARRY_GROUPS):
        fin_r = jnp.where(grp == g, fr[g * reps:(g + 1) * reps], fin_r)
        fin_i = jnp.where(grp == g, fi[g * reps:(g + 1) * reps], fin_i)
    rows_r, rows_i = [], []
    for b in range(bsz):
        sr = jnp.zeros((1, LANES), F32)
        si = jnp.zeros((1, LANES), F32)
        rows_r.append(sr)
        rows_i.append(si)
        for seg in range(1, N_SEG):
            r = b * N_SEG + seg - 1
            mr, mi = _cmul(sr, si, pr[0:1], pi[0:1])
            sr, si = mr + fin_r[r:r + 1], mi + fin_i[r:r + 1]
            rows_r.append(sr)
            rows_i.append(si)
    o_ref[0, 0] = jnp.concatenate(rows_r, axis=0)
    o_ref[0, 1] = jnp.concatenate(rows_i, axis=0)


def _s5_main_kernel(u_ref, bc_ref, cre_ref, cim_ref, lam_ref, sinit_ref, d_ref, wglu_ref, bglu_ref, o_ref,
                    il_scr, bu_scr, st_scr, y_scr):
    jb = pl.program_id(1)
    nseg, steps, width = u_ref.shape
    cs = S5_CHUNK_STATES
    nsub = steps // S5_SUB_STEPS
    srows = S5_SUB_STEPS * nseg

    @pl.when(jb == 0)
    def _():
        st_scr[...] = sinit_ref[...]

    for j in range(steps):
        il_scr[j * nseg:(j + 1) * nseg, :] = u_ref[:, j, :]
    us = [il_scr[sb * srows:(sb + 1) * srows, :] for sb in range(nsub)]
    for sb in range(nsub):
        u_bf = us[sb].astype(BF16)
        for c in range(S5_CHUNKS):
            bu_scr[c, sb * srows:(sb + 1) * srows, :] = _dot(
                u_bf[:, c * S5_CHUNK_LANES:(c + 1) * S5_CHUNK_LANES], bc_ref[c])
    state = [(st_scr[c, :, 0:cs], st_scr[c, :, cs:2 * cs]) for c in range(S5_CHUNKS)]
    for sb in range(nsub):
        sub_rows = slice(sb * srows, (sb + 1) * srows)
        ys = []
        for c in range(S5_CHUNKS):
            lr = jnp.broadcast_to(lam_ref[c, 0:1, :], (nseg, cs))
            li = jnp.broadcast_to(lam_ref[c, 1:2, :], (nseg, cs))
            xr, xi = state[c]
            for j in range(sb * S5_SUB_STEPS, (sb + 1) * S5_SUB_STEPS):
                rows = slice(j * nseg, (j + 1) * nseg)
                xr, xi = (lr * xr - li * xi + bu_scr[c, rows, 0:cs],
                          lr * xi + li * xr + bu_scr[c, rows, cs:2 * cs])
                bu_scr[c, rows, 0:cs] = xr
                bu_scr[c, rows, cs:2 * cs] = xi
            state[c] = (xr, xi)
            ys.append(_dot(bu_scr[c, sub_rows, 0:cs].astype(BF16), cre_ref[c])
                      + _dot(bu_scr[c, sub_rows, cs:2 * cs].astype(BF16), cim_ref[c]))
        y = jnp.concatenate(ys, axis=1) + d_ref[...] * us[sb]
        y = y * (0.5 * (1.0 + jnp.tanh(math.sqrt(2.0 / math.pi) * (y + 0.044715 * (y * y * y)))))
        z = _dot(y.astype(BF16), wglu_ref[...]) + bglu_ref[...]
        y = y * jax.nn.sigmoid(z)
        for t in range(width // LANES):
            y_scr[t, sub_rows, :] = y[:, t * LANES:(t + 1) * LANES]
    for c in range(S5_CHUNKS):
        st_scr[c, :, 0:cs] = state[c][0]
        st_scr[c, :, cs:2 * cs] = state[c][1]
    for s in range(nseg):
        for t in range(width // LANES):
            o_ref[s, :, t * LANES:(t + 1) * LANES] = y_scr[t, pl.ds(s, steps, stride=nseg), :]


def _s5(u, u_t, prm, d_skip, w_glu, b_glu):
    bsz, seq, _ = u.shape
    ls = seq // N_SEG
    steps = S5_STEPS
    rows = steps * N_SEG
    ngrp = SSM_GROUPS // S5_CARRY_GROUPS
    carry = pl.pallas_call(
        _s5_carry_kernel,
        grid=(ngrp,),
        in_specs=[pl.BlockSpec((bsz, u_t.shape[1], S5_CARRY_GROUPS * SSM_GROUP_CH, u_t.shape[3]),
                               lambda g: (0, 0, g, 0)),
                  pl.BlockSpec((1, 2, LANES), lambda g: (g, 0, 0)),
                  pl.BlockSpec((1, 2, SSM_GROUP_CH, LANES), lambda g: (g, 0, 0, 0))],
        out_specs=pl.BlockSpec((1, 2, bsz * N_SEG, LANES), lambda g: (g, 0, 0, 0)),
        out_shape=jax.ShapeDtypeStruct((ngrp, 2, bsz * N_SEG, LANES), F32),
        compiler_params=_params("arbitrary"),
        name="s5_carry",
    )(u_t, prm["lam_carry"], prm["bt_carry"])
    sinit = carry.reshape(S5_CHUNKS, ngrp // S5_CHUNKS, 2, bsz, N_SEG, LANES)
    sinit = sinit.transpose(3, 0, 4, 2, 1, 5).reshape(bsz, S5_CHUNKS, N_SEG, 2 * S5_CHUNK_STATES)

    def const(shape):
        return pl.BlockSpec(shape, lambda b, j: (0,) * len(shape))

    blk = pl.BlockSpec((None, N_SEG, steps, SSM_WIDTH), lambda b, j: (b, 0, j, 0))
    out = pl.pallas_call(
        _s5_main_kernel,
        grid=(bsz, ls // steps),
        in_specs=[blk, const(prm["bc"].shape), const(prm["cre"].shape), const(prm["cim"].shape),
                  const(prm["lam_chunk"].shape),
                  pl.BlockSpec((None, S5_CHUNKS, N_SEG, 2 * S5_CHUNK_STATES), lambda b, j: (b, 0, 0, 0)),
                  const(d_skip.shape), const(w_glu.shape), const(b_glu.shape)],
        out_specs=blk,
        out_shape=jax.ShapeDtypeStruct((bsz, N_SEG, ls, SSM_WIDTH), F32),
        scratch_shapes=[pltpu.VMEM((rows, SSM_WIDTH), F32),
                        pltpu.VMEM((S5_CHUNKS, rows, 2 * S5_CHUNK_STATES), F32),
                        pltpu.VMEM((S5_CHUNKS, N_SEG, 2 * S5_CHUNK_STATES), F32),
                        pltpu.VMEM((SSM_WIDTH // LANES, rows, LANES), F32)],
        compiler_params=_params("arbitrary", "arbitrary"),
        name="s5_main",
    )(u.reshape(bsz, N_SEG, ls, SSM_WIDTH), prm["bc"], prm["cre"], prm["cim"], prm["lam_chunk"], sinit,
      d_skip, w_glu, b_glu)
    return out.reshape(bsz, seq, SSM_WIDTH)


FF_CHUNK = 1024


def _outmlp_kernel(x_ref, attn_ref, ssm_ref, mod_ref, ga_ref, gs_ref, gm_ref, gf_ref,
                   wout_ref, w1_ref, w2_ref, o_ref, *, final_norm):
    x = x_ref[0]
    tm, d = x.shape
    gt1 = mod_ref[0, :, 2 * d:3 * d]
    sh2 = mod_ref[0, :, 3 * d:4 * d]
    sc2 = mod_ref[0, :, 4 * d:5 * d]
    gt2 = mod_ref[0, :, 5 * d:6 * d]
    a = _rmsnorm(attn_ref[0], ga_ref[...]).astype(BF16)
    s = _rmsnorm(ssm_ref[0], gs_ref[...]).astype(BF16)
    mixed = _dot(a, wout_ref[0:ATTN_WIDTH, :]) + _dot(s, wout_ref[ATTN_WIDTH:, :])
    x1 = x + gt1 * mixed
    h = (_rmsnorm(x1, gm_ref[...]) * (1.0 + sc2) + sh2).astype(BF16)
    ff = w1_ref.shape[1]
    acc = jnp.zeros((tm, d), F32)
    for c in range(ff // FF_CHUNK):
        t = jnp.maximum(_dot(h, w1_ref[:, c * FF_CHUNK:(c + 1) * FF_CHUNK]), 0.0)
        acc = acc + _dot((t * t).astype(BF16), w2_ref[c * FF_CHUNK:(c + 1) * FF_CHUNK, :])
    x2 = x1 + gt2 * acc
    o_ref[0] = _rmsnorm(x2, gf_ref[...]) if final_norm else x2


OUTMLP_ROWS = 512


def _outmlp(x, attn, ssm, mod3, g_attn, g_ssm, g_mlp, g_final, w_out, w_fc1, w_fc2, final_norm):
    bsz, seq, d = x.shape
    tm = OUTMLP_ROWS
    const = lambda b, t: (0, 0)
    return pl.pallas_call(
        functools.partial(_outmlp_kernel, final_norm=final_norm),
        grid=(bsz, seq // tm),
        in_specs=[pl.BlockSpec((1, tm, d), lambda b, t: (b, t, 0)),
                  pl.BlockSpec((1, tm, ATTN_WIDTH), lambda b, t: (b, t, 0)),
                  pl.BlockSpec((1, tm, SSM_WIDTH), lambda b, t: (b, t, 0)),
                  pl.BlockSpec((1, 1, mod3.shape[-1]), lambda b, t: (b, 0, 0)),
                  pl.BlockSpec(g_attn.shape, const),
                  pl.BlockSpec(g_ssm.shape, const),
                  pl.BlockSpec(g_mlp.shape, const),
                  pl.BlockSpec(g_final.shape, const),
                  pl.BlockSpec(w_out.shape, const),
                  pl.BlockSpec(w_fc1.shape, const),
                  pl.BlockSpec(w_fc2.shape, const)],
        out_specs=pl.BlockSpec((1, tm, d), lambda b, t: (b, t, 0)),
        out_shape=jax.ShapeDtypeStruct((bsz, seq, d), F32),
        compiler_params=_params("arbitrary", "arbitrary"),
        name="outmlp",
    )(x, attn, ssm, mod3, g_attn, g_ssm, g_mlp, g_final, w_out, w_fc1, w_fc2)


def _rope_tables(seq):
    half = HEAD_DIM // 2
    inv_freq = ROPE_THETA ** (-jnp.arange(half, dtype=F32) / half)
    ang = jnp.arange(seq, dtype=F32)[:, None] * inv_freq[None, :]
    cos, sin = jnp.cos(ang), jnp.sin(ang)
    reps = LANES // HEAD_DIM
    cos_t = jnp.tile(cos, (1, 2 * reps))
    sin_t = jnp.tile(jnp.concatenate([-sin, sin], axis=1), (1, reps))
    return cos_t, sin_t


def _s5_params(lam_re, lam_im, log_dt, b_re, b_im, c_re, c_im):
    lr, li = lam_re.astype(F32), lam_im.astype(F32)
    dt = jnp.exp(log_dt.astype(F32))[:, None]
    mag = jnp.exp(lr * dt)
    ar, ai = mag * jnp.cos(li * dt), mag * jnp.sin(li * dt)
    den = lr * lr + li * li
    cr = ((ar - 1.0) * lr + ai * li) / den
    ci = (ai * lr - (ar - 1.0) * li) / den
    bbar_re = cr[..., None] * b_re.astype(F32) - ci[..., None] * b_im.astype(F32)
    bbar_im = cr[..., None] * b_im.astype(F32) + ci[..., None] * b_re.astype(F32)
    eye = jnp.eye(S5_CHUNK_GROUPS, dtype=F32)

    def chunked(m):
        return m.reshape((S5_CHUNKS, S5_CHUNK_GROUPS) + m.shape[1:])

    def diag_in(m):
        return jnp.einsum('cgpn,gh->cgnhp', chunked(m), eye).reshape(S5_CHUNKS, S5_CHUNK_LANES, S5_CHUNK_STATES)

    def diag_out(m):
        return jnp.einsum('cgnp,gh->cgphn', chunked(m), eye).reshape(S5_CHUNKS, S5_CHUNK_STATES, S5_CHUNK_LANES)

    ngrp = SSM_GROUPS // S5_CARRY_GROUPS

    def lane_tiles(m):
        m = m.reshape(ngrp, S5_CARRY_GROUPS, SSM_GROUP_CH, SSM_STATE)
        return m.transpose(0, 2, 1, 3).reshape(ngrp, SSM_GROUP_CH, LANES)

    return {
        "bc": jnp.concatenate([diag_in(bbar_re), diag_in(bbar_im)], axis=2).astype(BF16),
        "cre": diag_out(c_re.astype(F32)).astype(BF16),
        "cim": diag_out(-c_im.astype(F32)).astype(BF16),
        "lam_chunk": jnp.stack([ar.reshape(S5_CHUNKS, S5_CHUNK_STATES),
                                ai.reshape(S5_CHUNKS, S5_CHUNK_STATES)], axis=1),
        "lam_carry": jnp.stack([ar.reshape(ngrp, LANES), ai.reshape(ngrp, LANES)], axis=1),
        "bt_carry": jnp.stack([lane_tiles(bbar_re.transpose(0, 2, 1)),
                               lane_tiles(bbar_im.transpose(0, 2, 1))], axis=1),
    }


def kernel(x, c, w_ada, b_ada, g_mix, w_in, g_attn_out, lam_re, lam_im, log_dt, b_re, b_im, c_re, c_im,
           d_skip, w_glu, b_glu, g_ssm_out, w_out, g_mlp, w_fc1, w_fc2, g_final):
    bsz, seq, d = x.shape
    depth = w_ada.shape[0]
    assert seq % (N_SEG * MOBA_BLOCK) == 0 and (seq // N_SEG) % S5_STEPS == 0 and bsz <= SUBLANES
    cos_t, sin_t = _rope_tables(seq)
    c_pad = jnp.zeros((SUBLANES, d), F32).at[:bsz].set(c.astype(F32))

    for l in range(depth):
        mod, win_hi, win_lo = _adaln(c_pad, w_ada[l], b_ada[l][None, :], w_in[l])
        mod3 = mod[:bsz, None, :]

        q_t, k_aug, v_t, u, u_t, wout_bf, wfc1_bf, wfc2_bf = _inproj(
            x, mod3, g_mix[l][None, :], win_hi, win_lo, cos_t, sin_t, w_out[l], w_fc1[l], w_fc2[l])
        attn = _attention(q_t, k_aug, v_t)

        prm = _s5_params(lam_re[l], lam_im[l], log_dt[l], b_re[l], b_im[l], c_re[l], c_im[l])
        ssm = _s5(u, u_t, prm, d_skip[l].reshape(1, SSM_WIDTH), w_glu[l].astype(BF16), b_glu[l][None, :])

        x = _outmlp(x, attn, ssm, mod3, g_attn_out[l][None, :], g_ssm_out[l][None, :], g_mlp[l][None, :],
                    g_final[None, :], wout_bf, wfc1_bf, wfc2_bf, final_norm=(l == depth - 1))
    return x
```

```python
import functools
import math

import jax
import jax.numpy as jnp
from jax import lax
from jax.experimental import pallas as pl
from jax.experimental.pallas import tpu as pltpu

F32 = jnp.float32
BF16 = jnp.bfloat16

HEAD_DIM = 64
ATTN_HEADS = 8
ATTN_WIDTH = ATTN_HEADS * HEAD_DIM
SSM_GROUPS = 32
SSM_GROUP_CH = 16
SSM_WIDTH = SSM_GROUPS * SSM_GROUP_CH
SSM_STATE = 64
MOBA_BLOCK = 256
MOBA_TOPK = 3
ROPE_THETA = 10000.0
EPS = 1e-6

LANES = 128
SUBLANES = 8
N_SEG = SUBLANES
HEADS_PER_TILE = LANES // HEAD_DIM
MASK_VALUE = -(2.0 ** 100)
VMEM_LIMIT_BYTES = 56 * 1024 * 1024


def _split_bf16(a):
    hi = a.astype(BF16)
    lo = (a - hi.astype(F32)).astype(BF16)
    return hi, lo


def _dot(a, b):
    return jnp.dot(a, b, preferred_element_type=F32)


def _rmsnorm(x, g):
    return x * lax.rsqrt(jnp.mean(x * x, axis=-1, keepdims=True) + EPS) * g


def _params(*semantics):
    return pltpu.CompilerParams(dimension_semantics=semantics, vmem_limit_bytes=VMEM_LIMIT_BYTES)


ADALN_STEPS = 4


def _adaln_kernel(c_ref, w_ref, b_ref, win_ref, o_ref, win_hi_ref, win_lo_ref):
    c = c_ref[...]
    s_hi, s_lo = _split_bf16(c * jax.nn.sigmoid(c))
    w_hi, w_lo = _split_bf16(w_ref[...])
    o_ref[...] = _dot(s_hi, w_hi) + _dot(s_lo, w_hi) + _dot(s_hi, w_lo) + b_ref[...]
    win_hi_ref[...], win_lo_ref[...] = _split_bf16(win_ref[...])


def _adaln(c_pad, w, b, w_in):
    rows, d = c_pad.shape
    n = w.shape[1]
    tn = n // ADALN_STEPS
    tw = w_in.shape[1] // ADALN_STEPS
    slab = pl.BlockSpec((d, tw), lambda j: (0, j))
    return pl.pallas_call(
        _adaln_kernel,
        grid=(ADALN_STEPS,),
        in_specs=[pl.BlockSpec((rows, d), lambda j: (0, 0)),
                  pl.BlockSpec((d, tn), lambda j: (0, j)),
                  pl.BlockSpec((1, tn), lambda j: (0, j)),
                  slab],
        out_specs=[pl.BlockSpec((rows, tn), lambda j: (0, j)), slab, slab],
        out_shape=[jax.ShapeDtypeStruct((rows, n), F32),
                   jax.ShapeDtypeStruct(w_in.shape, BF16),
                   jax.ShapeDtypeStruct(w_in.shape, BF16)],
        compiler_params=_params("arbitrary"),
        name="adaln",
    )(c_pad, w, b, w_in)


VT_ROWS = HEAD_DIM + 16


def _top_blocks(gate, blk, past, own):
    nb = gate.shape[0]
    g = jnp.where(past, gate, -jnp.inf)
    chosen = None
    for _ in range(MOBA_TOPK):
        best = jnp.max(g, axis=0, keepdims=True)
        first = jnp.min(jnp.where(g == best, blk, float(nb)), axis=0, keepdims=True)
        pick = blk == first
        chosen = pick if chosen is None else (chosen | pick)
        g = jnp.where(pick, -jnp.inf, g)
    return (chosen & past) | own


INPROJ_BLOCKS = 4


def _inproj_route(i, slot, qk, vu, cos_ref, sin_ref, qt_ref, ka_ref, vt_ref, u_ref, ut_ref, km_scr):
    tm = qk.shape[0]
    nb = km_scr.shape[0]
    rows = slice(slot * tm, (slot + 1) * tm)
    u = vu[:, ATTN_WIDTH:]
    u_ref[0, rows] = u
    ut_ref[0, slot] = u.T.astype(BF16)

    reps = ATTN_WIDTH // LANES
    cos = jnp.concatenate([cos_ref[rows]] * reps, axis=1)
    sin = jnp.concatenate([sin_ref[rows]] * reps, axis=1)
    wide_lane = lax.broadcasted_iota(jnp.int32, (tm, ATTN_WIDTH), 1)
    first_half = (wide_lane & (HEAD_DIM // 2)) == 0

    def rope(t):
        partner = jnp.where(first_half,
                            pltpu.roll(t, ATTN_WIDTH - HEAD_DIM // 2, 1),
                            pltpu.roll(t, HEAD_DIM // 2, 1))
        return t * cos + partner * sin

    q = rope(qk[:, :ATTN_WIDTH])
    k = rope(qk[:, ATTN_WIDTH:])
    v = vu[:, :ATTN_WIDTH]

    km_scr[pl.ds(i, 1), :] = jnp.sum(k, axis=0, keepdims=True) * (1.0 / tm)
    km_all = km_scr[...]

    lane = lax.broadcasted_iota(jnp.int32, (tm, LANES), 1)
    km_lane = lax.broadcasted_iota(jnp.int32, (nb, LANES), 1)
    blk_i = lax.broadcasted_iota(jnp.int32, (nb, tm), 0)
    blk = blk_i.astype(F32)
    past = blk_i < i
    own = blk_i == i
    ones_rows = jnp.where(lax.broadcasted_iota(jnp.int32, (VT_ROWS - HEAD_DIM, tm), 0) == 0, 1.0, 0.0)
    onehot = jnp.where(lane == HEAD_DIM + i, 1.0, 0.0)
    scale = HEAD_DIM ** -0.5 * math.log2(math.e)
    for tile in range(ATTN_WIDTH // LANES):
        cols = slice(tile * LANES, (tile + 1) * LANES)
        q_t = q[:, cols].T
        v_t = v[:, cols].T
        k_tile = k[:, cols]
        km = km_all[:, cols]
        qt_hi, qt_lo = _split_bf16(q_t)
        for hh in range(HEADS_PER_TILE):
            h_idx = tile * HEADS_PER_TILE + hh
            kmh_hi, kmh_lo = _split_bf16(jnp.where(km_lane // HEAD_DIM == hh, km, 0.0))
            gate = _dot(kmh_hi, qt_hi) + _dot(kmh_lo, qt_hi) + _dot(kmh_hi, qt_lo)
            bias_t = jnp.where(_top_blocks(gate, blk, past, own), 0.0, MASK_VALUE)
            head_rows = slice(hh * HEAD_DIM, (hh + 1) * HEAD_DIM)
            qt_ref[0, h_idx, slot] = jnp.concatenate(
                [q_t[head_rows] * scale, bias_t, jnp.zeros((LANES - HEAD_DIM - nb, tm), F32)],
                axis=0).astype(BF16)
            vt_ref[0, h_idx, slot] = jnp.concatenate([v_t[head_rows], ones_rows], axis=0).astype(BF16)
            kh = jnp.where(lane // HEAD_DIM == hh, k_tile, 0.0)
            if hh:
                kh = pltpu.roll(kh, LANES - hh * HEAD_DIM, 1)
            ka_ref[0, h_idx, rows] = jnp.where(lane < HEAD_DIM, kh, onehot).astype(BF16)


def _inproj_kernel(x_ref, mod_ref, g_ref, wqk_hi_ref, wqk_lo_ref, wvu_ref, cos_ref, sin_ref,
                   wout_ref, wfc1_ref, wfc2_ref,
                   qt_ref, ka_ref, vt_ref, u_ref, ut_ref, wout_bf_ref, wfc1_bf_ref, wfc2_bf_ref, km_scr):
    step = pl.program_id(1)

    @pl.when(step == 0)
    def _():
        km_scr[...] = jnp.zeros_like(km_scr)

    wout_bf_ref[...] = wout_ref[...].astype(BF16)
    wfc1_bf_ref[...] = wfc1_ref[...].astype(BF16)
    wfc2_bf_ref[...] = wfc2_ref[...].astype(BF16)
    d = x_ref.shape[2]
    tm = x_ref.shape[1] // INPROJ_BLOCKS
    sh1 = mod_ref[0, :, 0:d]
    sc1 = mod_ref[0, :, d:2 * d]
    hs = []
    for slot in range(INPROJ_BLOCKS):
        x = x_ref[0, slot * tm:(slot + 1) * tm, :]
        hs.append(_split_bf16(_rmsnorm(x, g_ref[...]) * (1.0 + sc1) + sh1))
    prj = []
    for h_hi, h_lo in hs:
        qk = (_dot(h_hi, wqk_hi_ref[...]) + _dot(h_lo, wqk_hi_ref[...])
              + _dot(h_hi, wqk_lo_ref[...]))
        prj.append((qk, _dot(h_hi, wvu_ref[...])))

    for slot, (qk, vu) in enumerate(prj):
        _inproj_route(step * INPROJ_BLOCKS + slot, slot, qk, vu, cos_ref, sin_ref,
                      qt_ref, ka_ref, vt_ref, u_ref, ut_ref, km_scr)


def _inproj(x, mod3, g_mix, win_hi, win_lo, cos_t, sin_t, w_out, w_fc1, w_fc2):
    bsz, seq, d = x.shape
    tm = MOBA_BLOCK
    nb = seq // tm
    assert nb % SUBLANES == 0 and nb <= LANES - HEAD_DIM and nb % INPROJ_BLOCKS == 0
    nblk = INPROJ_BLOCKS
    rows = nblk * tm
    const = lambda b, t: (0, 0)
    wqk = 2 * ATTN_WIDTH
    assert win_hi.shape[1] == 2 * wqk
    per_batch = nb // nblk
    steps = bsz * per_batch

    def slab(w):
        n = w.shape[0] // steps
        assert n * steps == w.shape[0] and n % (2 * SUBLANES) == 0
        return pl.BlockSpec((n, w.shape[1]), lambda b, t: (b * per_batch + t, 0))

    casts = [w_out, w_fc1, w_fc2]
    return pl.pallas_call(
        _inproj_kernel,
        grid=(bsz, per_batch),
        in_specs=[pl.BlockSpec((1, rows, d), lambda b, t: (b, t, 0)),
                  pl.BlockSpec((1, 1, mod3.shape[-1]), lambda b, t: (b, 0, 0)),
                  pl.BlockSpec((1, d), const),
                  pl.BlockSpec((d, wqk), const),
                  pl.BlockSpec((d, wqk), const),
                  pl.BlockSpec((d, wqk), lambda b, t: (0, 1)),
                  pl.BlockSpec((rows, LANES), lambda b, t: (t, 0)),
                  pl.BlockSpec((rows, LANES), lambda b, t: (t, 0))] + [slab(w) for w in casts],
        out_specs=[pl.BlockSpec((1, ATTN_HEADS, nblk, LANES, tm), lambda b, t: (b, 0, t, 0, 0)),
                   pl.BlockSpec((1, ATTN_HEADS, rows, LANES), lambda b, t: (b, 0, t, 0)),
                   pl.BlockSpec((1, ATTN_HEADS, nblk, VT_ROWS, tm), lambda b, t: (b, 0, t, 0, 0)),
                   pl.BlockSpec((1, rows, SSM_WIDTH), lambda b, t: (b, t, 0)),
                   pl.BlockSpec((1, nblk, SSM_WIDTH, tm), lambda b, t: (b, t, 0, 0))] + [slab(w) for w in casts],
        out_shape=[jax.ShapeDtypeStruct((bsz, ATTN_HEADS, nb, LANES, tm), BF16),
                   jax.ShapeDtypeStruct((bsz, ATTN_HEADS, seq, LANES), BF16),
                   jax.ShapeDtypeStruct((bsz, ATTN_HEADS, nb, VT_ROWS, tm), BF16),
                   jax.ShapeDtypeStruct((bsz, seq, SSM_WIDTH), F32),
                   jax.ShapeDtypeStruct((bsz, nb, SSM_WIDTH, tm), BF16)]
                  + [jax.ShapeDtypeStruct(w.shape, BF16) for w in casts],
        scratch_shapes=[pltpu.VMEM((nb, ATTN_WIDTH), F32)],
        compiler_params=_params("arbitrary", "arbitrary"),
        name="inproj",
    )(x, mod3, g_mix, win_hi, win_lo, win_hi, cos_t, sin_t, *casts)


ATTN_QBLOCKS = 2
RUNNING_MAX_INIT = -3.0e38


def _attn_kernel(qt_ref, qt_next_ref, ka_ref, vt_ref, o_ref, sa_scr, sb_scr, m_scr, acc_scr):
    pair = pl.program_id(1)
    i = ATTN_QBLOCKS * pair
    nblocks = vt_ref.shape[2]
    nh = qt_ref.shape[1]
    tq = qt_ref.shape[4]
    key = lax.broadcasted_iota(jnp.int32, (tq, tq), 0)
    qry = lax.broadcasted_iota(jnp.int32, (tq, tq), 1)
    causal = key <= qry

    def scores(qb, h, j, s_scr, mask=None, q_ref=qt_ref):
        start = pl.multiple_of(j * tq, tq)
        s = _dot(ka_ref[0, h, pl.ds(start, tq), :], q_ref[0, h, qb])
        if mask is not None:
            s = jnp.where(mask, s, MASK_VALUE)
        s_scr[qb * nh + h] = s

    def consume(qb, h, j, s_scr):
        c = qb * nh + h
        s = s_scr[c]
        m_old = m_scr[c]
        m_new = jnp.maximum(m_old, jnp.max(s, axis=0, keepdims=True))
        alpha = jnp.exp2(m_old - m_new)
        p = jnp.exp2(s - m_new).astype(BF16)
        acc_scr[c] = alpha * acc_scr[c] + _dot(vt_ref[0, h, j], p)
        m_scr[c] = m_new

    @pl.when(pair == 0)
    def _():
        for h in range(nh):
            scores(0, h, i, sa_scr, causal)
            scores(1, h, i + 1, sa_scr, causal)

    m_scr[...] = jnp.full(m_scr.shape, RUNNING_MAX_INIT, F32)
    acc_scr[...] = jnp.zeros(acc_scr.shape, F32)
    for h in range(nh):
        scores(1, h, i, sb_scr)
        consume(1, h, i + 1, sa_scr)

    def two_blocks(j0):
        j1 = j0 + 1
        prev = jnp.where(j0 == 0, i, j0 - 1)
        for h in range(nh):
            scores(0, h, j0, sb_scr)
            consume(0, h, prev, sa_scr)
            scores(1, h, j0, sa_scr)
            consume(1, h, prev, sb_scr)
        for h in range(nh):
            scores(0, h, j1, sa_scr)
            consume(0, h, j0, sb_scr)
            scores(1, h, j1, sb_scr)
            consume(1, h, j0, sa_scr)

    def body(t, carry):
        for k in range(0, 8, 2):
            two_blocks(8 * t + k)
        return carry

    lax.fori_loop(0, pair // 4, body, 0)

    @pl.when(pair % 4 >= 2)
    def _():
        two_blocks(8 * (pair // 4))
        two_blocks(8 * (pair // 4) + 2)

    @pl.when(pair % 2 == 1)
    def _():
        two_blocks(i - 2)

    last = jnp.where(pair == 0, i, i - 1)
    nxt = jnp.minimum(i + ATTN_QBLOCKS, nblocks - ATTN_QBLOCKS)
    for h in range(nh):
        consume(0, h, last, sa_scr)
        consume(1, h, last, sb_scr)
        scores(0, h, nxt, sa_scr, causal, qt_next_ref)
        scores(1, h, nxt + 1, sa_scr, causal, qt_next_ref)

    for qb in range(ATTN_QBLOCKS):
        for vt in range(nh // HEADS_PER_TILE):
            rows = []
            for hh in range(HEADS_PER_TILE):
                acc = acc_scr[qb * nh + vt * HEADS_PER_TILE + hh]
                rows.append(acc[0:HEAD_DIM] / acc[HEAD_DIM:HEAD_DIM + 1])
            o_ref[0, qb * tq:(qb + 1) * tq, vt * LANES:(vt + 1) * LANES] = jnp.concatenate(rows, axis=0).T


def _attention(q_t, k_aug, v_t):
    bsz, nh, nb, _, tq = q_t.shape
    seq = nb * tq
    assert nb % ATTN_QBLOCKS == 0
    chains = ATTN_QBLOCKS * nh
    resident = pl.Buffered(1)
    return pl.pallas_call(
        _attn_kernel,
        grid=(bsz, nb // ATTN_QBLOCKS),
        in_specs=[pl.BlockSpec((1, nh, ATTN_QBLOCKS, LANES, tq), lambda b, g: (b, 0, g, 0, 0)),
                  pl.BlockSpec((1, nh, ATTN_QBLOCKS, LANES, tq),
                               lambda b, g: (b, 0, jnp.minimum(g + 1, nb // ATTN_QBLOCKS - 1), 0, 0)),
                  pl.BlockSpec((1, nh, seq, LANES), lambda b, g: (b, 0, 0, 0), pipeline_mode=resident),
                  pl.BlockSpec((1, nh, nb, VT_ROWS, tq), lambda b, g: (b, 0, 0, 0, 0), pipeline_mode=resident)],
        out_specs=pl.BlockSpec((1, ATTN_QBLOCKS * tq, nh * HEAD_DIM), lambda b, g: (b, g, 0)),
        out_shape=jax.ShapeDtypeStruct((bsz, seq, nh * HEAD_DIM), F32),
        scratch_shapes=[pltpu.VMEM((chains, tq, tq), F32), pltpu.VMEM((chains, tq, tq), F32),
                        pltpu.VMEM((chains, 1, tq), F32), pltpu.VMEM((chains, VT_ROWS, tq), F32)],
        compiler_params=_params("arbitrary", "arbitrary"),
        name="attn",
    )(q_t, q_t, k_aug, v_t)


S5_STEPS = 128
S5_SUB_STEPS = 64
S5_CHUNKS = 4
S5_CHUNK_GROUPS = SSM_GROUPS // S5_CHUNKS
S5_CHUNK_STATES = S5_CHUNK_GROUPS * SSM_STATE
S5_CHUNK_LANES = S5_CHUNK_GROUPS * SSM_GROUP_CH


S5_CARRY_GROUPS = LANES // SSM_STATE


def _cmul(ar, ai, br, bi):
    return ar * br - ai * bi, ar * bi + ai * br


def _s5_carry_kernel(ut_ref, lam_ref, bt_ref, o_ref):
    bsz, nblk, nrows, tm = ut_ref.shape
    nch = nrows // S5_CARRY_GROUPS
    seg_blocks = nblk // N_SEG
    ls = seg_blocks * tm
    reps = bsz * N_SEG
    sub = lax.broadcasted_iota(jnp.int32, (SUBLANES, LANES), 0)
    lr = jnp.broadcast_to(lam_ref[0, 0:1], (SUBLANES, LANES))
    li = jnp.broadcast_to(lam_ref[0, 1:2], (SUBLANES, LANES))
    pr = jnp.ones((SUBLANES, LANES), F32)
    pi = jnp.zeros((SUBLANES, LANES), F32)
    tr = jnp.zeros((SUBLANES, LANES), F32)
    ti = jnp.zeros((SUBLANES, LANES), F32)
    for r in range(SUBLANES):
        tr = jnp.where(sub == SUBLANES - 1 - r, pr, tr)
        ti = jnp.where(sub == SUBLANES - 1 - r, pi, ti)
        pr, pi = _cmul(pr, pi, lr, li)
    span = SUBLANES
    while span < ls:
        nr, ni = _cmul(tr, ti, pr[0:1], pi[0:1])
        tr = jnp.concatenate([nr, tr], axis=0)
        ti = jnp.concatenate([ni, ti], axis=0)
        pr, pi = _cmul(pr, pi, pr, pi)
        span *= 2
    def seg_rows(g, b, s):
        return jnp.concatenate([ut_ref[b, s * seg_blocks + k, g * nch:(g + 1) * nch, :]
                                for k in range(seg_blocks)], axis=1)

    lhs = jnp.concatenate([seg_rows(g, b, s)
                           for g in range(S5_CARRY_GROUPS) for b in range(bsz) for s in range(N_SEG)],
                          axis=0)
    gr = _dot(lhs, tr.astype(BF16))
    gi = _dot(lhs, ti.astype(BF16))
    btr = jnp.concatenate([bt_ref[0, 0]] * (reps * S5_CARRY_GROUPS), axis=0)
    bti = jnp.concatenate([bt_ref[0, 1]] * (reps * S5_CARRY_GROUPS), axis=0)
    wr, wi = _cmul(gr, gi, btr, bti)
    fr = wr.reshape(reps * S5_CARRY_GROUPS, nch, LANES).sum(axis=1)
    fi = wi.reshape(reps * S5_CARRY_GROUPS, nch, LANES).sum(axis=1)
    grp = lax.broadcasted_iota(jnp.int32, (reps, LANES), 1) // SSM_STATE
    fin_r, fin_i = fr[0:reps], fi[0:reps]
    for g in range(1, S5_CARRY_GROUPS):
        fin_r = jnp.where(grp == g, fr[g * reps:(g + 1) * reps], fin_r)
        fin_i = jnp.where(grp == g, fi[g * reps:(g + 1) * reps], fin_i)
    rows_r, rows_i = [], []
    for b in range(bsz):
        sr = jnp.zeros((1, LANES), F32)
        si = jnp.zeros((1, LANES), F32)
        rows_r.append(sr)
        rows_i.append(si)
        for seg in range(1, N_SEG):
            r = b * N_SEG + seg - 1
            mr, mi = _cmul(sr, si, pr[0:1], pi[0:1])
            sr, si = mr + fin_r[r:r + 1], mi + fin_i[r:r + 1]
            rows_r.append(sr)
            rows_i.append(si)
    o_ref[0, 0] = jnp.concatenate(rows_r, axis=0)
    o_ref[0, 1] = jnp.concatenate(rows_i, axis=0)


def _s5_main_kernel(u_ref, bc_ref, cre_ref, cim_ref, lam_ref, sinit_ref, d_ref, wglu_ref, bglu_ref, o_ref,
                    il_scr, bu_scr, st_scr, y_scr):
    jb = pl.program_id(1)
    nseg, steps, width = u_ref.shape
    cs = S5_CHUNK_STATES
    nsub = steps // S5_SUB_STEPS
    srows = S5_SUB_STEPS * nseg

    @pl.when(jb == 0)
    def _():
        st_scr[...] = sinit_ref[...]

    for j in range(steps):
        il_scr[j * nseg:(j + 1) * nseg, :] = u_ref[:, j, :]
    us = [il_scr[sb * srows:(sb + 1) * srows, :] for sb in range(nsub)]
    for sb in range(nsub):
        u_bf = us[sb].astype(BF16)
        for c in range(S5_CHUNKS):
            bu_scr[c, sb * srows:(sb + 1) * srows, :] = _dot(
                u_bf[:, c * S5_CHUNK_LANES:(c + 1) * S5_CHUNK_LANES], bc_ref[c])
    state = [(st_scr[c, :, 0:cs], st_scr[c, :, cs:2 * cs]) for c in range(S5_CHUNKS)]
    for sb in range(nsub):
        sub_rows = slice(sb * srows, (sb + 1) * srows)
        ys = []
        for c in range(S5_CHUNKS):
            lr = jnp.broadcast_to(lam_ref[c, 0:1, :], (nseg, cs))
            li = jnp.broadcast_to(lam_ref[c, 1:2, :], (nseg, cs))
            xr, xi = state[c]
            for j in range(sb * S5_SUB_STEPS, (sb + 1) * S5_SUB_STEPS):
                rows = slice(j * nseg, (j + 1) * nseg)
                xr, xi = (lr * xr - li * xi + bu_scr[c, rows, 0:cs],
                          lr * xi + li * xr + bu_scr[c, rows, cs:2 * cs])
                bu_scr[c, rows, 0:cs] = xr
                bu_scr[c, rows, cs:2 * cs] = xi
            state[c] = (xr, xi)
            ys.append(_dot(bu_scr[c, sub_rows, 0:cs].astype(BF16), cre_ref[c])
                      + _dot(bu_scr[c, sub_rows, cs:2 * cs].astype(BF16), cim_ref[c]))
        y = jnp.concatenate(ys, axis=1) + d_ref[...] * us[sb]
        y = y * (0.5 * (1.0 + jnp.tanh(math.sqrt(2.0 / math.pi) * (y + 0.044715 * (y * y * y)))))
        z = _dot(y.astype(BF16), wglu_ref[...]) + bglu_ref[...]
        y = y * jax.nn.sigmoid(z)
        for t in range(width // LANES):
            y_scr[t, sub_rows, :] = y[:, t * LANES:(t + 1) * LANES]
    for c in range(S5_CHUNKS):
        st_scr[c, :, 0:cs] = state[c][0]
        st_scr[c, :, cs:2 * cs] = state[c][1]
    for s in range(nseg):
        for t in range(width // LANES):
            o_ref[s, :, t * LANES:(t + 1) * LANES] = y_scr[t, pl.ds(s, steps, stride=nseg), :]


def _s5(u, u_t, prm, d_skip, w_glu, b_glu):
    bsz, seq, _ = u.shape
    ls = seq // N_SEG
    steps = S5_STEPS
    rows = steps * N_SEG
    ngrp = SSM_GROUPS // S5_CARRY_GROUPS
    carry = pl.pallas_call(
        _s5_carry_kernel,
        grid=(ngrp,),
        in_specs=[pl.BlockSpec((bsz, u_t.shape[1], S5_CARRY_GROUPS * SSM_GROUP_CH, u_t.shape[3]),
                               lambda g: (0, 0, g, 0)),
                  pl.BlockSpec((1, 2, LANES), lambda g: (g, 0, 0)),
                  pl.BlockSpec((1, 2, SSM_GROUP_CH, LANES), lambda g: (g, 0, 0, 0))],
        out_specs=pl.BlockSpec((1, 2, bsz * N_SEG, LANES), lambda g: (g, 0, 0, 0)),
        out_shape=jax.ShapeDtypeStruct((ngrp, 2, bsz * N_SEG, LANES), F32),
        compiler_params=_params("arbitrary"),
        name="s5_carry",
    )(u_t, prm["lam_carry"], prm["bt_carry"])
    sinit = carry.reshape(S5_CHUNKS, ngrp // S5_CHUNKS, 2, bsz, N_SEG, LANES)
    sinit = sinit.transpose(3, 0, 4, 2, 1, 5).reshape(bsz, S5_CHUNKS, N_SEG, 2 * S5_CHUNK_STATES)

    def const(shape):
        return pl.BlockSpec(shape, lambda b, j: (0,) * len(shape))

    blk = pl.BlockSpec((None, N_SEG, steps, SSM_WIDTH), lambda b, j: (b, 0, j, 0))
    out = pl.pallas_call(
        _s5_main_kernel,
        grid=(bsz, ls // steps),
        in_specs=[blk, const(prm["bc"].shape), const(prm["cre"].shape), const(prm["cim"].shape),
                  const(prm["lam_chunk"].shape),
                  pl.BlockSpec((None, S5_CHUNKS, N_SEG, 2 * S5_CHUNK_STATES), lambda b, j: (b, 0, 0, 0)),
                  const(d_skip.shape), const(w_glu.shape), const(b_glu.shape)],
        out_specs=blk,
        out_shape=jax.ShapeDtypeStruct((bsz, N_SEG, ls, SSM_WIDTH), F32),
        scratch_shapes=[pltpu.VMEM((rows, SSM_WIDTH), F32),
                        pltpu.VMEM((S5_CHUNKS, rows, 2 * S5_CHUNK_STATES), F32),
                        pltpu.VMEM((S5_CHUNKS, N_SEG, 2 * S5_CHUNK_STATES), F32),
                        pltpu.VMEM((SSM_WIDTH // LANES, rows, LANES), F32)],
        compiler_params=_params("arbitrary", "arbitrary"),
        name="s5_main",
    )(u.reshape(bsz, N_SEG, ls, SSM_WIDTH), prm["bc"], prm["cre"], prm["cim"], prm["lam_chunk"], sinit,
      d_skip, w_glu, b_glu)
    return out.reshape(bsz, seq, SSM_WIDTH)


FF_CHUNK = 1024


def _outmlp_kernel(x_ref, attn_ref, ssm_ref, mod_ref, ga_ref, gs_ref, gm_ref, gf_ref,
                   wout_ref, w1_ref, w2_ref, o_ref, *, final_norm):
    x = x_ref[0]
    tm, d = x.shape
    gt1 = mod_ref[0, :, 2 * d:3 * d]
    sh2 = mod_ref[0, :, 3 * d:4 * d]
    sc2 = mod_ref[0, :, 4 * d:5 * d]
    gt2 = mod_ref[0, :, 5 * d:6 * d]
    a = _rmsnorm(attn_ref[0], ga_ref[...]).astype(BF16)
    s = _rmsnorm(ssm_ref[0], gs_ref[...]).astype(BF16)
    mixed = _dot(a, wout_ref[0:ATTN_WIDTH, :]) + _dot(s, wout_ref[ATTN_WIDTH:, :])
    x1 = x + gt1 * mixed
    h = (_rmsnorm(x1, gm_ref[...]) * (1.0 + sc2) + sh2).astype(BF16)
    ff = w1_ref.shape[1]
    acc = jnp.zeros((tm, d), F32)
    for c in range(ff // FF_CHUNK):
        t = jnp.maximum(_dot(h, w1_ref[:, c * FF_CHUNK:(c + 1) * FF_CHUNK]), 0.0)
        acc = acc + _dot((t * t).astype(BF16), w2_ref[c * FF_CHUNK:(c + 1) * FF_CHUNK, :])
    x2 = x1 + gt2 * acc
    o_ref[0] = _rmsnorm(x2, gf_ref[...]) if final_norm else x2


OUTMLP_ROWS = 512


def _outmlp(x, attn, ssm, mod3, g_attn, g_ssm, g_mlp, g_final, w_out, w_fc1, w_fc2, final_norm):
    bsz, seq, d = x.shape
    tm = OUTMLP_ROWS
    const = lambda b, t: (0, 0)
    return pl.pallas_call(
        functools.partial(_outmlp_kernel, final_norm=final_norm),
        grid=(bsz, seq // tm),
        in_specs=[pl.BlockSpec((1, tm, d), lambda b, t: (b, t, 0)),
                  pl.BlockSpec((1, tm, ATTN_WIDTH), lambda b, t: (b, t, 0)),
                  pl.BlockSpec((1, tm, SSM_WIDTH), lambda b, t: (b, t, 0)),
                  pl.BlockSpec((1, 1, mod3.shape[-1]), lambda b, t: (b, 0, 0)),
                  pl.BlockSpec(g_attn.shape, const),
                  pl.BlockSpec(g_ssm.shape, const),
                  pl.BlockSpec(g_mlp.shape, const),
                  pl.BlockSpec(g_final.shape, const),
                  pl.BlockSpec(w_out.shape, const),
                  pl.BlockSpec(w_fc1.shape, const),
                  pl.BlockSpec(w_fc2.shape, const)],
        out_specs=pl.BlockSpec((1, tm, d), lambda b, t: (b, t, 0)),
        out_shape=jax.ShapeDtypeStruct((bsz, seq, d), F32),
        compiler_params=_params("arbitrary", "arbitrary"),
        name="outmlp",
    )(x, attn, ssm, mod3, g_attn, g_ssm, g_mlp, g_final, w_out, w_fc1, w_fc2)


def _rope_tables(seq):
    half = HEAD_DIM // 2
    lane = jnp.arange(LANES)
    inv_freq = ROPE_THETA ** (-(lane % half).astype(F32) / half)
    ang = jnp.arange(seq, dtype=F32)[:, None] * inv_freq[None, :]
    sign = jnp.where((lane & half) == 0, -1.0, 1.0).astype(F32)
    return jnp.cos(ang), jnp.sin(ang) * sign[None, :]


def _s5_params(lam_re, lam_im, log_dt, b_re, b_im, c_re, c_im):
    lr, li = lam_re.astype(F32), lam_im.astype(F32)
    dt = jnp.exp(log_dt.astype(F32))[:, None]
    mag = jnp.exp(lr * dt)
    ar, ai = mag * jnp.cos(li * dt), mag * jnp.sin(li * dt)
    den = lr * lr + li * li
    cr = ((ar - 1.0) * lr + ai * li) / den
    ci = (ai * lr - (ar - 1.0) * li) / den
    bbar_re = cr[..., None] * b_re.astype(F32) - ci[..., None] * b_im.astype(F32)
    bbar_im = cr[..., None] * b_im.astype(F32) + ci[..., None] * b_re.astype(F32)
    eye = jnp.eye(S5_CHUNK_GROUPS, dtype=F32)

    def chunked(m):
        return m.reshape((S5_CHUNKS, S5_CHUNK_GROUPS) + m.shape[1:])

    def diag_in(m):
        return jnp.einsum('cgpn,gh->cgnhp', chunked(m), eye).reshape(S5_CHUNKS, S5_CHUNK_LANES, S5_CHUNK_STATES)

    def diag_out(m):
        return jnp.einsum('cgnp,gh->cgphn', chunked(m), eye).reshape(S5_CHUNKS, S5_CHUNK_STATES, S5_CHUNK_LANES)

    ngrp = SSM_GROUPS // S5_CARRY_GROUPS

    def lane_tiles(m):
        m = m.reshape(ngrp, S5_CARRY_GROUPS, SSM_GROUP_CH, SSM_STATE)
        return m.transpose(0, 2, 1, 3).reshape(ngrp, SSM_GROUP_CH, LANES)

    return {
        "bc": jnp.concatenate([diag_in(bbar_re), diag_in(bbar_im)], axis=2).astype(BF16),
        "cre": diag_out(c_re.astype(F32)).astype(BF16),
        "cim": diag_out(-c_im.astype(F32)).astype(BF16),
        "lam_chunk": jnp.stack([ar.reshape(S5_CHUNKS, S5_CHUNK_STATES),
                                ai.reshape(S5_CHUNKS, S5_CHUNK_STATES)], axis=1),
        "lam_carry": jnp.stack([ar.reshape(ngrp, LANES), ai.reshape(ngrp, LANES)], axis=1),
        "bt_carry": jnp.stack([lane_tiles(bbar_re.transpose(0, 2, 1)),
                               lane_tiles(bbar_im.transpose(0, 2, 1))], axis=1),
    }


def kernel(x, c, w_ada, b_ada, g_mix, w_in, g_attn_out, lam_re, lam_im, log_dt, b_re, b_im, c_re, c_im,
           d_skip, w_glu, b_glu, g_ssm_out, w_out, g_mlp, w_fc1, w_fc2, g_final):
    bsz, seq, d = x.shape
    depth = w_ada.shape[0]
    assert seq % (N_SEG * MOBA_BLOCK) == 0 and (seq // N_SEG) % S5_STEPS == 0 and bsz <= SUBLANES
    cos_t, sin_t = _rope_tables(seq)
    c_pad = jnp.zeros((SUBLANES, d), F32).at[:bsz].set(c.astype(F32))

    for l in range(depth):
        mod, win_hi, win_lo = _adaln(c_pad, w_ada[l], b_ada[l][None, :], w_in[l])
        mod3 = mod[:bsz, None, :]

        q_t, k_aug, v_t, u, u_t, wout_bf, wfc1_bf, wfc2_bf = _inproj(
            x, mod3, g_mix[l][None, :], win_hi, win_lo, cos_t, sin_t, w_out[l], w_fc1[l], w_fc2[l])
        attn = _attention(q_t, k_aug, v_t)

        prm = _s5_params(lam_re[l], lam_im[l], log_dt[l], b_re[l], b_im[l], c_re[l], c_im[l])
        ssm = _s5(u, u_t, prm, d_skip[l].reshape(1, SSM_WIDTH), w_glu[l].astype(BF16), b_glu[l][None, :])

        x = _outmlp(x, attn, ssm, mod3, g_attn_out[l][None, :], g_ssm_out[l][None, :], g_mlp[l][None, :],
                    g_final[None, :], wout_bf, wfc1_bf, wfc2_bf, final_norm=(l == depth - 1))
    return x
```

```python
import functools
import math

import jax
import jax.numpy as jnp
from jax import lax
from jax.experimental import pallas as pl
from jax.experimental.pallas import tpu as pltpu

F32 = jnp.float32
BF16 = jnp.bfloat16

HEAD_DIM = 64
ATTN_HEADS = 8
ATTN_WIDTH = ATTN_HEADS * HEAD_DIM
SSM_GROUPS = 32
SSM_GROUP_CH = 16
SSM_WIDTH = SSM_GROUPS * SSM_GROUP_CH
SSM_STATE = 64
MOBA_BLOCK = 256
MOBA_TOPK = 3
ROPE_THETA = 10000.0
EPS = 1e-6

LANES = 128
SUBLANES = 8
N_SEG = SUBLANES
HEADS_PER_TILE = LANES // HEAD_DIM
MASK_VALUE = -(2.0 ** 100)
VMEM_LIMIT_BYTES = 56 * 1024 * 1024


def _split_bf16(a):
    hi = a.astype(BF16)
    lo = (a - hi.astype(F32)).astype(BF16)
    return hi, lo


def _dot(a, b):
    return jnp.dot(a, b, preferred_element_type=F32)


def _rmsnorm(x, g):
    return x * lax.rsqrt(jnp.mean(x * x, axis=-1, keepdims=True) + EPS) * g


def _params(*semantics):
    return pltpu.CompilerParams(dimension_semantics=semantics, vmem_limit_bytes=VMEM_LIMIT_BYTES)


ADALN_STEPS = 4


def _adaln_kernel(c_ref, w_ref, b_ref, win_ref, o_ref, win_hi_ref, win_lo_ref):
    c = c_ref[...]
    s_hi, s_lo = _split_bf16(c * jax.nn.sigmoid(c))
    w_hi, w_lo = _split_bf16(w_ref[...])
    o_ref[...] = _dot(s_hi, w_hi) + _dot(s_lo, w_hi) + _dot(s_hi, w_lo) + b_ref[...]
    win_hi_ref[...], win_lo_ref[...] = _split_bf16(win_ref[...])


def _adaln(c_pad, w, b, w_in):
    rows, d = c_pad.shape
    n = w.shape[1]
    tn = n // ADALN_STEPS
    tw = w_in.shape[1] // ADALN_STEPS
    slab = pl.BlockSpec((d, tw), lambda j: (0, j))
    return pl.pallas_call(
        _adaln_kernel,
        grid=(ADALN_STEPS,),
        in_specs=[pl.BlockSpec((rows, d), lambda j: (0, 0)),
                  pl.BlockSpec((d, tn), lambda j: (0, j)),
                  pl.BlockSpec((1, tn), lambda j: (0, j)),
                  slab],
        out_specs=[pl.BlockSpec((rows, tn), lambda j: (0, j)), slab, slab],
        out_shape=[jax.ShapeDtypeStruct((rows, n), F32),
                   jax.ShapeDtypeStruct(w_in.shape, BF16),
                   jax.ShapeDtypeStruct(w_in.shape, BF16)],
        compiler_params=_params("arbitrary"),
        name="adaln",
    )(c_pad, w, b, w_in)


VT_ROWS = HEAD_DIM + 16


def _top_blocks(gate, blk, past, own):
    nb = gate.shape[0]
    g = jnp.where(past, gate, -jnp.inf)
    chosen = None
    for _ in range(MOBA_TOPK):
        best = jnp.max(g, axis=0, keepdims=True)
        first = jnp.min(jnp.where(g == best, blk, float(nb)), axis=0, keepdims=True)
        pick = blk == first
        chosen = pick if chosen is None else (chosen | pick)
        g = jnp.where(pick, -jnp.inf, g)
    return (chosen & past) | own


INPROJ_BLOCKS = 4


def _inproj_route(i, slot, qk, vu, cos_ref, sin_ref, qt_ref, ka_ref, vt_ref, u_ref, ut_ref, km_scr):
    tm = qk.shape[0]
    nb = km_scr.shape[0]
    rows = slice(slot * tm, (slot + 1) * tm)
    u = vu[:, ATTN_WIDTH:]
    u_ref[0, rows] = u
    ut_ref[0, slot] = u.T.astype(BF16)

    reps = ATTN_WIDTH // LANES
    cos = jnp.concatenate([cos_ref[rows]] * reps, axis=1)
    sin = jnp.concatenate([sin_ref[rows]] * reps, axis=1)
    wide_lane = lax.broadcasted_iota(jnp.int32, (tm, ATTN_WIDTH), 1)
    first_half = (wide_lane & (HEAD_DIM // 2)) == 0

    def rope(t):
        partner = jnp.where(first_half,
                            pltpu.roll(t, ATTN_WIDTH - HEAD_DIM // 2, 1),
                            pltpu.roll(t, HEAD_DIM // 2, 1))
        return t * cos + partner * sin

    q = rope(qk[:, :ATTN_WIDTH])
    k = rope(qk[:, ATTN_WIDTH:])
    v = vu[:, :ATTN_WIDTH]

    km_scr[pl.ds(i, 1), :] = jnp.sum(k, axis=0, keepdims=True) * (1.0 / tm)
    km_all = km_scr[...]

    lane = lax.broadcasted_iota(jnp.int32, (tm, LANES), 1)
    km_lane = lax.broadcasted_iota(jnp.int32, (nb, LANES), 1)
    blk_i = lax.broadcasted_iota(jnp.int32, (nb, tm), 0)
    blk = blk_i.astype(F32)
    past = blk_i < i
    own = blk_i == i
    ones_rows = jnp.where(lax.broadcasted_iota(jnp.int32, (VT_ROWS - HEAD_DIM, tm), 0) == 0, 1.0, 0.0)
    onehot = jnp.where(lane == HEAD_DIM + i, 1.0, 0.0)
    scale = HEAD_DIM ** -0.5 * math.log2(math.e)
    for tile in range(ATTN_WIDTH // LANES):
        cols = slice(tile * LANES, (tile + 1) * LANES)
        q_t = q[:, cols].T
        v_t = v[:, cols].T
        k_tile = k[:, cols]
        km = km_all[:, cols]
        qt_hi, qt_lo = _split_bf16(q_t)
        for hh in range(HEADS_PER_TILE):
            h_idx = tile * HEADS_PER_TILE + hh
            kmh_hi, kmh_lo = _split_bf16(jnp.where(km_lane // HEAD_DIM == hh, km, 0.0))
            gate = _dot(kmh_hi, qt_hi) + _dot(kmh_lo, qt_hi) + _dot(kmh_hi, qt_lo)
            bias_t = jnp.where(_top_blocks(gate, blk, past, own), 0.0, MASK_VALUE)
            head_rows = slice(hh * HEAD_DIM, (hh + 1) * HEAD_DIM)
            qt_ref[0, h_idx, slot] = jnp.concatenate(
                [q_t[head_rows] * scale, bias_t, jnp.zeros((LANES - HEAD_DIM - nb, tm), F32)],
                axis=0).astype(BF16)
            vt_ref[0, h_idx, slot] = jnp.concatenate([v_t[head_rows], ones_rows], axis=0).astype(BF16)
            kh = jnp.where(lane // HEAD_DIM == hh, k_tile, 0.0)
            if hh:
                kh = pltpu.roll(kh, LANES - hh * HEAD_DIM, 1)
            ka_ref[0, h_idx, rows] = jnp.where(lane < HEAD_DIM, kh, onehot).astype(BF16)


def _inproj_kernel(x_ref, mod_ref, g_ref, wqk_hi_ref, wqk_lo_ref, wvu_ref, cos_ref, sin_ref,
                   wout_ref, wfc1_ref, wfc2_ref,
                   qt_ref, ka_ref, vt_ref, u_ref, ut_ref, wout_bf_ref, wfc1_bf_ref, wfc2_bf_ref, km_scr):
    step = pl.program_id(1)

    @pl.when(step == 0)
    def _():
        km_scr[...] = jnp.zeros_like(km_scr)

    wout_bf_ref[...] = wout_ref[...].astype(BF16)
    wfc1_bf_ref[...] = wfc1_ref[...].astype(BF16)
    wfc2_bf_ref[...] = wfc2_ref[...].astype(BF16)
    d = x_ref.shape[2]
    tm = x_ref.shape[1] // INPROJ_BLOCKS
    sh1 = mod_ref[0, :, 0:d]
    sc1 = mod_ref[0, :, d:2 * d]
    hs = []
    for slot in range(INPROJ_BLOCKS):
        x = x_ref[0, slot * tm:(slot + 1) * tm, :]
        hs.append(_split_bf16(_rmsnorm(x, g_ref[...]) * (1.0 + sc1) + sh1))
    prj = []
    for h_hi, h_lo in hs:
        qk = (_dot(h_hi, wqk_hi_ref[...]) + _dot(h_lo, wqk_hi_ref[...])
              + _dot(h_hi, wqk_lo_ref[...]))
        prj.append((qk, _dot(h_hi, wvu_ref[...])))

    for slot, (qk, vu) in enumerate(prj):
        _inproj_route(step * INPROJ_BLOCKS + slot, slot, qk, vu, cos_ref, sin_ref,
                      qt_ref, ka_ref, vt_ref, u_ref, ut_ref, km_scr)


def _inproj(x, mod3, g_mix, win_hi, win_lo, cos_t, sin_t, w_out, w_fc1, w_fc2):
    bsz, seq, d = x.shape
    tm = MOBA_BLOCK
    nb = seq // tm
    assert nb % SUBLANES == 0 and nb <= LANES - HEAD_DIM and nb % INPROJ_BLOCKS == 0
    nblk = INPROJ_BLOCKS
    rows = nblk * tm
    const = lambda b, t: (0, 0)
    wqk = 2 * ATTN_WIDTH
    assert win_hi.shape[1] == 2 * wqk
    per_batch = nb // nblk
    steps = bsz * per_batch

    def slab(w):
        n = w.shape[0] // steps
        assert n * steps == w.shape[0] and n % (2 * SUBLANES) == 0
        return pl.BlockSpec((n, w.shape[1]), lambda b, t: (b * per_batch + t, 0))

    casts = [w_out, w_fc1, w_fc2]
    return pl.pallas_call(
        _inproj_kernel,
        grid=(bsz, per_batch),
        in_specs=[pl.BlockSpec((1, rows, d), lambda b, t: (b, t, 0)),
                  pl.BlockSpec((1, 1, mod3.shape[-1]), lambda b, t: (b, 0, 0)),
                  pl.BlockSpec((1, d), const),
                  pl.BlockSpec((d, wqk), const),
                  pl.BlockSpec((d, wqk), const),
                  pl.BlockSpec((d, wqk), lambda b, t: (0, 1)),
                  pl.BlockSpec((rows, LANES), lambda b, t: (t, 0)),
                  pl.BlockSpec((rows, LANES), lambda b, t: (t, 0))] + [slab(w) for w in casts],
        out_specs=[pl.BlockSpec((1, ATTN_HEADS, nblk, LANES, tm), lambda b, t: (b, 0, t, 0, 0)),
                   pl.BlockSpec((1, ATTN_HEADS, rows, LANES), lambda b, t: (b, 0, t, 0)),
                   pl.BlockSpec((1, ATTN_HEADS, nblk, VT_ROWS, tm), lambda b, t: (b, 0, t, 0, 0)),
                   pl.BlockSpec((1, rows, SSM_WIDTH), lambda b, t: (b, t, 0)),
                   pl.BlockSpec((1, nblk, SSM_WIDTH, tm), lambda b, t: (b, t, 0, 0))] + [slab(w) for w in casts],
        out_shape=[jax.ShapeDtypeStruct((bsz, ATTN_HEADS, nb, LANES, tm), BF16),
                   jax.ShapeDtypeStruct((bsz, ATTN_HEADS, seq, LANES), BF16),
                   jax.ShapeDtypeStruct((bsz, ATTN_HEADS, nb, VT_ROWS, tm), BF16),
                   jax.ShapeDtypeStruct((bsz, seq, SSM_WIDTH), F32),
                   jax.ShapeDtypeStruct((bsz, nb, SSM_WIDTH, tm), BF16)]
                  + [jax.ShapeDtypeStruct(w.shape, BF16) for w in casts],
        scratch_shapes=[pltpu.VMEM((nb, ATTN_WIDTH), F32)],
        compiler_params=_params("arbitrary", "arbitrary"),
        name="inproj",
    )(x, mod3, g_mix, win_hi, win_lo, win_hi, cos_t, sin_t, *casts)


ATTN_QBLOCKS = 2
RUNNING_MAX_INIT = -3.0e38


def _attn_kernel(qt_ref, qt_next_ref, ka_ref, vt_ref, o_ref, sa_scr, sb_scr, m_scr, acc_scr):
    pair = pl.program_id(1)
    i = ATTN_QBLOCKS * pair
    nblocks = vt_ref.shape[2]
    nh = qt_ref.shape[1]
    tq = qt_ref.shape[4]
    key = lax.broadcasted_iota(jnp.int32, (tq, tq), 0)
    qry = lax.broadcasted_iota(jnp.int32, (tq, tq), 1)
    causal = key <= qry

    def scores(qb, h, j, s_scr, mask=None, q_ref=qt_ref):
        start = pl.multiple_of(j * tq, tq)
        s = _dot(ka_ref[0, h, pl.ds(start, tq), :], q_ref[0, h, qb])
        if mask is not None:
            s = jnp.where(mask, s, MASK_VALUE)
        s_scr[qb * nh + h] = s

    def consume(qb, h, j, s_scr):
        c = qb * nh + h
        s = s_scr[c]
        m_old = m_scr[c]
        m_new = jnp.maximum(m_old, jnp.max(s, axis=0, keepdims=True))
        alpha = jnp.exp2(m_old - m_new)
        p = jnp.exp2(s - m_new).astype(BF16)
        acc_scr[c] = alpha * acc_scr[c] + _dot(vt_ref[0, h, j], p)
        m_scr[c] = m_new

    @pl.when(pair == 0)
    def _():
        for h in range(nh):
            scores(0, h, i, sa_scr, causal)
            scores(1, h, i + 1, sa_scr, causal)

    m_scr[...] = jnp.full(m_scr.shape, RUNNING_MAX_INIT, F32)
    acc_scr[...] = jnp.zeros(acc_scr.shape, F32)
    for h in range(nh):
        scores(1, h, i, sb_scr)
        consume(1, h, i + 1, sa_scr)

    def two_blocks(j0):
        j1 = j0 + 1
        prev = jnp.where(j0 == 0, i, j0 - 1)
        for h in range(nh):
            scores(0, h, j0, sb_scr)
            consume(0, h, prev, sa_scr)
            scores(1, h, j0, sa_scr)
            consume(1, h, prev, sb_scr)
        for h in range(nh):
            scores(0, h, j1, sa_scr)
            consume(0, h, j0, sb_scr)
            scores(1, h, j1, sb_scr)
            consume(1, h, j0, sa_scr)

    def body(t, carry):
        for k in range(0, 8, 2):
            two_blocks(8 * t + k)
        return carry

    lax.fori_loop(0, pair // 4, body, 0)

    @pl.when(pair % 4 >= 2)
    def _():
        two_blocks(8 * (pair // 4))
        two_blocks(8 * (pair // 4) + 2)

    @pl.when(pair % 2 == 1)
    def _():
        two_blocks(i - 2)

    last = jnp.where(pair == 0, i, i - 1)
    nxt = jnp.minimum(i + ATTN_QBLOCKS, nblocks - ATTN_QBLOCKS)
    for h in range(nh):
        consume(0, h, last, sa_scr)
        consume(1, h, last, sb_scr)
        scores(0, h, nxt, sa_scr, causal, qt_next_ref)
        scores(1, h, nxt + 1, sa_scr, causal, qt_next_ref)

    for qb in range(ATTN_QBLOCKS):
        for vt in range(nh // HEADS_PER_TILE):
            rows = []
            for hh in range(HEADS_PER_TILE):
                acc = acc_scr[qb * nh + vt * HEADS_PER_TILE + hh]
                rows.append(acc[0:HEAD_DIM] / acc[HEAD_DIM:HEAD_DIM + 1])
            o_ref[0, qb * tq:(qb + 1) * tq, vt * LANES:(vt + 1) * LANES] = jnp.concatenate(rows, axis=0).T


def _attention(q_t, k_aug, v_t):
    bsz, nh, nb, _, tq = q_t.shape
    seq = nb * tq
    assert nb % ATTN_QBLOCKS == 0
    chains = ATTN_QBLOCKS * nh
    resident = pl.Buffered(1)
    return pl.pallas_call(
        _attn_kernel,
        grid=(bsz, nb // ATTN_QBLOCKS),
        in_specs=[pl.BlockSpec((1, nh, ATTN_QBLOCKS, LANES, tq), lambda b, g: (b, 0, g, 0, 0)),
                  pl.BlockSpec((1, nh, ATTN_QBLOCKS, LANES, tq),
                               lambda b, g: (b, 0, jnp.minimum(g + 1, nb // ATTN_QBLOCKS - 1), 0, 0)),
                  pl.BlockSpec((1, nh, seq, LANES), lambda b, g: (b, 0, 0, 0), pipeline_mode=resident),
                  pl.BlockSpec((1, nh, nb, VT_ROWS, tq), lambda b, g: (b, 0, 0, 0, 0), pipeline_mode=resident)],
        out_specs=pl.BlockSpec((1, ATTN_QBLOCKS * tq, nh * HEAD_DIM), lambda b, g: (b, g, 0)),
        out_shape=jax.ShapeDtypeStruct((bsz, seq, nh * HEAD_DIM), F32),
        scratch_shapes=[pltpu.VMEM((chains, tq, tq), F32), pltpu.VMEM((chains, tq, tq), F32),
                        pltpu.VMEM((chains, 1, tq), F32), pltpu.VMEM((chains, VT_ROWS, tq), F32)],
        compiler_params=_params("arbitrary", "arbitrary"),
        name="attn",
    )(q_t, q_t, k_aug, v_t)


S5_STEPS = 128
S5_SUB_STEPS = 64
S5_CHUNKS = 4
S5_CHUNK_GROUPS = SSM_GROUPS // S5_CHUNKS
S5_CHUNK_STATES = S5_CHUNK_GROUPS * SSM_STATE
S5_CHUNK_LANES = S5_CHUNK_GROUPS * SSM_GROUP_CH


S5_CARRY_GROUPS = LANES // SSM_STATE


def _cmul(ar, ai, br, bi):
    return ar * br - ai * bi, ar * bi + ai * br


def _s5_carry_kernel(ut_ref, lam_ref, bt_ref, o_ref):
    bsz, nblk, nrows, tm = ut_ref.shape
    nch = nrows // S5_CARRY_GROUPS
    seg_blocks = nblk // N_SEG
    ls = seg_blocks * tm
    reps = bsz * N_SEG
    sub = lax.broadcasted_iota(jnp.int32, (SUBLANES, LANES), 0)
    lr = jnp.broadcast_to(lam_ref[0, 0:1], (SUBLANES, LANES))
    li = jnp.broadcast_to(lam_ref[0, 1:2], (SUBLANES, LANES))
    pr = jnp.ones((SUBLANES, LANES), F32)
    pi = jnp.zeros((SUBLANES, LANES), F32)
    tr = jnp.zeros((SUBLANES, LANES), F32)
    ti = jnp.zeros((SUBLANES, LANES), F32)
    for r in range(SUBLANES):
        tr = jnp.where(sub == SUBLANES - 1 - r, pr, tr)
        ti = jnp.where(sub == SUBLANES - 1 - r, pi, ti)
        pr, pi = _cmul(pr, pi, lr, li)
    span = SUBLANES
    while span < ls:
        nr, ni = _cmul(tr, ti, pr[0:1], pi[0:1])
        tr = jnp.concatenate([nr, tr], axis=0)
        ti = jnp.concatenate([ni, ti], axis=0)
        pr, pi = _cmul(pr, pi, pr, pi)
        span *= 2
    def seg_rows(g, b, s):
        return jnp.concatenate([ut_ref[b, s * seg_blocks + k, g * nch:(g + 1) * nch, :]
                                for k in range(seg_blocks)], axis=1)

    lhs = jnp.concatenate([seg_rows(g, b, s)
                           for g in range(S5_CARRY_GROUPS) for b in range(bsz) for s in range(N_SEG)],
                          axis=0)
    gr = _dot(lhs, tr.astype(BF16))
    gi = _dot(lhs, ti.astype(BF16))
    btr = jnp.concatenate([bt_ref[0, 0]] * (reps * S5_CARRY_GROUPS), axis=0)
    bti = jnp.concatenate([bt_ref[0, 1]] * (reps * S5_CARRY_GROUPS), axis=0)
    wr, wi = _cmul(gr, gi, btr, bti)
    fr = wr.reshape(reps * S5_CARRY_GROUPS, nch, LANES).sum(axis=1)
    fi = wi.reshape(reps * S5_CARRY_GROUPS, nch, LANES).sum(axis=1)
    grp = lax.broadcasted_iota(jnp.int32, (reps, LANES), 1) // SSM_STATE
    fin_r, fin_i = fr[0:reps], fi[0:reps]
    for g in range(1, S5_CARRY_GROUPS):
        fin_r = jnp.where(grp == g, fr[g * reps:(g + 1) * reps], fin_r)
        fin_i = jnp.where(grp == g, fi[g * reps:(g + 1) * reps], fin_i)
    rows_r, rows_i = [], []
    for b in range(bsz):
        sr = jnp.zeros((1, LANES), F32)
        si = jnp.zeros((1, LANES), F32)
        rows_r.append(sr)
        rows_i.append(si)
        for seg in range(1, N_SEG):
            r = b * N_SEG + seg - 1
            mr, mi = _cmul(sr, si, pr[0:1], pi[0:1])
            sr, si = mr + fin_r[r:r + 1], mi + fin_i[r:r + 1]
            rows_r.append(sr)
            rows_i.append(si)
    o_ref[0, 0] = jnp.concatenate(rows_r, axis=0)
    o_ref[0, 1] = jnp.concatenate(rows_i, axis=0)


def _s5_main_kernel(u_ref, bc_ref, cre_ref, cim_ref, lam_ref, sinit_ref, d_ref, wglu_ref, bglu_ref, o_ref,
                    il_scr, bu_scr, st_scr, y_scr):
    jb = pl.program_id(1)
    nseg, steps, width = u_ref.shape
    cs = S5_CHUNK_STATES
    nsub = steps // S5_SUB_STEPS
    srows = S5_SUB_STEPS * nseg

    @pl.when(jb == 0)
    def _():
        st_scr[...] = sinit_ref[...]

    for j in range(steps):
        il_scr[j * nseg:(j + 1) * nseg, :] = u_ref[:, j, :]
    us = [il_scr[sb * srows:(sb + 1) * srows, :] for sb in range(nsub)]
    for sb in range(nsub):
        u_bf = us[sb].astype(BF16)
        for c in range(S5_CHUNKS):
            bu_scr[c, sb * srows:(sb + 1) * srows, :] = _dot(
                u_bf[:, c * S5_CHUNK_LANES:(c + 1) * S5_CHUNK_LANES], bc_ref[c])
    state = [(st_scr[c, :, 0:cs], st_scr[c, :, cs:2 * cs]) for c in range(S5_CHUNKS)]
    for sb in range(nsub):
        sub_rows = slice(sb * srows, (sb + 1) * srows)
        ys = []
        for c in range(S5_CHUNKS):
            lr = jnp.broadcast_to(lam_ref[c, 0:1, :], (nseg, cs))
            li = jnp.broadcast_to(lam_ref[c, 1:2, :], (nseg, cs))
            xr, xi = state[c]
            for j in range(sb * S5_SUB_STEPS, (sb + 1) * S5_SUB_STEPS):
                rows = slice(j * nseg, (j + 1) * nseg)
                xr, xi = (lr * xr - li * xi + bu_scr[c, rows, 0:cs],
                          lr * xi + li * xr + bu_scr[c, rows, cs:2 * cs])
                bu_scr[c, rows, 0:cs] = xr
                bu_scr[c, rows, cs:2 * cs] = xi
            state[c] = (xr, xi)
            ys.append(_dot(bu_scr[c, sub_rows, 0:cs].astype(BF16), cre_ref[c])
                      + _dot(bu_scr[c, sub_rows, cs:2 * cs].astype(BF16), cim_ref[c]))
        y = jnp.concatenate(ys, axis=1) + d_ref[...] * us[sb]
        y = y * (0.5 * (1.0 + jnp.tanh(math.sqrt(2.0 / math.pi) * (y + 0.044715 * (y * y * y)))))
        z = _dot(y.astype(BF16), wglu_ref[...]) + bglu_ref[...]
        y = y * jax.nn.sigmoid(z)
        for t in range(width // LANES):
            y_scr[t, sub_rows, :] = y[:, t * LANES:(t + 1) * LANES]
    for c in range(S5_CHUNKS):
        st_scr[c, :, 0:cs] = state[c][0]
        st_scr[c, :, cs:2 * cs] = state[c][1]
    for s in range(nseg):
        for t in range(width // LANES):
            o_ref[s, :, t * LANES:(t + 1) * LANES] = y_scr[t, pl.ds(s, steps, stride=nseg), :]


def _s5(u, u_t, prm, d_skip, w_glu, b_glu):
    bsz, seq, _ = u.shape
    ls = seq // N_SEG
    steps = S5_STEPS
    rows = steps * N_SEG
    ngrp = SSM_GROUPS // S5_CARRY_GROUPS
    carry = pl.pallas_call(
        _s5_carry_kernel,
        grid=(ngrp,),
        in_specs=[pl.BlockSpec((bsz, u_t.shape[1], S5_CARRY_GROUPS * SSM_GROUP_CH, u_t.shape[3]),
                               lambda g: (0, 0, g, 0)),
                  pl.BlockSpec((1, 2, LANES), lambda g: (g, 0, 0)),
                  pl.BlockSpec((1, 2, SSM_GROUP_CH, LANES), lambda g: (g, 0, 0, 0))],
        out_specs=pl.BlockSpec((1, 2, bsz * N_SEG, LANES), lambda g: (g, 0, 0, 0)),
        out_shape=jax.ShapeDtypeStruct((ngrp, 2, bsz * N_SEG, LANES), F32),
        compiler_params=_params("arbitrary"),
        name="s5_carry",
    )(u_t, prm["lam_carry"], prm["bt_carry"])
    sinit = carry.reshape(S5_CHUNKS, ngrp // S5_CHUNKS, 2, bsz, N_SEG, LANES)
    sinit = sinit.transpose(3, 0, 4, 2, 1, 5).reshape(bsz, S5_CHUNKS, N_SEG, 2 * S5_CHUNK_STATES)

    def const(shape):
        return pl.BlockSpec(shape, lambda b, j: (0,) * len(shape))

    blk = pl.BlockSpec((None, N_SEG, steps, SSM_WIDTH), lambda b, j: (b, 0, j, 0))
    out = pl.pallas_call(
        _s5_main_kernel,
        grid=(bsz, ls // steps),
        in_specs=[blk, const(prm["bc"].shape), const(prm["cre"].shape), const(prm["cim"].shape),
                  const(prm["lam_chunk"].shape),
                  pl.BlockSpec((None, S5_CHUNKS, N_SEG, 2 * S5_CHUNK_STATES), lambda b, j: (b, 0, 0, 0)),
                  const(d_skip.shape), const(w_glu.shape), const(b_glu.shape)],
        out_specs=blk,
        out_shape=jax.ShapeDtypeStruct((bsz, N_SEG, ls, SSM_WIDTH), F32),
        scratch_shapes=[pltpu.VMEM((rows, SSM_WIDTH), F32),
                        pltpu.VMEM((S5_CHUNKS, rows, 2 * S5_CHUNK_STATES), F32),
                        pltpu.VMEM((S5_CHUNKS, N_SEG, 2 * S5_CHUNK_STATES), F32),
                        pltpu.VMEM((SSM_WIDTH // LANES, rows, LANES), F32)],
        compiler_params=_params("arbitrary", "arbitrary"),
        name="s5_main",
    )(u.reshape(bsz, N_SEG, ls, SSM_WIDTH), prm["bc"], prm["cre"], prm["cim"], prm["lam_chunk"], sinit,
      d_skip, w_glu, b_glu)
    return out.reshape(bsz, seq, SSM_WIDTH)


FF_CHUNK = 1024


def _outmlp_kernel(x_ref, attn_ref, ssm_ref, mod_ref, ga_ref, gs_ref, gm_ref, gf_ref,
                   wout_ref, w1_ref, w2_ref, o_ref, *, final_norm):
    x = x_ref[0]
    tm, d = x.shape
    gt1 = mod_ref[0, :, 2 * d:3 * d]
    sh2 = mod_ref[0, :, 3 * d:4 * d]
    sc2 = mod_ref[0, :, 4 * d:5 * d]
    gt2 = mod_ref[0, :, 5 * d:6 * d]
    a = _rmsnorm(attn_ref[0], ga_ref[...]).astype(BF16)
    s = _rmsnorm(ssm_ref[0], gs_ref[...]).astype(BF16)
    mixed = _dot(a, wout_ref[0:ATTN_WIDTH, :]) + _dot(s, wout_ref[ATTN_WIDTH:, :])
    x1 = x + gt1 * mixed
    h = (_rmsnorm(x1, gm_ref[...]) * (1.0 + sc2) + sh2).astype(BF16)
    ff = w1_ref.shape[1]
    acc = jnp.zeros((tm, d), F32)
    for c in range(ff // FF_CHUNK):
        t = jnp.maximum(_dot(h, w1_ref[:, c * FF_CHUNK:(c + 1) * FF_CHUNK]), 0.0)
        acc = acc + _dot((t * t).astype(BF16), w2_ref[c * FF_CHUNK:(c + 1) * FF_CHUNK, :])
    x2 = x1 + gt2 * acc
    o_ref[0] = _rmsnorm(x2, gf_ref[...]) if final_norm else x2


OUTMLP_ROWS = 512


def _outmlp(x, attn, ssm, mod3, g_attn, g_ssm, g_mlp, g_final, w_out, w_fc1, w_fc2, final_norm):
    bsz, seq, d = x.shape
    tm = OUTMLP_ROWS
    const = lambda b, t: (0, 0)
    return pl.pallas_call(
        functools.partial(_outmlp_kernel, final_norm=final_norm),
        grid=(bsz, seq // tm),
        in_specs=[pl.BlockSpec((1, tm, d), lambda b, t: (b, t, 0)),
                  pl.BlockSpec((1, tm, ATTN_WIDTH), lambda b, t: (b, t, 0)),
                  pl.BlockSpec((1, tm, SSM_WIDTH), lambda b, t: (b, t, 0)),
                  pl.BlockSpec((1, 1, mod3.shape[-1]), lambda b, t: (b, 0, 0)),
                  pl.BlockSpec(g_attn.shape, const),
                  pl.BlockSpec(g_ssm.shape, const),
                  pl.BlockSpec(g_mlp.shape, const),
                  pl.BlockSpec(g_final.shape, const),
                  pl.BlockSpec(w_out.shape, const),
                  pl.BlockSpec(w_fc1.shape, const),
                  pl.BlockSpec(w_fc2.shape, const)],
        out_specs=pl.BlockSpec((1, tm, d), lambda b, t: (b, t, 0)),
        out_shape=jax.ShapeDtypeStruct((bsz, seq, d), F32),
        compiler_params=_params("arbitrary", "arbitrary"),
        name="outmlp",
    )(x, attn, ssm, mod3, g_attn, g_ssm, g_mlp, g_final, w_out, w_fc1, w_fc2)


def _rope_tables(seq):
    half = HEAD_DIM // 2
    inv_freq = ROPE_THETA ** (-jnp.arange(half, dtype=F32) / half)
    ang = jnp.arange(seq, dtype=F32)[:, None] * inv_freq[None, :]
    cos, sin = jnp.cos(ang), jnp.sin(ang)
    reps = LANES // HEAD_DIM
    cos_t = jnp.tile(cos, (1, 2 * reps))
    sin_t = jnp.tile(jnp.concatenate([-sin, sin], axis=1), (1, reps))
    return cos_t, sin_t


def _s5_params(lam_re, lam_im, log_dt, b_re, b_im, c_re, c_im):
    lr, li = lam_re.astype(F32), lam_im.astype(F32)
    dt = jnp.exp(log_dt.astype(F32))[:, None]
    mag = jnp.exp(lr * dt)
    ar, ai = mag * jnp.cos(li * dt), mag * jnp.sin(li * dt)
    den = lr * lr + li * li
    cr = ((ar - 1.0) * lr + ai * li) / den
    ci = (ai * lr - (ar - 1.0) * li) / den
    bbar_re = cr[..., None] * b_re.astype(F32) - ci[..., None] * b_im.astype(F32)
    bbar_im = cr[..., None] * b_im.astype(F32) + ci[..., None] * b_re.astype(F32)
    eye = jnp.eye(S5_CHUNK_GROUPS, dtype=F32)

    def chunked(m):
        return m.reshape((S5_CHUNKS, S5_CHUNK_GROUPS) + m.shape[1:])

    def diag_in(m):
        return jnp.einsum('cgpn,gh->cgnhp', chunked(m), eye).reshape(S5_CHUNKS, S5_CHUNK_LANES, S5_CHUNK_STATES)

    def diag_out(m):
        return jnp.einsum('cgnp,gh->cgphn', chunked(m), eye).reshape(S5_CHUNKS, S5_CHUNK_STATES, S5_CHUNK_LANES)

    ngrp = SSM_GROUPS // S5_CARRY_GROUPS

    def lane_tiles(m):
        m = m.reshape(ngrp, S5_CARRY_GROUPS, SSM_GROUP_CH, SSM_STATE)
        return m.transpose(0, 2, 1, 3).reshape(ngrp, SSM_GROUP_CH, LANES)

    return {
        "bc": jnp.concatenate([diag_in(bbar_re), diag_in(bbar_im)], axis=2).astype(BF16),
        "cre": diag_out(c_re.astype(F32)).astype(BF16),
        "cim": diag_out(-c_im.astype(F32)).astype(BF16),
        "lam_chunk": jnp.stack([ar.reshape(S5_CHUNKS, S5_CHUNK_STATES),
                                ai.reshape(S5_CHUNKS, S5_CHUNK_STATES)], axis=1),
        "lam_carry": jnp.stack([ar.reshape(ngrp, LANES), ai.reshape(ngrp, LANES)], axis=1),
        "bt_carry": jnp.stack([lane_tiles(bbar_re.transpose(0, 2, 1)),
                               lane_tiles(bbar_im.transpose(0, 2, 1))], axis=1),
    }


def kernel(x, c, w_ada, b_ada, g_mix, w_in, g_attn_out, lam_re, lam_im, log_dt, b_re, b_im, c_re, c_im,
           d_skip, w_glu, b_glu, g_ssm_out, w_out, g_mlp, w_fc1, w_fc2, g_final):
    bsz, seq, d = x.shape
    depth = w_ada.shape[0]
    assert seq % (N_SEG * MOBA_BLOCK) == 0 and (seq // N_SEG) % S5_STEPS == 0 and bsz <= SUBLANES
    cos_t, sin_t = _rope_tables(seq)
    c_pad = jnp.zeros((SUBLANES, d), F32).at[:bsz].set(c.astype(F32))

    for l in range(depth):
        mod, win_hi, win_lo = _adaln(c_pad, w_ada[l], b_ada[l][None, :], w_in[l])
        mod3 = mod[:bsz, None, :]

        q_t, k_aug, v_t, u, u_t, wout_bf, wfc1_bf, wfc2_bf = _inproj(
            x, mod3, g_mix[l][None, :], win_hi, win_lo, cos_t, sin_t, w_out[l], w_fc1[l], w_fc2[l])
        attn = _attention(q_t, k_aug, v_t)

        prm = _s5_params(lam_re[l], lam_im[l], log_dt[l], b_re[l], b_im[l], c_re[l], c_im[l])
        ssm = _s5(u, u_t, prm, d_skip[l].reshape(1, SSM_WIDTH), w_glu[l].astype(BF16), b_glu[l][None, :])

        x = _outmlp(x, attn, ssm, mod3, g_attn_out[l][None, :], g_ssm_out[l][None, :], g_mlp[l][None, :],
                    g_final[None, :], wout_bf, wfc1_bf, wfc2_bf, final_norm=(l == depth - 1))
    return x
```

```python
import functools
import math

import jax
import jax.numpy as jnp
from jax import lax
from jax.experimental import pallas as pl
from jax.experimental.pallas import tpu as pltpu

F32 = jnp.float32
BF16 = jnp.bfloat16

HEAD_DIM = 64
ATTN_HEADS = 8
ATTN_WIDTH = ATTN_HEADS * HEAD_DIM
SSM_GROUPS = 32
SSM_GROUP_CH = 16
SSM_WIDTH = SSM_GROUPS * SSM_GROUP_CH
SSM_STATE = 64
MOBA_BLOCK = 256
MOBA_TOPK = 3
ROPE_THETA = 10000.0
EPS = 1e-6

LANES = 128
SUBLANES = 8
N_SEG = SUBLANES
HEADS_PER_TILE = LANES // HEAD_DIM
MASK_VALUE = -(2.0 ** 100)
VMEM_LIMIT_BYTES = 56 * 1024 * 1024


def _split_bf16(a):
    hi = a.astype(BF16)
    lo = (a - hi.astype(F32)).astype(BF16)
    return hi, lo


def _dot(a, b):
    return jnp.dot(a, b, preferred_element_type=F32)


def _rmsnorm(x, g):
    return x * lax.rsqrt(jnp.mean(x * x, axis=-1, keepdims=True) + EPS) * g


def _params(*semantics):
    return pltpu.CompilerParams(dimension_semantics=semantics, vmem_limit_bytes=VMEM_LIMIT_BYTES)


ADALN_STEPS = 4


def _adaln_kernel(c_ref, w_ref, b_ref, win_ref, o_ref, win_hi_ref, win_lo_ref):
    c = c_ref[...]
    s_hi, s_lo = _split_bf16(c * jax.nn.sigmoid(c))
    w_hi, w_lo = _split_bf16(w_ref[...])
    o_ref[...] = _dot(s_hi, w_hi) + _dot(s_lo, w_hi) + _dot(s_hi, w_lo) + b_ref[...]
    win_hi_ref[...], win_lo_ref[...] = _split_bf16(win_ref[...])


def _adaln(c_pad, w, b, w_in):
    rows, d = c_pad.shape
    n = w.shape[1]
    tn = n // ADALN_STEPS
    tw = w_in.shape[1] // ADALN_STEPS
    slab = pl.BlockSpec((d, tw), lambda j: (0, j))
    return pl.pallas_call(
        _adaln_kernel,
        grid=(ADALN_STEPS,),
        in_specs=[pl.BlockSpec((rows, d), lambda j: (0, 0)),
                  pl.BlockSpec((d, tn), lambda j: (0, j)),
                  pl.BlockSpec((1, tn), lambda j: (0, j)),
                  slab],
        out_specs=[pl.BlockSpec((rows, tn), lambda j: (0, j)), slab, slab],
        out_shape=[jax.ShapeDtypeStruct((rows, n), F32),
                   jax.ShapeDtypeStruct(w_in.shape, BF16),
                   jax.ShapeDtypeStruct(w_in.shape, BF16)],
        compiler_params=_params("arbitrary"),
        name="adaln",
    )(c_pad, w, b, w_in)


VT_ROWS = HEAD_DIM + 16


def _top_blocks(gate, blk, past, own):
    nb = gate.shape[0]
    g = jnp.where(past, gate, -jnp.inf)
    chosen = None
    for _ in range(MOBA_TOPK):
        best = jnp.max(g, axis=0, keepdims=True)
        first = jnp.min(jnp.where(g == best, blk, float(nb)), axis=0, keepdims=True)
        pick = blk == first
        chosen = pick if chosen is None else (chosen | pick)
        g = jnp.where(pick, -jnp.inf, g)
    return (chosen & past) | own


INPROJ_BLOCKS = 4


def _inproj_route(i, slot, qk, vu, cos_ref, sin_ref, qt_ref, ka_ref, vt_ref, u_ref, ut_ref, km_scr):
    tm = qk.shape[0]
    nb = km_scr.shape[0]
    rows = slice(slot * tm, (slot + 1) * tm)
    u = vu[:, ATTN_WIDTH:]
    u_ref[0, rows] = u
    ut_ref[0, slot] = u.T.astype(BF16)

    reps = ATTN_WIDTH // LANES
    cos = jnp.concatenate([cos_ref[rows]] * reps, axis=1)
    sin = jnp.concatenate([sin_ref[rows]] * reps, axis=1)
    wide_lane = lax.broadcasted_iota(jnp.int32, (tm, ATTN_WIDTH), 1)
    first_half = (wide_lane & (HEAD_DIM // 2)) == 0

    def rope(t):
        partner = jnp.where(first_half,
                            pltpu.roll(t, ATTN_WIDTH - HEAD_DIM // 2, 1),
                            pltpu.roll(t, HEAD_DIM // 2, 1))
        return t * cos + partner * sin

    q = rope(qk[:, :ATTN_WIDTH])
    k = rope(qk[:, ATTN_WIDTH:])
    v = vu[:, :ATTN_WIDTH]

    km_scr[pl.ds(i, 1), :] = jnp.sum(k, axis=0, keepdims=True) * (1.0 / tm)
    km_all = km_scr[...]

    lane = lax.broadcasted_iota(jnp.int32, (tm, LANES), 1)
    km_lane = lax.broadcasted_iota(jnp.int32, (nb, LANES), 1)
    blk_i = lax.broadcasted_iota(jnp.int32, (nb, tm), 0)
    blk = blk_i.astype(F32)
    past = blk_i < i
    own = blk_i == i
    ones_rows = jnp.where(lax.broadcasted_iota(jnp.int32, (VT_ROWS - HEAD_DIM, tm), 0) == 0, 1.0, 0.0)
    onehot = jnp.where(lane == HEAD_DIM + i, 1.0, 0.0)
    scale = HEAD_DIM ** -0.5 * math.log2(math.e)
    for tile in range(ATTN_WIDTH // LANES):
        cols = slice(tile * LANES, (tile + 1) * LANES)
        q_t = q[:, cols].T
        v_t = v[:, cols].T
        k_tile = k[:, cols]
        km = km_all[:, cols]
        qt_hi, qt_lo = _split_bf16(q_t)
        km_heads = jnp.concatenate([jnp.where(km_lane // HEAD_DIM == hh, km, 0.0)
                                    for hh in range(HEADS_PER_TILE)], axis=0)
        kmh_hi, kmh_lo = _split_bf16(km_heads)
        gates = _dot(kmh_hi, qt_hi) + _dot(kmh_lo, qt_hi) + _dot(kmh_hi, qt_lo)
        for hh in range(HEADS_PER_TILE):
            h_idx = tile * HEADS_PER_TILE + hh
            gate = gates[hh * nb:(hh + 1) * nb]
            bias_t = jnp.where(_top_blocks(gate, blk, past, own), 0.0, MASK_VALUE)
            head_rows = slice(hh * HEAD_DIM, (hh + 1) * HEAD_DIM)
            qt_ref[0, h_idx, slot] = jnp.concatenate(
                [q_t[head_rows] * scale, bias_t, jnp.zeros((LANES - HEAD_DIM - nb, tm), F32)],
                axis=0).astype(BF16)
            vt_ref[0, h_idx, slot] = jnp.concatenate([v_t[head_rows], ones_rows], axis=0).astype(BF16)
            kh = jnp.where(lane // HEAD_DIM == hh, k_tile, 0.0)
            if hh:
                kh = pltpu.roll(kh, LANES - hh * HEAD_DIM, 1)
            ka_ref[0, h_idx, rows] = jnp.where(lane < HEAD_DIM, kh, onehot).astype(BF16)


def _inproj_kernel(x_ref, mod_ref, g_ref, wqk_hi_ref, wqk_lo_ref, wvu_ref, cos_ref, sin_ref,
                   wout_ref, wfc1_ref, wfc2_ref,
                   qt_ref, ka_ref, vt_ref, u_ref, ut_ref, wout_bf_ref, wfc1_bf_ref, wfc2_bf_ref, km_scr):
    step = pl.program_id(1)

    @pl.when(step == 0)
    def _():
        km_scr[...] = jnp.zeros_like(km_scr)

    wout_bf_ref[...] = wout_ref[...].astype(BF16)
    wfc1_bf_ref[...] = wfc1_ref[...].astype(BF16)
    wfc2_bf_ref[...] = wfc2_ref[...].astype(BF16)
    d = x_ref.shape[2]
    tm = x_ref.shape[1] // INPROJ_BLOCKS
    sh1 = mod_ref[0, :, 0:d]
    sc1 = mod_ref[0, :, d:2 * d]
    hs = []
    for slot in range(INPROJ_BLOCKS):
        x = x_ref[0, slot * tm:(slot + 1) * tm, :]
        hs.append(_split_bf16(_rmsnorm(x, g_ref[...]) * (1.0 + sc1) + sh1))
    prj = []
    for h_hi, h_lo in hs:
        qk = (_dot(h_hi, wqk_hi_ref[...]) + _dot(h_lo, wqk_hi_ref[...])
              + _dot(h_hi, wqk_lo_ref[...]))
        prj.append((qk, _dot(h_hi, wvu_ref[...])))

    for slot, (qk, vu) in enumerate(prj):
        _inproj_route(step * INPROJ_BLOCKS + slot, slot, qk, vu, cos_ref, sin_ref,
                      qt_ref, ka_ref, vt_ref, u_ref, ut_ref, km_scr)


def _inproj(x, mod3, g_mix, win_hi, win_lo, cos_t, sin_t, w_out, w_fc1, w_fc2):
    bsz, seq, d = x.shape
    tm = MOBA_BLOCK
    nb = seq // tm
    assert nb % SUBLANES == 0 and nb <= LANES - HEAD_DIM and nb % INPROJ_BLOCKS == 0
    nblk = INPROJ_BLOCKS
    rows = nblk * tm
    const = lambda b, t: (0, 0)
    wqk = 2 * ATTN_WIDTH
    assert win_hi.shape[1] == 2 * wqk
    per_batch = nb // nblk
    steps = bsz * per_batch

    def slab(w):
        n = w.shape[0] // steps
        assert n * steps == w.shape[0] and n % (2 * SUBLANES) == 0
        return pl.BlockSpec((n, w.shape[1]), lambda b, t: (b * per_batch + t, 0))

    casts = [w_out, w_fc1, w_fc2]
    return pl.pallas_call(
        _inproj_kernel,
        grid=(bsz, per_batch),
        in_specs=[pl.BlockSpec((1, rows, d), lambda b, t: (b, t, 0)),
                  pl.BlockSpec((1, 1, mod3.shape[-1]), lambda b, t: (b, 0, 0)),
                  pl.BlockSpec((1, d), const),
                  pl.BlockSpec((d, wqk), const),
                  pl.BlockSpec((d, wqk), const),
                  pl.BlockSpec((d, wqk), lambda b, t: (0, 1)),
                  pl.BlockSpec((rows, LANES), lambda b, t: (t, 0)),
                  pl.BlockSpec((rows, LANES), lambda b, t: (t, 0))] + [slab(w) for w in casts],
        out_specs=[pl.BlockSpec((1, ATTN_HEADS, nblk, LANES, tm), lambda b, t: (b, 0, t, 0, 0)),
                   pl.BlockSpec((1, ATTN_HEADS, rows, LANES), lambda b, t: (b, 0, t, 0)),
                   pl.BlockSpec((1, ATTN_HEADS, nblk, VT_ROWS, tm), lambda b, t: (b, 0, t, 0, 0)),
                   pl.BlockSpec((1, rows, SSM_WIDTH), lambda b, t: (b, t, 0)),
                   pl.BlockSpec((1, nblk, SSM_WIDTH, tm), lambda b, t: (b, t, 0, 0))] + [slab(w) for w in casts],
        out_shape=[jax.ShapeDtypeStruct((bsz, ATTN_HEADS, nb, LANES, tm), BF16),
                   jax.ShapeDtypeStruct((bsz, ATTN_HEADS, seq, LANES), BF16),
                   jax.ShapeDtypeStruct((bsz, ATTN_HEADS, nb, VT_ROWS, tm), BF16),
                   jax.ShapeDtypeStruct((bsz, seq, SSM_WIDTH), F32),
                   jax.ShapeDtypeStruct((bsz, nb, SSM_WIDTH, tm), BF16)]
                  + [jax.ShapeDtypeStruct(w.shape, BF16) for w in casts],
        scratch_shapes=[pltpu.VMEM((nb, ATTN_WIDTH), F32)],
        compiler_params=_params("arbitrary", "arbitrary"),
        name="inproj",
    )(x, mod3, g_mix, win_hi, win_lo, win_hi, cos_t, sin_t, *casts)


ATTN_QBLOCKS = 2
RUNNING_MAX_INIT = -3.0e38


def _attn_kernel(qt_ref, qt_next_ref, ka_ref, vt_ref, o_ref, sa_scr, sb_scr, m_scr, acc_scr):
    pair = pl.program_id(1)
    i = ATTN_QBLOCKS * pair
    nblocks = vt_ref.shape[2]
    nh = qt_ref.shape[1]
    tq = qt_ref.shape[4]
    key = lax.broadcasted_iota(jnp.int32, (tq, tq), 0)
    qry = lax.broadcasted_iota(jnp.int32, (tq, tq), 1)
    causal = key <= qry

    def scores(qb, h, j, s_scr, mask=None, q_ref=qt_ref):
        start = pl.multiple_of(j * tq, tq)
        s = _dot(ka_ref[0, h, pl.ds(start, tq), :], q_ref[0, h, qb])
        if mask is not None:
            s = jnp.where(mask, s, MASK_VALUE)
        s_scr[qb * nh + h] = s

    def consume(qb, h, j, s_scr):
        c = qb * nh + h
        s = s_scr[c]
        m_old = m_scr[c]
        m_new = jnp.maximum(m_old, jnp.max(s, axis=0, keepdims=True))
        alpha = jnp.exp2(m_old - m_new)
        p = jnp.exp2(s - m_new).astype(BF16)
        acc_scr[c] = alpha * acc_scr[c] + _dot(vt_ref[0, h, j], p)
        m_scr[c] = m_new

    @pl.when(pair == 0)
    def _():
        for h in range(nh):
            scores(0, h, i, sa_scr, causal)
            scores(1, h, i + 1, sa_scr, causal)

    m_scr[...] = jnp.full(m_scr.shape, RUNNING_MAX_INIT, F32)
    acc_scr[...] = jnp.zeros(acc_scr.shape, F32)
    for h in range(nh):
        scores(1, h, i, sb_scr)
        consume(1, h, i + 1, sa_scr)

    def two_blocks(j0):
        j1 = j0 + 1
        prev = jnp.where(j0 == 0, i, j0 - 1)
        for h in range(nh):
            scores(0, h, j0, sb_scr)
            consume(0, h, prev, sa_scr)
            scores(1, h, j0, sa_scr)
            consume(1, h, prev, sb_scr)
        for h in range(nh):
            scores(0, h, j1, sa_scr)
            consume(0, h, j0, sb_scr)
            scores(1, h, j1, sb_scr)
            consume(1, h, j0, sa_scr)

    def body(t, carry):
        for k in range(0, 8, 2):
            two_blocks(8 * t + k)
        return carry

    lax.fori_loop(0, pair // 4, body, 0)

    @pl.when(pair % 4 >= 2)
    def _():
        two_blocks(8 * (pair // 4))
        two_blocks(8 * (pair // 4) + 2)

    @pl.when(pair % 2 == 1)
    def _():
        two_blocks(i - 2)

    last = jnp.where(pair == 0, i, i - 1)
    nxt = jnp.minimum(i + ATTN_QBLOCKS, nblocks - ATTN_QBLOCKS)
    for h in range(nh):
        consume(0, h, last, sa_scr)
        consume(1, h, last, sb_scr)
        scores(0, h, nxt, sa_scr, causal, qt_next_ref)
        scores(1, h, nxt + 1, sa_scr, causal, qt_next_ref)

    for qb in range(ATTN_QBLOCKS):
        for vt in range(nh // HEADS_PER_TILE):
            rows = []
            for hh in range(HEADS_PER_TILE):
                acc = acc_scr[qb * nh + vt * HEADS_PER_TILE + hh]
                rows.append(acc[0:HEAD_DIM] / acc[HEAD_DIM:HEAD_DIM + 1])
            o_ref[0, qb * tq:(qb + 1) * tq, vt * LANES:(vt + 1) * LANES] = jnp.concatenate(rows, axis=0).T


def _attention(q_t, k_aug, v_t):
    bsz, nh, nb, _, tq = q_t.shape
    seq = nb * tq
    assert nb % ATTN_QBLOCKS == 0
    chains = ATTN_QBLOCKS * nh
    resident = pl.Buffered(1)
    return pl.pallas_call(
        _attn_kernel,
        grid=(bsz, nb // ATTN_QBLOCKS),
        in_specs=[pl.BlockSpec((1, nh, ATTN_QBLOCKS, LANES, tq), lambda b, g: (b, 0, g, 0, 0)),
                  pl.BlockSpec((1, nh, ATTN_QBLOCKS, LANES, tq),
                               lambda b, g: (b, 0, jnp.minimum(g + 1, nb // ATTN_QBLOCKS - 1), 0, 0)),
                  pl.BlockSpec((1, nh, seq, LANES), lambda b, g: (b, 0, 0, 0), pipeline_mode=resident),
                  pl.BlockSpec((1, nh, nb, VT_ROWS, tq), lambda b, g: (b, 0, 0, 0, 0), pipeline_mode=resident)],
        out_specs=pl.BlockSpec((1, ATTN_QBLOCKS * tq, nh * HEAD_DIM), lambda b, g: (b, g, 0)),
        out_shape=jax.ShapeDtypeStruct((bsz, seq, nh * HEAD_DIM), F32),
        scratch_shapes=[pltpu.VMEM((chains, tq, tq), F32), pltpu.VMEM((chains, tq, tq), F32),
                        pltpu.VMEM((chains, 1, tq), F32), pltpu.VMEM((chains, VT_ROWS, tq), F32)],
        compiler_params=_params("arbitrary", "arbitrary"),
        name="attn",
    )(q_t, q_t, k_aug, v_t)


S5_STEPS = 128
S5_SUB_STEPS = 64
S5_CHUNKS = 4
S5_CHUNK_GROUPS = SSM_GROUPS // S5_CHUNKS
S5_CHUNK_STATES = S5_CHUNK_GROUPS * SSM_STATE
S5_CHUNK_LANES = S5_CHUNK_GROUPS * SSM_GROUP_CH


S5_CARRY_GROUPS = LANES // SSM_STATE


def _cmul(ar, ai, br, bi):
    return ar * br - ai * bi, ar * bi + ai * br


def _s5_carry_kernel(ut_ref, lam_ref, bt_ref, o_ref):
    bsz, nblk, nrows, tm = ut_ref.shape
    nch = nrows // S5_CARRY_GROUPS
    seg_blocks = nblk // N_SEG
    ls = seg_blocks * tm
    reps = bsz * N_SEG
    sub = lax.broadcasted_iota(jnp.int32, (SUBLANES, LANES), 0)
    lr = jnp.broadcast_to(lam_ref[0, 0:1], (SUBLANES, LANES))
    li = jnp.broadcast_to(lam_ref[0, 1:2], (SUBLANES, LANES))
    pr = jnp.ones((SUBLANES, LANES), F32)
    pi = jnp.zeros((SUBLANES, LANES), F32)
    tr = jnp.zeros((SUBLANES, LANES), F32)
    ti = jnp.zeros((SUBLANES, LANES), F32)
    for r in range(SUBLANES):
        tr = jnp.where(sub == SUBLANES - 1 - r, pr, tr)
        ti = jnp.where(sub == SUBLANES - 1 - r, pi, ti)
        pr, pi = _cmul(pr, pi, lr, li)
    span = SUBLANES
    while span < ls:
        nr, ni = _cmul(tr, ti, pr[0:1], pi[0:1])
        tr = jnp.concatenate([nr, tr], axis=0)
        ti = jnp.concatenate([ni, ti], axis=0)
        pr, pi = _cmul(pr, pi, pr, pi)
        span *= 2
    def seg_rows(g, b, s):
        return jnp.concatenate([ut_ref[b, s * seg_blocks + k, g * nch:(g + 1) * nch, :]
                                for k in range(seg_blocks)], axis=1)

    lhs = jnp.concatenate([seg_rows(g, b, s)
                           for g in range(S5_CARRY_GROUPS) for b in range(bsz) for s in range(N_SEG)],
                          axis=0)
    gr = _dot(lhs, tr.astype(BF16))
    gi = _dot(lhs, ti.astype(BF16))
    btr = jnp.concatenate([bt_ref[0, 0]] * (reps * S5_CARRY_GROUPS), axis=0)
    bti = jnp.concatenate([bt_ref[0, 1]] * (reps * S5_CARRY_GROUPS), axis=0)
    wr, wi = _cmul(gr, gi, btr, bti)
    fr = wr.reshape(reps * S5_CARRY_GROUPS, nch, LANES).sum(axis=1)
    fi = wi.reshape(reps * S5_CARRY_GROUPS, nch, LANES).sum(axis=1)
    grp = lax.broadcasted_iota(jnp.int32, (reps, LANES), 1) // SSM_STATE
    fin_r, fin_i = fr[0:reps], fi[0:reps]
    for g in range(1, S5_CARRY_GROUPS):
        fin_r = jnp.where(grp == g, fr[g * reps:(g + 1) * reps], fin_r)
        fin_i = jnp.where(grp == g, fi[g * reps:(g + 1) * reps], fin_i)
    rows_r, rows_i = [], []
    for b in range(bsz):
        sr = jnp.zeros((1, LANES), F32)
        si = jnp.zeros((1, LANES), F32)
        rows_r.append(sr)
        rows_i.append(si)
        for seg in range(1, N_SEG):
            r = b * N_SEG + seg - 1
            mr, mi = _cmul(sr, si, pr[0:1], pi[0:1])
            sr, si = mr + fin_r[r:r + 1], mi + fin_i[r:r + 1]
            rows_r.append(sr)
            rows_i.append(si)
    o_ref[0, 0] = jnp.concatenate(rows_r, axis=0)
    o_ref[0, 1] = jnp.concatenate(rows_i, axis=0)


def _s5_main_kernel(u_ref, bc_ref, cre_ref, cim_ref, lam_ref, sinit_ref, d_ref, wglu_ref, bglu_ref, o_ref,
                    il_scr, bu_scr, st_scr, y_scr):
    jb = pl.program_id(1)
    nseg, steps, width = u_ref.shape
    cs = S5_CHUNK_STATES
    nsub = steps // S5_SUB_STEPS
    srows = S5_SUB_STEPS * nseg

    @pl.when(jb == 0)
    def _():
        st_scr[...] = sinit_ref[...]

    for j in range(steps):
        il_scr[j * nseg:(j + 1) * nseg, :] = u_ref[:, j, :]
    us = [il_scr[sb * srows:(sb + 1) * srows, :] for sb in range(nsub)]
    for sb in range(nsub):
        u_bf = us[sb].astype(BF16)
        for c in range(S5_CHUNKS):
            bu_scr[c, sb * srows:(sb + 1) * srows, :] = _dot(
                u_bf[:, c * S5_CHUNK_LANES:(c + 1) * S5_CHUNK_LANES], bc_ref[c])
    state = [(st_scr[c, :, 0:cs], st_scr[c, :, cs:2 * cs]) for c in range(S5_CHUNKS)]
    for sb in range(nsub):
        sub_rows = slice(sb * srows, (sb + 1) * srows)
        ys = []
        for c in range(S5_CHUNKS):
            lr = jnp.broadcast_to(lam_ref[c, 0:1, :], (nseg, cs))
            li = jnp.broadcast_to(lam_ref[c, 1:2, :], (nseg, cs))
            xr, xi = state[c]
            for j in range(sb * S5_SUB_STEPS, (sb + 1) * S5_SUB_STEPS):
                rows = slice(j * nseg, (j + 1) * nseg)
                xr, xi = (lr * xr - li * xi + bu_scr[c, rows, 0:cs],
                          lr * xi + li * xr + bu_scr[c, rows, cs:2 * cs])
                bu_scr[c, rows, 0:cs] = xr
                bu_scr[c, rows, cs:2 * cs] = xi
            state[c] = (xr, xi)
            ys.append(_dot(bu_scr[c, sub_rows, 0:cs].astype(BF16), cre_ref[c])
                      + _dot(bu_scr[c, sub_rows, cs:2 * cs].astype(BF16), cim_ref[c]))
        y = jnp.concatenate(ys, axis=1) + d_ref[...] * us[sb]
        y = y * (0.5 * (1.0 + jnp.tanh(math.sqrt(2.0 / math.pi) * (y + 0.044715 * (y * y * y)))))
        z = _dot(y.astype(BF16), wglu_ref[...]) + bglu_ref[...]
        y = y * jax.nn.sigmoid(z)
        for t in range(width // LANES):
            y_scr[t, sub_rows, :] = y[:, t * LANES:(t + 1) * LANES]
    for c in range(S5_CHUNKS):
        st_scr[c, :, 0:cs] = state[c][0]
        st_scr[c, :, cs:2 * cs] = state[c][1]
    for s in range(nseg):
        for t in range(width // LANES):
            o_ref[s, :, t * LANES:(t + 1) * LANES] = y_scr[t, pl.ds(s, steps, stride=nseg), :]


def _s5(u, u_t, prm, d_skip, w_glu, b_glu):
    bsz, seq, _ = u.shape
    ls = seq // N_SEG
    steps = S5_STEPS
    rows = steps * N_SEG
    ngrp = SSM_GROUPS // S5_CARRY_GROUPS
    carry = pl.pallas_call(
        _s5_carry_kernel,
        grid=(ngrp,),
        in_specs=[pl.BlockSpec((bsz, u_t.shape[1], S5_CARRY_GROUPS * SSM_GROUP_CH, u_t.shape[3]),
                               lambda g: (0, 0, g, 0)),
                  pl.BlockSpec((1, 2, LANES), lambda g: (g, 0, 0)),
                  pl.BlockSpec((1, 2, SSM_GROUP_CH, LANES), lambda g: (g, 0, 0, 0))],
        out_specs=pl.BlockSpec((1, 2, bsz * N_SEG, LANES), lambda g: (g, 0, 0, 0)),
        out_shape=jax.ShapeDtypeStruct((ngrp, 2, bsz * N_SEG, LANES), F32),
        compiler_params=_params("arbitrary"),
        name="s5_carry",
    )(u_t, prm["lam_carry"], prm["bt_carry"])
    sinit = carry.reshape(S5_CHUNKS, ngrp // S5_CHUNKS, 2, bsz, N_SEG, LANES)
    sinit = sinit.transpose(3, 0, 4, 2, 1, 5).reshape(bsz, S5_CHUNKS, N_SEG, 2 * S5_CHUNK_STATES)

    def const(shape):
        return pl.BlockSpec(shape, lambda b, j: (0,) * len(shape))

    blk = pl.BlockSpec((None, N_SEG, steps, SSM_WIDTH), lambda b, j: (b, 0, j, 0))
    out = pl.pallas_call(
        _s5_main_kernel,
        grid=(bsz, ls // steps),
        in_specs=[blk, const(prm["bc"].shape), const(prm["cre"].shape), const(prm["cim"].shape),
                  const(prm["lam_chunk"].shape),
                  pl.BlockSpec((None, S5_CHUNKS, N_SEG, 2 * S5_CHUNK_STATES), lambda b, j: (b, 0, 0, 0)),
                  const(d_skip.shape), const(w_glu.shape), const(b_glu.shape)],
        out_specs=blk,
        out_shape=jax.ShapeDtypeStruct((bsz, N_SEG, ls, SSM_WIDTH), F32),
        scratch_shapes=[pltpu.VMEM((rows, SSM_WIDTH), F32),
                        pltpu.VMEM((S5_CHUNKS, rows, 2 * S5_CHUNK_STATES), F32),
                        pltpu.VMEM((S5_CHUNKS, N_SEG, 2 * S5_CHUNK_STATES), F32),
                        pltpu.VMEM((SSM_WIDTH // LANES, rows, LANES), F32)],
        compiler_params=_params("arbitrary", "arbitrary"),
        name="s5_main",
    )(u.reshape(bsz, N_SEG, ls, SSM_WIDTH), prm["bc"], prm["cre"], prm["cim"], prm["lam_chunk"], sinit,
      d_skip, w_glu, b_glu)
    return out.reshape(bsz, seq, SSM_WIDTH)


FF_CHUNK = 1024


OUTMLP_PARTS = 2


def _outmlp_kernel(x_ref, attn_ref, ssm_ref, mod_ref, ga_ref, gs_ref, gm_ref, gf_ref,
                   wout_ref, w1_ref, w2_ref, o_ref, *, final_norm):
    tm, d = x_ref.shape[1], x_ref.shape[2]
    gt1 = mod_ref[0, :, 2 * d:3 * d]
    sh2 = mod_ref[0, :, 3 * d:4 * d]
    sc2 = mod_ref[0, :, 4 * d:5 * d]
    gt2 = mod_ref[0, :, 5 * d:6 * d]
    ff = w1_ref.shape[1]
    part = tm // OUTMLP_PARTS
    rows = [slice(p * part, (p + 1) * part) for p in range(OUTMLP_PARTS)]
    mixes = []
    for r in rows:
        a = _rmsnorm(attn_ref[0, r, :], ga_ref[...]).astype(BF16)
        s = _rmsnorm(ssm_ref[0, r, :], gs_ref[...]).astype(BF16)
        mixes.append((a, s))
    x1s = [x_ref[0, r, :] + gt1 * (_dot(a, wout_ref[0:ATTN_WIDTH, :]) + _dot(s, wout_ref[ATTN_WIDTH:, :]))
           for r, (a, s) in zip(rows, mixes)]
    hs = [(_rmsnorm(x1, gm_ref[...]) * (1.0 + sc2) + sh2).astype(BF16) for x1 in x1s]
    for r, x1, h in zip(rows, x1s, hs):
        acc = jnp.zeros((part, d), F32)
        for c in range(ff // FF_CHUNK):
            t = jnp.maximum(_dot(h, w1_ref[:, c * FF_CHUNK:(c + 1) * FF_CHUNK]), 0.0)
            acc = acc + _dot((t * t).astype(BF16), w2_ref[c * FF_CHUNK:(c + 1) * FF_CHUNK, :])
        x2 = x1 + gt2 * acc
        o_ref[0, r, :] = _rmsnorm(x2, gf_ref[...]) if final_norm else x2


OUTMLP_ROWS = 512


def _outmlp(x, attn, ssm, mod3, g_attn, g_ssm, g_mlp, g_final, w_out, w_fc1, w_fc2, final_norm):
    bsz, seq, d = x.shape
    tm = OUTMLP_ROWS
    const = lambda b, t: (0, 0)
    return pl.pallas_call(
        functools.partial(_outmlp_kernel, final_norm=final_norm),
        grid=(bsz, seq // tm),
        in_specs=[pl.BlockSpec((1, tm, d), lambda b, t: (b, t, 0)),
                  pl.BlockSpec((1, tm, ATTN_WIDTH), lambda b, t: (b, t, 0)),
                  pl.BlockSpec((1, tm, SSM_WIDTH), lambda b, t: (b, t, 0)),
                  pl.BlockSpec((1, 1, mod3.shape[-1]), lambda b, t: (b, 0, 0)),
                  pl.BlockSpec(g_attn.shape, const),
                  pl.BlockSpec(g_ssm.shape, const),
                  pl.BlockSpec(g_mlp.shape, const),
                  pl.BlockSpec(g_final.shape, const),
                  pl.BlockSpec(w_out.shape, const),
                  pl.BlockSpec(w_fc1.shape, const),
                  pl.BlockSpec(w_fc2.shape, const)],
        out_specs=pl.BlockSpec((1, tm, d), lambda b, t: (b, t, 0)),
        out_shape=jax.ShapeDtypeStruct((bsz, seq, d), F32),
        compiler_params=_params("arbitrary", "arbitrary"),
        name="outmlp",
    )(x, attn, ssm, mod3, g_attn, g_ssm, g_mlp, g_final, w_out, w_fc1, w_fc2)


def _rope_tables(seq):
    half = HEAD_DIM // 2
    inv_freq = ROPE_THETA ** (-jnp.arange(half, dtype=F32) / half)
    ang = jnp.arange(seq, dtype=F32)[:, None] * inv_freq[None, :]
    cos, sin = jnp.cos(ang), jnp.sin(ang)
    reps = LANES // HEAD_DIM
    cos_t = jnp.tile(cos, (1, 2 * reps))
    sin_t = jnp.tile(jnp.concatenate([-sin, sin], axis=1), (1, reps))
    return cos_t, sin_t


def _s5_params(lam_re, lam_im, log_dt, b_re, b_im, c_re, c_im):
    lr, li = lam_re.astype(F32), lam_im.astype(F32)
    dt = jnp.exp(log_dt.astype(F32))[:, None]
    mag = jnp.exp(lr * dt)
    ar, ai = mag * jnp.cos(li * dt), mag * jnp.sin(li * dt)
    den = lr * lr + li * li
    cr = ((ar - 1.0) * lr + ai * li) / den
    ci = (ai * lr - (ar - 1.0) * li) / den
    bbar_re = cr[..., None] * b_re.astype(F32) - ci[..., None] * b_im.astype(F32)
    bbar_im = cr[..., None] * b_im.astype(F32) + ci[..., None] * b_re.astype(F32)
    eye = jnp.eye(S5_CHUNK_GROUPS, dtype=F32)

    def chunked(m):
        return m.reshape((S5_CHUNKS, S5_CHUNK_GROUPS) + m.shape[1:])

    def diag_in(m):
        return jnp.einsum('cgpn,gh->cgnhp', chunked(m), eye).reshape(S5_CHUNKS, S5_CHUNK_LANES, S5_CHUNK_STATES)

    def diag_out(m):
        return jnp.einsum('cgnp,gh->cgphn', chunked(m), eye).reshape(S5_CHUNKS, S5_CHUNK_STATES, S5_CHUNK_LANES)

    ngrp = SSM_GROUPS // S5_CARRY_GROUPS

    def lane_tiles(m):
        m = m.reshape(ngrp, S5_CARRY_GROUPS, SSM_GROUP_CH, SSM_STATE)
        return m.transpose(0, 2, 1, 3).reshape(ngrp, SSM_GROUP_CH, LANES)

    return {
        "bc": jnp.concatenate([diag_in(bbar_re), diag_in(bbar_im)], axis=2).astype(BF16),
        "cre": diag_out(c_re.astype(F32)).astype(BF16),
        "cim": diag_out(-c_im.astype(F32)).astype(BF16),
        "lam_chunk": jnp.stack([ar.reshape(S5_CHUNKS, S5_CHUNK_STATES),
                                ai.reshape(S5_CHUNKS, S5_CHUNK_STATES)], axis=1),
        "lam_carry": jnp.stack([ar.reshape(ngrp, LANES), ai.reshape(ngrp, LANES)], axis=1),
        "bt_carry": jnp.stack([lane_tiles(bbar_re.transpose(0, 2, 1)),
                               lane_tiles(bbar_im.transpose(0, 2, 1))], axis=1),
    }


def kernel(x, c, w_ada, b_ada, g_mix, w_in, g_attn_out, lam_re, lam_im, log_dt, b_re, b_im, c_re, c_im,
           d_skip, w_glu, b_glu, g_ssm_out, w_out, g_mlp, w_fc1, w_fc2, g_final):
    bsz, seq, d = x.shape
    depth = w_ada.shape[0]
    assert seq % (N_SEG * MOBA_BLOCK) == 0 and (seq // N_SEG) % S5_STEPS == 0 and bsz <= SUBLANES
    cos_t, sin_t = _rope_tables(seq)
    c_pad = jnp.zeros((SUBLANES, d), F32).at[:bsz].set(c.astype(F32))

    for l in range(depth):
        mod, win_hi, win_lo = _adaln(c_pad, w_ada[l], b_ada[l][None, :], w_in[l])
        mod3 = mod[:bsz, None, :]

        q_t, k_aug, v_t, u, u_t, wout_bf, wfc1_bf, wfc2_bf = _inproj(
            x, mod3, g_mix[l][None, :], win_hi, win_lo, cos_t, sin_t, w_out[l], w_fc1[l], w_fc2[l])
        attn = _attention(q_t, k_aug, v_t)

        prm = _s5_params(lam_re[l], lam_im[l], log_dt[l], b_re[l], b_im[l], c_re[l], c_im[l])
        ssm = _s5(u, u_t, prm, d_skip[l].reshape(1, SSM_WIDTH), w_glu[l].astype(BF16), b_glu[l][None, :])

        x = _outmlp(x, attn, ssm, mod3, g_attn_out[l][None, :], g_ssm_out[l][None, :], g_mlp[l][None, :],
                    g_final[None, :], wout_bf, wfc1_bf, wfc2_bf, final_norm=(l == depth - 1))
    return x
```

```python
import functools
import math

import jax
import jax.numpy as jnp
from jax import lax
from jax.experimental import pallas as pl
from jax.experimental.pallas import tpu as pltpu

F32 = jnp.float32
BF16 = jnp.bfloat16

HEAD_DIM = 64
ATTN_HEADS = 8
ATTN_WIDTH = ATTN_HEADS * HEAD_DIM
SSM_GROUPS = 32
SSM_GROUP_CH = 16
SSM_WIDTH = SSM_GROUPS * SSM_GROUP_CH
SSM_STATE = 64
MOBA_BLOCK = 256
MOBA_TOPK = 3
ROPE_THETA = 10000.0
EPS = 1e-6

LANES = 128
SUBLANES = 8
N_SEG = SUBLANES
HEADS_PER_TILE = LANES // HEAD_DIM
MASK_VALUE = -(2.0 ** 100)
VMEM_LIMIT_BYTES = 56 * 1024 * 1024


def _split_bf16(a):
    hi = a.astype(BF16)
    lo = (a - hi.astype(F32)).astype(BF16)
    return hi, lo


def _dot(a, b):
    return jnp.dot(a, b, preferred_element_type=F32)


def _rmsnorm(x, g):
    return x * lax.rsqrt(jnp.mean(x * x, axis=-1, keepdims=True) + EPS) * g


def _params(*semantics):
    return pltpu.CompilerParams(dimension_semantics=semantics, vmem_limit_bytes=VMEM_LIMIT_BYTES)


ADALN_STEPS = 4


def _adaln_kernel(c_ref, w_ref, b_ref, win_ref, o_ref, win_hi_ref, win_lo_ref):
    c = c_ref[...]
    s_hi, s_lo = _split_bf16(c * jax.nn.sigmoid(c))
    w_hi, w_lo = _split_bf16(w_ref[...])
    o_ref[...] = _dot(s_hi, w_hi) + _dot(s_lo, w_hi) + _dot(s_hi, w_lo) + b_ref[...]
    win_hi_ref[...], win_lo_ref[...] = _split_bf16(win_ref[...])


def _adaln(c_pad, w, b, w_in):
    rows, d = c_pad.shape
    n = w.shape[1]
    tn = n // ADALN_STEPS
    tw = w_in.shape[1] // ADALN_STEPS
    slab = pl.BlockSpec((d, tw), lambda j: (0, j))
    return pl.pallas_call(
        _adaln_kernel,
        grid=(ADALN_STEPS,),
        in_specs=[pl.BlockSpec((rows, d), lambda j: (0, 0)),
                  pl.BlockSpec((d, tn), lambda j: (0, j)),
                  pl.BlockSpec((1, tn), lambda j: (0, j)),
                  slab],
        out_specs=[pl.BlockSpec((rows, tn), lambda j: (0, j)), slab, slab],
        out_shape=[jax.ShapeDtypeStruct((rows, n), F32),
                   jax.ShapeDtypeStruct(w_in.shape, BF16),
                   jax.ShapeDtypeStruct(w_in.shape, BF16)],
        compiler_params=_params("arbitrary"),
        name="adaln",
    )(c_pad, w, b, w_in)


VT_ROWS = HEAD_DIM + 16


def _top_blocks(gate, blk, past, own):
    nb = gate.shape[0]
    g = jnp.where(past, gate, -jnp.inf)
    chosen = None
    for _ in range(MOBA_TOPK):
        best = jnp.max(g, axis=0, keepdims=True)
        first = jnp.min(jnp.where(g == best, blk, float(nb)), axis=0, keepdims=True)
        pick = blk == first
        chosen = pick if chosen is None else (chosen | pick)
        g = jnp.where(pick, -jnp.inf, g)
    return (chosen & past) | own


INPROJ_BLOCKS = 4


def _inproj_route(i, slot, qk, vu, cos_ref, sin_ref, qt_ref, ka_ref, vt_ref, u_ref, ut_ref, km_scr):
    tm = qk.shape[0]
    nb = km_scr.shape[0]
    rows = slice(slot * tm, (slot + 1) * tm)
    u = vu[:, ATTN_WIDTH:]
    u_ref[0, rows] = u
    ut_ref[0, slot] = u.T.astype(BF16)

    reps = ATTN_WIDTH // LANES
    cos = jnp.concatenate([cos_ref[rows]] * reps, axis=1)
    sin = jnp.concatenate([sin_ref[rows]] * reps, axis=1)
    wide_lane = lax.broadcasted_iota(jnp.int32, (tm, ATTN_WIDTH), 1)
    first_half = (wide_lane & (HEAD_DIM // 2)) == 0

    def rope(t):
        partner = jnp.where(first_half,
                            pltpu.roll(t, ATTN_WIDTH - HEAD_DIM // 2, 1),
                            pltpu.roll(t, HEAD_DIM // 2, 1))
        return t * cos + partner * sin

    q = rope(qk[:, :ATTN_WIDTH])
    k = rope(qk[:, ATTN_WIDTH:])
    v = vu[:, :ATTN_WIDTH]

    km_scr[pl.ds(i, 1), :] = jnp.sum(k, axis=0, keepdims=True) * (1.0 / tm)
    km_all = km_scr[...]

    lane = lax.broadcasted_iota(jnp.int32, (tm, LANES), 1)
    km_lane = lax.broadcasted_iota(jnp.int32, (nb, LANES), 1)
    blk_i = lax.broadcasted_iota(jnp.int32, (nb, tm), 0)
    blk = blk_i.astype(F32)
    past = blk_i < i
    own = blk_i == i
    ones_rows = jnp.where(lax.broadcasted_iota(jnp.int32, (VT_ROWS - HEAD_DIM, tm), 0) == 0, 1.0, 0.0)
    onehot = jnp.where(lane == HEAD_DIM + i, 1.0, 0.0)
    scale = HEAD_DIM ** -0.5 * math.log2(math.e)
    for tile in range(ATTN_WIDTH // LANES):
        cols = slice(tile * LANES, (tile + 1) * LANES)
        q_t = q[:, cols].T
        v_t = v[:, cols].T
        k_tile = k[:, cols]
        km = km_all[:, cols]
        qt_hi, qt_lo = _split_bf16(q_t)
        km_heads = jnp.concatenate([jnp.where(km_lane // HEAD_DIM == hh, km, 0.0)
                                    for hh in range(HEADS_PER_TILE)], axis=0)
        kmh_hi, kmh_lo = _split_bf16(km_heads)
        gates = _dot(kmh_hi, qt_hi) + _dot(kmh_lo, qt_hi) + _dot(kmh_hi, qt_lo)
        for hh in range(HEADS_PER_TILE):
            h_idx = tile * HEADS_PER_TILE + hh
            gate = gates[hh * nb:(hh + 1) * nb]
            bias_t = jnp.where(_top_blocks(gate, blk, past, own), 0.0, MASK_VALUE)
            head_rows = slice(hh * HEAD_DIM, (hh + 1) * HEAD_DIM)
            qt_ref[0, h_idx, slot] = jnp.concatenate(
                [q_t[head_rows] * scale, bias_t, jnp.zeros((LANES - HEAD_DIM - nb, tm), F32)],
                axis=0).astype(BF16)
            vt_ref[0, h_idx, slot] = jnp.concatenate([v_t[head_rows], ones_rows], axis=0).astype(BF16)
            kh = jnp.where(lane // HEAD_DIM == hh, k_tile, 0.0)
            if hh:
                kh = pltpu.roll(kh, LANES - hh * HEAD_DIM, 1)
            ka_ref[0, h_idx, rows] = jnp.where(lane < HEAD_DIM, kh, onehot).astype(BF16)


def _inproj_kernel(x_ref, mod_ref, g_ref, wqk_hi_ref, wqk_lo_ref, wvu_ref, cos_ref, sin_ref,
                   wout_ref, wfc1_ref, wfc2_ref,
                   qt_ref, ka_ref, vt_ref, u_ref, ut_ref, wout_bf_ref, wfc1_bf_ref, wfc2_bf_ref, km_scr):
    step = pl.program_id(1)

    @pl.when(step == 0)
    def _():
        km_scr[...] = jnp.zeros_like(km_scr)

    wout_bf_ref[...] = wout_ref[...].astype(BF16)
    wfc1_bf_ref[...] = wfc1_ref[...].astype(BF16)
    wfc2_bf_ref[...] = wfc2_ref[...].astype(BF16)
    d = x_ref.shape[2]
    tm = x_ref.shape[1] // INPROJ_BLOCKS
    sh1 = mod_ref[0, :, 0:d]
    sc1 = mod_ref[0, :, d:2 * d]
    hs = []
    for slot in range(INPROJ_BLOCKS):
        x = x_ref[0, slot * tm:(slot + 1) * tm, :]
        hs.append(_split_bf16(_rmsnorm(x, g_ref[...]) * (1.0 + sc1) + sh1))
    prj = []
    for h_hi, h_lo in hs:
        qk = (_dot(h_hi, wqk_hi_ref[...]) + _dot(h_lo, wqk_hi_ref[...])
              + _dot(h_hi, wqk_lo_ref[...]))
        prj.append((qk, _dot(h_hi, wvu_ref[...])))

    for slot, (qk, vu) in enumerate(prj):
        _inproj_route(step * INPROJ_BLOCKS + slot, slot, qk, vu, cos_ref, sin_ref,
                      qt_ref, ka_ref, vt_ref, u_ref, ut_ref, km_scr)


def _inproj(x, mod3, g_mix, win_hi, win_lo, cos_t, sin_t, w_out, w_fc1, w_fc2):
    bsz, seq, d = x.shape
    tm = MOBA_BLOCK
    nb = seq // tm
    assert nb % SUBLANES == 0 and nb <= LANES - HEAD_DIM and nb % INPROJ_BLOCKS == 0
    nblk = INPROJ_BLOCKS
    rows = nblk * tm
    const = lambda b, t: (0, 0)
    wqk = 2 * ATTN_WIDTH
    assert win_hi.shape[1] == 2 * wqk
    per_batch = nb // nblk
    steps = bsz * per_batch

    def slab(w):
        n = w.shape[0] // steps
        assert n * steps == w.shape[0] and n % (2 * SUBLANES) == 0
        return pl.BlockSpec((n, w.shape[1]), lambda b, t: (b * per_batch + t, 0))

    casts = [w_out, w_fc1, w_fc2]
    return pl.pallas_call(
        _inproj_kernel,
        grid=(bsz, per_batch),
        in_specs=[pl.BlockSpec((1, rows, d), lambda b, t: (b, t, 0)),
                  pl.BlockSpec((1, 1, mod3.shape[-1]), lambda b, t: (b, 0, 0)),
                  pl.BlockSpec((1, d), const),
                  pl.BlockSpec((d, wqk), const),
                  pl.BlockSpec((d, wqk), const),
                  pl.BlockSpec((d, wqk), lambda b, t: (0, 1)),
                  pl.BlockSpec((rows, LANES), lambda b, t: (t, 0)),
                  pl.BlockSpec((rows, LANES), lambda b, t: (t, 0))] + [slab(w) for w in casts],
        out_specs=[pl.BlockSpec((1, ATTN_HEADS, nblk, LANES, tm), lambda b, t: (b, 0, t, 0, 0)),
                   pl.BlockSpec((1, ATTN_HEADS, rows, LANES), lambda b, t: (b, 0, t, 0)),
                   pl.BlockSpec((1, ATTN_HEADS, nblk, VT_ROWS, tm), lambda b, t: (b, 0, t, 0, 0)),
                   pl.BlockSpec((1, rows, SSM_WIDTH), lambda b, t: (b, t, 0)),
                   pl.BlockSpec((1, nblk, SSM_WIDTH, tm), lambda b, t: (b, t, 0, 0))] + [slab(w) for w in casts],
        out_shape=[jax.ShapeDtypeStruct((bsz, ATTN_HEADS, nb, LANES, tm), BF16),
                   jax.ShapeDtypeStruct((bsz, ATTN_HEADS, seq, LANES), BF16),
                   jax.ShapeDtypeStruct((bsz, ATTN_HEADS, nb, VT_ROWS, tm), BF16),
                   jax.ShapeDtypeStruct((bsz, seq, SSM_WIDTH), F32),
                   jax.ShapeDtypeStruct((bsz, nb, SSM_WIDTH, tm), BF16)]
                  + [jax.ShapeDtypeStruct(w.shape, BF16) for w in casts],
        scratch_shapes=[pltpu.VMEM((nb, ATTN_WIDTH), F32)],
        compiler_params=_params("arbitrary", "arbitrary"),
        name="inproj",
    )(x, mod3, g_mix, win_hi, win_lo, win_hi, cos_t, sin_t, *casts)


ATTN_QBLOCKS = 2
RUNNING_MAX_INIT = -3.0e38


def _attn_kernel(qt_ref, qt_next_ref, ka_ref, vt_ref, o_ref, sa_scr, sb_scr, m_scr, acc_scr):
    pair = pl.program_id(1)
    i = ATTN_QBLOCKS * pair
    nblocks = vt_ref.shape[2]
    nh = qt_ref.shape[1]
    tq = qt_ref.shape[4]
    key = lax.broadcasted_iota(jnp.int32, (tq, tq), 0)
    qry = lax.broadcasted_iota(jnp.int32, (tq, tq), 1)
    causal = key <= qry

    def scores(qb, h, j, s_scr, mask=None, q_ref=qt_ref):
        start = pl.multiple_of(j * tq, tq)
        s = _dot(ka_ref[0, h, pl.ds(start, tq), :], q_ref[0, h, qb])
        if mask is not None:
            s = jnp.where(mask, s, MASK_VALUE)
        s_scr[qb * nh + h] = s

    def consume(qb, h, j, s_scr):
        c = qb * nh + h
        s = s_scr[c]
        m_old = m_scr[c]
        m_new = jnp.maximum(m_old, jnp.max(s, axis=0, keepdims=True))
        alpha = jnp.exp2(m_old - m_new)
        p = jnp.exp2(s - m_new).astype(BF16)
        acc_scr[c] = alpha * acc_scr[c] + _dot(vt_ref[0, h, j], p)
        m_scr[c] = m_new

    @pl.when(pair == 0)
    def _():
        for h in range(nh):
            scores(0, h, i, sa_scr, causal)
            scores(1, h, i + 1, sa_scr, causal)

    m_scr[...] = jnp.full(m_scr.shape, RUNNING_MAX_INIT, F32)
    acc_scr[...] = jnp.zeros(acc_scr.shape, F32)
    for h in range(nh):
        scores(1, h, i, sb_scr)
        consume(1, h, i + 1, sa_scr)

    def two_blocks(j0):
        j1 = j0 + 1
        prev = jnp.where(j0 == 0, i, j0 - 1)
        for h in range(nh):
            scores(0, h, j0, sb_scr)
            consume(0, h, prev, sa_scr)
            scores(1, h, j0, sa_scr)
            consume(1, h, prev, sb_scr)
        for h in range(nh):
            scores(0, h, j1, sa_scr)
            consume(0, h, j0, sb_scr)
            scores(1, h, j1, sb_scr)
            consume(1, h, j0, sa_scr)

    def body(t, carry):
        for k in range(0, 8, 2):
            two_blocks(8 * t + k)
        return carry

    lax.fori_loop(0, pair // 4, body, 0)

    @pl.when(pair % 4 >= 2)
    def _():
        two_blocks(8 * (pair // 4))
        two_blocks(8 * (pair // 4) + 2)

    @pl.when(pair % 2 == 1)
    def _():
        two_blocks(i - 2)

    last = jnp.where(pair == 0, i, i - 1)
    nxt = jnp.minimum(i + ATTN_QBLOCKS, nblocks - ATTN_QBLOCKS)
    for h in range(nh):
        consume(0, h, last, sa_scr)
        consume(1, h, last, sb_scr)
        scores(0, h, nxt, sa_scr, causal, qt_next_ref)
        scores(1, h, nxt + 1, sa_scr, causal, qt_next_ref)

    for qb in range(ATTN_QBLOCKS):
        for vt in range(nh // HEADS_PER_TILE):
            rows = []
            for hh in range(HEADS_PER_TILE):
                acc = acc_scr[qb * nh + vt * HEADS_PER_TILE + hh]
                rows.append(acc[0:HEAD_DIM] / acc[HEAD_DIM:HEAD_DIM + 1])
            o_ref[0, qb * tq:(qb + 1) * tq, vt * LANES:(vt + 1) * LANES] = jnp.concatenate(rows, axis=0).T


def _attention(q_t, k_aug, v_t):
    bsz, nh, nb, _, tq = q_t.shape
    seq = nb * tq
    assert nb % ATTN_QBLOCKS == 0
    chains = ATTN_QBLOCKS * nh
    resident = pl.Buffered(1)
    return pl.pallas_call(
        _attn_kernel,
        grid=(bsz, nb // ATTN_QBLOCKS),
        in_specs=[pl.BlockSpec((1, nh, ATTN_QBLOCKS, LANES, tq), lambda b, g: (b, 0, g, 0, 0)),
                  pl.BlockSpec((1, nh, ATTN_QBLOCKS, LANES, tq),
                               lambda b, g: (b, 0, jnp.minimum(g + 1, nb // ATTN_QBLOCKS - 1), 0, 0)),
                  pl.BlockSpec((1, nh, seq, LANES), lambda b, g: (b, 0, 0, 0), pipeline_mode=resident),
                  pl.BlockSpec((1, nh, nb, VT_ROWS, tq), lambda b, g: (b, 0, 0, 0, 0), pipeline_mode=resident)],
        out_specs=pl.BlockSpec((1, ATTN_QBLOCKS * tq, nh * HEAD_DIM), lambda b, g: (b, g, 0)),
        out_shape=jax.ShapeDtypeStruct((bsz, seq, nh * HEAD_DIM), F32),
        scratch_shapes=[pltpu.VMEM((chains, tq, tq), F32), pltpu.VMEM((chains, tq, tq), F32),
                        pltpu.VMEM((chains, 1, tq), F32), pltpu.VMEM((chains, VT_ROWS, tq), F32)],
        compiler_params=_params("arbitrary", "arbitrary"),
        name="attn",
    )(q_t, q_t, k_aug, v_t)


S5_STEPS = 128
S5_SUB_STEPS = 32
S5_CHUNKS = 4
S5_CHUNK_GROUPS = SSM_GROUPS // S5_CHUNKS
S5_CHUNK_STATES = S5_CHUNK_GROUPS * SSM_STATE
S5_CHUNK_LANES = S5_CHUNK_GROUPS * SSM_GROUP_CH


S5_CARRY_GROUPS = LANES // SSM_STATE


def _cmul(ar, ai, br, bi):
    return ar * br - ai * bi, ar * bi + ai * br


def _s5_carry_kernel(ut_ref, lam_ref, bt_ref, o_ref):
    bsz, nblk, nrows, tm = ut_ref.shape
    nch = nrows // S5_CARRY_GROUPS
    seg_blocks = nblk // N_SEG
    ls = seg_blocks * tm
    reps = bsz * N_SEG
    sub = lax.broadcasted_iota(jnp.int32, (SUBLANES, LANES), 0)
    lr = jnp.broadcast_to(lam_ref[0, 0:1], (SUBLANES, LANES))
    li = jnp.broadcast_to(lam_ref[0, 1:2], (SUBLANES, LANES))
    pr = jnp.ones((SUBLANES, LANES), F32)
    pi = jnp.zeros((SUBLANES, LANES), F32)
    tr = jnp.zeros((SUBLANES, LANES), F32)
    ti = jnp.zeros((SUBLANES, LANES), F32)
    for r in range(SUBLANES):
        tr = jnp.where(sub == SUBLANES - 1 - r, pr, tr)
        ti = jnp.where(sub == SUBLANES - 1 - r, pi, ti)
        pr, pi = _cmul(pr, pi, lr, li)
    span = SUBLANES
    while span < ls:
        nr, ni = _cmul(tr, ti, pr[0:1], pi[0:1])
        tr = jnp.concatenate([nr, tr], axis=0)
        ti = jnp.concatenate([ni, ti], axis=0)
        pr, pi = _cmul(pr, pi, pr, pi)
        span *= 2
    def seg_rows(g, b, s):
        return jnp.concatenate([ut_ref[b, s * seg_blocks + k, g * nch:(g + 1) * nch, :]
                                for k in range(seg_blocks)], axis=1)

    lhs = jnp.concatenate([seg_rows(g, b, s)
                           for g in range(S5_CARRY_GROUPS) for b in range(bsz) for s in range(N_SEG)],
                          axis=0)
    gr = _dot(lhs, tr.astype(BF16))
    gi = _dot(lhs, ti.astype(BF16))
    btr = jnp.concatenate([bt_ref[0, 0]] * (reps * S5_CARRY_GROUPS), axis=0)
    bti = jnp.concatenate([bt_ref[0, 1]] * (reps * S5_CARRY_GROUPS), axis=0)
    wr, wi = _cmul(gr, gi, btr, bti)
    fr = wr.reshape(reps * S5_CARRY_GROUPS, nch, LANES).sum(axis=1)
    fi = wi.reshape(reps * S5_CARRY_GROUPS, nch, LANES).sum(axis=1)
    grp = lax.broadcasted_iota(jnp.int32, (reps, LANES), 1) // SSM_STATE
    fin_r, fin_i = fr[0:reps], fi[0:reps]
    for g in range(1, S5_CARRY_GROUPS):
        fin_r = jnp.where(grp == g, fr[g * reps:(g + 1) * reps], fin_r)
        fin_i = jnp.where(grp == g, fi[g * reps:(g + 1) * reps], fin_i)
    rows_r, rows_i = [], []
    for b in range(bsz):
        sr = jnp.zeros((1, LANES), F32)
        si = jnp.zeros((1, LANES), F32)
        rows_r.append(sr)
        rows_i.append(si)
        for seg in range(1, N_SEG):
            r = b * N_SEG + seg - 1
            mr, mi = _cmul(sr, si, pr[0:1], pi[0:1])
            sr, si = mr + fin_r[r:r + 1], mi + fin_i[r:r + 1]
            rows_r.append(sr)
            rows_i.append(si)
    o_ref[0, 0] = jnp.concatenate(rows_r, axis=0)
    o_ref[0, 1] = jnp.concatenate(rows_i, axis=0)


def _s5_main_kernel(u_ref, bc_ref, cre_ref, cim_ref, lam_ref, sinit_ref, d_ref, wglu_ref, bglu_ref, o_ref,
                    il_scr, bu_scr, st_scr, y_scr):
    jb = pl.program_id(1)
    nseg, steps, width = u_ref.shape
    cs = S5_CHUNK_STATES
    nsub = steps // S5_SUB_STEPS
    srows = S5_SUB_STEPS * nseg

    @pl.when(jb == 0)
    def _():
        st_scr[...] = sinit_ref[...]

    for j in range(steps):
        il_scr[j * nseg:(j + 1) * nseg, :] = u_ref[:, j, :]
    us = [il_scr[sb * srows:(sb + 1) * srows, :] for sb in range(nsub)]
    for sb in range(nsub):
        u_bf = us[sb].astype(BF16)
        for c in range(S5_CHUNKS):
            bu_scr[c, sb * srows:(sb + 1) * srows, :] = _dot(
                u_bf[:, c * S5_CHUNK_LANES:(c + 1) * S5_CHUNK_LANES], bc_ref[c])
    state = [(st_scr[c, :, 0:cs], st_scr[c, :, cs:2 * cs]) for c in range(S5_CHUNKS)]
    for sb in range(nsub):
        sub_rows = slice(sb * srows, (sb + 1) * srows)
        ys = []
        for c in range(S5_CHUNKS):
            lr = jnp.broadcast_to(lam_ref[c, 0:1, :], (nseg, cs))
            li = jnp.broadcast_to(lam_ref[c, 1:2, :], (nseg, cs))
            xr, xi = state[c]
            for j in range(sb * S5_SUB_STEPS, (sb + 1) * S5_SUB_STEPS):
                rows = slice(j * nseg, (j + 1) * nseg)
                xr, xi = (lr * xr - li * xi + bu_scr[c, rows, 0:cs],
                          lr * xi + li * xr + bu_scr[c, rows, cs:2 * cs])
                bu_scr[c, rows, 0:cs] = xr
                bu_scr[c, rows, cs:2 * cs] = xi
            state[c] = (xr, xi)
            ys.append(_dot(bu_scr[c, sub_rows, 0:cs].astype(BF16), cre_ref[c])
                      + _dot(bu_scr[c, sub_rows, cs:2 * cs].astype(BF16), cim_ref[c]))
        y = jnp.concatenate(ys, axis=1) + d_ref[...] * us[sb]
        y = y * (0.5 * (1.0 + jnp.tanh(math.sqrt(2.0 / math.pi) * (y + 0.044715 * (y * y * y)))))
        z = _dot(y.astype(BF16), wglu_ref[...]) + bglu_ref[...]
        y = y * jax.nn.sigmoid(z)
        for t in range(width // LANES):
            y_scr[t, sub_rows, :] = y[:, t * LANES:(t + 1) * LANES]
    for c in range(S5_CHUNKS):
        st_scr[c, :, 0:cs] = state[c][0]
        st_scr[c, :, cs:2 * cs] = state[c][1]
    for s in range(nseg):
        for t in range(width // LANES):
            o_ref[s, :, t * LANES:(t + 1) * LANES] = y_scr[t, pl.ds(s, steps, stride=nseg), :]


def _s5(u, u_t, prm, d_skip, w_glu, b_glu):
    bsz, seq, _ = u.shape
    ls = seq // N_SEG
    steps = S5_STEPS
    rows = steps * N_SEG
    ngrp = SSM_GROUPS // S5_CARRY_GROUPS
    carry = pl.pallas_call(
        _s5_carry_kernel,
        grid=(ngrp,),
        in_specs=[pl.BlockSpec((bsz, u_t.shape[1], S5_CARRY_GROUPS * SSM_GROUP_CH, u_t.shape[3]),
                               lambda g: (0, 0, g, 0)),
                  pl.BlockSpec((1, 2, LANES), lambda g: (g, 0, 0)),
                  pl.BlockSpec((1, 2, SSM_GROUP_CH, LANES), lambda g: (g, 0, 0, 0))],
        out_specs=pl.BlockSpec((1, 2, bsz * N_SEG, LANES), lambda g: (g, 0, 0, 0)),
        out_shape=jax.ShapeDtypeStruct((ngrp, 2, bsz * N_SEG, LANES), F32),
        compiler_params=_params("arbitrary"),
        name="s5_carry",
    )(u_t, prm["lam_carry"], prm["bt_carry"])
    sinit = carry.reshape(S5_CHUNKS, ngrp // S5_CHUNKS, 2, bsz, N_SEG, LANES)
    sinit = sinit.transpose(3, 0, 4, 2, 1, 5).reshape(bsz, S5_CHUNKS, N_SEG, 2 * S5_CHUNK_STATES)

    def const(shape):
        return pl.BlockSpec(shape, lambda b, j: (0,) * len(shape))

    blk = pl.BlockSpec((None, N_SEG, steps, SSM_WIDTH), lambda b, j: (b, 0, j, 0))
    out = pl.pallas_call(
        _s5_main_kernel,
        grid=(bsz, ls // steps),
        in_specs=[blk, const(prm["bc"].shape), const(prm["cre"].shape), const(prm["cim"].shape),
                  const(prm["lam_chunk"].shape),
                  pl.BlockSpec((None, S5_CHUNKS, N_SEG, 2 * S5_CHUNK_STATES), lambda b, j: (b, 0, 0, 0)),
                  const(d_skip.shape), const(w_glu.shape), const(b_glu.shape)],
        out_specs=blk,
        out_shape=jax.ShapeDtypeStruct((bsz, N_SEG, ls, SSM_WIDTH), F32),
        scratch_shapes=[pltpu.VMEM((rows, SSM_WIDTH), F32),
                        pltpu.VMEM((S5_CHUNKS, rows, 2 * S5_CHUNK_STATES), F32),
                        pltpu.VMEM((S5_CHUNKS, N_SEG, 2 * S5_CHUNK_STATES), F32),
                        pltpu.VMEM((SSM_WIDTH // LANES, rows, LANES), F32)],
        compiler_params=_params("arbitrary", "arbitrary"),
        name="s5_main",
    )(u.reshape(bsz, N_SEG, ls, SSM_WIDTH), prm["bc"], prm["cre"], prm["cim"], prm["lam_chunk"], sinit,
      d_skip, w_glu, b_glu)
    return out.reshape(bsz, seq, SSM_WIDTH)


FF_CHUNK = 1024


OUTMLP_PARTS = 2


def _outmlp_kernel(x_ref, attn_ref, ssm_ref, mod_ref, ga_ref, gs_ref, gm_ref, gf_ref,
                   wout_ref, w1_ref, w2_ref, o_ref, *, final_norm):
    tm, d = x_ref.shape[1], x_ref.shape[2]
    gt1 = mod_ref[0, :, 2 * d:3 * d]
    sh2 = mod_ref[0, :, 3 * d:4 * d]
    sc2 = mod_ref[0, :, 4 * d:5 * d]
    gt2 = mod_ref[0, :, 5 * d:6 * d]
    ff = w1_ref.shape[1]
    part = tm // OUTMLP_PARTS
    rows = [slice(p * part, (p + 1) * part) for p in range(OUTMLP_PARTS)]
    mixes = []
    for r in rows:
        a = _rmsnorm(attn_ref[0, r, :], ga_ref[...]).astype(BF16)
        s = _rmsnorm(ssm_ref[0, r, :], gs_ref[...]).astype(BF16)
        mixes.append((a, s))
    x1s = [x_ref[0, r, :] + gt1 * (_dot(a, wout_ref[0:ATTN_WIDTH, :]) + _dot(s, wout_ref[ATTN_WIDTH:, :]))
           for r, (a, s) in zip(rows, mixes)]
    hs = [(_rmsnorm(x1, gm_ref[...]) * (1.0 + sc2) + sh2).astype(BF16) for x1 in x1s]
    for r, x1, h in zip(rows, x1s, hs):
        acc = jnp.zeros((part, d), F32)
        for c in range(ff // FF_CHUNK):
            t = jnp.maximum(_dot(h, w1_ref[:, c * FF_CHUNK:(c + 1) * FF_CHUNK]), 0.0)
            acc = acc + _dot((t * t).astype(BF16), w2_ref[c * FF_CHUNK:(c + 1) * FF_CHUNK, :])
        x2 = x1 + gt2 * acc
        o_ref[0, r, :] = _rmsnorm(x2, gf_ref[...]) if final_norm else x2


OUTMLP_ROWS = 512


def _outmlp(x, attn, ssm, mod3, g_attn, g_ssm, g_mlp, g_final, w_out, w_fc1, w_fc2, final_norm):
    bsz, seq, d = x.shape
    tm = OUTMLP_ROWS
    const = lambda b, t: (0, 0)
    return pl.pallas_call(
        functools.partial(_outmlp_kernel, final_norm=final_norm),
        grid=(bsz, seq // tm),
        in_specs=[pl.BlockSpec((1, tm, d), lambda b, t: (b, t, 0)),
                  pl.BlockSpec((1, tm, ATTN_WIDTH), lambda b, t: (b, t, 0)),
                  pl.BlockSpec((1, tm, SSM_WIDTH), lambda b, t: (b, t, 0)),
                  pl.BlockSpec((1, 1, mod3.shape[-1]), lambda b, t: (b, 0, 0)),
                  pl.BlockSpec(g_attn.shape, const),
                  pl.BlockSpec(g_ssm.shape, const),
                  pl.BlockSpec(g_mlp.shape, const),
                  pl.BlockSpec(g_final.shape, const),
                  pl.BlockSpec(w_out.shape, const),
                  pl.BlockSpec(w_fc1.shape, const),
                  pl.BlockSpec(w_fc2.shape, const)],
        out_specs=pl.BlockSpec((1, tm, d), lambda b, t: (b, t, 0)),
        out_shape=jax.ShapeDtypeStruct((bsz, seq, d), F32),
        compiler_params=_params("arbitrary", "arbitrary"),
        name="outmlp",
    )(x, attn, ssm, mod3, g_attn, g_ssm, g_mlp, g_final, w_out, w_fc1, w_fc2)


def _rope_tables(seq):
    half = HEAD_DIM // 2
    inv_freq = ROPE_THETA ** (-jnp.arange(half, dtype=F32) / half)
    ang = jnp.arange(seq, dtype=F32)[:, None] * inv_freq[None, :]
    cos, sin = jnp.cos(ang), jnp.sin(ang)
    reps = LANES // HEAD_DIM
    cos_t = jnp.tile(cos, (1, 2 * reps))
    sin_t = jnp.tile(jnp.concatenate([-sin, sin], axis=1), (1, reps))
    return cos_t, sin_t


def _s5_params(lam_re, lam_im, log_dt, b_re, b_im, c_re, c_im):
    lr, li = lam_re.astype(F32), lam_im.astype(F32)
    dt = jnp.exp(log_dt.astype(F32))[:, None]
    mag = jnp.exp(lr * dt)
    ar, ai = mag * jnp.cos(li * dt), mag * jnp.sin(li * dt)
    den = lr * lr + li * li
    cr = ((ar - 1.0) * lr + ai * li) / den
    ci = (ai * lr - (ar - 1.0) * li) / den
    bbar_re = cr[..., None] * b_re.astype(F32) - ci[..., None] * b_im.astype(F32)
    bbar_im = cr[..., None] * b_im.astype(F32) + ci[..., None] * b_re.astype(F32)
    eye = jnp.eye(S5_CHUNK_GROUPS, dtype=F32)

    def chunked(m):
        return m.reshape((S5_CHUNKS, S5_CHUNK_GROUPS) + m.shape[1:])

    def diag_in(m):
        return jnp.einsum('cgpn,gh->cgnhp', chunked(m), eye).reshape(S5_CHUNKS, S5_CHUNK_LANES, S5_CHUNK_STATES)

    def diag_out(m):
        return jnp.einsum('cgnp,gh->cgphn', chunked(m), eye).reshape(S5_CHUNKS, S5_CHUNK_STATES, S5_CHUNK_LANES)

    ngrp = SSM_GROUPS // S5_CARRY_GROUPS

    def lane_tiles(m):
        m = m.reshape(ngrp, S5_CARRY_GROUPS, SSM_GROUP_CH, SSM_STATE)
        return m.transpose(0, 2, 1, 3).reshape(ngrp, SSM_GROUP_CH, LANES)

    return {
        "bc": jnp.concatenate([diag_in(bbar_re), diag_in(bbar_im)], axis=2).astype(BF16),
        "cre": diag_out(c_re.astype(F32)).astype(BF16),
        "cim": diag_out(-c_im.astype(F32)).astype(BF16),
        "lam_chunk": jnp.stack([ar.reshape(S5_CHUNKS, S5_CHUNK_STATES),
                                ai.reshape(S5_CHUNKS, S5_CHUNK_STATES)], axis=1),
        "lam_carry": jnp.stack([ar.reshape(ngrp, LANES), ai.reshape(ngrp, LANES)], axis=1),
        "bt_carry": jnp.stack([lane_tiles(bbar_re.transpose(0, 2, 1)),
                               lane_tiles(bbar_im.transpose(0, 2, 1))], axis=1),
    }


def kernel(x, c, w_ada, b_ada, g_mix, w_in, g_attn_out, lam_re, lam_im, log_dt, b_re, b_im, c_re, c_im,
           d_skip, w_glu, b_glu, g_ssm_out, w_out, g_mlp, w_fc1, w_fc2, g_final):
    bsz, seq, d = x.shape
    depth = w_ada.shape[0]
    assert seq % (N_SEG * MOBA_BLOCK) == 0 and (seq // N_SEG) % S5_STEPS == 0 and bsz <= SUBLANES
    cos_t, sin_t = _rope_tables(seq)
    c_pad = jnp.zeros((SUBLANES, d), F32).at[:bsz].set(c.astype(F32))

    for l in range(depth):
        mod, win_hi, win_lo = _adaln(c_pad, w_ada[l], b_ada[l][None, :], w_in[l])
        mod3 = mod[:bsz, None, :]

        q_t, k_aug, v_t, u, u_t, wout_bf, wfc1_bf, wfc2_bf = _inproj(
            x, mod3, g_mix[l][None, :], win_hi, win_lo, cos_t, sin_t, w_out[l], w_fc1[l], w_fc2[l])
        attn = _attention(q_t, k_aug, v_t)

        prm = _s5_params(lam_re[l], lam_im[l], log_dt[l], b_re[l], b_im[l], c_re[l], c_im[l])
        ssm = _s5(u, u_t, prm, d_skip[l].reshape(1, SSM_WIDTH), w_glu[l].astype(BF16), b_glu[l][None, :])

        x = _outmlp(x, attn, ssm, mod3, g_attn_out[l][None, :], g_ssm_out[l][None, :], g_mlp[l][None, :],
                    g_final[None, :], wout_bf, wfc1_bf, wfc2_bf, final_norm=(l == depth - 1))
    return x
```

```python
import functools
import math

import jax
import jax.numpy as jnp
from jax import lax
from jax.experimental import pallas as pl
from jax.experimental.pallas import tpu as pltpu

F32 = jnp.float32
BF16 = jnp.bfloat16

HEAD_DIM = 64
ATTN_HEADS = 8
ATTN_WIDTH = ATTN_HEADS * HEAD_DIM
SSM_GROUPS = 32
SSM_GROUP_CH = 16
SSM_WIDTH = SSM_GROUPS * SSM_GROUP_CH
SSM_STATE = 64
MOBA_BLOCK = 256
MOBA_TOPK = 3
ROPE_THETA = 10000.0
EPS = 1e-6

LANES = 128
SUBLANES = 8
N_SEG = SUBLANES
HEADS_PER_TILE = LANES // HEAD_DIM
MASK_VALUE = -(2.0 ** 100)
VMEM_LIMIT_BYTES = 56 * 1024 * 1024


def _split_bf16(a):
    hi = a.astype(BF16)
    lo = (a - hi.astype(F32)).astype(BF16)
    return hi, lo


def _dot(a, b):
    return jnp.dot(a, b, preferred_element_type=F32)


def _rmsnorm(x, g):
    return x * lax.rsqrt(jnp.mean(x * x, axis=-1, keepdims=True) + EPS) * g


def _params(*semantics):
    return pltpu.CompilerParams(dimension_semantics=semantics, vmem_limit_bytes=VMEM_LIMIT_BYTES)


ADALN_STEPS = 4


def _adaln_kernel(c_ref, w_ref, b_ref, win_ref, o_ref, win_hi_ref, win_lo_ref):
    c = c_ref[...]
    s_hi, s_lo = _split_bf16(c * jax.nn.sigmoid(c))
    w_hi, w_lo = _split_bf16(w_ref[...])
    o_ref[...] = _dot(s_hi, w_hi) + _dot(s_lo, w_hi) + _dot(s_hi, w_lo) + b_ref[...]
    win_hi_ref[...], win_lo_ref[...] = _split_bf16(win_ref[...])


def _adaln(c_pad, w, b, w_in):
    rows, d = c_pad.shape
    n = w.shape[1]
    tn = n // ADALN_STEPS
    tw = w_in.shape[1] // ADALN_STEPS
    slab = pl.BlockSpec((d, tw), lambda j: (0, j))
    return pl.pallas_call(
        _adaln_kernel,
        grid=(ADALN_STEPS,),
        in_specs=[pl.BlockSpec((rows, d), lambda j: (0, 0)),
                  pl.BlockSpec((d, tn), lambda j: (0, j)),
                  pl.BlockSpec((1, tn), lambda j: (0, j)),
                  slab],
        out_specs=[pl.BlockSpec((rows, tn), lambda j: (0, j)), slab, slab],
        out_shape=[jax.ShapeDtypeStruct((rows, n), F32),
                   jax.ShapeDtypeStruct(w_in.shape, BF16),
                   jax.ShapeDtypeStruct(w_in.shape, BF16)],
        compiler_params=_params("arbitrary"),
        name="adaln",
    )(c_pad, w, b, w_in)


VT_ROWS = HEAD_DIM + 16


def _top_blocks(gate, blk, past, own):
    nb = gate.shape[0]
    g = jnp.where(past, gate, -jnp.inf)
    chosen = None
    for _ in range(MOBA_TOPK):
        best = jnp.max(g, axis=0, keepdims=True)
        first = jnp.min(jnp.where(g == best, blk, float(nb)), axis=0, keepdims=True)
        pick = blk == first
        chosen = pick if chosen is None else (chosen | pick)
        g = jnp.where(pick, -jnp.inf, g)
    return (chosen & past) | own


INPROJ_BLOCKS = 4


def _inproj_route(i, slot, qk, vu, cos_ref, sin_ref, qt_ref, ka_ref, vt_ref, u_ref, ut_ref, km_scr):
    tm = qk.shape[0]
    nb = km_scr.shape[0]
    rows = slice(slot * tm, (slot + 1) * tm)
    u = vu[:, ATTN_WIDTH:]
    u_ref[0, rows] = u
    ut_ref[0, slot] = u.T.astype(BF16)

    reps = ATTN_WIDTH // LANES
    cos = jnp.concatenate([cos_ref[rows]] * reps, axis=1)
    sin = jnp.concatenate([sin_ref[rows]] * reps, axis=1)
    wide_lane = lax.broadcasted_iota(jnp.int32, (tm, ATTN_WIDTH), 1)
    first_half = (wide_lane & (HEAD_DIM // 2)) == 0

    def rope(t):
        partner = jnp.where(first_half,
                            pltpu.roll(t, ATTN_WIDTH - HEAD_DIM // 2, 1),
                            pltpu.roll(t, HEAD_DIM // 2, 1))
        return t * cos + partner * sin

    q = rope(qk[:, :ATTN_WIDTH])
    k = rope(qk[:, ATTN_WIDTH:])
    v = vu[:, :ATTN_WIDTH]

    km_scr[pl.ds(i, 1), :] = jnp.sum(k, axis=0, keepdims=True) * (1.0 / tm)
    km_all = km_scr[...]

    lane = lax.broadcasted_iota(jnp.int32, (tm, LANES), 1)
    km_lane = lax.broadcasted_iota(jnp.int32, (nb, LANES), 1)
    blk_i = lax.broadcasted_iota(jnp.int32, (nb, tm), 0)
    blk = blk_i.astype(F32)
    past = blk_i < i
    own = blk_i == i
    ones_rows = jnp.where(lax.broadcasted_iota(jnp.int32, (VT_ROWS - HEAD_DIM, tm), 0) == 0, 1.0, 0.0)
    onehot = jnp.where(lane == HEAD_DIM + i, 1.0, 0.0)
    scale = HEAD_DIM ** -0.5 * math.log2(math.e)
    for tile in range(ATTN_WIDTH // LANES):
        cols = slice(tile * LANES, (tile + 1) * LANES)
        q_t = q[:, cols].T
        v_t = v[:, cols].T
        k_tile = k[:, cols]
        km = km_all[:, cols]
        qt_hi, qt_lo = _split_bf16(q_t)
        km_heads = jnp.concatenate([jnp.where(km_lane // HEAD_DIM == hh, km, 0.0)
                                    for hh in range(HEADS_PER_TILE)], axis=0)
        kmh_hi, kmh_lo = _split_bf16(km_heads)
        gates = _dot(kmh_hi, qt_hi) + _dot(kmh_lo, qt_hi) + _dot(kmh_hi, qt_lo)
        for hh in range(HEADS_PER_TILE):
            h_idx = tile * HEADS_PER_TILE + hh
            gate = gates[hh * nb:(hh + 1) * nb]
            bias_t = jnp.where(_top_blocks(gate, blk, past, own), 0.0, MASK_VALUE)
            head_rows = slice(hh * HEAD_DIM, (hh + 1) * HEAD_DIM)
            qt_ref[0, h_idx, slot] = jnp.concatenate(
                [q_t[head_rows] * scale, bias_t, jnp.zeros((LANES - HEAD_DIM - nb, tm), F32)],
                axis=0).astype(BF16)
            vt_ref[0, h_idx, slot] = jnp.concatenate([v_t[head_rows], ones_rows], axis=0).astype(BF16)
            kh = jnp.where(lane // HEAD_DIM == hh, k_tile, 0.0)
            if hh:
                kh = pltpu.roll(kh, LANES - hh * HEAD_DIM, 1)
            ka_ref[0, h_idx, rows] = jnp.where(lane < HEAD_DIM, kh, onehot).astype(BF16)


def _inproj_kernel(x_ref, mod_ref, g_ref, wqk_hi_ref, wqk_lo_ref, wvu_ref, cos_ref, sin_ref,
                   wout_ref, wfc1_ref, wfc2_ref,
                   qt_ref, ka_ref, vt_ref, u_ref, ut_ref, wout_bf_ref, wfc1_bf_ref, wfc2_bf_ref, km_scr):
    step = pl.program_id(1)

    @pl.when(step == 0)
    def _():
        km_scr[...] = jnp.zeros_like(km_scr)

    wout_bf_ref[...] = wout_ref[...].astype(BF16)
    wfc1_bf_ref[...] = wfc1_ref[...].astype(BF16)
    wfc2_bf_ref[...] = wfc2_ref[...].astype(BF16)
    d = x_ref.shape[2]
    tm = x_ref.shape[1] // INPROJ_BLOCKS
    sh1 = mod_ref[0, :, 0:d]
    sc1 = mod_ref[0, :, d:2 * d]
    hs = []
    for slot in range(INPROJ_BLOCKS):
        x = x_ref[0, slot * tm:(slot + 1) * tm, :]
        hs.append(_split_bf16(_rmsnorm(x, g_ref[...]) * (1.0 + sc1) + sh1))
    prj = []
    for h_hi, h_lo in hs:
        qk = (_dot(h_hi, wqk_hi_ref[...]) + _dot(h_lo, wqk_hi_ref[...])
              + _dot(h_hi, wqk_lo_ref[...]))
        prj.append((qk, _dot(h_hi, wvu_ref[...])))

    for slot, (qk, vu) in enumerate(prj):
        _inproj_route(step * INPROJ_BLOCKS + slot, slot, qk, vu, cos_ref, sin_ref,
                      qt_ref, ka_ref, vt_ref, u_ref, ut_ref, km_scr)


def _inproj(x, mod3, g_mix, win_hi, win_lo, cos_t, sin_t, w_out, w_fc1, w_fc2):
    bsz, seq, d = x.shape
    tm = MOBA_BLOCK
    nb = seq // tm
    assert nb % SUBLANES == 0 and nb <= LANES - HEAD_DIM and nb % INPROJ_BLOCKS == 0
    nblk = INPROJ_BLOCKS
    rows = nblk * tm
    const = lambda b, t: (0, 0)
    wqk = 2 * ATTN_WIDTH
    assert win_hi.shape[1] == 2 * wqk
    per_batch = nb // nblk
    steps = bsz * per_batch

    def slab(w):
        n = w.shape[0] // steps
        assert n * steps == w.shape[0] and n % (2 * SUBLANES) == 0
        return pl.BlockSpec((n, w.shape[1]), lambda b, t: (b * per_batch + t, 0))

    casts = [w_out, w_fc1, w_fc2]
    return pl.pallas_call(
        _inproj_kernel,
        grid=(bsz, per_batch),
        in_specs=[pl.BlockSpec((1, rows, d), lambda b, t: (b, t, 0)),
                  pl.BlockSpec((1, 1, mod3.shape[-1]), lambda b, t: (b, 0, 0)),
                  pl.BlockSpec((1, d), const),
                  pl.BlockSpec((d, wqk), const),
                  pl.BlockSpec((d, wqk), const),
                  pl.BlockSpec((d, wqk), lambda b, t: (0, 1)),
                  pl.BlockSpec((rows, LANES), lambda b, t: (t, 0)),
                  pl.BlockSpec((rows, LANES), lambda b, t: (t, 0))] + [slab(w) for w in casts],
        out_specs=[pl.BlockSpec((1, ATTN_HEADS, nblk, LANES, tm), lambda b, t: (b, 0, t, 0, 0)),
                   pl.BlockSpec((1, ATTN_HEADS, rows, LANES), lambda b, t: (b, 0, t, 0)),
                   pl.BlockSpec((1, ATTN_HEADS, nblk, VT_ROWS, tm), lambda b, t: (b, 0, t, 0, 0)),
                   pl.BlockSpec((1, rows, SSM_WIDTH), lambda b, t: (b, t, 0)),
                   pl.BlockSpec((1, nblk, SSM_WIDTH, tm), lambda b, t: (b, t, 0, 0))] + [slab(w) for w in casts],
        out_shape=[jax.ShapeDtypeStruct((bsz, ATTN_HEADS, nb, LANES, tm), BF16),
                   jax.ShapeDtypeStruct((bsz, ATTN_HEADS, seq, LANES), BF16),
                   jax.ShapeDtypeStruct((bsz, ATTN_HEADS, nb, VT_ROWS, tm), BF16),
                   jax.ShapeDtypeStruct((bsz, seq, SSM_WIDTH), F32),
                   jax.ShapeDtypeStruct((bsz, nb, SSM_WIDTH, tm), BF16)]
                  + [jax.ShapeDtypeStruct(w.shape, BF16) for w in casts],
        scratch_shapes=[pltpu.VMEM((nb, ATTN_WIDTH), F32)],
        compiler_params=_params("arbitrary", "arbitrary"),
        name="inproj",
    )(x, mod3, g_mix, win_hi, win_lo, win_hi, cos_t, sin_t, *casts)


ATTN_QBLOCKS = 2
RUNNING_MAX_INIT = -3.0e38


def _attn_kernel(qt_ref, qt_next_ref, ka_ref, vt_ref, o_ref, sa_scr, sb_scr, m_scr, acc_scr):
    pair = pl.program_id(1)
    i = ATTN_QBLOCKS * pair
    nblocks = vt_ref.shape[2]
    nh = qt_ref.shape[1]
    tq = qt_ref.shape[4]
    key = lax.broadcasted_iota(jnp.int32, (tq, tq), 0)
    qry = lax.broadcasted_iota(jnp.int32, (tq, tq), 1)
    causal = key <= qry

    def scores(qb, h, j, s_scr, mask=None, q_ref=qt_ref):
        start = pl.multiple_of(j * tq, tq)
        s = _dot(ka_ref[0, h, pl.ds(start, tq), :], q_ref[0, h, qb])
        if mask is not None:
            s = jnp.where(mask, s, MASK_VALUE)
        s_scr[qb * nh + h] = s

    def consume(qb, h, j, s_scr):
        c = qb * nh + h
        s = s_scr[c]
        m_old = m_scr[c]
        m_new = jnp.maximum(m_old, jnp.max(s, axis=0, keepdims=True))
        alpha = jnp.exp2(m_old - m_new)
        p = jnp.exp2(s - m_new).astype(BF16)
        acc_scr[c] = alpha * acc_scr[c] + _dot(vt_ref[0, h, j], p)
        m_scr[c] = m_new

    @pl.when(pair == 0)
    def _():
        for h in range(nh):
            scores(0, h, i, sa_scr, causal)
            scores(1, h, i + 1, sa_scr, causal)

    m_scr[...] = jnp.full(m_scr.shape, RUNNING_MAX_INIT, F32)
    acc_scr[...] = jnp.zeros(acc_scr.shape, F32)
    for h in range(nh):
        scores(1, h, i, sb_scr)
        consume(1, h, i + 1, sa_scr)

    def two_blocks(j0):
        j1 = j0 + 1
        prev = jnp.where(j0 == 0, i, j0 - 1)
        for h in range(nh):
            scores(0, h, j0, sb_scr)
            consume(0, h, prev, sa_scr)
            scores(1, h, j0, sa_scr)
            consume(1, h, prev, sb_scr)
        for h in range(nh):
            scores(0, h, j1, sa_scr)
            consume(0, h, j0, sb_scr)
            scores(1, h, j1, sb_scr)
            consume(1, h, j0, sa_scr)

    def body(t, carry):
        for k in range(0, 8, 2):
            two_blocks(8 * t + k)
        return carry

    lax.fori_loop(0, pair // 4, body, 0)

    @pl.when(pair % 4 >= 2)
    def _():
        two_blocks(8 * (pair // 4))
        two_blocks(8 * (pair // 4) + 2)

    @pl.when(pair % 2 == 1)
    def _():
        two_blocks(i - 2)

    last = jnp.where(pair == 0, i, i - 1)
    nxt = jnp.minimum(i + ATTN_QBLOCKS, nblocks - ATTN_QBLOCKS)
    for h in range(nh):
        consume(0, h, last, sa_scr)
        consume(1, h, last, sb_scr)
        scores(0, h, nxt, sa_scr, causal, qt_next_ref)
        scores(1, h, nxt + 1, sa_scr, causal, qt_next_ref)

    for qb in range(ATTN_QBLOCKS):
        for vt in range(nh // HEADS_PER_TILE):
            rows = []
            for hh in range(HEADS_PER_TILE):
                acc = acc_scr[qb * nh + vt * HEADS_PER_TILE + hh]
                rows.append(acc[0:HEAD_DIM] / acc[HEAD_DIM:HEAD_DIM + 1])
            o_ref[0, qb * tq:(qb + 1) * tq, vt * LANES:(vt + 1) * LANES] = jnp.concatenate(rows, axis=0).T


def _attention(q_t, k_aug, v_t):
    bsz, nh, nb, _, tq = q_t.shape
    seq = nb * tq
    assert nb % ATTN_QBLOCKS == 0
    chains = ATTN_QBLOCKS * nh
    resident = pl.Buffered(1)
    return pl.pallas_call(
        _attn_kernel,
        grid=(bsz, nb // ATTN_QBLOCKS),
        in_specs=[pl.BlockSpec((1, nh, ATTN_QBLOCKS, LANES, tq), lambda b, g: (b, 0, g, 0, 0)),
                  pl.BlockSpec((1, nh, ATTN_QBLOCKS, LANES, tq),
                               lambda b, g: (b, 0, jnp.minimum(g + 1, nb // ATTN_QBLOCKS - 1), 0, 0)),
                  pl.BlockSpec((1, nh, seq, LANES), lambda b, g: (b, 0, 0, 0), pipeline_mode=resident),
                  pl.BlockSpec((1, nh, nb, VT_ROWS, tq), lambda b, g: (b, 0, 0, 0, 0))],
        out_specs=pl.BlockSpec((1, ATTN_QBLOCKS * tq, nh * HEAD_DIM), lambda b, g: (b, g, 0)),
        out_shape=jax.ShapeDtypeStruct((bsz, seq, nh * HEAD_DIM), F32),
        scratch_shapes=[pltpu.VMEM((chains, tq, tq), F32), pltpu.VMEM((chains, tq, tq), F32),
                        pltpu.VMEM((chains, 1, tq), F32), pltpu.VMEM((chains, VT_ROWS, tq), F32)],
        compiler_params=_params("arbitrary", "arbitrary"),
        name="attn",
    )(q_t, q_t, k_aug, v_t)


S5_STEPS = 128
S5_SUB_STEPS = 32
S5_CHUNKS = 4
S5_CHUNK_GROUPS = SSM_GROUPS // S5_CHUNKS
S5_CHUNK_STATES = S5_CHUNK_GROUPS * SSM_STATE
S5_CHUNK_LANES = S5_CHUNK_GROUPS * SSM_GROUP_CH


S5_CARRY_GROUPS = LANES // SSM_STATE


def _cmul(ar, ai, br, bi):
    return ar * br - ai * bi, ar * bi + ai * br


def _s5_carry_kernel(ut_ref, lam_ref, bt_ref, o_ref):
    bsz, nblk, nrows, tm = ut_ref.shape
    nch = nrows // S5_CARRY_GROUPS
    seg_blocks = nblk // N_SEG
    ls = seg_blocks * tm
    reps = bsz * N_SEG
    sub = lax.broadcasted_iota(jnp.int32, (SUBLANES, LANES), 0)
    lr = jnp.broadcast_to(lam_ref[0, 0:1], (SUBLANES, LANES))
    li = jnp.broadcast_to(lam_ref[0, 1:2], (SUBLANES, LANES))
    pr = jnp.ones((SUBLANES, LANES), F32)
    pi = jnp.zeros((SUBLANES, LANES), F32)
    tr = jnp.zeros((SUBLANES, LANES), F32)
    ti = jnp.zeros((SUBLANES, LANES), F32)
    for r in range(SUBLANES):
        tr = jnp.where(sub == SUBLANES - 1 - r, pr, tr)
        ti = jnp.where(sub == SUBLANES - 1 - r, pi, ti)
        pr, pi = _cmul(pr, pi, lr, li)
    span = SUBLANES
    while span < ls:
        nr, ni = _cmul(tr, ti, pr[0:1], pi[0:1])
        tr = jnp.concatenate([nr, tr], axis=0)
        ti = jnp.concatenate([ni, ti], axis=0)
        pr, pi = _cmul(pr, pi, pr, pi)
        span *= 2
    def seg_rows(g, b, s):
        return jnp.concatenate([ut_ref[b, s * seg_blocks + k, g * nch:(g + 1) * nch, :]
                                for k in range(seg_blocks)], axis=1)

    lhs = jnp.concatenate([seg_rows(g, b, s)
                           for g in range(S5_CARRY_GROUPS) for b in range(bsz) for s in range(N_SEG)],
                          axis=0)
    gr = _dot(lhs, tr.astype(BF16))
    gi = _dot(lhs, ti.astype(BF16))
    btr = jnp.concatenate([bt_ref[0, 0]] * (reps * S5_CARRY_GROUPS), axis=0)
    bti = jnp.concatenate([bt_ref[0, 1]] * (reps * S5_CARRY_GROUPS), axis=0)
    wr, wi = _cmul(gr, gi, btr, bti)
    fr = wr.reshape(reps * S5_CARRY_GROUPS, nch, LANES).sum(axis=1)
    fi = wi.reshape(reps * S5_CARRY_GROUPS, nch, LANES).sum(axis=1)
    grp = lax.broadcasted_iota(jnp.int32, (reps, LANES), 1) // SSM_STATE
    fin_r, fin_i = fr[0:reps], fi[0:reps]
    for g in range(1, S5_CARRY_GROUPS):
        fin_r = jnp.where(grp == g, fr[g * reps:(g + 1) * reps], fin_r)
        fin_i = jnp.where(grp == g, fi[g * reps:(g + 1) * reps], fin_i)
    rows_r, rows_i = [], []
    for b in range(bsz):
        sr = jnp.zeros((1, LANES), F32)
        si = jnp.zeros((1, LANES), F32)
        rows_r.append(sr)
        rows_i.append(si)
        for seg in range(1, N_SEG):
            r = b * N_SEG + seg - 1
            mr, mi = _cmul(sr, si, pr[0:1], pi[0:1])
            sr, si = mr + fin_r[r:r + 1], mi + fin_i[r:r + 1]
            rows_r.append(sr)
            rows_i.append(si)
    o_ref[0, 0] = jnp.concatenate(rows_r, axis=0)
    o_ref[0, 1] = jnp.concatenate(rows_i, axis=0)


def _s5_main_kernel(u_ref, bc_ref, cre_ref, cim_ref, lam_ref, sinit_ref, d_ref, wglu_ref, bglu_ref, o_ref,
                    il_scr, bu_scr, st_scr, y_scr):
    jb = pl.program_id(1)
    nseg, steps, width = u_ref.shape
    cs = S5_CHUNK_STATES
    nsub = steps // S5_SUB_STEPS
    srows = S5_SUB_STEPS * nseg

    @pl.when(jb == 0)
    def _():
        st_scr[...] = sinit_ref[...]

    for j in range(steps):
        il_scr[j * nseg:(j + 1) * nseg, :] = u_ref[:, j, :]
    us = [il_scr[sb * srows:(sb + 1) * srows, :] for sb in range(nsub)]
    for sb in range(nsub):
        u_bf = us[sb].astype(BF16)
        for c in range(S5_CHUNKS):
            bu_scr[c, sb * srows:(sb + 1) * srows, :] = _dot(
                u_bf[:, c * S5_CHUNK_LANES:(c + 1) * S5_CHUNK_LANES], bc_ref[c])
    state = [(st_scr[c, :, 0:cs], st_scr[c, :, cs:2 * cs]) for c in range(S5_CHUNKS)]
    for sb in range(nsub):
        sub_rows = slice(sb * srows, (sb + 1) * srows)
        ys = []
        for c in range(S5_CHUNKS):
            lr = jnp.broadcast_to(lam_ref[c, 0:1, :], (nseg, cs))
            li = jnp.broadcast_to(lam_ref[c, 1:2, :], (nseg, cs))
            xr, xi = state[c]
            for j in range(sb * S5_SUB_STEPS, (sb + 1) * S5_SUB_STEPS):
                rows = slice(j * nseg, (j + 1) * nseg)
                xr, xi = (lr * xr - li * xi + bu_scr[c, rows, 0:cs],
                          lr * xi + li * xr + bu_scr[c, rows, cs:2 * cs])
                bu_scr[c, rows, 0:cs] = xr
                bu_scr[c, rows, cs:2 * cs] = xi
            state[c] = (xr, xi)
            ys.append(_dot(bu_scr[c, sub_rows, 0:cs].astype(BF16), cre_ref[c])
                      + _dot(bu_scr[c, sub_rows, cs:2 * cs].astype(BF16), cim_ref[c]))
        y = jnp.concatenate(ys, axis=1) + d_ref[...] * us[sb]
        y = y * (0.5 * (1.0 + jnp.tanh(math.sqrt(2.0 / math.pi) * (y + 0.044715 * (y * y * y)))))
        z = _dot(y.astype(BF16), wglu_ref[...]) + bglu_ref[...]
        y = y * jax.nn.sigmoid(z)
        for t in range(width // LANES):
            y_scr[t, sub_rows, :] = y[:, t * LANES:(t + 1) * LANES]
    for c in range(S5_CHUNKS):
        st_scr[c, :, 0:cs] = state[c][0]
        st_scr[c, :, cs:2 * cs] = state[c][1]
    for s in range(nseg):
        for t in range(width // LANES):
            o_ref[s, :, t * LANES:(t + 1) * LANES] = y_scr[t, pl.ds(s, steps, stride=nseg), :]


def _s5(u, u_t, prm, d_skip, w_glu, b_glu):
    bsz, seq, _ = u.shape
    ls = seq // N_SEG
    steps = S5_STEPS
    rows = steps * N_SEG
    ngrp = SSM_GROUPS // S5_CARRY_GROUPS
    carry = pl.pallas_call(
        _s5_carry_kernel,
        grid=(ngrp,),
        in_specs=[pl.BlockSpec((bsz, u_t.shape[1], S5_CARRY_GROUPS * SSM_GROUP_CH, u_t.shape[3]),
                               lambda g: (0, 0, g, 0)),
                  pl.BlockSpec((1, 2, LANES), lambda g: (g, 0, 0)),
                  pl.BlockSpec((1, 2, SSM_GROUP_CH, LANES), lambda g: (g, 0, 0, 0))],
        out_specs=pl.BlockSpec((1, 2, bsz * N_SEG, LANES), lambda g: (g, 0, 0, 0)),
        out_shape=jax.ShapeDtypeStruct((ngrp, 2, bsz * N_SEG, LANES), F32),
        compiler_params=_params("arbitrary"),
        name="s5_carry",
    )(u_t, prm["lam_carry"], prm["bt_carry"])
    sinit = carry.reshape(S5_CHUNKS, ngrp // S5_CHUNKS, 2, bsz, N_SEG, LANES)
    sinit = sinit.transpose(3, 0, 4, 2, 1, 5).reshape(bsz, S5_CHUNKS, N_SEG, 2 * S5_CHUNK_STATES)

    def const(shape):
        return pl.BlockSpec(shape, lambda b, j: (0,) * len(shape))

    blk = pl.BlockSpec((None, N_SEG, steps, SSM_WIDTH), lambda b, j: (b, 0, j, 0))
    out = pl.pallas_call(
        _s5_main_kernel,
        grid=(bsz, ls // steps),
        in_specs=[blk, const(prm["bc"].shape), const(prm["cre"].shape), const(prm["cim"].shape),
                  const(prm["lam_chunk"].shape),
                  pl.BlockSpec((None, S5_CHUNKS, N_SEG, 2 * S5_CHUNK_STATES), lambda b, j: (b, 0, 0, 0)),
                  const(d_skip.shape), const(w_glu.shape), const(b_glu.shape)],
        out_specs=blk,
        out_shape=jax.ShapeDtypeStruct((bsz, N_SEG, ls, SSM_WIDTH), F32),
        scratch_shapes=[pltpu.VMEM((rows, SSM_WIDTH), F32),
                        pltpu.VMEM((S5_CHUNKS, rows, 2 * S5_CHUNK_STATES), F32),
                        pltpu.VMEM((S5_CHUNKS, N_SEG, 2 * S5_CHUNK_STATES), F32),
                        pltpu.VMEM((SSM_WIDTH // LANES, rows, LANES), F32)],
        compiler_params=_params("arbitrary", "arbitrary"),
        name="s5_main",
    )(u.reshape(bsz, N_SEG, ls, SSM_WIDTH), prm["bc"], prm["cre"], prm["cim"], prm["lam_chunk"], sinit,
      d_skip, w_glu, b_glu)
    return out.reshape(bsz, seq, SSM_WIDTH)


FF_CHUNK = 1024


OUTMLP_PARTS = 2


def _outmlp_kernel(x_ref, attn_ref, ssm_ref, mod_ref, ga_ref, gs_ref, gm_ref, gf_ref,
                   wout_ref, w1_ref, w2_ref, o_ref, *, final_norm):
    tm, d = x_ref.shape[1], x_ref.shape[2]
    gt1 = mod_ref[0, :, 2 * d:3 * d]
    sh2 = mod_ref[0, :, 3 * d:4 * d]
    sc2 = mod_ref[0, :, 4 * d:5 * d]
    gt2 = mod_ref[0, :, 5 * d:6 * d]
    ff = w1_ref.shape[1]
    part = tm // OUTMLP_PARTS
    rows = [slice(p * part, (p + 1) * part) for p in range(OUTMLP_PARTS)]
    mixes = []
    for r in rows:
        a = _rmsnorm(attn_ref[0, r, :], ga_ref[...]).astype(BF16)
        s = _rmsnorm(ssm_ref[0, r, :], gs_ref[...]).astype(BF16)
        mixes.append((a, s))
    x1s = [x_ref[0, r, :] + gt1 * (_dot(a, wout_ref[0:ATTN_WIDTH, :]) + _dot(s, wout_ref[ATTN_WIDTH:, :]))
           for r, (a, s) in zip(rows, mixes)]
    hs = [(_rmsnorm(x1, gm_ref[...]) * (1.0 + sc2) + sh2).astype(BF16) for x1 in x1s]
    for r, x1, h in zip(rows, x1s, hs):
        acc = jnp.zeros((part, d), F32)
        for c in range(ff // FF_CHUNK):
            t = jnp.maximum(_dot(h, w1_ref[:, c * FF_CHUNK:(c + 1) * FF_CHUNK]), 0.0)
            acc = acc + _dot((t * t).astype(BF16), w2_ref[c * FF_CHUNK:(c + 1) * FF_CHUNK, :])
        x2 = x1 + gt2 * acc
        o_ref[0, r, :] = _rmsnorm(x2, gf_ref[...]) if final_norm else x2


OUTMLP_ROWS = 512


def _outmlp(x, attn, ssm, mod3, g_attn, g_ssm, g_mlp, g_final, w_out, w_fc1, w_fc2, final_norm):
    bsz, seq, d = x.shape
    tm = OUTMLP_ROWS
    const = lambda b, t: (0, 0)
    return pl.pallas_call(
        functools.partial(_outmlp_kernel, final_norm=final_norm),
        grid=(bsz, seq // tm),
        in_specs=[pl.BlockSpec((1, tm, d), lambda b, t: (b, t, 0)),
                  pl.BlockSpec((1, tm, ATTN_WIDTH), lambda b, t: (b, t, 0)),
                  pl.BlockSpec((1, tm, SSM_WIDTH), lambda b, t: (b, t, 0)),
                  pl.BlockSpec((1, 1, mod3.shape[-1]), lambda b, t: (b, 0, 0)),
                  pl.BlockSpec(g_attn.shape, const),
                  pl.BlockSpec(g_ssm.shape, const),
                  pl.BlockSpec(g_mlp.shape, const),
                  pl.BlockSpec(g_final.shape, const),
                  pl.BlockSpec(w_out.shape, const),
                  pl.BlockSpec(w_fc1.shape, const),
                  pl.BlockSpec(w_fc2.shape, const)],
        out_specs=pl.BlockSpec((1, tm, d), lambda b, t: (b, t, 0)),
        out_shape=jax.ShapeDtypeStruct((bsz, seq, d), F32),
        compiler_params=_params("arbitrary", "arbitrary"),
        name="outmlp",
    )(x, attn, ssm, mod3, g_attn, g_ssm, g_mlp, g_final, w_out, w_fc1, w_fc2)


def _rope_tables(seq):
    half = HEAD_DIM // 2
    inv_freq = ROPE_THETA ** (-jnp.arange(half, dtype=F32) / half)
    ang = jnp.arange(seq, dtype=F32)[:, None] * inv_freq[None, :]
    cos, sin = jnp.cos(ang), jnp.sin(ang)
    reps = LANES // HEAD_DIM
    cos_t = jnp.tile(cos, (1, 2 * reps))
    sin_t = jnp.tile(jnp.concatenate([-sin, sin], axis=1), (1, reps))
    return cos_t, sin_t


def _s5_params(lam_re, lam_im, log_dt, b_re, b_im, c_re, c_im):
    lr, li = lam_re.astype(F32), lam_im.astype(F32)
    dt = jnp.exp(log_dt.astype(F32))[:, None]
    mag = jnp.exp(lr * dt)
    ar, ai = mag * jnp.cos(li * dt), mag * jnp.sin(li * dt)
    den = lr * lr + li * li
    cr = ((ar - 1.0) * lr + ai * li) / den
    ci = (ai * lr - (ar - 1.0) * li) / den
    bbar_re = cr[..., None] * b_re.astype(F32) - ci[..., None] * b_im.astype(F32)
    bbar_im = cr[..., None] * b_im.astype(F32) + ci[..., None] * b_re.astype(F32)
    eye = jnp.eye(S5_CHUNK_GROUPS, dtype=F32)

    def chunked(m):
        return m.reshape((S5_CHUNKS, S5_CHUNK_GROUPS) + m.shape[1:])

    def diag_in(m):
        return jnp.einsum('cgpn,gh->cgnhp', chunked(m), eye).reshape(S5_CHUNKS, S5_CHUNK_LANES, S5_CHUNK_STATES)

    def diag_out(m):
        return jnp.einsum('cgnp,gh->cgphn', chunked(m), eye).reshape(S5_CHUNKS, S5_CHUNK_STATES, S5_CHUNK_LANES)

    ngrp = SSM_GROUPS // S5_CARRY_GROUPS

    def lane_tiles(m):
        m = m.reshape(ngrp, S5_CARRY_GROUPS, SSM_GROUP_CH, SSM_STATE)
        return m.transpose(0, 2, 1, 3).reshape(ngrp, SSM_GROUP_CH, LANES)

    return {
        "bc": jnp.concatenate([diag_in(bbar_re), diag_in(bbar_im)], axis=2).astype(BF16),
        "cre": diag_out(c_re.astype(F32)).astype(BF16),
        "cim": diag_out(-c_im.astype(F32)).astype(BF16),
        "lam_chunk": jnp.stack([ar.reshape(S5_CHUNKS, S5_CHUNK_STATES),
                                ai.reshape(S5_CHUNKS, S5_CHUNK_STATES)], axis=1),
        "lam_carry": jnp.stack([ar.reshape(ngrp, LANES), ai.reshape(ngrp, LANES)], axis=1),
        "bt_carry": jnp.stack([lane_tiles(bbar_re.transpose(0, 2, 1)),
                               lane_tiles(bbar_im.transpose(0, 2, 1))], axis=1),
    }


def kernel(x, c, w_ada, b_ada, g_mix, w_in, g_attn_out, lam_re, lam_im, log_dt, b_re, b_im, c_re, c_im,
           d_skip, w_glu, b_glu, g_ssm_out, w_out, g_mlp, w_fc1, w_fc2, g_final):
    bsz, seq, d = x.shape
    depth = w_ada.shape[0]
    assert seq % (N_SEG * MOBA_BLOCK) == 0 and (seq // N_SEG) % S5_STEPS == 0 and bsz <= SUBLANES
    cos_t, sin_t = _rope_tables(seq)
    c_pad = jnp.zeros((SUBLANES, d), F32).at[:bsz].set(c.astype(F32))

    for l in range(depth):
        mod, win_hi, win_lo = _adaln(c_pad, w_ada[l], b_ada[l][None, :], w_in[l])
        mod3 = mod[:bsz, None, :]

        q_t, k_aug, v_t, u, u_t, wout_bf, wfc1_bf, wfc2_bf = _inproj(
            x, mod3, g_mix[l][None, :], win_hi, win_lo, cos_t, sin_t, w_out[l], w_fc1[l], w_fc2[l])
        attn = _attention(q_t, k_aug, v_t)

        prm = _s5_params(lam_re[l], lam_im[l], log_dt[l], b_re[l], b_im[l], c_re[l], c_im[l])
        ssm = _s5(u, u_t, prm, d_skip[l].reshape(1, SSM_WIDTH), w_glu[l].astype(BF16), b_glu[l][None, :])

        x = _outmlp(x, attn, ssm, mod3, g_attn_out[l][None, :], g_ssm_out[l][None, :], g_mlp[l][None, :],
                    g_final[None, :], wout_bf, wfc1_bf, wfc2_bf, final_norm=(l == depth - 1))
    return x
```
